```python
import math
import jax, jax.numpy as jnp
from jax import lax
import numpy as np

D_MODEL = 1024
BATCH = 16
SEQ = 4096
DEPTH = 4

PLE_DIM = 256
D_FF = 2816
MIX_WIDTH = D_MODEL
SSM_WIDTH = MIX_WIDTH // 2
POOL_WIDTH = MIX_WIDTH - SSM_WIDTH
SSM_GROUP_CH = 16
SSM_GROUPS = SSM_WIDTH // SSM_GROUP_CH
SSM_STATE = 64
POOL_WINDOWS = (2, 4, 8, 16)
POOL_GROUP_CH = POOL_WIDTH // len(POOL_WINDOWS)
EPS = 1e-6
DT_MIN = 1e-3
DT_MAX = 1e-1

kernel_name = "hybrid_s5_pool_macaron_ple"


def rms_norm(x, g):
    x32 = x.astype(jnp.float32)
    y = x32 * lax.rsqrt(jnp.mean(x32 * x32, axis=-1, keepdims=True) + EPS)
    return (y * g.astype(jnp.float32)).astype(x.dtype)


def swiglu(x, wi, wo):
    gu = x @ wi
    g, u = jnp.split(gu, 2, axis=-1)
    return (jax.nn.silu(g) * u) @ wo


def _ssm_combine(e1, e2):
    a1, b1 = e1
    a2, b2 = e2
    return a1 * a2, a2 * b1 + b2


def s5_mixer(u, lam_re, lam_im, log_dt, b_re, b_im, c_re, c_im, d_skip, w_glu):
    bsz, seq, _ = u.shape
    f32 = jnp.float32
    u32 = u.astype(f32)
    ug = u32.reshape(bsz, seq, SSM_GROUPS, SSM_GROUP_CH)
    lam = lax.complex(lam_re.astype(f32), lam_im.astype(f32))
    dt = jnp.exp(log_dt.astype(f32))[:, None]
    lam_bar = jnp.exp(lam * dt)
    b = lax.complex(b_re.astype(f32), b_im.astype(f32))
    b_bar = ((lam_bar - 1.0) / lam)[..., None] * b
    bu = jnp.einsum('blgh,gph->blgp', ug.astype(jnp.complex64), b_bar)
    a = jnp.broadcast_to(lam_bar[None, None], (1, seq, SSM_GROUPS, SSM_STATE))
    _, states = lax.associative_scan(_ssm_combine, (a, bu), axis=1)
    c = lax.complex(c_re.astype(f32), c_im.astype(f32))
    y = jnp.real(jnp.einsum('blgp,ghp->blgh', states, c)).reshape(bsz, seq, SSM_WIDTH)
    y = y + d_skip.astype(f32) * u32
    y = jax.nn.gelu(y)
    y = y * jax.nn.sigmoid(y @ w_glu.astype(f32))
    return y.astype(u.dtype)


def pool_mixer(u, w_pool, scale):
    bsz, seq, _ = u.shape
    u32 = u.astype(jnp.float32)
    cs = lax.cumsum(u32, axis=1)
    count = jnp.arange(1, seq + 1, dtype=jnp.float32)[:, None]
    outs = []
    for gi, win in enumerate(POOL_WINDOWS):
        sl = slice(gi * POOL_GROUP_CH, (gi + 1) * POOL_GROUP_CH)
        cg = cs[..., sl]
        prev = jnp.pad(cg, ((0, 0), (win, 0), (0, 0)))[:, :seq]
        mean = (cg - prev) / jnp.minimum(count, float(win))
        outs.append((mean - u32[..., sl]) @ w_pool[gi].astype(jnp.float32))
    y = jnp.concatenate(outs, axis=-1) * scale.astype(jnp.float32)
    return y.astype(u.dtype)


def _fwd_setup_inputs(seed: int = 0) -> dict:
    key = jax.random.key(seed)
    ks = jax.random.split(key, 32)
    f32 = jnp.float32
    nrm = lambda k, shape, s: jax.random.normal(k, shape, f32) * s
    n_idx = jnp.arange(SSM_STATE, dtype=f32)
    lam_re = -0.5 + nrm(ks[0], (DEPTH, SSM_GROUPS, SSM_STATE), 0.01)
    lam_im = math.pi * n_idx[None, None, :] + nrm(ks[1], (DEPTH, SSM_GROUPS, SSM_STATE), 0.01)
    log_dt = jax.random.uniform(ks[2], (DEPTH, SSM_GROUPS), f32, math.log(DT_MIN), math.log(DT_MAX))
    return {
        "x": nrm(ks[3], (BATCH, SEQ, D_MODEL), 1.0),
        "p": nrm(ks[4], (DEPTH, BATCH, SEQ, PLE_DIM), 1.0),
        "ffn1_norm": 1.0 + nrm(ks[5], (DEPTH, D_MODEL), 0.02),
        "ffn1_wi": nrm(ks[6], (DEPTH, D_MODEL, 2 * D_FF), D_MODEL ** -0.5),
        "ffn1_wo": nrm(ks[7], (DEPTH, D_FF, D_MODEL), D_FF ** -0.5),
        "mix_norm": 1.0 + nrm(ks[8], (DEPTH, D_MODEL), 0.02),
        "w_in": nrm(ks[9], (DEPTH, D_MODEL, MIX_WIDTH), D_MODEL ** -0.5),
        "ssm_lambda_re": lam_re,
        "ssm_lambda_im": lam_im,
        "ssm_log_dt": log_dt,
        "ssm_b_re": nrm(ks[10], (DEPTH, SSM_GROUPS, SSM_STATE, SSM_GROUP_CH), (2.0 * SSM_GROUP_CH) ** -0.5),
        "ssm_b_im": nrm(ks[11], (DEPTH, SSM_GROUPS, SSM_STATE, SSM_GROUP_CH), (2.0 * SSM_GROUP_CH) ** -0.5),
        "ssm_c_re": nrm(ks[12], (DEPTH, SSM_GROUPS, SSM_GROUP_CH, SSM_STATE), (2.0 * SSM_STATE) ** -0.5),
        "ssm_c_im": nrm(ks[13], (DEPTH, SSM_GROUPS, SSM_GROUP_CH, SSM_STATE), (2.0 * SSM_STATE) ** -0.5),
        "ssm_d": nrm(ks[14], (DEPTH, SSM_WIDTH), 1.0),
        "ssm_w_glu": nrm(ks[15], (DEPTH, SSM_WIDTH, SSM_WIDTH), SSM_WIDTH ** -0.5),
        "pool_w": nrm(ks[16], (DEPTH, len(POOL_WINDOWS), POOL_GROUP_CH, POOL_GROUP_CH), POOL_GROUP_CH ** -0.5),
        "pool_scale": 1.0 + nrm(ks[17], (DEPTH, POOL_WIDTH), 0.02),
        "w_out": nrm(ks[18], (DEPTH, MIX_WIDTH, D_MODEL), MIX_WIDTH ** -0.5),
        "ffn2_norm": 1.0 + nrm(ks[19], (DEPTH, D_MODEL), 0.02),
        "ffn2_wi": nrm(ks[20], (DEPTH, D_MODEL, 2 * D_FF), D_MODEL ** -0.5),
        "ffn2_wo": nrm(ks[21], (DEPTH, D_FF, D_MODEL), D_FF ** -0.5),
        "ple_norm": 1.0 + nrm(ks[22], (DEPTH, D_MODEL), 0.02),
        "ple_w_gate": nrm(ks[23], (DEPTH, D_MODEL, D_MODEL), D_MODEL ** -0.5),
        "ple_w_proj": nrm(ks[24], (DEPTH, PLE_DIM, D_MODEL), PLE_DIM ** -0.5),
        "final_norm": 1.0 + nrm(ks[25], (D_MODEL,), 0.02),
    }


def _fwd_reference(x, p, ffn1_norm, ffn1_wi, ffn1_wo, mix_norm, w_in,
              ssm_lambda_re, ssm_lambda_im, ssm_log_dt, ssm_b_re, ssm_b_im, ssm_c_re, ssm_c_im,
              ssm_d, ssm_w_glu, pool_w, pool_scale, w_out,
              ffn2_norm, ffn2_wi, ffn2_wo, ple_norm, ple_w_gate, ple_w_proj, final_norm):
    h = x
    for i in range(DEPTH):
        h = h + 0.5 * swiglu(rms_norm(h, ffn1_norm[i]), ffn1_wi[i], ffn1_wo[i])
        z = rms_norm(h, mix_norm[i]) @ w_in[i]
        y_ssm = s5_mixer(z[..., :SSM_WIDTH], ssm_lambda_re[i], ssm_lambda_im[i], ssm_log_dt[i],
                         ssm_b_re[i], ssm_b_im[i], ssm_c_re[i], ssm_c_im[i], ssm_d[i], ssm_w_glu[i])
        y_pool = pool_mixer(z[..., SSM_WIDTH:], pool_w[i], pool_scale[i])
        h = h + jnp.concatenate([y_ssm, y_pool], axis=-1) @ w_out[i]
        h = h + 0.5 * swiglu(rms_norm(h, ffn2_norm[i]), ffn2_wi[i], ffn2_wo[i])
        gate = jax.nn.sigmoid((rms_norm(h, ple_norm[i]) @ ple_w_gate[i]).astype(jnp.float32))
        h = h + (gate * (p[i] @ ple_w_proj[i]).astype(jnp.float32)).astype(h.dtype)
    return rms_norm(h, final_norm)


import jax as _jax
import jax.numpy as _jnp

TWIN_FORMAT = 'train_step'
FWD_PARAMS = ['x', 'p', 'ffn1_norm', 'ffn1_wi', 'ffn1_wo', 'mix_norm', 'w_in', 'ssm_lambda_re', 'ssm_lambda_im', 'ssm_log_dt', 'ssm_b_re', 'ssm_b_im', 'ssm_c_re', 'ssm_c_im', 'ssm_d', 'ssm_w_glu', 'pool_w', 'pool_scale', 'w_out', 'ffn2_norm', 'ffn2_wi', 'ffn2_wo', 'ple_norm', 'ple_w_gate', 'ple_w_proj', 'final_norm']
TWIN_WEIGHTS = ['ffn1_norm', 'ffn1_wi', 'ffn1_wo', 'mix_norm', 'w_in', 'ssm_lambda_re', 'ssm_lambda_im', 'ssm_log_dt', 'ssm_b_re', 'ssm_b_im', 'ssm_c_re', 'ssm_c_im', 'ssm_d', 'ssm_w_glu', 'pool_w', 'pool_scale', 'w_out', 'ffn2_norm', 'ffn2_wi', 'ffn2_wo', 'ple_norm', 'ple_w_gate', 'ple_w_proj', 'final_norm']
TWIN_DIFF_INPUT = 'x'
TWIN_INPUTS = ['x', 'p', 'ffn1_norm', 'ffn1_wi', 'ffn1_wo', 'mix_norm', 'w_in', 'ssm_lambda_re', 'ssm_lambda_im', 'ssm_log_dt', 'ssm_b_re', 'ssm_b_im', 'ssm_c_re', 'ssm_c_im', 'ssm_d', 'ssm_w_glu', 'pool_w', 'pool_scale', 'w_out', 'ffn2_norm', 'ffn2_wi', 'ffn2_wo', 'ple_norm', 'ple_w_gate', 'ple_w_proj', 'final_norm', 'loss_target', 'm_ffn1_norm', 'm_ffn1_wi', 'm_ffn1_wo', 'm_mix_norm', 'm_w_in', 'm_ssm_lambda_re', 'm_ssm_lambda_im', 'm_ssm_log_dt', 'm_ssm_b_re', 'm_ssm_b_im', 'm_ssm_c_re', 'm_ssm_c_im', 'm_ssm_d', 'm_ssm_w_glu', 'm_pool_w', 'm_pool_scale', 'm_w_out', 'm_ffn2_norm', 'm_ffn2_wi', 'm_ffn2_wo', 'm_ple_norm', 'm_ple_w_gate', 'm_ple_w_proj', 'm_final_norm', 'v_ffn1_norm', 'v_ffn1_wi', 'v_ffn1_wo', 'v_mix_norm', 'v_w_in', 'v_ssm_lambda_re', 'v_ssm_lambda_im', 'v_ssm_log_dt', 'v_ssm_b_re', 'v_ssm_b_im', 'v_ssm_c_re', 'v_ssm_c_im', 'v_ssm_d', 'v_ssm_w_glu', 'v_pool_w', 'v_pool_scale', 'v_w_out', 'v_ffn2_norm', 'v_ffn2_wi', 'v_ffn2_wo', 'v_ple_norm', 'v_ple_w_gate', 'v_ple_w_proj', 'v_final_norm']
TWIN_OUTPUTS = ['loss', 'grad_x', 'grad_ffn1_norm', 'grad_ffn1_wi', 'grad_ffn1_wo', 'grad_mix_norm', 'grad_w_in', 'grad_ssm_lambda_re', 'grad_ssm_lambda_im', 'grad_ssm_log_dt', 'grad_ssm_b_re', 'grad_ssm_b_im', 'grad_ssm_c_re', 'grad_ssm_c_im', 'grad_ssm_d', 'grad_ssm_w_glu', 'grad_pool_w', 'grad_pool_scale', 'grad_w_out', 'grad_ffn2_norm', 'grad_ffn2_wi', 'grad_ffn2_wo', 'grad_ple_norm', 'grad_ple_w_gate', 'grad_ple_w_proj', 'grad_final_norm', 'delta_ffn1_norm', 'delta_ffn1_wi', 'delta_ffn1_wo', 'delta_mix_norm', 'delta_w_in', 'delta_ssm_lambda_re', 'delta_ssm_lambda_im', 'delta_ssm_log_dt', 'delta_ssm_b_re', 'delta_ssm_b_im', 'delta_ssm_c_re', 'delta_ssm_c_im', 'delta_ssm_d', 'delta_ssm_w_glu', 'delta_pool_w', 'delta_pool_scale', 'delta_w_out', 'delta_ffn2_norm', 'delta_ffn2_wi', 'delta_ffn2_wo', 'delta_ple_norm', 'delta_ple_w_gate', 'delta_ple_w_proj', 'delta_final_norm', 'new_m_ffn1_norm', 'new_m_ffn1_wi', 'new_m_ffn1_wo', 'new_m_mix_norm', 'new_m_w_in', 'new_m_ssm_lambda_re', 'new_m_ssm_lambda_im', 'new_m_ssm_log_dt', 'new_m_ssm_b_re', 'new_m_ssm_b_im', 'new_m_ssm_c_re', 'new_m_ssm_c_im', 'new_m_ssm_d', 'new_m_ssm_w_glu', 'new_m_pool_w', 'new_m_pool_scale', 'new_m_w_out', 'new_m_ffn2_norm', 'new_m_ffn2_wi', 'new_m_ffn2_wo', 'new_m_ple_norm', 'new_m_ple_w_gate', 'new_m_ple_w_proj', 'new_m_final_norm', 'new_v_ffn1_norm', 'new_v_ffn1_wi', 'new_v_ffn1_wo', 'new_v_mix_norm', 'new_v_w_in', 'new_v_ssm_lambda_re', 'new_v_ssm_lambda_im', 'new_v_ssm_log_dt', 'new_v_ssm_b_re', 'new_v_ssm_b_im', 'new_v_ssm_c_re', 'new_v_ssm_c_im', 'new_v_ssm_d', 'new_v_ssm_w_glu', 'new_v_pool_w', 'new_v_pool_scale', 'new_v_w_out', 'new_v_ffn2_norm', 'new_v_ffn2_wi', 'new_v_ffn2_wo', 'new_v_ple_norm', 'new_v_ple_w_gate', 'new_v_ple_w_proj', 'new_v_final_norm']
TWIN_LEAF_KINDS = {'loss': 'loss', 'grad_x': 'grad_x', 'grad_ffn1_norm': 'grad_w', 'grad_ffn1_wi': 'grad_w', 'grad_ffn1_wo': 'grad_w', 'grad_mix_norm': 'grad_w', 'grad_w_in': 'grad_w', 'grad_ssm_lambda_re': 'grad_w', 'grad_ssm_lambda_im': 'grad_w', 'grad_ssm_log_dt': 'grad_w', 'grad_ssm_b_re': 'grad_w', 'grad_ssm_b_im': 'grad_w', 'grad_ssm_c_re': 'grad_w', 'grad_ssm_c_im': 'grad_w', 'grad_ssm_d': 'grad_w', 'grad_ssm_w_glu': 'grad_w', 'grad_pool_w': 'grad_w', 'grad_pool_scale': 'grad_w', 'grad_w_out': 'grad_w', 'grad_ffn2_norm': 'grad_w', 'grad_ffn2_wi': 'grad_w', 'grad_ffn2_wo': 'grad_w', 'grad_ple_norm': 'grad_w', 'grad_ple_w_gate': 'grad_w', 'grad_ple_w_proj': 'grad_w', 'grad_final_norm': 'grad_w', 'delta_ffn1_norm': 'delta_w', 'delta_ffn1_wi': 'delta_w', 'delta_ffn1_wo': 'delta_w', 'delta_mix_norm': 'delta_w', 'delta_w_in': 'delta_w', 'delta_ssm_lambda_re': 'delta_w', 'delta_ssm_lambda_im': 'delta_w', 'delta_ssm_log_dt': 'delta_w', 'delta_ssm_b_re': 'delta_w', 'delta_ssm_b_im': 'delta_w', 'delta_ssm_c_re': 'delta_w', 'delta_ssm_c_im': 'delta_w', 'delta_ssm_d': 'delta_w', 'delta_ssm_w_glu': 'delta_w', 'delta_pool_w': 'delta_w', 'delta_pool_scale': 'delta_w', 'delta_w_out': 'delta_w', 'delta_ffn2_norm': 'delta_w', 'delta_ffn2_wi': 'delta_w', 'delta_ffn2_wo': 'delta_w', 'delta_ple_norm': 'delta_w', 'delta_ple_w_gate': 'delta_w', 'delta_ple_w_proj': 'delta_w', 'delta_final_norm': 'delta_w', 'new_m_ffn1_norm': 'new_m', 'new_m_ffn1_wi': 'new_m', 'new_m_ffn1_wo': 'new_m', 'new_m_mix_norm': 'new_m', 'new_m_w_in': 'new_m', 'new_m_ssm_lambda_re': 'new_m', 'new_m_ssm_lambda_im': 'new_m', 'new_m_ssm_log_dt': 'new_m', 'new_m_ssm_b_re': 'new_m', 'new_m_ssm_b_im': 'new_m', 'new_m_ssm_c_re': 'new_m', 'new_m_ssm_c_im': 'new_m', 'new_m_ssm_d': 'new_m', 'new_m_ssm_w_glu': 'new_m', 'new_m_pool_w': 'new_m', 'new_m_pool_scale': 'new_m', 'new_m_w_out': 'new_m', 'new_m_ffn2_norm': 'new_m', 'new_m_ffn2_wi': 'new_m', 'new_m_ffn2_wo': 'new_m', 'new_m_ple_norm': 'new_m', 'new_m_ple_w_gate': 'new_m', 'new_m_ple_w_proj': 'new_m', 'new_m_final_norm': 'new_m', 'new_v_ffn1_norm': 'new_v', 'new_v_ffn1_wi': 'new_v', 'new_v_ffn1_wo': 'new_v', 'new_v_mix_norm': 'new_v', 'new_v_w_in': 'new_v', 'new_v_ssm_lambda_re': 'new_v', 'new_v_ssm_lambda_im': 'new_v', 'new_v_ssm_log_dt': 'new_v', 'new_v_ssm_b_re': 'new_v', 'new_v_ssm_b_im': 'new_v', 'new_v_ssm_c_re': 'new_v', 'new_v_ssm_c_im': 'new_v', 'new_v_ssm_d': 'new_v', 'new_v_ssm_w_glu': 'new_v', 'new_v_pool_w': 'new_v', 'new_v_pool_scale': 'new_v', 'new_v_w_out': 'new_v', 'new_v_ffn2_norm': 'new_v', 'new_v_ffn2_wi': 'new_v', 'new_v_ffn2_wo': 'new_v', 'new_v_ple_norm': 'new_v', 'new_v_ple_w_gate': 'new_v', 'new_v_ple_w_proj': 'new_v', 'new_v_final_norm': 'new_v'}


def _forward(args):
    return _fwd_reference(*[args[k] for k in FWD_PARAMS])


def _output_shape():
    out = _jax.eval_shape(lambda: _forward(_fwd_setup_inputs(0)))
    return out.shape, out.dtype

N_MICROBATCH = 1
ADAM_LR = 0.001
ADAM_B1 = 0.9
ADAM_B2 = 0.999
ADAM_EPS = 1e-08
ADAM_WD = 0.01
ADAM_STEP = 10
PER_EXAMPLE_BATCH_AXIS = {'x': 0, 'p': 1, 'loss_target': 0}
SHARED_INPUTS = []
_WEIGHT_DTYPES = {'ffn1_norm': _jnp.float32, 'ffn1_wi': _jnp.float32, 'ffn1_wo': _jnp.float32, 'mix_norm': _jnp.float32, 'w_in': _jnp.float32, 'ssm_lambda_re': _jnp.float32, 'ssm_lambda_im': _jnp.float32, 'ssm_log_dt': _jnp.float32, 'ssm_b_re': _jnp.float32, 'ssm_b_im': _jnp.float32, 'ssm_c_re': _jnp.float32, 'ssm_c_im': _jnp.float32, 'ssm_d': _jnp.float32, 'ssm_w_glu': _jnp.float32, 'pool_w': _jnp.float32, 'pool_scale': _jnp.float32, 'w_out': _jnp.float32, 'ffn2_norm': _jnp.float32, 'ffn2_wi': _jnp.float32, 'ffn2_wo': _jnp.float32, 'ple_norm': _jnp.float32, 'ple_w_gate': _jnp.float32, 'ple_w_proj': _jnp.float32, 'final_norm': _jnp.float32}
MOMENT_SCALE = {'ffn1_norm': 7.413294e-02, 'ffn1_wi': 3.150597e-02, 'ffn1_wo': 5.136713e-02, 'mix_norm': 1.138652e-01, 'w_in': 1.049229e-01, 'ssm_lambda_re': 3.587508e-03, 'ssm_lambda_im': 3.269178e-03, 'ssm_log_dt': 1.805057e+00, 'ssm_b_re': 1.973497e-03, 'ssm_b_im': 1.988556e-03, 'ssm_c_re': 4.099619e-03, 'ssm_c_im': 4.012863e-03, 'ssm_d': 5.628845e-02, 'ssm_w_glu': 1.572980e-02, 'pool_w': 1.357421e-01, 'pool_scale': 1.488632e-01, 'w_out': 1.035145e-01, 'ffn2_norm': 6.335618e-02, 'ffn2_wi': 2.681457e-02, 'ffn2_wo': 4.374033e-02, 'ple_norm': 3.106187e-02, 'ple_w_gate': 3.070662e-02, 'ple_w_proj': 7.850828e-02, 'final_norm': 6.383596e+01}


def _to_microbatches(a, axis):
    t = _jnp.moveaxis(a, axis, 0)
    t = t.reshape((N_MICROBATCH, t.shape[0] // N_MICROBATCH) + t.shape[1:])
    return _jnp.moveaxis(t, 1, axis + 1)


def setup_inputs(seed: int = 0) -> dict:
    inp = _fwd_setup_inputs(seed)
    key = _jax.random.fold_in(_jax.random.key(seed), 7919)
    shape, _ = _output_shape()
    out = dict(inp)
    out["loss_target"] = _jax.random.normal(_jax.random.fold_in(key, 0), shape, _jnp.float32)
    for i, name in enumerate(TWIN_WEIGHTS):
        w = inp[name].astype(_jnp.float32)
        if MOMENT_SCALE is None:
            s = _jnp.sqrt(_jnp.mean(_jnp.square(w)) + 1e-30)
        else:
            s = MOMENT_SCALE[name]
        km, kv = _jax.random.split(_jax.random.fold_in(key, i + 1))
        out[name] = w
        out["m_" + name] = s * _jax.random.normal(km, w.shape, _jnp.float32)
        out["v_" + name] = (s * s) * _jax.random.uniform(kv, w.shape, _jnp.float32, 0.5, 1.5)
    if N_MICROBATCH > 1:
        for name, axis in PER_EXAMPLE_BATCH_AXIS.items():
            out[name] = _to_microbatches(out[name], axis)
    return {'x': out['x'], 'p': out['p'], 'ffn1_norm': out['ffn1_norm'], 'ffn1_wi': out['ffn1_wi'], 'ffn1_wo': out['ffn1_wo'], 'mix_norm': out['mix_norm'], 'w_in': out['w_in'], 'ssm_lambda_re': out['ssm_lambda_re'], 'ssm_lambda_im': out['ssm_lambda_im'], 'ssm_log_dt': out['ssm_log_dt'], 'ssm_b_re': out['ssm_b_re'], 'ssm_b_im': out['ssm_b_im'], 'ssm_c_re': out['ssm_c_re'], 'ssm_c_im': out['ssm_c_im'], 'ssm_d': out['ssm_d'], 'ssm_w_glu': out['ssm_w_glu'], 'pool_w': out['pool_w'], 'pool_scale': out['pool_scale'], 'w_out': out['w_out'], 'ffn2_norm': out['ffn2_norm'], 'ffn2_wi': out['ffn2_wi'], 'ffn2_wo': out['ffn2_wo'], 'ple_norm': out['ple_norm'], 'ple_w_gate': out['ple_w_gate'], 'ple_w_proj': out['ple_w_proj'], 'final_norm': out['final_norm'], 'loss_target': out['loss_target'], 'm_ffn1_norm': out['m_ffn1_norm'], 'm_ffn1_wi': out['m_ffn1_wi'], 'm_ffn1_wo': out['m_ffn1_wo'], 'm_mix_norm': out['m_mix_norm'], 'm_w_in': out['m_w_in'], 'm_ssm_lambda_re': out['m_ssm_lambda_re'], 'm_ssm_lambda_im': out['m_ssm_lambda_im'], 'm_ssm_log_dt': out['m_ssm_log_dt'], 'm_ssm_b_re': out['m_ssm_b_re'], 'm_ssm_b_im': out['m_ssm_b_im'], 'm_ssm_c_re': out['m_ssm_c_re'], 'm_ssm_c_im': out['m_ssm_c_im'], 'm_ssm_d': out['m_ssm_d'], 'm_ssm_w_glu': out['m_ssm_w_glu'], 'm_pool_w': out['m_pool_w'], 'm_pool_scale': out['m_pool_scale'], 'm_w_out': out['m_w_out'], 'm_ffn2_norm': out['m_ffn2_norm'], 'm_ffn2_wi': out['m_ffn2_wi'], 'm_ffn2_wo': out['m_ffn2_wo'], 'm_ple_norm': out['m_ple_norm'], 'm_ple_w_gate': out['m_ple_w_gate'], 'm_ple_w_proj': out['m_ple_w_proj'], 'm_final_norm': out['m_final_norm'], 'v_ffn1_norm': out['v_ffn1_norm'], 'v_ffn1_wi': out['v_ffn1_wi'], 'v_ffn1_wo': out['v_ffn1_wo'], 'v_mix_norm': out['v_mix_norm'], 'v_w_in': out['v_w_in'], 'v_ssm_lambda_re': out['v_ssm_lambda_re'], 'v_ssm_lambda_im': out['v_ssm_lambda_im'], 'v_ssm_log_dt': out['v_ssm_log_dt'], 'v_ssm_b_re': out['v_ssm_b_re'], 'v_ssm_b_im': out['v_ssm_b_im'], 'v_ssm_c_re': out['v_ssm_c_re'], 'v_ssm_c_im': out['v_ssm_c_im'], 'v_ssm_d': out['v_ssm_d'], 'v_ssm_w_glu': out['v_ssm_w_glu'], 'v_pool_w': out['v_pool_w'], 'v_pool_scale': out['v_pool_scale'], 'v_w_out': out['v_w_out'], 'v_ffn2_norm': out['v_ffn2_norm'], 'v_ffn2_wi': out['v_ffn2_wi'], 'v_ffn2_wo': out['v_ffn2_wo'], 'v_ple_norm': out['v_ple_norm'], 'v_ple_w_gate': out['v_ple_w_gate'], 'v_ple_w_proj': out['v_ple_w_proj'], 'v_final_norm': out['v_final_norm']}


def _loss(weights, diff, rest, loss_target):
    with _jax.named_scope("forward"):
        args = {**rest, TWIN_DIFF_INPUT: diff, **{k: w.astype(_WEIGHT_DTYPES[k]) for k, w in weights.items()}}
        y = _forward(args)
    with _jax.named_scope("loss_head"):
        err = _jnp.square(y.astype(_jnp.float32) - loss_target)
        return 0.5 * _jnp.sum(_jnp.mean(err, axis=-1)) if err.ndim else 0.5 * err


def _adamw(w, g, m, v):
    m = ADAM_B1 * m + (1.0 - ADAM_B1) * g
    v = ADAM_B2 * v + (1.0 - ADAM_B2) * _jnp.square(g)
    m_hat = m / (1.0 - ADAM_B1 ** ADAM_STEP)
    v_hat = v / (1.0 - ADAM_B2 ** ADAM_STEP)
    delta = -ADAM_LR * (m_hat / (_jnp.sqrt(v_hat) + ADAM_EPS) + ADAM_WD * w)
    return delta, m, v


def reference(x, p, ffn1_norm, ffn1_wi, ffn1_wo, mix_norm, w_in, ssm_lambda_re, ssm_lambda_im, ssm_log_dt, ssm_b_re, ssm_b_im, ssm_c_re, ssm_c_im, ssm_d, ssm_w_glu, pool_w, pool_scale, w_out, ffn2_norm, ffn2_wi, ffn2_wo, ple_norm, ple_w_gate, ple_w_proj, final_norm, loss_target, m_ffn1_norm, m_ffn1_wi, m_ffn1_wo, m_mix_norm, m_w_in, m_ssm_lambda_re, m_ssm_lambda_im, m_ssm_log_dt, m_ssm_b_re, m_ssm_b_im, m_ssm_c_re, m_ssm_c_im, m_ssm_d, m_ssm_w_glu, m_pool_w, m_pool_scale, m_w_out, m_ffn2_norm, m_ffn2_wi, m_ffn2_wo, m_ple_norm, m_ple_w_gate, m_ple_w_proj, m_final_norm, v_ffn1_norm, v_ffn1_wi, v_ffn1_wo, v_mix_norm, v_w_in, v_ssm_lambda_re, v_ssm_lambda_im, v_ssm_log_dt, v_ssm_b_re, v_ssm_b_im, v_ssm_c_re, v_ssm_c_im, v_ssm_d, v_ssm_w_glu, v_pool_w, v_pool_scale, v_w_out, v_ffn2_norm, v_ffn2_wi, v_ffn2_wo, v_ple_norm, v_ple_w_gate, v_ple_w_proj, v_final_norm):
    given = dict(x=x, p=p, ffn1_norm=ffn1_norm, ffn1_wi=ffn1_wi, ffn1_wo=ffn1_wo, mix_norm=mix_norm, w_in=w_in, ssm_lambda_re=ssm_lambda_re, ssm_lambda_im=ssm_lambda_im, ssm_log_dt=ssm_log_dt, ssm_b_re=ssm_b_re, ssm_b_im=ssm_b_im, ssm_c_re=ssm_c_re, ssm_c_im=ssm_c_im, ssm_d=ssm_d, ssm_w_glu=ssm_w_glu, pool_w=pool_w, pool_scale=pool_scale, w_out=w_out, ffn2_norm=ffn2_norm, ffn2_wi=ffn2_wi, ffn2_wo=ffn2_wo, ple_norm=ple_norm, ple_w_gate=ple_w_gate, ple_w_proj=ple_w_proj, final_norm=final_norm, loss_target=loss_target, m_ffn1_norm=m_ffn1_norm, m_ffn1_wi=m_ffn1_wi, m_ffn1_wo=m_ffn1_wo, m_mix_norm=m_mix_norm, m_w_in=m_w_in, m_ssm_lambda_re=m_ssm_lambda_re, m_ssm_lambda_im=m_ssm_lambda_im, m_ssm_log_dt=m_ssm_log_dt, m_ssm_b_re=m_ssm_b_re, m_ssm_b_im=m_ssm_b_im, m_ssm_c_re=m_ssm_c_re, m_ssm_c_im=m_ssm_c_im, m_ssm_d=m_ssm_d, m_ssm_w_glu=m_ssm_w_glu, m_pool_w=m_pool_w, m_pool_scale=m_pool_scale, m_w_out=m_w_out, m_ffn2_norm=m_ffn2_norm, m_ffn2_wi=m_ffn2_wi, m_ffn2_wo=m_ffn2_wo, m_ple_norm=m_ple_norm, m_ple_w_gate=m_ple_w_gate, m_ple_w_proj=m_ple_w_proj, m_final_norm=m_final_norm, v_ffn1_norm=v_ffn1_norm, v_ffn1_wi=v_ffn1_wi, v_ffn1_wo=v_ffn1_wo, v_mix_norm=v_mix_norm, v_w_in=v_w_in, v_ssm_lambda_re=v_ssm_lambda_re, v_ssm_lambda_im=v_ssm_lambda_im, v_ssm_log_dt=v_ssm_log_dt, v_ssm_b_re=v_ssm_b_re, v_ssm_b_im=v_ssm_b_im, v_ssm_c_re=v_ssm_c_re, v_ssm_c_im=v_ssm_c_im, v_ssm_d=v_ssm_d, v_ssm_w_glu=v_ssm_w_glu, v_pool_w=v_pool_w, v_pool_scale=v_pool_scale, v_w_out=v_w_out, v_ffn2_norm=v_ffn2_norm, v_ffn2_wi=v_ffn2_wi, v_ffn2_wo=v_ffn2_wo, v_ple_norm=v_ple_norm, v_ple_w_gate=v_ple_w_gate, v_ple_w_proj=v_ple_w_proj, v_final_norm=v_final_norm)
    weights = {n: given[n] for n in TWIN_WEIGHTS}
    shared = {n: given[n] for n in SHARED_INPUTS}
    per_example = {n: given[n] for n in ['x', 'p']}
    grad_fn = _jax.value_and_grad(_loss, argnums=(0, 1))

    def one_microbatch(ex, loss_target):
        ex = dict(ex)
        diff = ex.pop(TWIN_DIFF_INPUT)
        return grad_fn(weights, diff, {**shared, **ex}, loss_target)

    if N_MICROBATCH == 1:
        loss, (grad_w, grad_x) = one_microbatch(per_example, given["loss_target"])
    else:
        def body(carry, xs):
            loss_sum, grad_sum = carry
            l_k, (gw_k, gx_k) = one_microbatch(xs[0], xs[1])
            with _jax.named_scope("update"):
                return (loss_sum + l_k, _jax.tree.map(_jnp.add, grad_sum, gw_k)), gx_k

        init = (_jnp.zeros((), _jnp.float32), _jax.tree.map(_jnp.zeros_like, weights))
        (loss, grad_w), grad_x = _jax.lax.scan(body, init, (per_example, given["loss_target"]))
    with _jax.named_scope("update"):
        delta_w, new_m, new_v = {}, {}, {}
        for n in TWIN_WEIGHTS:
            delta_w[n], new_m[n], new_v[n] = _adamw(weights[n], grad_w[n], given["m_" + n], given["v_" + n])
    return (loss, grad_x, *[grad_w[n] for n in TWIN_WEIGHTS], *[delta_w[n] for n in TWIN_WEIGHTS],
            *[new_m[n] for n in TWIN_WEIGHTS], *[new_v[n] for n in TWIN_WEIGHTS])
```

```python
import functools
import math

import jax
import jax.numpy as jnp
from jax import lax
from jax.experimental import pallas as pl
from jax.experimental.pallas import tpu as pltpu

F32 = jnp.float32
BF16 = jnp.bfloat16
MXU_DTYPE = jnp.bfloat16
VMEM_LIMIT = 56 * 1024 * 1024
LANE = 128
SUBLANE = 8

N_CHIPS = 4
SSM_GROUPS = 32
SSM_STATE = 64
SSM_CH = 16
SSM_WIDTH = SSM_GROUPS * SSM_CH
SSM_COLS = SSM_GROUPS * SSM_STATE
POOL_GROUPS = 4
POOL_CH = 128
POOL_WIDTH = POOL_GROUPS * POOL_CH
SSM_TILE = 256
NORM_EPS = 1e-6
ADAM_LR = 0.001
ADAM_B1 = 0.9
ADAM_B2 = 0.999
ADAM_EPS = 1e-08
ADAM_WD = 0.01
ADAM_STEP = 10
MESH_ID = pl.DeviceIdType.MESH

BIG = ("ffn1_wi", "ffn1_wo", "w_in", "ssm_w_glu", "w_out", "ffn2_wi", "ffn2_wo", "ple_w_gate", "ple_w_proj")
COL_SHARDED = ("ffn1_wi", "ffn2_wi", "ple_w_proj")
SMALL = ("ffn1_norm", "mix_norm", "ssm_lambda_re", "ssm_lambda_im", "ssm_log_dt", "ssm_b_re", "ssm_b_im",
         "ssm_c_re", "ssm_c_im", "ssm_d", "pool_w", "pool_scale", "ffn2_norm", "ple_norm", "final_norm")
WEIGHTS = ("ffn1_norm", "ffn1_wi", "ffn1_wo", "mix_norm", "w_in", "ssm_lambda_re", "ssm_lambda_im", "ssm_log_dt",
           "ssm_b_re", "ssm_b_im", "ssm_c_re", "ssm_c_im", "ssm_d", "ssm_w_glu", "pool_w", "pool_scale", "w_out",
           "ffn2_norm", "ffn2_wi", "ffn2_wo", "ple_norm", "ple_w_gate", "ple_w_proj", "final_norm")


def _tile(dim, target):
    best = None
    t = LANE
    while t <= min(dim, target):
        if dim % t == 0:
            best = t
        t += LANE
    return best if best is not None else dim


def _params(sem):
    return pltpu.CompilerParams(dimension_semantics=sem, vmem_limit_bytes=VMEM_LIMIT)


def _mm(name, a, b, out_shape, out_dtype, grid, a_spec, b_spec, o_spec, contract, alpha=1.0, res=None):
    n_k = grid[2]
    acc_shape = tuple(d for d in o_spec.block_shape if d is not None)

    def body(*refs):
        if res is None:
            a_ref, b_ref, o_ref, acc = refs
            r_ref = None
        else:
            a_ref, b_ref, r_ref, o_ref, acc = refs
        k = pl.program_id(2)

        @pl.when(k == 0)
        def _():
            acc[...] = jnp.zeros_like(acc)

        acc[...] += lax.dot_general(a_ref[...].astype(MXU_DTYPE), b_ref[...].astype(MXU_DTYPE),
                                    (contract, ((), ())), preferred_element_type=F32)

        @pl.when(k == n_k - 1)
        def _():
            v = acc[...]
            if alpha != 1.0:
                v = v * alpha
            if r_ref is not None:
                v = v + r_ref[...].astype(F32)
            o_ref[...] = v.astype(out_dtype)

    in_specs = [a_spec, b_spec]
    operands = [a, b]
    if res is not None:
        in_specs.append(o_spec)
        operands.append(res)
    return pl.pallas_call(
        body, name=name, grid=grid, in_specs=in_specs, out_specs=o_spec,
        out_shape=jax.ShapeDtypeStruct(out_shape, out_dtype),
        scratch_shapes=[pltpu.VMEM(acc_shape, F32)],
        compiler_params=_params(("parallel", "parallel", "arbitrary")),
    )(*operands)


NN = ((1,), (0,))
NT = ((1,), (1,))
TN = ((0,), (0,))


def mm_nn(name, a, b, out_dtype, alpha=1.0, res=None, tm=1024, tn=512, tk=512):
    m, k = a.shape
    n = b.shape[1]
    tm, tn, tk = _tile(m, tm), _tile(n, tn), _tile(k, tk)
    return _mm(name, a, b, (m, n), out_dtype, (m // tm, n // tn, k // tk),
               pl.BlockSpec((tm, tk), lambda i, j, kk: (i, kk)),
               pl.BlockSpec((tk, tn), lambda i, j, kk: (kk, j)),
               pl.BlockSpec((tm, tn), lambda i, j, kk: (i, j)), NN, alpha, res)


def mm_nt(name, a, b, out_dtype, alpha=1.0, res=None, tm=1024, tn=512, tk=512):
    m, k = a.shape
    n = b.shape[0]
    tm, tn, tk = _tile(m, tm), _tile(n, tn), _tile(k, tk)
    return _mm(name, a, b, (m, n), out_dtype, (m // tm, n // tn, k // tk),
               pl.BlockSpec((tm, tk), lambda i, j, kk: (i, kk)),
               pl.BlockSpec((tn, tk), lambda i, j, kk: (j, kk)),
               pl.BlockSpec((tm, tn), lambda i, j, kk: (i, j)), NT, alpha, res)


def mm_tn(name, a, b, out_dtype, alpha=1.0, tm=512, tn=512, tk=1024):
    k, m = a.shape
    n = b.shape[1]
    tm, tn, tk = _tile(m, tm), _tile(n, tn), _tile(k, tk)
    return _mm(name, a, b, (m, n), out_dtype, (m // tm, n // tn, k // tk),
               pl.BlockSpec((tk, tm), lambda i, j, kk: (kk, i)),
               pl.BlockSpec((tk, tn), lambda i, j, kk: (kk, j)),
               pl.BlockSpec((tm, tn), lambda i, j, kk: (i, j)), TN, alpha)


def mm_nn_colsharded(name, a, w, out_dtype, tm=512, tk=512):
    m, k = a.shape
    c = w.shape[2]
    tm, tk = _tile(m, tm), _tile(k, tk)
    return _mm(name, a, w, (m, N_CHIPS * c), out_dtype, (m // tm, N_CHIPS, k // tk),
               pl.BlockSpec((tm, tk), lambda i, j, kk: (i, kk)),
               pl.BlockSpec((None, tk, c), lambda i, j, kk: (j, kk, 0)),
               pl.BlockSpec((tm, c), lambda i, j, kk: (i, j)), NN)


def mm_nt_colsharded(name, a, w, out_dtype, tm=512, tn=512):
    m = a.shape[0]
    k, c = w.shape[1], w.shape[2]
    tm, tn = _tile(m, tm), _tile(k, tn)
    return _mm(name, a, w, (m, k), out_dtype, (m // tm, k // tn, N_CHIPS),
               pl.BlockSpec((tm, c), lambda i, j, kk: (i, kk)),
               pl.BlockSpec((None, tn, c), lambda i, j, kk: (kk, j, 0)),
               pl.BlockSpec((tm, tn), lambda i, j, kk: (i, j)), NT)


def mm_tn_colsharded(name, a, b, out_dtype, tm=512, tk=1024):
    t, k = a.shape
    c = b.shape[1] // N_CHIPS
    tm, tk = _tile(k, tm), _tile(t, tk)
    return _mm(name, a, b, (N_CHIPS, k, c), out_dtype, (k // tm, N_CHIPS, t // tk),
               pl.BlockSpec((tk, tm), lambda i, j, kk: (kk, i)),
               pl.BlockSpec((tk, c), lambda i, j, kk: (kk, j)),
               pl.BlockSpec((None, tm, c), lambda i, j, kk: (j, i, 0)), TN)


def _rowwise(name, fn, n_rows, tm, row_ins, bcast_ins, outs, accs=()):
    tm = min(tm, n_rows)
    grid = (n_rows // tm,)
    n_row, n_b, n_out = len(row_ins), len(bcast_ins), len(outs)

    def body(*refs):
        ins = [r[...] for r in refs[:n_row + n_b]]
        out_refs = refs[n_row + n_b:n_row + n_b + n_out]
        acc_refs = refs[n_row + n_b + n_out:]
        res = fn(*ins)
        if not isinstance(res, (tuple, list)):
            res = (res,)
        for o_ref, v in zip(out_refs, res[:n_out]):
            o_ref[...] = v.astype(o_ref.dtype)
        if acc_refs:
            @pl.when(pl.program_id(0) == 0)
            def _():
                for a_ref in acc_refs:
                    a_ref[...] = jnp.zeros_like(a_ref)
            for a_ref, v in zip(acc_refs, res[n_out:]):
                a_ref[...] += v

    in_specs, operands = [], []
    for spec in row_ins:
        arr, width, cb = spec[0], spec[1], spec[2]
        rb = spec[3] if len(spec) > 3 else 0
        in_specs.append(pl.BlockSpec((tm, width), functools.partial(lambda i, cb, rb: (i + rb, cb), cb=cb, rb=rb)))
        operands.append(arr)
    for arr in bcast_ins:
        in_specs.append(pl.BlockSpec(arr.shape, functools.partial(lambda i, nd: (0,) * nd, nd=arr.ndim)))
        operands.append(arr)
    out_specs = [pl.BlockSpec((tm, w), lambda i: (i, 0)) for w, _ in outs]
    out_specs += [pl.BlockSpec((r, w), lambda i: (0, 0)) for r, w in accs]
    out_shape = [jax.ShapeDtypeStruct((n_rows, w), dt) for w, dt in outs]
    out_shape += [jax.ShapeDtypeStruct((r, w), F32) for r, w in accs]
    res = pl.pallas_call(
        body, name=name, grid=grid, in_specs=in_specs, out_specs=out_specs, out_shape=out_shape,
        compiler_params=_params(("arbitrary",) if accs else ("parallel",)),
    )(*operands)
    return res


def _rms(x, g):
    r = lax.rsqrt(jnp.mean(x * x, axis=-1, keepdims=True) + NORM_EPS)
    return x * r * g


def _rms_bwd(dy, x, g):
    r = lax.rsqrt(jnp.mean(x * x, axis=-1, keepdims=True) + NORM_EPS)
    xh = x * r
    dxh = dy * g
    dx = r * (dxh - xh * jnp.mean(dxh * xh, axis=-1, keepdims=True))
    return dx, jnp.sum(dy * xh, axis=0, keepdims=True)


def norm_fwd(name, h, g):
    n, d = h.shape
    return _rowwise(name, lambda x, gg: _rms(x, gg), n, 512, [(h, d, 0)], [g.reshape(1, d)], [(d, BF16)])[0]


def norm_bwd(name, dxn, h, g, d_res):
    n, d = h.shape

    def fn(dy, x, dr, gg):
        dx, dg = _rms_bwd(dy, x, gg)
        return dr + dx, dg

    return _rowwise(name, fn, n, 512, [(dxn, d, 0), (h, d, 0), (d_res, d, 0)], [g.reshape(1, d)], [(d, F32)], [(1, d)])


_GELU_C = math.sqrt(2.0 / math.pi)


def _gelu(x):
    return 0.5 * x * (1.0 + jnp.tanh(_GELU_C * (x + 0.044715 * (x * x * x))))


def _gelu_grad(x):
    th = jnp.tanh(_GELU_C * (x + 0.044715 * (x * x * x)))
    return 0.5 * (1.0 + th) + 0.5 * x * (1.0 - th * th) * (_GELU_C * (1.0 + 3.0 * 0.044715 * (x * x)))


def _ssm_discretize(lam_re, lam_im, log_dt, b_re, b_im):
    dt = jnp.exp(log_dt)
    e = jnp.exp(lam_re * dt)
    lb_re = e * jnp.cos(lam_im * dt)
    lb_im = e * jnp.sin(lam_im * dt)
    nr, ni = lb_re - 1.0, lb_im
    den = lam_re * lam_re + lam_im * lam_im
    cr = (nr * lam_re + ni * lam_im) / den
    ci = (ni * lam_re - nr * lam_im) / den
    return lb_re, lb_im, cr * b_re - ci * b_im, cr * b_im + ci * b_re


def ssm_prep(name, lam_re, lam_im, log_dt, b_re, b_im):
    def body(lr, li, ld, br, bi, pr_ref, pi_ref, bbr_ref, bbi_ref):
        lb_re, lb_im, bb_re, bb_im = _ssm_discretize(lr[...], li[...], ld[...], br[...], bi[...])
        bbr_ref[...] = bb_re
        bbi_ref[...] = bb_im
        pr, pi = lb_re, lb_im
        cols_r, cols_i = [pr], [pi]
        for _ in range(SUBLANE - 1):
            pr, pi = pr * lb_re - pi * lb_im, pr * lb_im + pi * lb_re
            cols_r.append(pr)
            cols_i.append(pi)
        lane = lax.broadcasted_iota(jnp.int32, (SSM_COLS, SUBLANE), 1)
        out_r = jnp.zeros((SSM_COLS, SUBLANE), F32)
        out_i = jnp.zeros((SSM_COLS, SUBLANE), F32)
        for r in range(SUBLANE):
            out_r = jnp.where(lane == r, cols_r[r], out_r)
            out_i = jnp.where(lane == r, cols_i[r], out_i)
        pr_ref[...] = out_r
        pi_ref[...] = out_i

    shapes = [jax.ShapeDtypeStruct((SSM_COLS, SUBLANE), F32)] * 2 + [jax.ShapeDtypeStruct((SSM_COLS, SSM_CH), F32)] * 2
    return pl.pallas_call(body, name=name, out_shape=shapes,
                          compiler_params=pltpu.CompilerParams(vmem_limit_bytes=VMEM_LIMIT))(lam_re, lam_im, log_dt, b_re, b_im)


def ssm_prep_bwd(name, lam_re, lam_im, log_dt, b_re, b_im, d_lb_re, d_lb_im, d_bb_re, d_bb_im):
    def body(lr, li, ld, br, bi, g0, g1, g2, g3, o0, o1, o2, o3, o4):
        _, vjp = jax.vjp(_ssm_discretize, lr[...], li[...], ld[...], br[...], bi[...])
        res = vjp((g0[...], g1[...], g2[...], g3[...]))
        for o, v in zip((o0, o1, o2, o3, o4), res):
            o[...] = v

    col = jax.ShapeDtypeStruct((SSM_COLS, 1), F32)
    mat = jax.ShapeDtypeStruct((SSM_COLS, SSM_CH), F32)
    return pl.pallas_call(body, name=name, out_shape=[col, col, col, mat, mat],
                          compiler_params=pltpu.CompilerParams(vmem_limit_bytes=VMEM_LIMIT))(
        lam_re, lam_im, log_dt, b_re, b_im, d_lb_re, d_lb_im, d_bb_re, d_bb_im)


def scan_coefficients(pw_re, pw_im, reverse):
    pr, pi = pw_re.T, pw_im.T
    if reverse:
        pi = -pi
    row = jnp.arange(SUBLANE)[:, None]
    out = []
    for d in (1, 2, 4):
        valid = (row < SUBLANE - d) if reverse else (row >= d)
        out.append(jnp.where(valid, pr[d - 1][None, :], 0.0))
        out.append(jnp.where(valid, pi[d - 1][None, :], 0.0))
    out.append(pr[::-1] if reverse else pr)
    out.append(pi[::-1] if reverse else pi)
    return jnp.stack(out)


def ssm_scan(name, coef, x, seq, reverse, states=None):
    n = x.shape[1]
    n_seq = n // seq
    cw = LANE
    n_cb = SSM_COLS // cw
    n_t = seq // SUBLANE
    with_dlam = states is not None

    def body(*refs):
        if with_dlam:
            coef_ref, x_ref, s_ref, o_ref, dl_ref = refs
        else:
            coef_ref, x_ref, o_ref = refs
        c = [coef_ref[i] for i in range(8)]
        row = lax.broadcasted_iota(jnp.int32, (SUBLANE, cw), 0)
        zero = jnp.zeros((SUBLANE, cw), F32)

        def step(i, carry):
            r = (n_t - 1 - i) if reverse else i
            off = pl.multiple_of(r * SUBLANE, SUBLANE)
            xr = x_ref[0, pl.ds(off, SUBLANE), :]
            xi = x_ref[1, pl.ds(off, SUBLANE), :]
            for si, d in enumerate((1, 2, 4)):
                sh = (SUBLANE - d) if reverse else d
                sr, sm = pltpu.roll(xr, sh, 0), pltpu.roll(xi, sh, 0)
                lre, lim = c[2 * si], c[2 * si + 1]
                xr, xi = xr + lre * sr - lim * sm, xi + lre * sm + lim * sr
            cre, cim = carry[0], carry[1]
            xr, xi = xr + c[6] * cre - c[7] * cim, xi + c[6] * cim + c[7] * cre
            o_ref[0, pl.ds(off, SUBLANE), :] = xr
            o_ref[1, pl.ds(off, SUBLANE), :] = xi
            edge = 0 if reverse else SUBLANE - 1
            new = (jnp.broadcast_to(xr[edge:edge + 1, :], (SUBLANE, cw)), jnp.broadcast_to(xi[edge:edge + 1, :], (SUBLANE, cw)))
            if not with_dlam:
                return new
            poff = pl.multiple_of(jnp.maximum(r - 1, 0) * SUBLANE, SUBLANE)
            first = r > 0
            pr_last = jnp.where(first, jnp.broadcast_to(s_ref[0, pl.ds(poff, SUBLANE), :][SUBLANE - 1:, :], (SUBLANE, cw)), zero)
            pi_last = jnp.where(first, jnp.broadcast_to(s_ref[1, pl.ds(poff, SUBLANE), :][SUBLANE - 1:, :], (SUBLANE, cw)), zero)
            spr = jnp.where(row == 0, pr_last, pltpu.roll(s_ref[0, pl.ds(off, SUBLANE), :], 1, 0))
            spi = jnp.where(row == 0, pi_last, pltpu.roll(s_ref[1, pl.ds(off, SUBLANE), :], 1, 0))
            return new + (carry[2] + xr * spr + xi * spi, carry[3] + xi * spr - xr * spi)

        init = (zero, zero, zero, zero) if with_dlam else (zero, zero)
        fin = lax.fori_loop(0, n_t, step, init)
        if with_dlam:
            @pl.when(pl.program_id(1) == 0)
            def _():
                dl_ref[...] = jnp.zeros_like(dl_ref)
            dl_ref[0] += fin[2]
            dl_ref[1] += fin[3]

    blk = pl.BlockSpec((2, seq, cw), lambda j, b: (0, b, j))
    in_specs = [pl.BlockSpec((8, SUBLANE, cw), lambda j, b: (0, 0, j)), blk]
    operands = [coef, x]
    out_specs = [blk]
    out_shape = [jax.ShapeDtypeStruct(x.shape, F32)]
    if with_dlam:
        in_specs.append(blk)
        operands.append(states)
        out_specs.append(pl.BlockSpec((2, SUBLANE, cw), lambda j, b: (0, 0, j)))
        out_shape.append(jax.ShapeDtypeStruct((2, SUBLANE, SSM_COLS), F32))
    res = pl.pallas_call(
        body, name=name, grid=(n_cb, n_seq), in_specs=in_specs, out_specs=out_specs, out_shape=out_shape,
        compiler_params=_params(("parallel", "arbitrary")),
    )(*operands)
    return res if with_dlam else res[0]


def _state_col(j, kk):
    return 2 * j + kk + 2 * (kk // 2)


def ssm_in(name, z, bbd, tm=1024):
    n = z.shape[0]
    tm = _tile(n, tm)
    t = SSM_TILE
    return _mm(name, z, bbd, (2, n, SSM_COLS), F32, (n // tm, 2 * SSM_COLS // t, 1),
               pl.BlockSpec((tm, t), lambda i, j, kk: (i, (j % 8) // 4)),
               pl.BlockSpec((t, t), lambda i, j, kk: ((j % 8) // 4, j)),
               pl.BlockSpec((None, tm, t), lambda i, j, kk: (j // 8, i, j % 8)), NN)


def ssm_out(name, s, cbd, tm=1024):
    n = s.shape[1]
    tm = _tile(n, tm)
    t = SSM_TILE
    return _mm(name, s, cbd, (n, SSM_WIDTH), F32, (n // tm, SSM_WIDTH // t, 4),
               pl.BlockSpec((None, tm, 512), lambda i, j, kk: (kk // 2, i, 2 * j + kk % 2)),
               pl.BlockSpec((512, t), lambda i, j, kk: (_state_col(j, kk), j)),
               pl.BlockSpec((tm, t), lambda i, j, kk: (i, j)), NN)


def ssm_out_t(name, dy, cbd, tm=1024):
    n = dy.shape[0]
    tm = _tile(n, tm)
    t = SSM_TILE
    return _mm(name, dy, cbd, (2, n, SSM_COLS), F32, (n // tm, 2 * SSM_COLS // t, 1),
               pl.BlockSpec((tm, t), lambda i, j, kk: (i, (j % 8) // 4)),
               pl.BlockSpec((t, t), lambda i, j, kk: (j, (j % 8) // 4)),
               pl.BlockSpec((None, tm, t), lambda i, j, kk: (j // 8, i, j % 8)), NT)


def ssm_in_t(name, a, bbd, res, tm=1024):
    n = a.shape[1]
    tm = _tile(n, tm)
    t = SSM_TILE
    return _mm(name, a, bbd, (n, SSM_WIDTH), F32, (n // tm, SSM_WIDTH // t, 4),
               pl.BlockSpec((None, tm, 512), lambda i, j, kk: (kk // 2, i, 2 * j + kk % 2)),
               pl.BlockSpec((t, 512), lambda i, j, kk: (j, _state_col(j, kk))),
               pl.BlockSpec((tm, t), lambda i, j, kk: (i, j)), NT, 1.0, res)


def ssm_grad_c(name, s, dy, tk=1024):
    n = s.shape[1]
    tk = _tile(n, tk)
    t = SSM_TILE
    return _mm(name, s, dy, (2 * SSM_COLS, t), F32, (2 * SSM_COLS // t, 1, n // tk),
               pl.BlockSpec((None, tk, t), lambda i, j, kk: (i // 8, kk, i % 8)),
               pl.BlockSpec((tk, t), lambda i, j, kk: (kk, (i % 8) // 4)),
               pl.BlockSpec((t, t), lambda i, j, kk: (i, 0)), TN)


def ssm_grad_b(name, z, a, tk=1024):
    n = z.shape[0]
    tk = _tile(n, tk)
    t = SSM_TILE
    return _mm(name, z, a, (t, 2 * SSM_COLS), F32, (1, 2 * SSM_COLS // t, n // tk),
               pl.BlockSpec((tk, t), lambda i, j, kk: (kk, (j % 8) // 4)),
               pl.BlockSpec((None, tk, t), lambda i, j, kk: (j // 8, kk, j % 8)),
               pl.BlockSpec((t, t), lambda i, j, kk: (0, j)), TN)


_GROUP_TILE = SSM_TILE // SSM_CH


def expand_b(bb_re, bb_im):
    b = jnp.stack([bb_re, bb_im]).reshape(2, SSM_GROUPS, SSM_STATE, SSM_CH)
    eye = jnp.eye(SSM_GROUPS, dtype=F32)
    return jnp.einsum("rgph,gk->ghrkp", b, eye).reshape(SSM_WIDTH, 2 * SSM_COLS).astype(MXU_DTYPE)


def expand_c(c_re, c_im):
    c = jnp.stack([c_re, -c_im])
    eye = jnp.eye(SSM_GROUPS, dtype=F32)
    return jnp.einsum("rghp,gk->rgpkh", c, eye).reshape(2 * SSM_COLS, SSM_WIDTH).astype(MXU_DTYPE)


def _group_pick():
    return (jnp.arange(SSM_GROUPS)[:, None] % _GROUP_TILE == jnp.arange(_GROUP_TILE)[None, :]).astype(F32)


def compact_c(dc):
    x = dc.reshape(2, SSM_GROUPS, SSM_STATE, _GROUP_TILE, SSM_CH)
    g = jnp.einsum("rgpch,gc->rghp", x, _group_pick())
    return g[0], -g[1]


def compact_b(db):
    x = db.reshape(_GROUP_TILE, SSM_CH, 2, SSM_GROUPS, SSM_STATE)
    g = jnp.einsum("chrgp,gc->rgph", x, _group_pick()).reshape(2, SSM_COLS, SSM_CH)
    return g[0], g[1]


def pool_window(name, x, col_block0, seq, out_dtype, adjoint):
    n = x.shape[0]

    def body(x_ref, o_ref):
        win = 2 << pl.program_id(1)
        row = lax.broadcasted_iota(jnp.int32, (seq, POOL_CH), 0)
        v = x_ref[...].astype(F32)
        cnt = jnp.minimum(row + 1, win).astype(F32)
        s = v / cnt if adjoint else v
        for d in (1, 2, 4, 8):
            if adjoint:
                sh = jnp.where((row < seq - d) & (d < win), pltpu.roll(s, seq - d, 0), 0.0)
            else:
                sh = jnp.where((row >= d) & (d < win), pltpu.roll(s, d, 0), 0.0)
            s = s + sh
        o_ref[...] = ((s - v) if adjoint else (s / cnt - v)).astype(out_dtype)

    return pl.pallas_call(
        body, name=name, grid=(n // seq, POOL_GROUPS),
        in_specs=[pl.BlockSpec((seq, POOL_CH), lambda b, g: (b, col_block0 + g))],
        out_specs=pl.BlockSpec((seq, POOL_CH), lambda b, g: (b, g)),
        out_shape=jax.ShapeDtypeStruct((n, POOL_WIDTH), out_dtype),
        compiler_params=_params(("parallel", "parallel")),
    )(x)


def pool_mm(name, q, w, out_dtype, tm=1024):
    n = q.shape[0]
    tm = _tile(n, tm)
    return _mm(name, q, w, (n, POOL_WIDTH), out_dtype, (n // tm, POOL_GROUPS, 1),
               pl.BlockSpec((tm, POOL_CH), lambda i, j, kk: (i, j)),
               pl.BlockSpec((None, POOL_CH, POOL_CH), lambda i, j, kk: (j, 0, 0)),
               pl.BlockSpec((tm, POOL_CH), lambda i, j, kk: (i, j)), NN)


def pool_mm_t(name, dy, col_block0, w, out_dtype, tm=1024):
    n = dy.shape[0]
    tm = _tile(n, tm)
    return _mm(name, dy, w, (n, POOL_WIDTH), out_dtype, (n // tm, POOL_GROUPS, 1),
               pl.BlockSpec((tm, POOL_CH), lambda i, j, kk: (i, col_block0 + j)),
               pl.BlockSpec((None, POOL_CH, POOL_CH), lambda i, j, kk: (j, 0, 0)),
               pl.BlockSpec((tm, POOL_CH), lambda i, j, kk: (i, j)), NT)


def pool_grad_w(name, q, dy, col_block0, tk=1024):
    n = q.shape[0]
    tk = _tile(n, tk)
    return _mm(name, q, dy, (POOL_GROUPS, POOL_CH, POOL_CH), F32, (POOL_GROUPS, 1, n // tk),
               pl.BlockSpec((tk, POOL_CH), lambda i, j, kk: (kk, i)),
               pl.BlockSpec((tk, POOL_CH), lambda i, j, kk: (kk, col_block0 + i)),
               pl.BlockSpec((None, POOL_CH, POOL_CH), lambda i, j, kk: (i, 0, 0)), TN)


def _any_specs(n):
    return [pl.BlockSpec(memory_space=pl.ANY)] * n


def _place():
    x, y, c = lax.axis_index("x"), lax.axis_index("y"), lax.axis_index("c")
    return x, y, c


def _at_axis(ref, axis, start, size):
    return ref.at[(slice(None),) * axis + (pl.ds(start, size),)]


def sibling_swap_halves(name, arrays, axis):
    n = len(arrays)
    halves = [a.shape[axis] // 2 for a in arrays]

    def body(*refs):
        ins, own, got = refs[:n], refs[n:2 * n], refs[2 * n:3 * n]
        send_sems, recv_sems, local_sems = refs[3 * n:]
        x, y, c = _place()
        copies = []
        for i in range(n):
            h = halves[i]
            mine = pltpu.make_async_copy(_at_axis(ins[i], axis, c * h, h), own[i], local_sems.at[i])
            mine.start()
            away = pltpu.make_async_remote_copy(
                src_ref=_at_axis(ins[i], axis, (1 - c) * h, h), dst_ref=got[i],
                send_sem=send_sems.at[i], recv_sem=recv_sems.at[i], device_id=(x, y, 1 - c), device_id_type=MESH_ID)
            away.start()
            copies += [mine, away]
        for cp in copies:
            cp.wait()

    def half_shape(a, h):
        return jax.ShapeDtypeStruct(a.shape[:axis] + (h,) + a.shape[axis + 1:], a.dtype)

    shapes = [half_shape(a, h) for a, h in zip(arrays, halves)]
    res = pl.pallas_call(
        body, name=name, in_specs=_any_specs(n), out_specs=_any_specs(2 * n), out_shape=shapes + shapes,
        scratch_shapes=[pltpu.SemaphoreType.DMA((n,)), pltpu.SemaphoreType.DMA((n,)), pltpu.SemaphoreType.DMA((n,))],
    )(*arrays)
    return res[:n], res[n:]


def sibling_join_halves(name, arrays, axis):
    n = len(arrays)

    def body(*refs):
        ins, outs = refs[:n], refs[n:2 * n]
        send_sems, recv_sems, local_sems = refs[2 * n:]
        x, y, c = _place()
        copies = []
        for i in range(n):
            h = ins[i].shape[axis]
            dst = _at_axis(outs[i], axis, c * h, h)
            mine = pltpu.make_async_copy(ins[i], dst, local_sems.at[i])
            mine.start()
            away = pltpu.make_async_remote_copy(
                src_ref=ins[i], dst_ref=dst, send_sem=send_sems.at[i], recv_sem=recv_sems.at[i],
                device_id=(x, y, 1 - c), device_id_type=MESH_ID)
            away.start()
            copies += [mine, away]
        for cp in copies:
            cp.wait()

    shapes = [jax.ShapeDtypeStruct(a.shape[:axis] + (2 * a.shape[axis],) + a.shape[axis + 1:], a.dtype) for a in arrays]
    return pl.pallas_call(
        body, name=name, in_specs=_any_specs(n), out_specs=_any_specs(n), out_shape=shapes,
        scratch_shapes=[pltpu.SemaphoreType.DMA((n,)), pltpu.SemaphoreType.DMA((n,)), pltpu.SemaphoreType.DMA((n,))],
    )(*arrays)


def sibling_swap(name, arrays):
    n = len(arrays)

    def body(*refs):
        ins, outs = refs[:n], refs[n:2 * n]
        send_sems, recv_sems = refs[2 * n:]
        x, y, c = _place()
        copies = []
        for i in range(n):
            away = pltpu.make_async_remote_copy(
                src_ref=ins[i], dst_ref=outs[i], send_sem=send_sems.at[i], recv_sem=recv_sems.at[i],
                device_id=(x, y, 1 - c), device_id_type=MESH_ID)
            away.start()
            copies.append(away)
        for cp in copies:
            cp.wait()

    return pl.pallas_call(
        body, name=name, in_specs=_any_specs(n), out_specs=_any_specs(n),
        out_shape=[jax.ShapeDtypeStruct(a.shape, a.dtype) for a in arrays],
        scratch_shapes=[pltpu.SemaphoreType.DMA((n,)), pltpu.SemaphoreType.DMA((n,))],
    )(*arrays)


_FLIPS = ((1, 0), (0, 1), (1, 1))


def chip_exchange(name, arrays, axis, all_to_all):
    n = len(arrays)

    def body(*refs):
        ins, outs = refs[:n], refs[n:2 * n]
        send_sems, recv_sems, local_sems = refs[2 * n:]
        x, y, c = _place()
        me = 2 * x + y
        copies = []
        for i in range(n):
            dst = _at_axis(outs[i], axis, me, 1)
            mine = pltpu.make_async_copy(_at_axis(ins[i], axis, me, 1) if all_to_all else ins[i], dst, local_sems.at[i])
            mine.start()
            copies.append(mine)
            for f, (fx, fy) in enumerate(_FLIPS):
                px = (1 - x) if fx else x
                py = (1 - y) if fy else y
                src = _at_axis(ins[i], axis, 2 * px + py, 1) if all_to_all else ins[i]
                away = pltpu.make_async_remote_copy(
                    src_ref=src, dst_ref=dst, send_sem=send_sems.at[3 * i + f], recv_sem=recv_sems.at[3 * i + f],
                    device_id=(px, py, c), device_id_type=MESH_ID)
                away.start()
                copies.append(away)
        for cp in copies:
            cp.wait()

    shapes = [jax.ShapeDtypeStruct(a.shape[:axis] + (N_CHIPS,) + a.shape[axis + 1:], a.dtype) for a in arrays]
    return pl.pallas_call(
        body, name=name, in_specs=_any_specs(n), out_specs=_any_specs(n), out_shape=shapes,
        scratch_shapes=[pltpu.SemaphoreType.DMA((3 * n,)), pltpu.SemaphoreType.DMA((3 * n,)), pltpu.SemaphoreType.DMA((n,))],
    )(*arrays)


def add2(name, a, b):
    shape = a.shape
    a2, b2 = a.reshape(-1, shape[-1]), b.reshape(-1, shape[-1])
    rows, w = a2.shape
    tm = _rows_tile(rows, w)
    return _rowwise(name, lambda u, v: u + v, rows, tm, [(a2, w, 0), (b2, w, 0)], [], [(w, F32)])[0].reshape(shape)


def _rows_tile(rows, width, budget=2 * 1024 * 1024):
    best = SUBLANE
    t = SUBLANE
    while t <= rows:
        if rows % t == 0 and t * width * 4 <= budget:
            best = t
        t += SUBLANE
    return best


def sum_slots(name, a):
    nl, _, r, c = a.shape
    tr = _rows_tile(r, c)

    def body(s0, s1, s2, s3, o_ref):
        o_ref[...] = ((s0[...] + s1[...]) + s2[...]) + s3[...]

    specs = [pl.BlockSpec((None, None, tr, c), functools.partial(lambda l, i, k: (l, k, i, 0), k=k)) for k in range(N_CHIPS)]
    return pl.pallas_call(
        body, name=name, grid=(nl, r // tr), in_specs=specs,
        out_specs=pl.BlockSpec((None, tr, c), lambda l, i: (l, i, 0)),
        out_shape=jax.ShapeDtypeStruct((nl, r, c), F32),
        compiler_params=_params(("parallel", "parallel")),
    )(a, a, a, a)


def reduce_scatter_big(grads):
    own, got = sibling_swap_halves("rs_swap_halves", grads, 2)
    chip_sum = [add2("rs_add_cores", a, b) for a, b in zip(own, got)]
    spread = chip_exchange("rs_chips", chip_sum, 1, True)
    mine = [sum_slots("rs_sum_chips", a) for a in spread]
    return sibling_join_halves("rs_join_halves", mine, 1)


def all_reduce_small(flat):
    other = sibling_swap("ar_swap", [flat])[0]
    chip = add2("ar_add_cores", flat, other)
    slots = chip_exchange("ar_chips", [chip.reshape((1,) + chip.shape)], 0, False)[0]
    rows = flat.shape[0]
    tm = _rows_tile(rows, LANE)
    nb = rows // tm
    s2 = slots.reshape(N_CHIPS * rows, LANE)
    return _rowwise("ar_sum_chips", lambda a, b, c, d: ((a + b) + c) + d, rows, tm,
                    [(s2, LANE, 0, k * nb) for k in range(N_CHIPS)], [], [(LANE, F32)])[0]


def _adamw_math(w, g, m, v):
    m = ADAM_B1 * m + (1.0 - ADAM_B1) * g
    v = ADAM_B2 * v + (1.0 - ADAM_B2) * (g * g)
    m_hat = m / (1.0 - ADAM_B1 ** ADAM_STEP)
    v_hat = v / (1.0 - ADAM_B2 ** ADAM_STEP)
    delta = -ADAM_LR * (m_hat / (jnp.sqrt(v_hat) + ADAM_EPS) + ADAM_WD * w)
    return delta, m, v


def adamw(name, w, g, m, v):
    shape = w.shape
    width = shape[-1]
    flat = [t.reshape(-1, width) for t in (w, g, m, v)]
    rows = flat[0].shape[0]
    tm = _rows_tile(rows, width, budget=1024 * 1024)
    res = _rowwise(name, _adamw_math, rows, tm, [(t, width, 0) for t in flat], [], [(width, F32)] * 3)
    return [r.reshape(shape) for r in res]


def _ffn_fwd(tag, h, g_norm, wi, wo):
    n, d = h.shape
    ff = wo.shape[0]
    xn = norm_fwd(tag + "_norm", h, g_norm)
    gu = mm_nn_colsharded(tag + "_wi", xn, wi, BF16)
    act = _rowwise(tag + "_swiglu", lambda g, u: (g.astype(F32) * jax.nn.sigmoid(g.astype(F32))) * u.astype(F32),
                   n, 512, [(gu, ff, 0), (gu, ff, 1)], [], [(ff, BF16)])[0]
    out = mm_nn(tag + "_wo", act, wo, F32, alpha=0.5, res=h)
    return out, (xn, gu, act)


def _ffn_bwd(tag, d, h, g_norm, wi, wo, saved):
    xn, gu, act = saved
    n, dm = h.shape
    ff = wo.shape[0]
    dact = mm_nt(tag + "_dact", d, wo, BF16, alpha=0.5)
    dwo = mm_tn(tag + "_dwo", act, d, F32, alpha=0.5)

    def swiglu_bwd(g, u, da):
        g, u, da = g.astype(F32), u.astype(F32), da.astype(F32)
        sg = jax.nn.sigmoid(g)
        dg = da * u * (sg * (1.0 + g * (1.0 - sg)))
        du = da * (g * sg)
        return jnp.concatenate([dg, du], axis=1)

    dgu = _rowwise(tag + "_dswiglu", swiglu_bwd, n, 512, [(gu, ff, 0), (gu, ff, 1), (dact, ff, 0)], [], [(2 * ff, BF16)])[0]
    dwi = mm_tn_colsharded(tag + "_dwi", xn, dgu, F32)
    dxn = mm_nt_colsharded(tag + "_dxn", dgu, wi, F32)
    d_in, dg_norm = norm_bwd(tag + "_dnorm", dxn, h, g_norm, d)
    return d_in, dg_norm.reshape(dm), dwi, dwo.reshape(N_CHIPS, ff // N_CHIPS, dm)


def _col(v):
    return v.reshape(SSM_COLS, 1)


def _layer_fwd(h, lw, p_l, seq):
    n, d = h.shape
    h1, ffn1_saved = _ffn_fwd("ffn1", h, lw["ffn1_norm"], lw["ffn1_wi"], lw["ffn1_wo"])

    xn2 = norm_fwd("mix_norm", h1, lw["mix_norm"])
    z = mm_nn("mix_in", xn2, lw["w_in"], F32)
    log_dt = jnp.repeat(lw["ssm_log_dt"], SSM_STATE)
    b_re, b_im = lw["ssm_b_re"].reshape(SSM_COLS, SSM_CH), lw["ssm_b_im"].reshape(SSM_COLS, SSM_CH)
    pw_re, pw_im, bb_re, bb_im = ssm_prep("ssm_prep", _col(lw["ssm_lambda_re"]), _col(lw["ssm_lambda_im"]), _col(log_dt), b_re, b_im)
    bbd = expand_b(bb_re, bb_im)
    cbd = expand_c(lw["ssm_c_re"], lw["ssm_c_im"])
    bu = ssm_in("ssm_in", z, bbd)
    s = ssm_scan("ssm_scan", scan_coefficients(pw_re, pw_im, False), bu, seq, False)
    y0c = ssm_out("ssm_out", s, cbd)

    def skip_gelu(yc, zs, dvec):
        y0 = yc + dvec * zs
        return y0, _gelu(y0)

    y0, y1 = _rowwise("ssm_gelu", skip_gelu, n, 512, [(y0c, SSM_WIDTH, 0), (z, SSM_WIDTH, 0)],
                      [lw["ssm_d"].reshape(1, SSM_WIDTH)], [(SSM_WIDTH, F32), (SSM_WIDTH, F32)])
    t = mm_nn("ssm_glu_mm", y1, lw["ssm_w_glu"], F32)
    y2 = _rowwise("ssm_glu", lambda a, b: a * jax.nn.sigmoid(b), n, 512, [(y1, SSM_WIDTH, 0), (t, SSM_WIDTH, 0)], [],
                  [(SSM_WIDTH, BF16)])[0]

    q = pool_window("pool_window", z, SSM_WIDTH // POOL_CH, seq, BF16, False)
    wp_eff = lw["pool_w"] * lw["pool_scale"].reshape(POOL_GROUPS, 1, POOL_CH)
    yp = pool_mm("pool_mm", q, wp_eff, BF16)
    m = jnp.concatenate([y2, yp], axis=1)
    h2 = mm_nn("mix_out", m, lw["w_out"], F32, res=h1)

    h3, ffn2_saved = _ffn_fwd("ffn2", h2, lw["ffn2_norm"], lw["ffn2_wi"], lw["ffn2_wo"])

    xn4 = norm_fwd("ple_norm", h3, lw["ple_norm"])
    tg = mm_nn("ple_gate", xn4, lw["ple_w_gate"], F32)
    e = mm_nn_colsharded("ple_proj", p_l, lw["ple_w_proj"], F32)
    h4 = _rowwise("ple_add", lambda a, b, c: a + jax.nn.sigmoid(b) * c, n, 512, [(h3, d, 0), (tg, d, 0), (e, d, 0)], [], [(d, F32)])[0]
    saved = dict(h=h, h1=h1, h2=h2, h3=h3, ffn1=ffn1_saved, ffn2=ffn2_saved, xn2=xn2, z=z, s=s, y0=y0, y1=y1, t=t, m=m, q=q,
                 xn4=xn4, tg=tg, e=e, pw_re=pw_re, pw_im=pw_im, bbd=bbd, cbd=cbd)
    return h4, saved


def _layer_bwd(d, lw, p_l, sv, seq):
    n, dm = d.shape
    g = {}
    def ple_bwd(dd, tg, e):
        gate = jax.nn.sigmoid(tg)
        return dd * e * gate * (1.0 - gate), dd * gate

    dtg, de = _rowwise("ple_dadd", ple_bwd, n, 512, [(d, dm, 0), (sv["tg"], dm, 0), (sv["e"], dm, 0)], [], [(dm, BF16), (dm, BF16)])
    g["ple_w_proj"] = mm_tn_colsharded("ple_dproj", p_l, de, F32)
    g["ple_w_gate"] = mm_tn("ple_dgate_w", sv["xn4"], dtg, F32).reshape(N_CHIPS, dm // N_CHIPS, dm)
    dxn4 = mm_nt("ple_dgate_x", dtg, lw["ple_w_gate"], F32)
    d, dg = norm_bwd("ple_dnorm", dxn4, sv["h3"], lw["ple_norm"], d)
    g["ple_norm"] = dg.reshape(dm)

    d, g["ffn2_norm"], g["ffn2_wi"], g["ffn2_wo"] = _ffn_bwd("ffn2b", d, sv["h2"], lw["ffn2_norm"], lw["ffn2_wi"], lw["ffn2_wo"], sv["ffn2"])

    dmix = mm_nt("mix_dout_x", d, lw["w_out"], F32)
    g["w_out"] = mm_tn("mix_dout_w", sv["m"], d, F32).reshape(N_CHIPS, dm // N_CHIPS, dm)
    pool_cb = SSM_WIDTH // POOL_CH
    wp_eff = lw["pool_w"] * lw["pool_scale"].reshape(POOL_GROUPS, 1, POOL_CH)
    dq = pool_mm_t("pool_dmm_x", dmix, pool_cb, wp_eff, F32)
    dwp_eff = pool_grad_w("pool_dmm_w", sv["q"], dmix, pool_cb)
    g["pool_w"] = dwp_eff * lw["pool_scale"].reshape(POOL_GROUPS, 1, POOL_CH)
    g["pool_scale"] = jnp.sum(dwp_eff * lw["pool_w"], axis=1).reshape(POOL_WIDTH)
    dzp = pool_window("pool_dwindow", dq, 0, seq, BF16, True)

    def glu_bwd(dy2, y1, t):
        sg = jax.nn.sigmoid(t)
        return dy2 * y1 * sg * (1.0 - sg), dy2 * sg

    dt_, dy1a = _rowwise("ssm_dglu", glu_bwd, n, 512, [(dmix, SSM_WIDTH, 0), (sv["y1"], SSM_WIDTH, 0), (sv["t"], SSM_WIDTH, 0)], [],
                         [(SSM_WIDTH, BF16), (SSM_WIDTH, F32)])
    g["ssm_w_glu"] = mm_tn("ssm_dglu_w", sv["y1"], dt_, F32).reshape(N_CHIPS, SSM_WIDTH // N_CHIPS, SSM_WIDTH)
    dy1b = mm_nt("ssm_dglu_x", dt_, lw["ssm_w_glu"], F32)

    def gelu_bwd(da, db, y0, zs, dvec):
        dy0 = (da + db) * _gelu_grad(y0)
        return dy0, dy0 * dvec, jnp.sum(dy0 * zs, axis=0, keepdims=True)

    dy0, dzs_a, dd = _rowwise("ssm_dgelu", gelu_bwd, n, 512,
                              [(dy1a, SSM_WIDTH, 0), (dy1b, SSM_WIDTH, 0), (sv["y0"], SSM_WIDTH, 0), (sv["z"], SSM_WIDTH, 0)],
                              [lw["ssm_d"].reshape(1, SSM_WIDTH)], [(SSM_WIDTH, F32), (SSM_WIDTH, F32)], [(1, SSM_WIDTH)])
    g["ssm_d"] = dd.reshape(SSM_WIDTH)
    g["ssm_c_re"], g["ssm_c_im"] = compact_c(ssm_grad_c("ssm_dc", sv["s"], dy0))
    v = ssm_out_t("ssm_dout", dy0, sv["cbd"])
    a, dlam = ssm_scan("ssm_scan_adj", scan_coefficients(sv["pw_re"], sv["pw_im"], True), v, seq, True, states=sv["s"])
    dbb_re, dbb_im = compact_b(ssm_grad_b("ssm_db", sv["z"], a))
    dzs = ssm_in_t("ssm_din", a, sv["bbd"], dzs_a)
    dlam = jnp.sum(dlam, axis=1)
    log_dt = jnp.repeat(lw["ssm_log_dt"], SSM_STATE)
    b_re, b_im = lw["ssm_b_re"].reshape(SSM_COLS, SSM_CH), lw["ssm_b_im"].reshape(SSM_COLS, SSM_CH)
    glr, gli, gld, gbr, gbi = ssm_prep_bwd("ssm_prep_bwd", _col(lw["ssm_lambda_re"]), _col(lw["ssm_lambda_im"]), _col(log_dt), b_re, b_im,
                                           _col(dlam[0]), _col(dlam[1]), dbb_re, dbb_im)
    g["ssm_lambda_re"] = glr.reshape(SSM_GROUPS, SSM_STATE)
    g["ssm_lambda_im"] = gli.reshape(SSM_GROUPS, SSM_STATE)
    g["ssm_log_dt"] = jnp.sum(gld.reshape(SSM_GROUPS, SSM_STATE), axis=1)
    g["ssm_b_re"] = gbr.reshape(SSM_GROUPS, SSM_STATE, SSM_CH)
    g["ssm_b_im"] = gbi.reshape(SSM_GROUPS, SSM_STATE, SSM_CH)

    dz = jnp.concatenate([dzs.astype(BF16), dzp], axis=1)
    g["w_in"] = mm_tn("mix_din_w", sv["xn2"], dz, F32).reshape(N_CHIPS, dm // N_CHIPS, dm)
    dxn2 = mm_nt("mix_din_x", dz, lw["w_in"], F32)
    d, dg = norm_bwd("mix_dnorm", dxn2, sv["h1"], lw["mix_norm"], d)
    g["mix_norm"] = dg.reshape(dm)

    d, g["ffn1_norm"], g["ffn1_wi"], g["ffn1_wo"] = _ffn_bwd("ffn1b", d, sv["h"], lw["ffn1_norm"], lw["ffn1_wi"], lw["ffn1_wo"], sv["ffn1"])
    return d, g


def _flatten_small(tensors):
    flat = jnp.concatenate([t.reshape(-1) for t in tensors])
    pad = (-flat.shape[0]) % (SUBLANE * LANE)
    return jnp.pad(flat, (0, pad)).reshape(-1, LANE)


def _unflatten_small(flat, like):
    flat = flat.reshape(-1)
    out, off = [], 0
    for t in like:
        out.append(flat[off:off + t.size].reshape(t.shape))
        off += t.size
    return out


def kernel(x, p, ffn1_norm, ffn1_wi, ffn1_wo, mix_norm, w_in, ssm_lambda_re, ssm_lambda_im, ssm_log_dt, ssm_b_re, ssm_b_im, ssm_c_re, ssm_c_im, ssm_d, ssm_w_glu, pool_w, pool_scale, w_out, ffn2_norm, ffn2_wi, ffn2_wo, ple_norm, ple_w_gate, ple_w_proj, final_norm, loss_target, m_ffn1_norm, m_ffn1_wi, m_ffn1_wo, m_mix_norm, m_w_in, m_ssm_lambda_re, m_ssm_lambda_im, m_ssm_log_dt, m_ssm_b_re, m_ssm_b_im, m_ssm_c_re, m_ssm_c_im, m_ssm_d, m_ssm_w_glu, m_pool_w, m_pool_scale, m_w_out, m_ffn2_norm, m_ffn2_wi, m_ffn2_wo, m_ple_norm, m_ple_w_gate, m_ple_w_proj, m_final_norm, v_ffn1_norm, v_ffn1_wi, v_ffn1_wo, v_mix_norm, v_w_in, v_ssm_lambda_re, v_ssm_lambda_im, v_ssm_log_dt, v_ssm_b_re, v_ssm_b_im, v_ssm_c_re, v_ssm_c_im, v_ssm_d, v_ssm_w_glu, v_pool_w, v_pool_scale, v_w_out, v_ffn2_norm, v_ffn2_wi, v_ffn2_wo, v_ple_norm, v_ple_w_gate, v_ple_w_proj, v_final_norm):
    given = dict(locals())
    w = {k: given[k] for k in WEIGHTS}
    mom = {k: given["m_" + k] for k in WEIGHTS}
    var = {k: given["v_" + k] for k in WEIGHTS}
    bsz, seq, dm = x.shape
    n = bsz * seq
    depth = ffn1_wi.shape[0]

    shards = [w[k].astype(MXU_DTYPE).reshape((depth, 1) + w[k].shape[1:]) for k in BIG]
    gathered = chip_exchange("gather_weights", shards, 1, False)
    full = {}
    for k, a in zip(BIG, gathered):
        full[k] = a if k in COL_SHARDED else a.reshape(depth, N_CHIPS * a.shape[2], a.shape[3])
    layer_w = dict(full)
    for k in SMALL:
        if k != "final_norm":
            layer_w[k] = w[k]

    def fwd_body(h, xs):
        lw, p_l = xs
        h_out, saved = _layer_fwd(h, lw, p_l, seq)
        return h_out, saved

    p2 = p.reshape(depth, n, p.shape[-1])
    h_last, saved = lax.scan(fwd_body, x.reshape(n, dm), (layer_w, p2))

    def head(hh, tgt, gf):
        r = lax.rsqrt(jnp.mean(hh * hh, axis=-1, keepdims=True) + NORM_EPS)
        xh = hh * r
        diff = xh * gf - tgt
        dy = diff * (1.0 / dm)
        dxh = dy * gf
        dx = r * (dxh - xh * jnp.mean(dxh * xh, axis=-1, keepdims=True))
        return dx, jnp.sum(diff * diff, axis=0, keepdims=True) * (0.5 / dm), jnp.sum(dy * xh, axis=0, keepdims=True)

    d_last, loss_cols, g_final = _rowwise("loss_head", head, n, 512, [(h_last, dm, 0), (loss_target.reshape(n, dm), dm, 0)],
                                          [final_norm.reshape(1, dm)], [(dm, F32)], [(1, dm), (1, dm)])
    loss = lax.psum(jnp.sum(loss_cols), ("x", "y", "c"))

    def bwd_body(d, xs):
        lw, p_l, sv = xs
        d_in, g = _layer_bwd(d, lw, p_l, sv, seq)
        return d_in, g

    d_x, grads = lax.scan(bwd_body, d_last, (layer_w, p2, saved), reverse=True)
    grad_x = d_x.reshape(bsz, seq, dm)

    big_sum = reduce_scatter_big([grads[k] for k in BIG])
    small_keys = [k for k in SMALL]
    small_parts = [grads[k] if k != "final_norm" else g_final.reshape(dm) for k in small_keys]
    small_sum = _unflatten_small(all_reduce_small(_flatten_small(small_parts)), small_parts)
    g_out = dict(zip(BIG, big_sum))
    g_out.update(dict(zip(small_keys, small_sum)))
    for k in BIG:
        g_out[k] = g_out[k].reshape(w[k].shape)

    delta, new_m, new_v = {}, {}, {}
    for k in BIG:
        delta[k], new_m[k], new_v[k] = adamw("adamw_" + k, w[k], g_out[k], mom[k], var[k])
    sw = adamw("adamw_small", *[_flatten_small([t[k] for k in small_keys]) for t in (w, g_out, mom, var)])
    for name, flat in zip((delta, new_m, new_v), sw):
        for k, t in zip(small_keys, _unflatten_small(flat, [w[k] for k in small_keys])):
            name[k] = t

    return (loss, grad_x, *[g_out[k] for k in WEIGHTS], *[delta[k] for k in WEIGHTS],
            *[new_m[k] for k in WEIGHTS], *[new_v[k] for k in WEIGHTS])
```

```python
import functools
import math

import jax
import jax.numpy as jnp
from jax import lax
from jax.experimental import pallas as pl
from jax.experimental.pallas import tpu as pltpu

F32 = jnp.float32
BF16 = jnp.bfloat16
MXU_DTYPE = jnp.bfloat16
VMEM_LIMIT = 56 * 1024 * 1024
LANE = 128
SUBLANE = 8

N_CHIPS = 4
SSM_GROUPS = 32
SSM_STATE = 64
SSM_CH = 16
SSM_WIDTH = SSM_GROUPS * SSM_CH
SSM_COLS = SSM_GROUPS * SSM_STATE
POOL_GROUPS = 4
POOL_CH = 128
POOL_WIDTH = POOL_GROUPS * POOL_CH
SSM_TILE = 256
NORM_EPS = 1e-6
ADAM_LR = 0.001
ADAM_B1 = 0.9
ADAM_B2 = 0.999
ADAM_EPS = 1e-08
ADAM_WD = 0.01
ADAM_STEP = 10
MESH_ID = pl.DeviceIdType.MESH

BIG = ("ffn1_wi", "ffn1_wo", "w_in", "ssm_w_glu", "w_out", "ffn2_wi", "ffn2_wo", "ple_w_gate", "ple_w_proj")
COL_SHARDED = ("ffn1_wi", "ffn2_wi", "ple_w_proj")
SMALL = ("ffn1_norm", "mix_norm", "ssm_lambda_re", "ssm_lambda_im", "ssm_log_dt", "ssm_b_re", "ssm_b_im",
         "ssm_c_re", "ssm_c_im", "ssm_d", "pool_w", "pool_scale", "ffn2_norm", "ple_norm", "final_norm")
WEIGHTS = ("ffn1_norm", "ffn1_wi", "ffn1_wo", "mix_norm", "w_in", "ssm_lambda_re", "ssm_lambda_im", "ssm_log_dt",
           "ssm_b_re", "ssm_b_im", "ssm_c_re", "ssm_c_im", "ssm_d", "ssm_w_glu", "pool_w", "pool_scale", "w_out",
           "ffn2_norm", "ffn2_wi", "ffn2_wo", "ple_norm", "ple_w_gate", "ple_w_proj", "final_norm")


def _tile(dim, target):
    best = None
    t = LANE
    while t <= min(dim, target):
        if dim % t == 0:
            best = t
        t += LANE
    return best if best is not None else dim


def _params(sem):
    return pltpu.CompilerParams(dimension_semantics=sem, vmem_limit_bytes=VMEM_LIMIT)


def _mm(name, a, b, out_shape, out_dtype, grid, a_spec, b_spec, o_spec, contract, alpha=1.0, res=None):
    n_k = grid[2]
    acc_shape = tuple(d for d in o_spec.block_shape if d is not None)

    def body(*refs):
        a_ref, b_ref = refs[0], refs[1]
        r_ref = refs[2] if res is not None else None
        o_ref = refs[3] if res is not None else refs[2]

        def product():
            return lax.dot_general(a_ref[...].astype(MXU_DTYPE), b_ref[...].astype(MXU_DTYPE),
                                   (contract, ((), ())), preferred_element_type=F32)

        def finish(v):
            if alpha != 1.0:
                v = v * alpha
            if r_ref is not None:
                v = v + r_ref[...].astype(F32)
            o_ref[...] = v.astype(out_dtype)

        if n_k == 1:
            finish(product())
            return
        acc = refs[-1]
        k = pl.program_id(2)

        @pl.when(k == 0)
        def _():
            acc[...] = product()

        @pl.when(k > 0)
        def _():
            acc[...] += product()

        @pl.when(k == n_k - 1)
        def _():
            finish(acc[...])

    in_specs = [a_spec, b_spec]
    operands = [a, b]
    if res is not None:
        in_specs.append(o_spec)
        operands.append(res)
    return pl.pallas_call(
        body, name=name, grid=grid, in_specs=in_specs, out_specs=o_spec,
        out_shape=jax.ShapeDtypeStruct(out_shape, out_dtype),
        scratch_shapes=[pltpu.VMEM(acc_shape, F32)] if n_k > 1 else [],
        compiler_params=_params(("parallel", "parallel", "arbitrary")),
    )(*operands)


NN = ((1,), (0,))
NT = ((1,), (1,))
TN = ((0,), (0,))


def mm_nn(name, a, b, out_dtype, alpha=1.0, res=None, tm=1024, tn=1024, tk=1024):
    m, k = a.shape
    n = b.shape[1]
    tm, tn, tk = _tile(m, tm), _tile(n, tn), _tile(k, tk)
    return _mm(name, a, b, (m, n), out_dtype, (m // tm, n // tn, k // tk),
               pl.BlockSpec((tm, tk), lambda i, j, kk: (i, kk)),
               pl.BlockSpec((tk, tn), lambda i, j, kk: (kk, j)),
               pl.BlockSpec((tm, tn), lambda i, j, kk: (i, j)), NN, alpha, res)


def mm_nt(name, a, b, out_dtype, alpha=1.0, res=None, tm=1024, tn=512, tk=512):
    m, k = a.shape
    n = b.shape[0]
    tm, tn, tk = _tile(m, tm), _tile(n, tn), _tile(k, tk)
    return _mm(name, a, b, (m, n), out_dtype, (m // tm, n // tn, k // tk),
               pl.BlockSpec((tm, tk), lambda i, j, kk: (i, kk)),
               pl.BlockSpec((tn, tk), lambda i, j, kk: (j, kk)),
               pl.BlockSpec((tm, tn), lambda i, j, kk: (i, j)), NT, alpha, res)


def mm_tn(name, a, b, out_dtype, alpha=1.0, tm=1024, tn=1024, tk=1024):
    k, m = a.shape
    n = b.shape[1]
    tm, tn, tk = _tile(m, tm), _tile(n, tn), _tile(k, tk)
    return _mm(name, a, b, (m, n), out_dtype, (m // tm, n // tn, k // tk),
               pl.BlockSpec((tk, tm), lambda i, j, kk: (kk, i)),
               pl.BlockSpec((tk, tn), lambda i, j, kk: (kk, j)),
               pl.BlockSpec((tm, tn), lambda i, j, kk: (i, j)), TN, alpha)


def mm_nn_colsharded(name, a, w, out_dtype, tm=1024, tk=1024):
    m, k = a.shape
    c = w.shape[2]
    tm, tk = _tile(m, tm), _tile(k, tk)
    return _mm(name, a, w, (m, N_CHIPS * c), out_dtype, (m // tm, N_CHIPS, k // tk),
               pl.BlockSpec((tm, tk), lambda i, j, kk: (i, kk)),
               pl.BlockSpec((None, tk, c), lambda i, j, kk: (j, kk, 0)),
               pl.BlockSpec((tm, c), lambda i, j, kk: (i, j)), NN)


def transpose_colsharded(w):
    nl, _, k, c = w.shape
    return jnp.swapaxes(w, 2, 3).reshape(nl, N_CHIPS * c, k)


def mm_tn_colsharded(name, a, b, out_dtype, tm=1024, tk=1024):
    t, k = a.shape
    c = b.shape[1] // N_CHIPS
    tm, tk = _tile(k, tm), _tile(t, tk)
    return _mm(name, a, b, (N_CHIPS, k, c), out_dtype, (k // tm, N_CHIPS, t // tk),
               pl.BlockSpec((tk, tm), lambda i, j, kk: (kk, i)),
               pl.BlockSpec((tk, c), lambda i, j, kk: (kk, j)),
               pl.BlockSpec((None, tm, c), lambda i, j, kk: (j, i, 0)), TN)


def _rowwise(name, fn, n_rows, tm, row_ins, bcast_ins, outs, accs=()):
    tm = min(tm, n_rows)
    grid = (n_rows // tm,)
    n_row, n_b, n_out = len(row_ins), len(bcast_ins), len(outs)

    def body(*refs):
        ins = [r[...] for r in refs[:n_row + n_b]]
        out_refs = refs[n_row + n_b:n_row + n_b + n_out]
        acc_refs = refs[n_row + n_b + n_out:]
        res = fn(*ins)
        if not isinstance(res, (tuple, list)):
            res = (res,)
        for o_ref, v in zip(out_refs, res[:n_out]):
            o_ref[...] = v.astype(o_ref.dtype)
        if acc_refs:
            @pl.when(pl.program_id(0) == 0)
            def _():
                for a_ref in acc_refs:
                    a_ref[...] = jnp.zeros_like(a_ref)
            for a_ref, v in zip(acc_refs, res[n_out:]):
                a_ref[...] += v

    in_specs, operands = [], []
    for spec in row_ins:
        arr, width, cb = spec[0], spec[1], spec[2]
        rb = spec[3] if len(spec) > 3 else 0
        in_specs.append(pl.BlockSpec((tm, width), functools.partial(lambda i, cb, rb: (i + rb, cb), cb=cb, rb=rb)))
        operands.append(arr)
    for arr in bcast_ins:
        in_specs.append(pl.BlockSpec(arr.shape, functools.partial(lambda i, nd: (0,) * nd, nd=arr.ndim)))
        operands.append(arr)
    out_specs = [pl.BlockSpec((tm, w), lambda i: (i, 0)) for w, _ in outs]
    out_specs += [pl.BlockSpec((r, w), lambda i: (0, 0)) for r, w in accs]
    out_shape = [jax.ShapeDtypeStruct((n_rows, w), dt) for w, dt in outs]
    out_shape += [jax.ShapeDtypeStruct((r, w), F32) for r, w in accs]
    res = pl.pallas_call(
        body, name=name, grid=grid, in_specs=in_specs, out_specs=out_specs, out_shape=out_shape,
        compiler_params=_params(("arbitrary",) if accs else ("parallel",)),
    )(*operands)
    return res


def _rms(x, g):
    r = lax.rsqrt(jnp.mean(x * x, axis=-1, keepdims=True) + NORM_EPS)
    return x * r * g


def _rms_bwd(dy, x, g):
    r = lax.rsqrt(jnp.mean(x * x, axis=-1, keepdims=True) + NORM_EPS)
    xh = x * r
    dxh = dy * g
    dx = r * (dxh - xh * jnp.mean(dxh * xh, axis=-1, keepdims=True))
    return dx, jnp.sum(dy * xh, axis=0, keepdims=True)


def norm_fwd(name, h, g):
    n, d = h.shape
    return _rowwise(name, lambda x, gg: _rms(x, gg), n, 512, [(h, d, 0)], [g.reshape(1, d)], [(d, BF16)])[0]


def norm_bwd(name, dxn, h, g, d_res):
    n, d = h.shape

    def fn(dy, x, dr, gg):
        dx, dg = _rms_bwd(dy, x, gg)
        return dr + dx, dg

    return _rowwise(name, fn, n, 512, [(dxn, d, 0), (h, d, 0), (d_res, d, 0)], [g.reshape(1, d)], [(d, F32)], [(1, d)])


_GELU_C = math.sqrt(2.0 / math.pi)


def _gelu(x):
    return 0.5 * x * (1.0 + jnp.tanh(_GELU_C * (x + 0.044715 * (x * x * x))))


def _gelu_grad(x):
    th = jnp.tanh(_GELU_C * (x + 0.044715 * (x * x * x)))
    return 0.5 * (1.0 + th) + 0.5 * x * (1.0 - th * th) * (_GELU_C * (1.0 + 3.0 * 0.044715 * (x * x)))


def _ssm_discretize(lam_re, lam_im, log_dt, b_re, b_im):
    dt = jnp.exp(log_dt)
    e = jnp.exp(lam_re * dt)
    lb_re = e * jnp.cos(lam_im * dt)
    lb_im = e * jnp.sin(lam_im * dt)
    nr, ni = lb_re - 1.0, lb_im
    den = lam_re * lam_re + lam_im * lam_im
    cr = (nr * lam_re + ni * lam_im) / den
    ci = (ni * lam_re - nr * lam_im) / den
    return lb_re, lb_im, cr * b_re - ci * b_im, cr * b_im + ci * b_re


def ssm_prep(name, lam_re, lam_im, log_dt, b_re, b_im):
    def body(lr, li, ld, br, bi, pr_ref, pi_ref, bbr_ref, bbi_ref):
        lb_re, lb_im, bb_re, bb_im = _ssm_discretize(lr[...], li[...], ld[...], br[...], bi[...])
        bbr_ref[...] = bb_re
        bbi_ref[...] = bb_im
        pr, pi = lb_re, lb_im
        cols_r, cols_i = [pr], [pi]
        for _ in range(SUBLANE - 1):
            pr, pi = pr * lb_re - pi * lb_im, pr * lb_im + pi * lb_re
            cols_r.append(pr)
            cols_i.append(pi)
        lane = lax.broadcasted_iota(jnp.int32, (SSM_COLS, SUBLANE), 1)
        out_r = jnp.zeros((SSM_COLS, SUBLANE), F32)
        out_i = jnp.zeros((SSM_COLS, SUBLANE), F32)
        for r in range(SUBLANE):
            out_r = jnp.where(lane == r, cols_r[r], out_r)
            out_i = jnp.where(lane == r, cols_i[r], out_i)
        pr_ref[...] = out_r
        pi_ref[...] = out_i

    shapes = [jax.ShapeDtypeStruct((SSM_COLS, SUBLANE), F32)] * 2 + [jax.ShapeDtypeStruct((SSM_COLS, SSM_CH), F32)] * 2
    return pl.pallas_call(body, name=name, out_shape=shapes,
                          compiler_params=pltpu.CompilerParams(vmem_limit_bytes=VMEM_LIMIT))(lam_re, lam_im, log_dt, b_re, b_im)


def ssm_prep_bwd(name, lam_re, lam_im, log_dt, b_re, b_im, d_lb_re, d_lb_im, d_bb_re, d_bb_im):
    def body(lr, li, ld, br, bi, g0, g1, g2, g3, o0, o1, o2, o3, o4):
        _, vjp = jax.vjp(_ssm_discretize, lr[...], li[...], ld[...], br[...], bi[...])
        res = vjp((g0[...], g1[...], g2[...], g3[...]))
        for o, v in zip((o0, o1, o2, o3, o4), res):
            o[...] = v

    col = jax.ShapeDtypeStruct((SSM_COLS, 1), F32)
    mat = jax.ShapeDtypeStruct((SSM_COLS, SSM_CH), F32)
    return pl.pallas_call(body, name=name, out_shape=[col, col, col, mat, mat],
                          compiler_params=pltpu.CompilerParams(vmem_limit_bytes=VMEM_LIMIT))(
        lam_re, lam_im, log_dt, b_re, b_im, d_lb_re, d_lb_im, d_bb_re, d_bb_im)


def scan_coefficients(pw_re, pw_im, reverse):
    pr, pi = pw_re.T, pw_im.T
    if reverse:
        pi = -pi
    row = jnp.arange(SUBLANE)[:, None]
    out = []
    for d in (1, 2, 4):
        valid = (row < SUBLANE - d) if reverse else (row >= d)
        out.append(jnp.where(valid, pr[d - 1][None, :], 0.0))
        out.append(jnp.where(valid, pi[d - 1][None, :], 0.0))
    out.append(pr[::-1] if reverse else pr)
    out.append(pi[::-1] if reverse else pi)
    return jnp.stack(out)


def ssm_scan(name, coef, x, seq, reverse, states=None):
    n = x.shape[1]
    n_seq = n // seq
    cw = LANE
    n_cb = SSM_COLS // cw
    n_t = seq // SUBLANE
    with_dlam = states is not None

    def body(*refs):
        if with_dlam:
            coef_ref, x_ref, s_ref, o_ref, dl_ref = refs
        else:
            coef_ref, x_ref, o_ref = refs
        c = [coef_ref[i] for i in range(8)]
        row = lax.broadcasted_iota(jnp.int32, (SUBLANE, cw), 0)
        zero = jnp.zeros((SUBLANE, cw), F32)

        def step(i, carry):
            r = (n_t - 1 - i) if reverse else i
            off = pl.multiple_of(r * SUBLANE, SUBLANE)
            xr = x_ref[0, pl.ds(off, SUBLANE), :]
            xi = x_ref[1, pl.ds(off, SUBLANE), :]
            for si, d in enumerate((1, 2, 4)):
                sh = (SUBLANE - d) if reverse else d
                sr, sm = pltpu.roll(xr, sh, 0), pltpu.roll(xi, sh, 0)
                lre, lim = c[2 * si], c[2 * si + 1]
                xr, xi = xr + lre * sr - lim * sm, xi + lre * sm + lim * sr
            cre, cim = carry[0], carry[1]
            xr, xi = xr + c[6] * cre - c[7] * cim, xi + c[6] * cim + c[7] * cre
            o_ref[0, pl.ds(off, SUBLANE), :] = xr
            o_ref[1, pl.ds(off, SUBLANE), :] = xi
            edge = 0 if reverse else SUBLANE - 1
            new = (jnp.broadcast_to(xr[edge:edge + 1, :], (SUBLANE, cw)), jnp.broadcast_to(xi[edge:edge + 1, :], (SUBLANE, cw)))
            if not with_dlam:
                return new
            poff = pl.multiple_of(jnp.maximum(r - 1, 0) * SUBLANE, SUBLANE)
            first = r > 0
            pr_last = jnp.where(first, jnp.broadcast_to(s_ref[0, pl.ds(poff, SUBLANE), :][SUBLANE - 1:, :], (SUBLANE, cw)), zero)
            pi_last = jnp.where(first, jnp.broadcast_to(s_ref[1, pl.ds(poff, SUBLANE), :][SUBLANE - 1:, :], (SUBLANE, cw)), zero)
            spr = jnp.where(row == 0, pr_last, pltpu.roll(s_ref[0, pl.ds(off, SUBLANE), :], 1, 0))
            spi = jnp.where(row == 0, pi_last, pltpu.roll(s_ref[1, pl.ds(off, SUBLANE), :], 1, 0))
            return new + (carry[2] + xr * spr + xi * spi, carry[3] + xi * spr - xr * spi)

        init = (zero, zero, zero, zero) if with_dlam else (zero, zero)
        fin = lax.fori_loop(0, n_t, step, init)
        if with_dlam:
            @pl.when(pl.program_id(1) == 0)
            def _():
                dl_ref[...] = jnp.zeros_like(dl_ref)
            dl_ref[0] += fin[2]
            dl_ref[1] += fin[3]

    blk = pl.BlockSpec((2, seq, cw), lambda j, b: (0, b, j))
    in_specs = [pl.BlockSpec((8, SUBLANE, cw), lambda j, b: (0, 0, j)), blk]
    operands = [coef, x]
    out_specs = [blk]
    out_shape = [jax.ShapeDtypeStruct(x.shape, F32)]
    if with_dlam:
        in_specs.append(blk)
        operands.append(states)
        out_specs.append(pl.BlockSpec((2, SUBLANE, cw), lambda j, b: (0, 0, j)))
        out_shape.append(jax.ShapeDtypeStruct((2, SUBLANE, SSM_COLS), F32))
    res = pl.pallas_call(
        body, name=name, grid=(n_cb, n_seq), in_specs=in_specs, out_specs=out_specs, out_shape=out_shape,
        compiler_params=_params(("parallel", "arbitrary")),
    )(*operands)
    return res if with_dlam else res[0]


def _state_col(j, kk):
    return 2 * j + kk + 2 * (kk // 2)


SSM_WIDE = 4 * SSM_TILE


def ssm_in(name, z, bbd, tm=1024):
    n = z.shape[0]
    tm = _tile(n, tm)
    t, w = SSM_TILE, SSM_WIDE
    return _mm(name, z, bbd, (2, n, SSM_COLS), F32, (n // tm, 2 * SSM_COLS // w, 1),
               pl.BlockSpec((tm, t), lambda i, j, kk: (i, j % 2)),
               pl.BlockSpec((t, w), lambda i, j, kk: (j % 2, j)),
               pl.BlockSpec((None, tm, w), lambda i, j, kk: (j // 2, i, j % 2)), NN)


def ssm_out(name, s, cbd, tm=1024):
    n = s.shape[1]
    tm = _tile(n, tm)
    t = SSM_TILE
    return _mm(name, s, cbd, (n, SSM_WIDTH), F32, (n // tm, SSM_WIDTH // t, 4),
               pl.BlockSpec((None, tm, 512), lambda i, j, kk: (kk // 2, i, 2 * j + kk % 2)),
               pl.BlockSpec((512, t), lambda i, j, kk: (_state_col(j, kk), j)),
               pl.BlockSpec((tm, t), lambda i, j, kk: (i, j)), NN)


def ssm_out_t(name, dy, cbd, tm=1024):
    n = dy.shape[0]
    tm = _tile(n, tm)
    t, w = SSM_TILE, SSM_WIDE
    return _mm(name, dy, cbd, (2, n, SSM_COLS), F32, (n // tm, 2 * SSM_COLS // w, 1),
               pl.BlockSpec((tm, t), lambda i, j, kk: (i, j % 2)),
               pl.BlockSpec((w, t), lambda i, j, kk: (j, j % 2)),
               pl.BlockSpec((None, tm, w), lambda i, j, kk: (j // 2, i, j % 2)), NT)


def ssm_in_t(name, a, bbd, res, tm=1024):
    n = a.shape[1]
    tm = _tile(n, tm)
    t = SSM_TILE
    return _mm(name, a, bbd, (n, SSM_WIDTH), F32, (n // tm, SSM_WIDTH // t, 4),
               pl.BlockSpec((None, tm, 512), lambda i, j, kk: (kk // 2, i, 2 * j + kk % 2)),
               pl.BlockSpec((t, 512), lambda i, j, kk: (j, _state_col(j, kk))),
               pl.BlockSpec((tm, t), lambda i, j, kk: (i, j)), NT, 1.0, res)


def ssm_grad_c(name, s, dy, tk=1024):
    n = s.shape[1]
    tk = _tile(n, tk)
    t, w = SSM_TILE, SSM_WIDE
    return _mm(name, s, dy, (2 * SSM_COLS, t), F32, (2 * SSM_COLS // w, 1, n // tk),
               pl.BlockSpec((None, tk, w), lambda i, j, kk: (i // 2, kk, i % 2)),
               pl.BlockSpec((tk, t), lambda i, j, kk: (kk, i % 2)),
               pl.BlockSpec((w, t), lambda i, j, kk: (i, 0)), TN)


def ssm_grad_b(name, z, a, tk=1024):
    n = z.shape[0]
    tk = _tile(n, tk)
    t, w = SSM_TILE, SSM_WIDE
    return _mm(name, z, a, (t, 2 * SSM_COLS), F32, (1, 2 * SSM_COLS // w, n // tk),
               pl.BlockSpec((tk, t), lambda i, j, kk: (kk, j % 2)),
               pl.BlockSpec((None, tk, w), lambda i, j, kk: (j // 2, kk, j % 2)),
               pl.BlockSpec((t, w), lambda i, j, kk: (0, j)), TN)


_GROUP_TILE = SSM_TILE // SSM_CH


def expand_b(bb_re, bb_im):
    b = jnp.stack([bb_re, bb_im]).reshape(2, SSM_GROUPS, SSM_STATE, SSM_CH)
    eye = jnp.eye(SSM_GROUPS, dtype=F32)
    return jnp.einsum("rgph,gk->ghrkp", b, eye).reshape(SSM_WIDTH, 2 * SSM_COLS).astype(MXU_DTYPE)


def expand_c(c_re, c_im):
    c = jnp.stack([c_re, -c_im])
    eye = jnp.eye(SSM_GROUPS, dtype=F32)
    return jnp.einsum("rghp,gk->rgpkh", c, eye).reshape(2 * SSM_COLS, SSM_WIDTH).astype(MXU_DTYPE)


def _group_pick():
    return (jnp.arange(SSM_GROUPS)[:, None] % _GROUP_TILE == jnp.arange(_GROUP_TILE)[None, :]).astype(F32)


def compact_c(dc):
    x = dc.reshape(2, SSM_GROUPS, SSM_STATE, _GROUP_TILE, SSM_CH)
    g = jnp.einsum("rgpch,gc->rghp", x, _group_pick())
    return g[0], -g[1]


def compact_b(db):
    x = db.reshape(_GROUP_TILE, SSM_CH, 2, SSM_GROUPS, SSM_STATE)
    g = jnp.einsum("chrgp,gc->rgph", x, _group_pick()).reshape(2, SSM_COLS, SSM_CH)
    return g[0], g[1]


def pool_window(name, x, col_block0, seq, out_dtype, adjoint):
    n = x.shape[0]

    def body(x_ref, o_ref):
        win = 2 << pl.program_id(1)
        row = lax.broadcasted_iota(jnp.int32, (seq, POOL_CH), 0)
        v = x_ref[...].astype(F32)
        cnt = jnp.minimum(row + 1, win).astype(F32)
        s = v / cnt if adjoint else v
        for d in (1, 2, 4, 8):
            if adjoint:
                sh = jnp.where((row < seq - d) & (d < win), pltpu.roll(s, seq - d, 0), 0.0)
            else:
                sh = jnp.where((row >= d) & (d < win), pltpu.roll(s, d, 0), 0.0)
            s = s + sh
        o_ref[...] = ((s - v) if adjoint else (s / cnt - v)).astype(out_dtype)

    return pl.pallas_call(
        body, name=name, grid=(n // seq, POOL_GROUPS),
        in_specs=[pl.BlockSpec((seq, POOL_CH), lambda b, g: (b, col_block0 + g))],
        out_specs=pl.BlockSpec((seq, POOL_CH), lambda b, g: (b, g)),
        out_shape=jax.ShapeDtypeStruct((n, POOL_WIDTH), out_dtype),
        compiler_params=_params(("parallel", "parallel")),
    )(x)


def pool_mm(name, q, w, out_dtype, tm=1024):
    n = q.shape[0]
    tm = _tile(n, tm)
    return _mm(name, q, w, (n, POOL_WIDTH), out_dtype, (n // tm, POOL_GROUPS, 1),
               pl.BlockSpec((tm, POOL_CH), lambda i, j, kk: (i, j)),
               pl.BlockSpec((None, POOL_CH, POOL_CH), lambda i, j, kk: (j, 0, 0)),
               pl.BlockSpec((tm, POOL_CH), lambda i, j, kk: (i, j)), NN)


def pool_mm_t(name, dy, col_block0, w, out_dtype, tm=1024):
    n = dy.shape[0]
    tm = _tile(n, tm)
    return _mm(name, dy, w, (n, POOL_WIDTH), out_dtype, (n // tm, POOL_GROUPS, 1),
               pl.BlockSpec((tm, POOL_CH), lambda i, j, kk: (i, col_block0 + j)),
               pl.BlockSpec((None, POOL_CH, POOL_CH), lambda i, j, kk: (j, 0, 0)),
               pl.BlockSpec((tm, POOL_CH), lambda i, j, kk: (i, j)), NT)


def pool_grad_w(name, q, dy, col_block0, tk=1024):
    n = q.shape[0]
    tk = _tile(n, tk)
    return _mm(name, q, dy, (POOL_GROUPS, POOL_CH, POOL_CH), F32, (POOL_GROUPS, 1, n // tk),
               pl.BlockSpec((tk, POOL_CH), lambda i, j, kk: (kk, i)),
               pl.BlockSpec((tk, POOL_CH), lambda i, j, kk: (kk, col_block0 + i)),
               pl.BlockSpec((None, POOL_CH, POOL_CH), lambda i, j, kk: (i, 0, 0)), TN)


def _any_specs(n):
    return [pl.BlockSpec(memory_space=pl.ANY)] * n


def _place():
    x, y, c = lax.axis_index("x"), lax.axis_index("y"), lax.axis_index("c")
    return x, y, c


def _at_axis(ref, axis, start, size):
    return ref.at[(slice(None),) * axis + (pl.ds(start, size),)]


def sibling_swap_halves(name, arrays, axis):
    n = len(arrays)
    halves = [a.shape[axis] // 2 for a in arrays]

    def body(*refs):
        ins, own, got = refs[:n], refs[n:2 * n], refs[2 * n:3 * n]
        send_sems, recv_sems, local_sems = refs[3 * n:]
        x, y, c = _place()
        copies = []
        for i in range(n):
            h = halves[i]
            mine = pltpu.make_async_copy(_at_axis(ins[i], axis, c * h, h), own[i], local_sems.at[i])
            mine.start()
            away = pltpu.make_async_remote_copy(
                src_ref=_at_axis(ins[i], axis, (1 - c) * h, h), dst_ref=got[i],
                send_sem=send_sems.at[i], recv_sem=recv_sems.at[i], device_id=(x, y, 1 - c), device_id_type=MESH_ID)
            away.start()
            copies += [mine, away]
        for cp in copies:
            cp.wait()

    def half_shape(a, h):
        return jax.ShapeDtypeStruct(a.shape[:axis] + (h,) + a.shape[axis + 1:], a.dtype)

    shapes = [half_shape(a, h) for a, h in zip(arrays, halves)]
    res = pl.pallas_call(
        body, name=name, in_specs=_any_specs(n), out_specs=_any_specs(2 * n), out_shape=shapes + shapes,
        scratch_shapes=[pltpu.SemaphoreType.DMA((n,)), pltpu.SemaphoreType.DMA((n,)), pltpu.SemaphoreType.DMA((n,))],
    )(*arrays)
    return res[:n], res[n:]


def sibling_join_halves(name, arrays, axis):
    n = len(arrays)

    def body(*refs):
        ins, outs = refs[:n], refs[n:2 * n]
        send_sems, recv_sems, local_sems = refs[2 * n:]
        x, y, c = _place()
        copies = []
        for i in range(n):
            h = ins[i].shape[axis]
            dst = _at_axis(outs[i], axis, c * h, h)
            mine = pltpu.make_async_copy(ins[i], dst, local_sems.at[i])
            mine.start()
            away = pltpu.make_async_remote_copy(
                src_ref=ins[i], dst_ref=dst, send_sem=send_sems.at[i], recv_sem=recv_sems.at[i],
                device_id=(x, y, 1 - c), device_id_type=MESH_ID)
            away.start()
            copies += [mine, away]
        for cp in copies:
            cp.wait()

    shapes = [jax.ShapeDtypeStruct(a.shape[:axis] + (2 * a.shape[axis],) + a.shape[axis + 1:], a.dtype) for a in arrays]
    return pl.pallas_call(
        body, name=name, in_specs=_any_specs(n), out_specs=_any_specs(n), out_shape=shapes,
        scratch_shapes=[pltpu.SemaphoreType.DMA((n,)), pltpu.SemaphoreType.DMA((n,)), pltpu.SemaphoreType.DMA((n,))],
    )(*arrays)


def sibling_swap(name, arrays):
    n = len(arrays)

    def body(*refs):
        ins, outs = refs[:n], refs[n:2 * n]
        send_sems, recv_sems = refs[2 * n:]
        x, y, c = _place()
        copies = []
        for i in range(n):
            away = pltpu.make_async_remote_copy(
                src_ref=ins[i], dst_ref=outs[i], send_sem=send_sems.at[i], recv_sem=recv_sems.at[i],
                device_id=(x, y, 1 - c), device_id_type=MESH_ID)
            away.start()
            copies.append(away)
        for cp in copies:
            cp.wait()

    return pl.pallas_call(
        body, name=name, in_specs=_any_specs(n), out_specs=_any_specs(n),
        out_shape=[jax.ShapeDtypeStruct(a.shape, a.dtype) for a in arrays],
        scratch_shapes=[pltpu.SemaphoreType.DMA((n,)), pltpu.SemaphoreType.DMA((n,))],
    )(*arrays)


_FLIPS = ((1, 0), (0, 1), (1, 1))


def chip_exchange(name, arrays, axis, all_to_all):
    n = len(arrays)

    def body(*refs):
        ins, outs = refs[:n], refs[n:2 * n]
        send_sems, recv_sems, local_sems = refs[2 * n:]
        x, y, c = _place()
        me = 2 * x + y
        copies = []
        for i in range(n):
            dst = _at_axis(outs[i], axis, me, 1)
            mine = pltpu.make_async_copy(_at_axis(ins[i], axis, me, 1) if all_to_all else ins[i], dst, local_sems.at[i])
            mine.start()
            copies.append(mine)
            for f, (fx, fy) in enumerate(_FLIPS):
                px = (1 - x) if fx else x
                py = (1 - y) if fy else y
                src = _at_axis(ins[i], axis, 2 * px + py, 1) if all_to_all else ins[i]
                away = pltpu.make_async_remote_copy(
                    src_ref=src, dst_ref=dst, send_sem=send_sems.at[3 * i + f], recv_sem=recv_sems.at[3 * i + f],
                    device_id=(px, py, c), device_id_type=MESH_ID)
                away.start()
                copies.append(away)
        for cp in copies:
            cp.wait()

    shapes = [jax.ShapeDtypeStruct(a.shape[:axis] + (N_CHIPS,) + a.shape[axis + 1:], a.dtype) for a in arrays]
    return pl.pallas_call(
        body, name=name, in_specs=_any_specs(n), out_specs=_any_specs(n), out_shape=shapes,
        scratch_shapes=[pltpu.SemaphoreType.DMA((3 * n,)), pltpu.SemaphoreType.DMA((3 * n,)), pltpu.SemaphoreType.DMA((n,))],
    )(*arrays)


SIBLING = ((0, 0, 1),)
CHIPS = ((1, 0, 0), (0, 1, 0), (1, 1, 0))


def _peer(flip):
    x, y, c = _place()
    return tuple((1 - v) if f else v for v, f in zip((x, y, c), flip))


def _core():
    return lax.axis_index("c")


def _chip():
    return 2 * lax.axis_index("x") + lax.axis_index("y")


def _linear_step(grid):
    i = pl.program_id(0)
    for a in range(1, len(grid)):
        i = i * grid[a] + pl.program_id(a)
    return i


def stream_reduce(name, x, grid, block, own_map, send_maps, flips, out_shape, out_block, out_map):
    n_steps = math.prod(grid)
    n_p = len(flips)
    vm_block = block = tuple(1 if d is None else d for d in block)
    out_block = tuple(1 if d is None else d for d in out_block)

    def body(own_ref, *rest):
        send_refs = rest[:n_p]
        o_ref, recv, send_sems, recv_sems, credits = rest[n_p:]
        i = _linear_step(grid)
        s = i % 2
        copies = []
        for j, flip in enumerate(flips):
            @pl.when(i >= 2)
            def _():
                pl.semaphore_wait(credits.at[j, s], 1)
            cp = pltpu.make_async_remote_copy(
                src_ref=send_refs[j], dst_ref=recv.at[j, s], send_sem=send_sems.at[j, s], recv_sem=recv_sems.at[j, s],
                device_id=_peer(flip), device_id_type=MESH_ID)
            cp.start()
            copies.append(cp)
        acc = own_ref[...]
        for j, cp in enumerate(copies):
            cp.wait_recv()
            acc = acc + recv[j, s]
        o_ref[...] = acc.reshape(o_ref.shape)
        for cp in copies:
            cp.wait_send()
        for j, flip in enumerate(flips):
            @pl.when(i < n_steps - 2)
            def _():
                pl.semaphore_signal(credits.at[j, s], inc=1, device_id=_peer(flip), device_id_type=MESH_ID)

    in_specs = [pl.BlockSpec(block, own_map)] + [pl.BlockSpec(block, m) for m in send_maps]
    return pl.pallas_call(
        body, name=name, grid=grid, in_specs=in_specs, out_specs=pl.BlockSpec(out_block, out_map),
        out_shape=jax.ShapeDtypeStruct(out_shape, x.dtype),
        scratch_shapes=[pltpu.VMEM((n_p, 2) + vm_block, x.dtype), pltpu.SemaphoreType.DMA((n_p, 2)),
                        pltpu.SemaphoreType.DMA((n_p, 2)), pltpu.SemaphoreType.REGULAR((n_p, 2))],
        compiler_params=_params(("arbitrary",) * len(grid)),
    )(*([x] * (1 + n_p)))


def stream_gather(name, x, grid, block, in_map, flips, out_shape, out_block, out_map):
    n_p = len(flips)
    assert grid[-1] == n_p + 1
    n_steps = math.prod(grid[:-1])
    vm_block = block = tuple(1 if d is None else d for d in block)
    out_block = tuple(1 if d is None else d for d in out_block)

    def body(x_ref, o_ref, recv, send_sems, recv_sems, credits):
        i = _linear_step(grid[:-1])
        q = pl.program_id(len(grid) - 1)
        s = i % 2

        def copy(j):
            return pltpu.make_async_remote_copy(
                src_ref=x_ref, dst_ref=recv.at[j, s], send_sem=send_sems.at[j, s], recv_sem=recv_sems.at[j, s],
                device_id=_peer(flips[j]), device_id_type=MESH_ID)

        @pl.when(q == 0)
        def _():
            for j in range(n_p):
                @pl.when(i >= 2)
                def _():
                    pl.semaphore_wait(credits.at[j, s], 1)
                copy(j).start()
            o_ref[...] = x_ref[...].reshape(o_ref.shape)
            for j in range(n_p):
                copy(j).wait_send()

        for j in range(n_p):
            @pl.when(q == j + 1)
            def _():
                copy(j).wait_recv()
                o_ref[...] = recv[j, s].reshape(o_ref.shape)

                @pl.when(i < n_steps - 2)
                def _():
                    pl.semaphore_signal(credits.at[j, s], inc=1, device_id=_peer(flips[j]), device_id_type=MESH_ID)

    return pl.pallas_call(
        body, name=name, grid=grid, in_specs=[pl.BlockSpec(block, in_map)], out_specs=pl.BlockSpec(out_block, out_map),
        out_shape=jax.ShapeDtypeStruct(out_shape, x.dtype),
        scratch_shapes=[pltpu.VMEM((n_p, 2) + vm_block, x.dtype), pltpu.SemaphoreType.DMA((n_p, 2)),
                        pltpu.SemaphoreType.DMA((n_p, 2)), pltpu.SemaphoreType.REGULAR((n_p, 2))],
        compiler_params=_params(("arbitrary",) * len(grid)),
    )(x)


def _chip_of_substep(q):
    mask = jnp.where(q == 1, 2, jnp.where(q == 2, 1, jnp.where(q == 3, 3, 0)))
    return jnp.bitwise_xor(_chip(), mask)


def gather_weight(name, shard):
    nl, r, c = shard.shape
    tr = _rows_tile(r, c, budget=1024 * 1024 * 4 // shard.dtype.itemsize, step=16)
    return stream_gather(
        name, shard, (nl, r // tr, N_CHIPS), (None, tr, c), lambda l, i, q: (l, i, 0), CHIPS,
        (nl, N_CHIPS, r, c), (None, None, tr, c), lambda l, i, q: (l, _chip_of_substep(q), i, 0))


def reduce_scatter_streamed(name, g):
    nl, _, r, c = g.shape
    r2 = r // 2
    tr = _rows_tile(r2, c)
    nb = r2 // tr
    blk4 = (None, None, tr, c)
    chip_sum = stream_reduce(
        name + "_cores", g, (nl, N_CHIPS, nb), blk4,
        lambda l, k, i: (l, k, _core() * nb + i, 0), [lambda l, k, i: (l, k, (1 - _core()) * nb + i, 0)], SIBLING,
        (nl, N_CHIPS, r2, c), blk4, lambda l, k, i: (l, k, i, 0))
    masks = (2, 1, 3)
    mine = stream_reduce(
        name + "_chips", chip_sum, (nl, nb), blk4,
        lambda l, i: (l, _chip(), i, 0),
        [functools.partial(lambda l, i, m: (l, jnp.bitwise_xor(_chip(), m), i, 0), m=m) for m in masks], CHIPS,
        (nl, r2, c), (None, tr, c), lambda l, i: (l, i, 0))
    both = stream_gather(
        name + "_join", mine, (nl, nb, 2), (None, tr, c), lambda l, i, q: (l, i, 0), SIBLING,
        (nl, 2, r2, c), (None, None, tr, c), lambda l, i, q: (l, _core() + q - 2 * _core() * q, i, 0))
    return both.reshape(nl, r, c)


def add2(name, a, b):
    shape = a.shape
    a2, b2 = a.reshape(-1, shape[-1]), b.reshape(-1, shape[-1])
    rows, w = a2.shape
    tm = _rows_tile(rows, w)
    return _rowwise(name, lambda u, v: u + v, rows, tm, [(a2, w, 0), (b2, w, 0)], [], [(w, F32)])[0].reshape(shape)


def _rows_tile(rows, width, budget=2 * 1024 * 1024, step=SUBLANE):
    best = step
    t = step
    while t <= rows:
        if rows % t == 0 and t * width * 4 <= budget:
            best = t
        t += step
    return best


def sum_slots(name, a):
    nl, _, r, c = a.shape
    tr = _rows_tile(r, c)

    def body(s0, s1, s2, s3, o_ref):
        o_ref[...] = ((s0[...] + s1[...]) + s2[...]) + s3[...]

    specs = [pl.BlockSpec((None, None, tr, c), functools.partial(lambda l, i, k: (l, k, i, 0), k=k)) for k in range(N_CHIPS)]
    return pl.pallas_call(
        body, name=name, grid=(nl, r // tr), in_specs=specs,
        out_specs=pl.BlockSpec((None, tr, c), lambda l, i: (l, i, 0)),
        out_shape=jax.ShapeDtypeStruct((nl, r, c), F32),
        compiler_params=_params(("parallel", "parallel")),
    )(a, a, a, a)


def reduce_scatter_big(grads):
    own, got = sibling_swap_halves("rs_swap_halves", grads, 2)
    chip_sum = [add2("rs_add_cores", a, b) for a, b in zip(own, got)]
    spread = chip_exchange("rs_chips", chip_sum, 1, True)
    mine = [sum_slots("rs_sum_chips", a) for a in spread]
    return sibling_join_halves("rs_join_halves", mine, 1)


def all_reduce_small(flat):
    other = sibling_swap("ar_swap", [flat])[0]
    chip = add2("ar_add_cores", flat, other)
    slots = chip_exchange("ar_chips", [chip.reshape((1,) + chip.shape)], 0, False)[0]
    rows = flat.shape[0]
    tm = _rows_tile(rows, LANE)
    nb = rows // tm
    s2 = slots.reshape(N_CHIPS * rows, LANE)
    return _rowwise("ar_sum_chips", lambda a, b, c, d: ((a + b) + c) + d, rows, tm,
                    [(s2, LANE, 0, k * nb) for k in range(N_CHIPS)], [], [(LANE, F32)])[0]


def _adamw_math(w, g, m, v):
    m = ADAM_B1 * m + (1.0 - ADAM_B1) * g
    v = ADAM_B2 * v + (1.0 - ADAM_B2) * (g * g)
    m_hat = m / (1.0 - ADAM_B1 ** ADAM_STEP)
    v_hat = v / (1.0 - ADAM_B2 ** ADAM_STEP)
    delta = -ADAM_LR * (m_hat / (jnp.sqrt(v_hat) + ADAM_EPS) + ADAM_WD * w)
    return delta, m, v


def adamw(name, w, g, m, v):
    shape = w.shape
    width = shape[-1]
    flat = [t.reshape(-1, width) for t in (w, g, m, v)]
    rows = flat[0].shape[0]
    tm = _rows_tile(rows, width, budget=1024 * 1024)
    res = _rowwise(name, _adamw_math, rows, tm, [(t, width, 0) for t in flat], [], [(width, F32)] * 3)
    return [r.reshape(shape) for r in res]


def _ffn_fwd(tag, h, g_norm, wi, wo):
    n, d = h.shape
    ff = wo.shape[0]
    xn = norm_fwd(tag + "_norm", h, g_norm)
    gu = mm_nn_colsharded(tag + "_wi", xn, wi, BF16)
    act = _rowwise(tag + "_swiglu", lambda g, u: (g.astype(F32) * jax.nn.sigmoid(g.astype(F32))) * u.astype(F32),
                   n, 512, [(gu, ff, 0), (gu, ff, 1)], [], [(ff, BF16)])[0]
    out = mm_nn(tag + "_wo", act, wo, F32, alpha=0.5, res=h, tm=512, tk=2816)
    return out, (xn, gu, act)


def _ffn_bwd(tag, d, h, g_norm, wi_t, wo_t, saved):
    xn, gu, act = saved
    n, dm = h.shape
    ff = wo_t.shape[1]
    dact = mm_nn(tag + "_dact", d, wo_t, BF16, alpha=0.5, tn=1408)
    dwo = mm_tn(tag + "_dwo", act, d, F32, alpha=0.5, tm=1408)

    def swiglu_bwd(g, u, da):
        g, u, da = g.astype(F32), u.astype(F32), da.astype(F32)
        sg = jax.nn.sigmoid(g)
        dg = da * u * (sg * (1.0 + g * (1.0 - sg)))
        du = da * (g * sg)
        return jnp.concatenate([dg, du], axis=1)

    dgu = _rowwise(tag + "_dswiglu", swiglu_bwd, n, 512, [(gu, ff, 0), (gu, ff, 1), (dact, ff, 0)], [], [(2 * ff, BF16)])[0]
    dwi = mm_tn_colsharded(tag + "_dwi", xn, dgu, F32)
    dxn = mm_nn(tag + "_dxn", dgu, wi_t, F32, tk=1408)
    d_in, dg_norm = norm_bwd(tag + "_dnorm", dxn, h, g_norm, d)
    return d_in, dg_norm.reshape(dm), dwi, dwo.reshape(N_CHIPS, ff // N_CHIPS, dm)


def _col(v):
    return v.reshape(SSM_COLS, 1)


def _layer_fwd(h, lw, p_l, seq):
    n, d = h.shape
    h1, ffn1_saved = _ffn_fwd("ffn1", h, lw["ffn1_norm"], lw["ffn1_wi"], lw["ffn1_wo"])

    xn2 = norm_fwd("mix_norm", h1, lw["mix_norm"])
    z = mm_nn("mix_in", xn2, lw["w_in"], F32)
    log_dt = jnp.repeat(lw["ssm_log_dt"], SSM_STATE)
    b_re, b_im = lw["ssm_b_re"].reshape(SSM_COLS, SSM_CH), lw["ssm_b_im"].reshape(SSM_COLS, SSM_CH)
    pw_re, pw_im, bb_re, bb_im = ssm_prep("ssm_prep", _col(lw["ssm_lambda_re"]), _col(lw["ssm_lambda_im"]), _col(log_dt), b_re, b_im)
    bbd = expand_b(bb_re, bb_im)
    cbd = expand_c(lw["ssm_c_re"], lw["ssm_c_im"])
    bu = ssm_in("ssm_in", z, bbd)
    s = ssm_scan("ssm_scan", scan_coefficients(pw_re, pw_im, False), bu, seq, False)
    y0c = ssm_out("ssm_out", s, cbd)

    def skip_gelu(yc, zs, dvec):
        y0 = yc + dvec * zs
        return y0, _gelu(y0)

    y0, y1 = _rowwise("ssm_gelu", skip_gelu, n, 512, [(y0c, SSM_WIDTH, 0), (z, SSM_WIDTH, 0)],
                      [lw["ssm_d"].reshape(1, SSM_WIDTH)], [(SSM_WIDTH, F32), (SSM_WIDTH, F32)])
    t = mm_nn("ssm_glu_mm", y1, lw["ssm_w_glu"], F32)
    y2 = _rowwise("ssm_glu", lambda a, b: a * jax.nn.sigmoid(b), n, 512, [(y1, SSM_WIDTH, 0), (t, SSM_WIDTH, 0)], [],
                  [(SSM_WIDTH, BF16)])[0]

    q = pool_window("pool_window", z, SSM_WIDTH // POOL_CH, seq, BF16, False)
    wp_eff = lw["pool_w"] * lw["pool_scale"].reshape(POOL_GROUPS, 1, POOL_CH)
    yp = pool_mm("pool_mm", q, wp_eff, BF16)
    m = jnp.concatenate([y2, yp], axis=1)
    h2 = mm_nn("mix_out", m, lw["w_out"], F32, res=h1)

    h3, ffn2_saved = _ffn_fwd("ffn2", h2, lw["ffn2_norm"], lw["ffn2_wi"], lw["ffn2_wo"])

    xn4 = norm_fwd("ple_norm", h3, lw["ple_norm"])
    tg = mm_nn("ple_gate", xn4, lw["ple_w_gate"], F32)
    e = mm_nn_colsharded("ple_proj", p_l, lw["ple_w_proj"], F32)
    h4 = _rowwise("ple_add", lambda a, b, c: a + jax.nn.sigmoid(b) * c, n, 512, [(h3, d, 0), (tg, d, 0), (e, d, 0)], [], [(d, F32)])[0]
    saved = dict(h=h, h1=h1, h2=h2, h3=h3, ffn1=ffn1_saved, ffn2=ffn2_saved, xn2=xn2, z=z, s=s, y0=y0, y1=y1, t=t, m=m, q=q,
                 xn4=xn4, tg=tg, e=e, pw_re=pw_re, pw_im=pw_im, bbd=bbd, cbd=cbd)
    return h4, saved


def _layer_bwd(d, lw, p_l, sv, seq):
    n, dm = d.shape
    g = {}
    def ple_bwd(dd, tg, e):
        gate = jax.nn.sigmoid(tg)
        return dd * e * gate * (1.0 - gate), dd * gate

    dtg, de = _rowwise("ple_dadd", ple_bwd, n, 512, [(d, dm, 0), (sv["tg"], dm, 0), (sv["e"], dm, 0)], [], [(dm, BF16), (dm, BF16)])
    g["ple_w_proj"] = mm_tn_colsharded("ple_dproj", p_l, de, F32)
    g["ple_w_gate"] = mm_tn("ple_dgate_w", sv["xn4"], dtg, F32).reshape(N_CHIPS, dm // N_CHIPS, dm)
    dxn4 = mm_nn("ple_dgate_x", dtg, lw["ple_w_gate_t"], F32)
    d, dg = norm_bwd("ple_dnorm", dxn4, sv["h3"], lw["ple_norm"], d)
    g["ple_norm"] = dg.reshape(dm)

    d, g["ffn2_norm"], g["ffn2_wi"], g["ffn2_wo"] = _ffn_bwd("ffn2b", d, sv["h2"], lw["ffn2_norm"], lw["ffn2_wi_t"], lw["ffn2_wo_t"], sv["ffn2"])

    dmix = mm_nn("mix_dout_x", d, lw["w_out_t"], F32)
    g["w_out"] = mm_tn("mix_dout_w", sv["m"], d, F32).reshape(N_CHIPS, dm // N_CHIPS, dm)
    pool_cb = SSM_WIDTH // POOL_CH
    wp_eff = lw["pool_w"] * lw["pool_scale"].reshape(POOL_GROUPS, 1, POOL_CH)
    dq = pool_mm_t("pool_dmm_x", dmix, pool_cb, wp_eff, F32)
    dwp_eff = pool_grad_w("pool_dmm_w", sv["q"], dmix, pool_cb)
    g["pool_w"] = dwp_eff * lw["pool_scale"].reshape(POOL_GROUPS, 1, POOL_CH)
    g["pool_scale"] = jnp.sum(dwp_eff * lw["pool_w"], axis=1).reshape(POOL_WIDTH)
    dzp = pool_window("pool_dwindow", dq, 0, seq, BF16, True)

    def glu_bwd(dy2, y1, t):
        sg = jax.nn.sigmoid(t)
        return dy2 * y1 * sg * (1.0 - sg), dy2 * sg

    dt_, dy1a = _rowwise("ssm_dglu", glu_bwd, n, 512, [(dmix, SSM_WIDTH, 0), (sv["y1"], SSM_WIDTH, 0), (sv["t"], SSM_WIDTH, 0)], [],
                         [(SSM_WIDTH, BF16), (SSM_WIDTH, F32)])
    g["ssm_w_glu"] = mm_tn("ssm_dglu_w", sv["y1"], dt_, F32).reshape(N_CHIPS, SSM_WIDTH // N_CHIPS, SSM_WIDTH)
    dy1b = mm_nn("ssm_dglu_x", dt_, lw["ssm_w_glu_t"], F32)

    def gelu_bwd(da, db, y0, zs, dvec):
        dy0 = (da + db) * _gelu_grad(y0)
        return dy0, dy0 * dvec, jnp.sum(dy0 * zs, axis=0, keepdims=True)

    dy0, dzs_a, dd = _rowwise("ssm_dgelu", gelu_bwd, n, 512,
                              [(dy1a, SSM_WIDTH, 0), (dy1b, SSM_WIDTH, 0), (sv["y0"], SSM_WIDTH, 0), (sv["z"], SSM_WIDTH, 0)],
                              [lw["ssm_d"].reshape(1, SSM_WIDTH)], [(SSM_WIDTH, F32), (SSM_WIDTH, F32)], [(1, SSM_WIDTH)])
    g["ssm_d"] = dd.reshape(SSM_WIDTH)
    g["ssm_c_re"], g["ssm_c_im"] = compact_c(ssm_grad_c("ssm_dc", sv["s"], dy0))
    v = ssm_out_t("ssm_dout", dy0, sv["cbd"])
    a, dlam = ssm_scan("ssm_scan_adj", scan_coefficients(sv["pw_re"], sv["pw_im"], True), v, seq, True, states=sv["s"])
    dbb_re, dbb_im = compact_b(ssm_grad_b("ssm_db", sv["z"], a))
    dzs = ssm_in_t("ssm_din", a, sv["bbd"], dzs_a)
    dlam = jnp.sum(dlam, axis=1)
    log_dt = jnp.repeat(lw["ssm_log_dt"], SSM_STATE)
    b_re, b_im = lw["ssm_b_re"].reshape(SSM_COLS, SSM_CH), lw["ssm_b_im"].reshape(SSM_COLS, SSM_CH)
    glr, gli, gld, gbr, gbi = ssm_prep_bwd("ssm_prep_bwd", _col(lw["ssm_lambda_re"]), _col(lw["ssm_lambda_im"]), _col(log_dt), b_re, b_im,
                                           _col(dlam[0]), _col(dlam[1]), dbb_re, dbb_im)
    g["ssm_lambda_re"] = glr.reshape(SSM_GROUPS, SSM_STATE)
    g["ssm_lambda_im"] = gli.reshape(SSM_GROUPS, SSM_STATE)
    g["ssm_log_dt"] = jnp.sum(gld.reshape(SSM_GROUPS, SSM_STATE), axis=1)
    g["ssm_b_re"] = gbr.reshape(SSM_GROUPS, SSM_STATE, SSM_CH)
    g["ssm_b_im"] = gbi.reshape(SSM_GROUPS, SSM_STATE, SSM_CH)

    dz = jnp.concatenate([dzs.astype(BF16), dzp], axis=1)
    g["w_in"] = mm_tn("mix_din_w", sv["xn2"], dz, F32).reshape(N_CHIPS, dm // N_CHIPS, dm)
    dxn2 = mm_nn("mix_din_x", dz, lw["w_in_t"], F32)
    d, dg = norm_bwd("mix_dnorm", dxn2, sv["h1"], lw["mix_norm"], d)
    g["mix_norm"] = dg.reshape(dm)

    d, g["ffn1_norm"], g["ffn1_wi"], g["ffn1_wo"] = _ffn_bwd("ffn1b", d, sv["h"], lw["ffn1_norm"], lw["ffn1_wi_t"], lw["ffn1_wo_t"], sv["ffn1"])
    return d, g


def _flatten_small(tensors):
    flat = jnp.concatenate([t.reshape(-1) for t in tensors])
    pad = (-flat.shape[0]) % (SUBLANE * LANE)
    return jnp.pad(flat, (0, pad)).reshape(-1, LANE)


def _unflatten_small(flat, like):
    flat = flat.reshape(-1)
    out, off = [], 0
    for t in like:
        out.append(flat[off:off + t.size].reshape(t.shape))
        off += t.size
    return out


def kernel(x, p, ffn1_norm, ffn1_wi, ffn1_wo, mix_norm, w_in, ssm_lambda_re, ssm_lambda_im, ssm_log_dt, ssm_b_re, ssm_b_im, ssm_c_re, ssm_c_im, ssm_d, ssm_w_glu, pool_w, pool_scale, w_out, ffn2_norm, ffn2_wi, ffn2_wo, ple_norm, ple_w_gate, ple_w_proj, final_norm, loss_target, m_ffn1_norm, m_ffn1_wi, m_ffn1_wo, m_mix_norm, m_w_in, m_ssm_lambda_re, m_ssm_lambda_im, m_ssm_log_dt, m_ssm_b_re, m_ssm_b_im, m_ssm_c_re, m_ssm_c_im, m_ssm_d, m_ssm_w_glu, m_pool_w, m_pool_scale, m_w_out, m_ffn2_norm, m_ffn2_wi, m_ffn2_wo, m_ple_norm, m_ple_w_gate, m_ple_w_proj, m_final_norm, v_ffn1_norm, v_ffn1_wi, v_ffn1_wo, v_mix_norm, v_w_in, v_ssm_lambda_re, v_ssm_lambda_im, v_ssm_log_dt, v_ssm_b_re, v_ssm_b_im, v_ssm_c_re, v_ssm_c_im, v_ssm_d, v_ssm_w_glu, v_pool_w, v_pool_scale, v_w_out, v_ffn2_norm, v_ffn2_wi, v_ffn2_wo, v_ple_norm, v_ple_w_gate, v_ple_w_proj, v_final_norm):
    given = dict(locals())
    w = {k: given[k] for k in WEIGHTS}
    mom = {k: given["m_" + k] for k in WEIGHTS}
    var = {k: given["v_" + k] for k in WEIGHTS}
    bsz, seq, dm = x.shape
    n = bsz * seq
    depth = ffn1_wi.shape[0]

    full = {}
    for k in BIG:
        a = gather_weight("gather_" + k, w[k].astype(MXU_DTYPE))
        if k in COL_SHARDED:
            full[k] = a
            if k != "ple_w_proj":
                full[k + "_t"] = transpose_colsharded(a)
        else:
            full[k] = a.reshape(depth, N_CHIPS * a.shape[2], a.shape[3])
            full[k + "_t"] = jnp.swapaxes(full[k], 1, 2)
    for k in SMALL:
        if k != "final_norm":
            full[k] = w[k]
    layers = [{k: v[l] for k, v in full.items()} for l in range(depth)]

    p2 = p.reshape(depth, n, p.shape[-1])
    h_last = x.reshape(n, dm)
    saved = []
    for l in range(depth):
        h_last, sv = _layer_fwd(h_last, layers[l], p2[l], seq)
        saved.append(sv)

    def head(hh, tgt, gf):
        r = lax.rsqrt(jnp.mean(hh * hh, axis=-1, keepdims=True) + NORM_EPS)
        xh = hh * r
        diff = xh * gf - tgt
        dy = diff * (1.0 / dm)
        dxh = dy * gf
        dx = r * (dxh - xh * jnp.mean(dxh * xh, axis=-1, keepdims=True))
        return dx, jnp.sum(diff * diff, axis=0, keepdims=True) * (0.5 / dm), jnp.sum(dy * xh, axis=0, keepdims=True)

    d_last, loss_cols, g_final = _rowwise("loss_head", head, n, 512, [(h_last, dm, 0), (loss_target.reshape(n, dm), dm, 0)],
                                          [final_norm.reshape(1, dm)], [(dm, F32)], [(1, dm), (1, dm)])
    loss = lax.psum(jnp.sum(loss_cols), ("x", "y", "c"))

    d_x = d_last
    layer_grads = [None] * depth
    for l in reversed(range(depth)):
        d_x, layer_grads[l] = _layer_bwd(d_x, layers[l], p2[l], saved[l], seq)
    grads = {k: jnp.stack([g[k] for g in layer_grads]) for k in layer_grads[0]}
    grad_x = d_x.reshape(bsz, seq, dm)

    big_sum = [reduce_scatter_streamed("rs_" + k, grads[k]) for k in BIG]
    small_keys = [k for k in SMALL]
    small_parts = [grads[k] if k != "final_norm" else g_final.reshape(dm) for k in small_keys]
    small_sum = _unflatten_small(all_reduce_small(_flatten_small(small_parts)), small_parts)
    g_out = dict(zip(BIG, big_sum))
    g_out.update(dict(zip(small_keys, small_sum)))
    for k in BIG:
        g_out[k] = g_out[k].reshape(w[k].shape)

    delta, new_m, new_v = {}, {}, {}
    for k in BIG:
        delta[k], new_m[k], new_v[k] = adamw("adamw_" + k, w[k], g_out[k], mom[k], var[k])
    sw = adamw("adamw_small", *[_flatten_small([t[k] for k in small_keys]) for t in (w, g_out, mom, var)])
    for name, flat in zip((delta, new_m, new_v), sw):
        for k, t in zip(small_keys, _unflatten_small(flat, [w[k] for k in small_keys])):
            name[k] = t

    return (loss, grad_x, *[g_out[k] for k in WEIGHTS], *[delta[k] for k in WEIGHTS],
            *[new_m[k] for k in WEIGHTS], *[new_v[k] for k in WEIGHTS])
```

```python
import functools
import math

import jax
import jax.numpy as jnp
from jax import lax
from jax.experimental import pallas as pl
from jax.experimental.pallas import tpu as pltpu

F32 = jnp.float32
BF16 = jnp.bfloat16
MXU_DTYPE = jnp.bfloat16
GRAD_WIRE_DTYPE = jnp.bfloat16
VMEM_LIMIT = 56 * 1024 * 1024
LANE = 128
SUBLANE = 8

N_CHIPS = 4
SSM_GROUPS = 32
SSM_STATE = 64
SSM_CH = 16
SSM_WIDTH = SSM_GROUPS * SSM_CH
SSM_COLS = SSM_GROUPS * SSM_STATE
POOL_GROUPS = 4
POOL_CH = 128
POOL_WIDTH = POOL_GROUPS * POOL_CH
SSM_TILE = 256
NORM_EPS = 1e-6
ADAM_LR = 0.001
ADAM_B1 = 0.9
ADAM_B2 = 0.999
ADAM_EPS = 1e-08
ADAM_WD = 0.01
ADAM_STEP = 10
MESH_ID = pl.DeviceIdType.MESH

BIG = ("ffn1_wi", "ffn1_wo", "w_in", "ssm_w_glu", "w_out", "ffn2_wi", "ffn2_wo", "ple_w_gate", "ple_w_proj")
COL_SHARDED = ("ffn1_wi", "ffn2_wi", "ple_w_proj")
SMALL = ("ffn1_norm", "mix_norm", "ssm_lambda_re", "ssm_lambda_im", "ssm_log_dt", "ssm_b_re", "ssm_b_im",
         "ssm_c_re", "ssm_c_im", "ssm_d", "pool_w", "pool_scale", "ffn2_norm", "ple_norm", "final_norm")
WEIGHTS = ("ffn1_norm", "ffn1_wi", "ffn1_wo", "mix_norm", "w_in", "ssm_lambda_re", "ssm_lambda_im", "ssm_log_dt",
           "ssm_b_re", "ssm_b_im", "ssm_c_re", "ssm_c_im", "ssm_d", "ssm_w_glu", "pool_w", "pool_scale", "w_out",
           "ffn2_norm", "ffn2_wi", "ffn2_wo", "ple_norm", "ple_w_gate", "ple_w_proj", "final_norm")


def _tile(dim, target):
    best = None
    t = LANE
    while t <= min(dim, target):
        if dim % t == 0:
            best = t
        t += LANE
    return best if best is not None else dim


def _params(sem):
    return pltpu.CompilerParams(dimension_semantics=sem, vmem_limit_bytes=VMEM_LIMIT)


def _mm(name, a, b, out_shape, out_dtype, grid, a_spec, b_spec, o_spec, contract, alpha=1.0, res=None):
    n_k = grid[2]
    acc_shape = tuple(d for d in o_spec.block_shape if d is not None)

    def body(*refs):
        a_ref, b_ref = refs[0], refs[1]
        r_ref = refs[2] if res is not None else None
        o_ref = refs[3] if res is not None else refs[2]

        def product():
            return lax.dot_general(a_ref[...].astype(MXU_DTYPE), b_ref[...].astype(MXU_DTYPE),
                                   (contract, ((), ())), preferred_element_type=F32)

        def finish(v):
            if alpha != 1.0:
                v = v * alpha
            if r_ref is not None:
                v = v + r_ref[...].astype(F32)
            o_ref[...] = v.astype(out_dtype)

        if n_k == 1:
            finish(product())
            return
        acc = refs[-1]
        k = pl.program_id(2)

        @pl.when(k == 0)
        def _():
            acc[...] = product()

        @pl.when(k > 0)
        def _():
            acc[...] += product()

        @pl.when(k == n_k - 1)
        def _():
            finish(acc[...])

    in_specs = [a_spec, b_spec]
    operands = [a, b]
    if res is not None:
        in_specs.append(o_spec)
        operands.append(res)
    return pl.pallas_call(
        body, name=name, grid=grid, in_specs=in_specs, out_specs=o_spec,
        out_shape=jax.ShapeDtypeStruct(out_shape, out_dtype),
        scratch_shapes=[pltpu.VMEM(acc_shape, F32)] if n_k > 1 else [],
        compiler_params=_params(("parallel", "parallel", "arbitrary")),
    )(*operands)


NN = ((1,), (0,))
NT = ((1,), (1,))
TN = ((0,), (0,))


def mm_nn(name, a, b, out_dtype, alpha=1.0, res=None, tm=1024, tn=1024, tk=1024):
    m, k = a.shape
    n = b.shape[1]
    tm, tn, tk = _tile(m, tm), _tile(n, tn), _tile(k, tk)
    return _mm(name, a, b, (m, n), out_dtype, (m // tm, n // tn, k // tk),
               pl.BlockSpec((tm, tk), lambda i, j, kk: (i, kk)),
               pl.BlockSpec((tk, tn), lambda i, j, kk: (kk, j)),
               pl.BlockSpec((tm, tn), lambda i, j, kk: (i, j)), NN, alpha, res)


def mm_nt(name, a, b, out_dtype, alpha=1.0, res=None, tm=1024, tn=512, tk=512):
    m, k = a.shape
    n = b.shape[0]
    tm, tn, tk = _tile(m, tm), _tile(n, tn), _tile(k, tk)
    return _mm(name, a, b, (m, n), out_dtype, (m // tm, n // tn, k // tk),
               pl.BlockSpec((tm, tk), lambda i, j, kk: (i, kk)),
               pl.BlockSpec((tn, tk), lambda i, j, kk: (j, kk)),
               pl.BlockSpec((tm, tn), lambda i, j, kk: (i, j)), NT, alpha, res)


def mm_tn(name, a, b, out_dtype, alpha=1.0, tm=1024, tn=1024, tk=1024):
    k, m = a.shape
    n = b.shape[1]
    tm, tn, tk = _tile(m, tm), _tile(n, tn), _tile(k, tk)
    return _mm(name, a, b, (m, n), out_dtype, (m // tm, n // tn, k // tk),
               pl.BlockSpec((tk, tm), lambda i, j, kk: (kk, i)),
               pl.BlockSpec((tk, tn), lambda i, j, kk: (kk, j)),
               pl.BlockSpec((tm, tn), lambda i, j, kk: (i, j)), TN, alpha)


def mm_nn_colsharded(name, a, w, out_dtype, tm=1024, tk=1024):
    m, k = a.shape
    c = w.shape[2]
    tm, tk = _tile(m, tm), _tile(k, tk)
    return _mm(name, a, w, (m, N_CHIPS * c), out_dtype, (m // tm, N_CHIPS, k // tk),
               pl.BlockSpec((tm, tk), lambda i, j, kk: (i, kk)),
               pl.BlockSpec((None, tk, c), lambda i, j, kk: (j, kk, 0)),
               pl.BlockSpec((tm, c), lambda i, j, kk: (i, j)), NN)


def transpose_colsharded(w):
    nl, _, k, c = w.shape
    return jnp.swapaxes(w, 2, 3).reshape(nl, N_CHIPS * c, k)


def mm_tn_colsharded(name, a, b, out_dtype, tm=1024, tk=1024):
    t, k = a.shape
    c = b.shape[1] // N_CHIPS
    tm, tk = _tile(k, tm), _tile(t, tk)
    return _mm(name, a, b, (N_CHIPS, k, c), out_dtype, (k // tm, N_CHIPS, t // tk),
               pl.BlockSpec((tk, tm), lambda i, j, kk: (kk, i)),
               pl.BlockSpec((tk, c), lambda i, j, kk: (kk, j)),
               pl.BlockSpec((None, tm, c), lambda i, j, kk: (j, i, 0)), TN)


def _rowwise(name, fn, n_rows, tm, row_ins, bcast_ins, outs, accs=()):
    tm = min(tm, n_rows)
    grid = (n_rows // tm,)
    n_row, n_b, n_out = len(row_ins), len(bcast_ins), len(outs)

    def body(*refs):
        ins = [r[...] for r in refs[:n_row + n_b]]
        out_refs = refs[n_row + n_b:n_row + n_b + n_out]
        acc_refs = refs[n_row + n_b + n_out:]
        res = fn(*ins)
        if not isinstance(res, (tuple, list)):
            res = (res,)
        for o_ref, v in zip(out_refs, res[:n_out]):
            o_ref[...] = v.astype(o_ref.dtype)
        if acc_refs:
            @pl.when(pl.program_id(0) == 0)
            def _():
                for a_ref in acc_refs:
                    a_ref[...] = jnp.zeros_like(a_ref)
            for a_ref, v in zip(acc_refs, res[n_out:]):
                a_ref[...] += v

    in_specs, operands = [], []
    for spec in row_ins:
        arr, width, cb = spec[0], spec[1], spec[2]
        rb = spec[3] if len(spec) > 3 else 0
        in_specs.append(pl.BlockSpec((tm, width), functools.partial(lambda i, cb, rb: (i + rb, cb), cb=cb, rb=rb)))
        operands.append(arr)
    for arr in bcast_ins:
        in_specs.append(pl.BlockSpec(arr.shape, functools.partial(lambda i, nd: (0,) * nd, nd=arr.ndim)))
        operands.append(arr)
    out_specs = [pl.BlockSpec((tm, w), lambda i: (i, 0)) for w, _ in outs]
    out_specs += [pl.BlockSpec((r, w), lambda i: (0, 0)) for r, w in accs]
    out_shape = [jax.ShapeDtypeStruct((n_rows, w), dt) for w, dt in outs]
    out_shape += [jax.ShapeDtypeStruct((r, w), F32) for r, w in accs]
    res = pl.pallas_call(
        body, name=name, grid=grid, in_specs=in_specs, out_specs=out_specs, out_shape=out_shape,
        compiler_params=_params(("arbitrary",) if accs else ("parallel",)),
    )(*operands)
    return res


def _rms(x, g):
    r = lax.rsqrt(jnp.mean(x * x, axis=-1, keepdims=True) + NORM_EPS)
    return x * r * g


def _rms_bwd(dy, x, g):
    r = lax.rsqrt(jnp.mean(x * x, axis=-1, keepdims=True) + NORM_EPS)
    xh = x * r
    dxh = dy * g
    dx = r * (dxh - xh * jnp.mean(dxh * xh, axis=-1, keepdims=True))
    return dx, jnp.sum(dy * xh, axis=0, keepdims=True)


def norm_fwd(name, h, g):
    n, d = h.shape
    return _rowwise(name, lambda x, gg: _rms(x, gg), n, 512, [(h, d, 0)], [g.reshape(1, d)], [(d, BF16)])[0]


def norm_bwd(name, dxn, h, g, d_res):
    n, d = h.shape

    def fn(dy, x, dr, gg):
        dx, dg = _rms_bwd(dy, x, gg)
        return dr + dx, dg

    return _rowwise(name, fn, n, 512, [(dxn, d, 0), (h, d, 0), (d_res, d, 0)], [g.reshape(1, d)], [(d, F32)], [(1, d)])


_GELU_C = math.sqrt(2.0 / math.pi)


def _gelu(x):
    return 0.5 * x * (1.0 + jnp.tanh(_GELU_C * (x + 0.044715 * (x * x * x))))


def _gelu_grad(x):
    th = jnp.tanh(_GELU_C * (x + 0.044715 * (x * x * x)))
    return 0.5 * (1.0 + th) + 0.5 * x * (1.0 - th * th) * (_GELU_C * (1.0 + 3.0 * 0.044715 * (x * x)))


def _ssm_discretize(lam_re, lam_im, log_dt, b_re, b_im):
    dt = jnp.exp(log_dt)
    e = jnp.exp(lam_re * dt)
    lb_re = e * jnp.cos(lam_im * dt)
    lb_im = e * jnp.sin(lam_im * dt)
    nr, ni = lb_re - 1.0, lb_im
    den = lam_re * lam_re + lam_im * lam_im
    cr = (nr * lam_re + ni * lam_im) / den
    ci = (ni * lam_re - nr * lam_im) / den
    return lb_re, lb_im, cr * b_re - ci * b_im, cr * b_im + ci * b_re


def ssm_prep(name, lam_re, lam_im, log_dt, b_re, b_im):
    def body(lr, li, ld, br, bi, pr_ref, pi_ref, bbr_ref, bbi_ref):
        lb_re, lb_im, bb_re, bb_im = _ssm_discretize(lr[...], li[...], ld[...], br[...], bi[...])
        bbr_ref[...] = bb_re
        bbi_ref[...] = bb_im
        pr, pi = lb_re, lb_im
        cols_r, cols_i = [pr], [pi]
        for _ in range(SUBLANE - 1):
            pr, pi = pr * lb_re - pi * lb_im, pr * lb_im + pi * lb_re
            cols_r.append(pr)
            cols_i.append(pi)
        lane = lax.broadcasted_iota(jnp.int32, (SSM_COLS, SUBLANE), 1)
        out_r = jnp.zeros((SSM_COLS, SUBLANE), F32)
        out_i = jnp.zeros((SSM_COLS, SUBLANE), F32)
        for r in range(SUBLANE):
            out_r = jnp.where(lane == r, cols_r[r], out_r)
            out_i = jnp.where(lane == r, cols_i[r], out_i)
        pr_ref[...] = out_r
        pi_ref[...] = out_i

    shapes = [jax.ShapeDtypeStruct((SSM_COLS, SUBLANE), F32)] * 2 + [jax.ShapeDtypeStruct((SSM_COLS, SSM_CH), F32)] * 2
    return pl.pallas_call(body, name=name, out_shape=shapes,
                          compiler_params=pltpu.CompilerParams(vmem_limit_bytes=VMEM_LIMIT))(lam_re, lam_im, log_dt, b_re, b_im)


def ssm_prep_bwd(name, lam_re, lam_im, log_dt, b_re, b_im, d_lb_re, d_lb_im, d_bb_re, d_bb_im):
    def body(lr, li, ld, br, bi, g0, g1, g2, g3, o0, o1, o2, o3, o4):
        _, vjp = jax.vjp(_ssm_discretize, lr[...], li[...], ld[...], br[...], bi[...])
        res = vjp((g0[...], g1[...], g2[...], g3[...]))
        for o, v in zip((o0, o1, o2, o3, o4), res):
            o[...] = v

    col = jax.ShapeDtypeStruct((SSM_COLS, 1), F32)
    mat = jax.ShapeDtypeStruct((SSM_COLS, SSM_CH), F32)
    return pl.pallas_call(body, name=name, out_shape=[col, col, col, mat, mat],
                          compiler_params=pltpu.CompilerParams(vmem_limit_bytes=VMEM_LIMIT))(
        lam_re, lam_im, log_dt, b_re, b_im, d_lb_re, d_lb_im, d_bb_re, d_bb_im)


def scan_coefficients(pw_re, pw_im, reverse):
    pr, pi = pw_re.T, pw_im.T
    if reverse:
        pi = -pi
    row = jnp.arange(SUBLANE)[:, None]
    out = []
    for d in (1, 2, 4):
        valid = (row < SUBLANE - d) if reverse else (row >= d)
        out.append(jnp.where(valid, pr[d - 1][None, :], 0.0))
        out.append(jnp.where(valid, pi[d - 1][None, :], 0.0))
    out.append(pr[::-1] if reverse else pr)
    out.append(pi[::-1] if reverse else pi)
    return jnp.stack(out)


def ssm_scan(name, coef, x, seq, reverse, states=None):
    n = x.shape[1]
    n_seq = n // seq
    cw = LANE
    n_cb = SSM_COLS // cw
    n_t = seq // SUBLANE
    with_dlam = states is not None
    unroll = 4 if n_t % 4 == 0 else 1

    def body(*refs):
        if with_dlam:
            coef_ref, x_ref, s_ref, o_ref, dl_ref = refs
        else:
            coef_ref, x_ref, o_ref = refs
        c = [coef_ref[i] for i in range(8)]
        row = lax.broadcasted_iota(jnp.int32, (SUBLANE, cw), 0)
        zero = jnp.zeros((SUBLANE, cw), F32)

        edge = 0 if reverse else SUBLANE - 1

        def bcast_row(v, r):
            return jnp.broadcast_to(v[r:r + 1, :], (SUBLANE, cw))

        p8r, p8i = bcast_row(c[6], edge), bcast_row(c[7], edge)

        def local_scan(off):
            xr = x_ref[0, pl.ds(off, SUBLANE), :]
            xi = x_ref[1, pl.ds(off, SUBLANE), :]
            for si, d in enumerate((1, 2, 4)):
                sh = (SUBLANE - d) if reverse else d
                sr, sm = pltpu.roll(xr, sh, 0), pltpu.roll(xi, sh, 0)
                lre, lim = c[2 * si], c[2 * si + 1]
                xr, xi = xr + lre * sr - lim * sm, xi + lre * sm + lim * sr
            return xr, xi

        def step(it, carry):
            tiles = []
            for u in range(unroll):
                k = it * unroll + u
                r = (n_t - 1 - k) if reverse else k
                off = pl.multiple_of(r * SUBLANE, SUBLANE)
                tiles.append((r, off) + local_scan(off))
            cre, cim = carry[0], carry[1]
            acc = carry[2:]
            for r, off, lr, li in tiles:
                xr, xi = lr + c[6] * cre - c[7] * cim, li + c[6] * cim + c[7] * cre
                o_ref[0, pl.ds(off, SUBLANE), :] = xr
                o_ref[1, pl.ds(off, SUBLANE), :] = xi
                cre, cim = (bcast_row(lr, edge) + p8r * cre - p8i * cim, bcast_row(li, edge) + p8r * cim + p8i * cre)
                if with_dlam:
                    poff = pl.multiple_of(jnp.maximum(r - 1, 0) * SUBLANE, SUBLANE)
                    first = r > 0
                    pr_last = jnp.where(first, bcast_row(s_ref[0, pl.ds(poff, SUBLANE), :], SUBLANE - 1), zero)
                    pi_last = jnp.where(first, bcast_row(s_ref[1, pl.ds(poff, SUBLANE), :], SUBLANE - 1), zero)
                    spr = jnp.where(row == 0, pr_last, pltpu.roll(s_ref[0, pl.ds(off, SUBLANE), :], 1, 0))
                    spi = jnp.where(row == 0, pi_last, pltpu.roll(s_ref[1, pl.ds(off, SUBLANE), :], 1, 0))
                    acc = (acc[0] + xr * spr + xi * spi, acc[1] + xi * spr - xr * spi)
            return (cre, cim) + tuple(acc)

        init = (zero, zero, zero, zero) if with_dlam else (zero, zero)
        fin = lax.fori_loop(0, n_t // unroll, step, init)
        if with_dlam:
            @pl.when(pl.program_id(1) == 0)
            def _():
                dl_ref[...] = jnp.zeros_like(dl_ref)
            dl_ref[0] += fin[2]
            dl_ref[1] += fin[3]

    blk = pl.BlockSpec((2, seq, cw), lambda j, b: (0, b, j))
    in_specs = [pl.BlockSpec((8, SUBLANE, cw), lambda j, b: (0, 0, j)), blk]
    operands = [coef, x]
    out_specs = [blk]
    out_shape = [jax.ShapeDtypeStruct(x.shape, F32)]
    if with_dlam:
        in_specs.append(blk)
        operands.append(states)
        out_specs.append(pl.BlockSpec((2, SUBLANE, cw), lambda j, b: (0, 0, j)))
        out_shape.append(jax.ShapeDtypeStruct((2, SUBLANE, SSM_COLS), F32))
    res = pl.pallas_call(
        body, name=name, grid=(n_cb, n_seq), in_specs=in_specs, out_specs=out_specs, out_shape=out_shape,
        compiler_params=_params(("parallel", "arbitrary")),
    )(*operands)
    return res if with_dlam else res[0]


def _state_col(j, kk):
    return 2 * j + kk + 2 * (kk // 2)


SSM_WIDE = 4 * SSM_TILE


def ssm_in(name, z, bbd, tm=1024):
    n = z.shape[0]
    tm = _tile(n, tm)
    t, w = SSM_TILE, SSM_WIDE
    return _mm(name, z, bbd, (2, n, SSM_COLS), F32, (n // tm, 2 * SSM_COLS // w, 1),
               pl.BlockSpec((tm, t), lambda i, j, kk: (i, j % 2)),
               pl.BlockSpec((t, w), lambda i, j, kk: (j % 2, j)),
               pl.BlockSpec((None, tm, w), lambda i, j, kk: (j // 2, i, j % 2)), NN)


def ssm_out(name, s, cbd, tm=1024):
    n = s.shape[1]
    tm = _tile(n, tm)
    t = SSM_TILE
    return _mm(name, s, cbd, (n, SSM_WIDTH), F32, (n // tm, SSM_WIDTH // t, 4),
               pl.BlockSpec((None, tm, 512), lambda i, j, kk: (kk // 2, i, 2 * j + kk % 2)),
               pl.BlockSpec((512, t), lambda i, j, kk: (_state_col(j, kk), j)),
               pl.BlockSpec((tm, t), lambda i, j, kk: (i, j)), NN)


def ssm_out_t(name, dy, cbd, tm=1024):
    n = dy.shape[0]
    tm = _tile(n, tm)
    t, w = SSM_TILE, SSM_WIDE
    return _mm(name, dy, cbd, (2, n, SSM_COLS), F32, (n // tm, 2 * SSM_COLS // w, 1),
               pl.BlockSpec((tm, t), lambda i, j, kk: (i, j % 2)),
               pl.BlockSpec((w, t), lambda i, j, kk: (j, j % 2)),
               pl.BlockSpec((None, tm, w), lambda i, j, kk: (j // 2, i, j % 2)), NT)


def ssm_in_t(name, a, bbd, res, tm=1024):
    n = a.shape[1]
    tm = _tile(n, tm)
    t = SSM_TILE
    return _mm(name, a, bbd, (n, SSM_WIDTH), F32, (n // tm, SSM_WIDTH // t, 4),
               pl.BlockSpec((None, tm, 512), lambda i, j, kk: (kk // 2, i, 2 * j + kk % 2)),
               pl.BlockSpec((t, 512), lambda i, j, kk: (j, _state_col(j, kk))),
               pl.BlockSpec((tm, t), lambda i, j, kk: (i, j)), NT, 1.0, res)


def ssm_grad_c(name, s, dy, tk=1024):
    n = s.shape[1]
    tk = _tile(n, tk)
    t, w = SSM_TILE, SSM_WIDE
    return _mm(name, s, dy, (2 * SSM_COLS, t), F32, (2 * SSM_COLS // w, 1, n // tk),
               pl.BlockSpec((None, tk, w), lambda i, j, kk: (i // 2, kk, i % 2)),
               pl.BlockSpec((tk, t), lambda i, j, kk: (kk, i % 2)),
               pl.BlockSpec((w, t), lambda i, j, kk: (i, 0)), TN)


def ssm_grad_b(name, z, a, tk=1024):
    n = z.shape[0]
    tk = _tile(n, tk)
    t, w = SSM_TILE, SSM_WIDE
    return _mm(name, z, a, (t, 2 * SSM_COLS), F32, (1, 2 * SSM_COLS // w, n // tk),
               pl.BlockSpec((tk, t), lambda i, j, kk: (kk, j % 2)),
               pl.BlockSpec((None, tk, w), lambda i, j, kk: (j // 2, kk, j % 2)),
               pl.BlockSpec((t, w), lambda i, j, kk: (0, j)), TN)


_GROUP_TILE = SSM_TILE // SSM_CH


def expand_b(bb_re, bb_im):
    b = jnp.stack([bb_re, bb_im]).reshape(2, SSM_GROUPS, SSM_STATE, SSM_CH)
    eye = jnp.eye(SSM_GROUPS, dtype=F32)
    return jnp.einsum("rgph,gk->ghrkp", b, eye).reshape(SSM_WIDTH, 2 * SSM_COLS).astype(MXU_DTYPE)


def expand_c(c_re, c_im):
    c = jnp.stack([c_re, -c_im])
    eye = jnp.eye(SSM_GROUPS, dtype=F32)
    return jnp.einsum("rghp,gk->rgpkh", c, eye).reshape(2 * SSM_COLS, SSM_WIDTH).astype(MXU_DTYPE)


def _group_pick():
    return (jnp.arange(SSM_GROUPS)[:, None] % _GROUP_TILE == jnp.arange(_GROUP_TILE)[None, :]).astype(F32)


def compact_c(dc):
    x = dc.reshape(2, SSM_GROUPS, SSM_STATE, _GROUP_TILE, SSM_CH)
    g = jnp.einsum("rgpch,gc->rghp", x, _group_pick())
    return g[0], -g[1]


def compact_b(db):
    x = db.reshape(_GROUP_TILE, SSM_CH, 2, SSM_GROUPS, SSM_STATE)
    g = jnp.einsum("chrgp,gc->rgph", x, _group_pick()).reshape(2, SSM_COLS, SSM_CH)
    return g[0], g[1]


def pool_window(name, x, col_block0, seq, out_dtype, adjoint):
    n = x.shape[0]

    def body(x_ref, o_ref):
        win = 2 << pl.program_id(1)
        row = lax.broadcasted_iota(jnp.int32, (seq, POOL_CH), 0)
        v = x_ref[...].astype(F32)
        cnt = jnp.minimum(row + 1, win).astype(F32)
        s = v / cnt if adjoint else v
        for d in (1, 2, 4, 8):
            if adjoint:
                sh = jnp.where((row < seq - d) & (d < win), pltpu.roll(s, seq - d, 0), 0.0)
            else:
                sh = jnp.where((row >= d) & (d < win), pltpu.roll(s, d, 0), 0.0)
            s = s + sh
        o_ref[...] = ((s - v) if adjoint else (s / cnt - v)).astype(out_dtype)

    return pl.pallas_call(
        body, name=name, grid=(n // seq, POOL_GROUPS),
        in_specs=[pl.BlockSpec((seq, POOL_CH), lambda b, g: (b, col_block0 + g))],
        out_specs=pl.BlockSpec((seq, POOL_CH), lambda b, g: (b, g)),
        out_shape=jax.ShapeDtypeStruct((n, POOL_WIDTH), out_dtype),
        compiler_params=_params(("parallel", "parallel")),
    )(x)


def pool_mm(name, q, w, out_dtype, tm=1024):
    n = q.shape[0]
    tm = _tile(n, tm)
    return _mm(name, q, w, (n, POOL_WIDTH), out_dtype, (n // tm, POOL_GROUPS, 1),
               pl.BlockSpec((tm, POOL_CH), lambda i, j, kk: (i, j)),
               pl.BlockSpec((None, POOL_CH, POOL_CH), lambda i, j, kk: (j, 0, 0)),
               pl.BlockSpec((tm, POOL_CH), lambda i, j, kk: (i, j)), NN)


def pool_mm_t(name, dy, col_block0, w, out_dtype, tm=1024):
    n = dy.shape[0]
    tm = _tile(n, tm)
    return _mm(name, dy, w, (n, POOL_WIDTH), out_dtype, (n // tm, POOL_GROUPS, 1),
               pl.BlockSpec((tm, POOL_CH), lambda i, j, kk: (i, col_block0 + j)),
               pl.BlockSpec((None, POOL_CH, POOL_CH), lambda i, j, kk: (j, 0, 0)),
               pl.BlockSpec((tm, POOL_CH), lambda i, j, kk: (i, j)), NT)


def pool_grad_w(name, q, dy, col_block0, tk=1024):
    n = q.shape[0]
    tk = _tile(n, tk)
    return _mm(name, q, dy, (POOL_GROUPS, POOL_CH, POOL_CH), F32, (POOL_GROUPS, 1, n // tk),
               pl.BlockSpec((tk, POOL_CH), lambda i, j, kk: (kk, i)),
               pl.BlockSpec((tk, POOL_CH), lambda i, j, kk: (kk, col_block0 + i)),
               pl.BlockSpec((None, POOL_CH, POOL_CH), lambda i, j, kk: (i, 0, 0)), TN)


def _any_specs(n):
    return [pl.BlockSpec(memory_space=pl.ANY)] * n


def _place():
    x, y, c = lax.axis_index("x"), lax.axis_index("y"), lax.axis_index("c")
    return x, y, c


def _at_axis(ref, axis, start, size):
    return ref.at[(slice(None),) * axis + (pl.ds(start, size),)]


def sibling_swap_halves(name, arrays, axis):
    n = len(arrays)
    halves = [a.shape[axis] // 2 for a in arrays]

    def body(*refs):
        ins, own, got = refs[:n], refs[n:2 * n], refs[2 * n:3 * n]
        send_sems, recv_sems, local_sems = refs[3 * n:]
        x, y, c = _place()
        copies = []
        for i in range(n):
            h = halves[i]
            mine = pltpu.make_async_copy(_at_axis(ins[i], axis, c * h, h), own[i], local_sems.at[i])
            mine.start()
            away = pltpu.make_async_remote_copy(
                src_ref=_at_axis(ins[i], axis, (1 - c) * h, h), dst_ref=got[i],
                send_sem=send_sems.at[i], recv_sem=recv_sems.at[i], device_id=(x, y, 1 - c), device_id_type=MESH_ID)
            away.start()
            copies += [mine, away]
        for cp in copies:
            cp.wait()

    def half_shape(a, h):
        return jax.ShapeDtypeStruct(a.shape[:axis] + (h,) + a.shape[axis + 1:], a.dtype)

    shapes = [half_shape(a, h) for a, h in zip(arrays, halves)]
    res = pl.pallas_call(
        body, name=name, in_specs=_any_specs(n), out_specs=_any_specs(2 * n), out_shape=shapes + shapes,
        scratch_shapes=[pltpu.SemaphoreType.DMA((n,)), pltpu.SemaphoreType.DMA((n,)), pltpu.SemaphoreType.DMA((n,))],
    )(*arrays)
    return res[:n], res[n:]


def sibling_join_halves(name, arrays, axis):
    n = len(arrays)

    def body(*refs):
        ins, outs = refs[:n], refs[n:2 * n]
        send_sems, recv_sems, local_sems = refs[2 * n:]
        x, y, c = _place()
        copies = []
        for i in range(n):
            h = ins[i].shape[axis]
            dst = _at_axis(outs[i], axis, c * h, h)
            mine = pltpu.make_async_copy(ins[i], dst, local_sems.at[i])
            mine.start()
            away = pltpu.make_async_remote_copy(
                src_ref=ins[i], dst_ref=dst, send_sem=send_sems.at[i], recv_sem=recv_sems.at[i],
                device_id=(x, y, 1 - c), device_id_type=MESH_ID)
            away.start()
            copies += [mine, away]
        for cp in copies:
            cp.wait()

    shapes = [jax.ShapeDtypeStruct(a.shape[:axis] + (2 * a.shape[axis],) + a.shape[axis + 1:], a.dtype) for a in arrays]
    return pl.pallas_call(
        body, name=name, in_specs=_any_specs(n), out_specs=_any_specs(n), out_shape=shapes,
        scratch_shapes=[pltpu.SemaphoreType.DMA((n,)), pltpu.SemaphoreType.DMA((n,)), pltpu.SemaphoreType.DMA((n,))],
    )(*arrays)


def sibling_swap(name, arrays):
    n = len(arrays)

    def body(*refs):
        ins, outs = refs[:n], refs[n:2 * n]
        send_sems, recv_sems = refs[2 * n:]
        x, y, c = _place()
        copies = []
        for i in range(n):
            away = pltpu.make_async_remote_copy(
                src_ref=ins[i], dst_ref=outs[i], send_sem=send_sems.at[i], recv_sem=recv_sems.at[i],
                device_id=(x, y, 1 - c), device_id_type=MESH_ID)
            away.start()
            copies.append(away)
        for cp in copies:
            cp.wait()

    return pl.pallas_call(
        body, name=name, in_specs=_any_specs(n), out_specs=_any_specs(n),
        out_shape=[jax.ShapeDtypeStruct(a.shape, a.dtype) for a in arrays],
        scratch_shapes=[pltpu.SemaphoreType.DMA((n,)), pltpu.SemaphoreType.DMA((n,))],
    )(*arrays)


_FLIPS = ((1, 0), (0, 1), (1, 1))


def chip_exchange(name, arrays, axis, all_to_all):
    n = len(arrays)

    def body(*refs):
        ins, outs = refs[:n], refs[n:2 * n]
        send_sems, recv_sems, local_sems = refs[2 * n:]
        x, y, c = _place()
        me = 2 * x + y
        copies = []
        for i in range(n):
            dst = _at_axis(outs[i], axis, me, 1)
            mine = pltpu.make_async_copy(_at_axis(ins[i], axis, me, 1) if all_to_all else ins[i], dst, local_sems.at[i])
            mine.start()
            copies.append(mine)
            for f, (fx, fy) in enumerate(_FLIPS):
                px = (1 - x) if fx else x
                py = (1 - y) if fy else y
                src = _at_axis(ins[i], axis, 2 * px + py, 1) if all_to_all else ins[i]
                away = pltpu.make_async_remote_copy(
                    src_ref=src, dst_ref=dst, send_sem=send_sems.at[3 * i + f], recv_sem=recv_sems.at[3 * i + f],
                    device_id=(px, py, c), device_id_type=MESH_ID)
                away.start()
                copies.append(away)
        for cp in copies:
            cp.wait()

    shapes = [jax.ShapeDtypeStruct(a.shape[:axis] + (N_CHIPS,) + a.shape[axis + 1:], a.dtype) for a in arrays]
    return pl.pallas_call(
        body, name=name, in_specs=_any_specs(n), out_specs=_any_specs(n), out_shape=shapes,
        scratch_shapes=[pltpu.SemaphoreType.DMA((3 * n,)), pltpu.SemaphoreType.DMA((3 * n,)), pltpu.SemaphoreType.DMA((n,))],
    )(*arrays)


SIBLING = ((0, 0, 1),)
CHIPS = ((1, 0, 0), (0, 1, 0), (1, 1, 0))


def _peer(flip):
    x, y, c = _place()
    return tuple((1 - v) if f else v for v, f in zip((x, y, c), flip))


def _core():
    return lax.axis_index("c")


def _chip():
    return 2 * lax.axis_index("x") + lax.axis_index("y")


def _linear_step(grid):
    i = pl.program_id(0)
    for a in range(1, len(grid)):
        i = i * grid[a] + pl.program_id(a)
    return i


def stream_reduce(name, x, grid, block, own_map, send_maps, flips, out_shape, out_block, out_map, wire_dtype=None):
    n_steps = math.prod(grid)
    n_p = len(flips)
    vm_block = block = tuple(1 if d is None else d for d in block)
    out_block = tuple(1 if d is None else d for d in out_block)
    staged = wire_dtype is not None and wire_dtype != x.dtype
    slot_dtype = wire_dtype if staged else x.dtype

    def body(own_ref, *rest):
        send_refs = rest[:n_p]
        o_ref, recv, send_sems, recv_sems, credits = rest[n_p:n_p + 5]
        stage = rest[n_p + 5] if staged else None
        i = _linear_step(grid)
        s = i % 2
        copies = []
        for j, flip in enumerate(flips):
            src = send_refs[j]
            if staged:
                stage[j, s] = send_refs[j][...].astype(wire_dtype)
                src = stage.at[j, s]

            @pl.when(i >= 2)
            def _():
                pl.semaphore_wait(credits.at[j, s], 1)
            cp = pltpu.make_async_remote_copy(
                src_ref=src, dst_ref=recv.at[j, s], send_sem=send_sems.at[j, s], recv_sem=recv_sems.at[j, s],
                device_id=_peer(flip), device_id_type=MESH_ID)
            cp.start()
            copies.append(cp)
        acc = own_ref[...]
        for j, cp in enumerate(copies):
            cp.wait_recv()
            acc = acc + recv[j, s].astype(acc.dtype)
        o_ref[...] = acc.reshape(o_ref.shape)
        for cp in copies:
            cp.wait_send()
        for j, flip in enumerate(flips):
            @pl.when(i < n_steps - 2)
            def _():
                pl.semaphore_signal(credits.at[j, s], inc=1, device_id=_peer(flip), device_id_type=MESH_ID)

    in_specs = [pl.BlockSpec(block, own_map)] + [pl.BlockSpec(block, m) for m in send_maps]
    scratch = [pltpu.VMEM((n_p, 2) + vm_block, slot_dtype), pltpu.SemaphoreType.DMA((n_p, 2)),
               pltpu.SemaphoreType.DMA((n_p, 2)), pltpu.SemaphoreType.REGULAR((n_p, 2))]
    if staged:
        scratch.append(pltpu.VMEM((n_p, 2) + vm_block, slot_dtype))
    return pl.pallas_call(
        body, name=name, grid=grid, in_specs=in_specs, out_specs=pl.BlockSpec(out_block, out_map),
        out_shape=jax.ShapeDtypeStruct(out_shape, x.dtype), scratch_shapes=scratch,
        compiler_params=_params(("arbitrary",) * len(grid)),
    )(*([x] * (1 + n_p)))


def stream_gather(name, x, grid, block, in_map, flips, out_shape, out_block, out_map):
    n_p = len(flips)
    assert grid[-1] == n_p + 1
    n_steps = math.prod(grid[:-1])
    vm_block = block = tuple(1 if d is None else d for d in block)
    out_block = tuple(1 if d is None else d for d in out_block)

    def body(x_ref, o_ref, recv, send_sems, recv_sems, credits):
        i = _linear_step(grid[:-1])
        q = pl.program_id(len(grid) - 1)
        s = i % 2

        def copy(j):
            return pltpu.make_async_remote_copy(
                src_ref=x_ref, dst_ref=recv.at[j, s], send_sem=send_sems.at[j, s], recv_sem=recv_sems.at[j, s],
                device_id=_peer(flips[j]), device_id_type=MESH_ID)

        @pl.when(q == 0)
        def _():
            for j in range(n_p):
                @pl.when(i >= 2)
                def _():
                    pl.semaphore_wait(credits.at[j, s], 1)
                copy(j).start()
            o_ref[...] = x_ref[...].reshape(o_ref.shape)
            for j in range(n_p):
                copy(j).wait_send()

        for j in range(n_p):
            @pl.when(q == j + 1)
            def _():
                copy(j).wait_recv()
                o_ref[...] = recv[j, s].reshape(o_ref.shape)

                @pl.when(i < n_steps - 2)
                def _():
                    pl.semaphore_signal(credits.at[j, s], inc=1, device_id=_peer(flips[j]), device_id_type=MESH_ID)

    return pl.pallas_call(
        body, name=name, grid=grid, in_specs=[pl.BlockSpec(block, in_map)], out_specs=pl.BlockSpec(out_block, out_map),
        out_shape=jax.ShapeDtypeStruct(out_shape, x.dtype),
        scratch_shapes=[pltpu.VMEM((n_p, 2) + vm_block, x.dtype), pltpu.SemaphoreType.DMA((n_p, 2)),
                        pltpu.SemaphoreType.DMA((n_p, 2)), pltpu.SemaphoreType.REGULAR((n_p, 2))],
        compiler_params=_params(("arbitrary",) * len(grid)),
    )(x)


def _chip_of_substep(q):
    mask = jnp.where(q == 1, 2, jnp.where(q == 2, 1, jnp.where(q == 3, 3, 0)))
    return jnp.bitwise_xor(_chip(), mask)


def gather_weight(name, shard):
    nl, r, c = shard.shape
    r2 = r // 2
    tr = _rows_tile(r2, c, budget=1024 * 1024 * 4 // shard.dtype.itemsize, step=16)
    nb = r2 // tr
    half = stream_gather(
        name + "_chips", shard, (nl, nb, N_CHIPS), (None, tr, c), lambda l, i, q: (l, _core() * nb + i, 0), CHIPS,
        (nl, N_CHIPS, r2, c), (None, None, tr, c), lambda l, i, q: (l, _chip_of_substep(q), i, 0))
    both = stream_gather(
        name + "_cores", half, (nl, N_CHIPS, nb, 2), (None, None, tr, c), lambda l, k, i, q: (l, k, i, 0), SIBLING,
        (nl, N_CHIPS, 2, r2, c), (None, None, None, tr, c), lambda l, k, i, q: (l, k, _core() + q - 2 * _core() * q, i, 0))
    return both.reshape(nl, N_CHIPS, r, c)


def reduce_scatter_streamed(name, g):
    nl, _, r, c = g.shape
    r2 = r // 2
    tr = _rows_tile(r2, c, step=16)
    nb = r2 // tr
    blk4 = (None, None, tr, c)
    chip_sum = stream_reduce(
        name + "_cores", g, (nl, N_CHIPS, nb), blk4,
        lambda l, k, i: (l, k, _core() * nb + i, 0), [lambda l, k, i: (l, k, (1 - _core()) * nb + i, 0)], SIBLING,
        (nl, N_CHIPS, r2, c), blk4, lambda l, k, i: (l, k, i, 0))
    masks = (2, 1, 3)
    mine = stream_reduce(
        name + "_chips", chip_sum, (nl, nb), blk4,
        lambda l, i: (l, _chip(), i, 0),
        [functools.partial(lambda l, i, m: (l, jnp.bitwise_xor(_chip(), m), i, 0), m=m) for m in masks], CHIPS,
        (nl, r2, c), (None, tr, c), lambda l, i: (l, i, 0), wire_dtype=GRAD_WIRE_DTYPE)
    both = stream_gather(
        name + "_join", mine, (nl, nb, 2), (None, tr, c), lambda l, i, q: (l, i, 0), SIBLING,
        (nl, 2, r2, c), (None, None, tr, c), lambda l, i, q: (l, _core() + q - 2 * _core() * q, i, 0))
    return both.reshape(nl, r, c)


def add2(name, a, b):
    shape = a.shape
    a2, b2 = a.reshape(-1, shape[-1]), b.reshape(-1, shape[-1])
    rows, w = a2.shape
    tm = _rows_tile(rows, w)
    return _rowwise(name, lambda u, v: u + v, rows, tm, [(a2, w, 0), (b2, w, 0)], [], [(w, F32)])[0].reshape(shape)


def _rows_tile(rows, width, budget=2 * 1024 * 1024, step=SUBLANE):
    best = step
    t = step
    while t <= rows:
        if rows % t == 0 and t * width * 4 <= budget:
            best = t
        t += step
    return best


def sum_slots(name, a):
    nl, _, r, c = a.shape
    tr = _rows_tile(r, c)

    def body(s0, s1, s2, s3, o_ref):
        o_ref[...] = ((s0[...] + s1[...]) + s2[...]) + s3[...]

    specs = [pl.BlockSpec((None, None, tr, c), functools.partial(lambda l, i, k: (l, k, i, 0), k=k)) for k in range(N_CHIPS)]
    return pl.pallas_call(
        body, name=name, grid=(nl, r // tr), in_specs=specs,
        out_specs=pl.BlockSpec((None, tr, c), lambda l, i: (l, i, 0)),
        out_shape=jax.ShapeDtypeStruct((nl, r, c), F32),
        compiler_params=_params(("parallel", "parallel")),
    )(a, a, a, a)


def reduce_scatter_big(grads):
    own, got = sibling_swap_halves("rs_swap_halves", grads, 2)
    chip_sum = [add2("rs_add_cores", a, b) for a, b in zip(own, got)]
    spread = chip_exchange("rs_chips", chip_sum, 1, True)
    mine = [sum_slots("rs_sum_chips", a) for a in spread]
    return sibling_join_halves("rs_join_halves", mine, 1)


def all_reduce_small(flat):
    other = sibling_swap("ar_swap", [flat])[0]
    chip = add2("ar_add_cores", flat, other)
    slots = chip_exchange("ar_chips", [chip.reshape((1,) + chip.shape)], 0, False)[0]
    rows = flat.shape[0]
    tm = _rows_tile(rows, LANE)
    nb = rows // tm
    s2 = slots.reshape(N_CHIPS * rows, LANE)
    return _rowwise("ar_sum_chips", lambda a, b, c, d: ((a + b) + c) + d, rows, tm,
                    [(s2, LANE, 0, k * nb) for k in range(N_CHIPS)], [], [(LANE, F32)])[0]


def _adamw_math(w, g, m, v):
    m = ADAM_B1 * m + (1.0 - ADAM_B1) * g
    v = ADAM_B2 * v + (1.0 - ADAM_B2) * (g * g)
    m_hat = m / (1.0 - ADAM_B1 ** ADAM_STEP)
    v_hat = v / (1.0 - ADAM_B2 ** ADAM_STEP)
    delta = -ADAM_LR * (m_hat / (jnp.sqrt(v_hat) + ADAM_EPS) + ADAM_WD * w)
    return delta, m, v


def adamw(name, w, g, m, v):
    shape = w.shape
    width = shape[-1]
    flat = [t.reshape(-1, width) for t in (w, g, m, v)]
    rows = flat[0].shape[0]
    tm = _rows_tile(rows, width, budget=1024 * 1024)
    res = _rowwise(name, _adamw_math, rows, tm, [(t, width, 0) for t in flat], [], [(width, F32)] * 3)
    return [r.reshape(shape) for r in res]


def _ffn_fwd(tag, h, g_norm, wi, wo):
    n, d = h.shape
    ff = wo.shape[0]
    xn = norm_fwd(tag + "_norm", h, g_norm)
    gu = mm_nn_colsharded(tag + "_wi", xn, wi, BF16)
    act = _rowwise(tag + "_swiglu", lambda g, u: (g.astype(F32) * jax.nn.sigmoid(g.astype(F32))) * u.astype(F32),
                   n, 512, [(gu, ff, 0), (gu, ff, 1)], [], [(ff, BF16)])[0]
    out = mm_nn(tag + "_wo", act, wo, F32, alpha=0.5, res=h, tm=512, tk=2816)
    return out, (xn, gu, act)


def _ffn_bwd(tag, d, h, g_norm, wi_t, wo_t, saved):
    xn, gu, act = saved
    n, dm = h.shape
    ff = wo_t.shape[1]
    dact = mm_nn(tag + "_dact", d, wo_t, BF16, alpha=0.5, tn=1408)
    dwo = mm_tn(tag + "_dwo", act, d, F32, alpha=0.5, tm=1408)

    def swiglu_bwd(g, u, da):
        g, u, da = g.astype(F32), u.astype(F32), da.astype(F32)
        sg = jax.nn.sigmoid(g)
        dg = da * u * (sg * (1.0 + g * (1.0 - sg)))
        du = da * (g * sg)
        return jnp.concatenate([dg, du], axis=1)

    dgu = _rowwise(tag + "_dswiglu", swiglu_bwd, n, 512, [(gu, ff, 0), (gu, ff, 1), (dact, ff, 0)], [], [(2 * ff, BF16)])[0]
    dwi = mm_tn_colsharded(tag + "_dwi", xn, dgu, F32)
    dxn = mm_nn(tag + "_dxn", dgu, wi_t, F32, tk=1408)
    d_in, dg_norm = norm_bwd(tag + "_dnorm", dxn, h, g_norm, d)
    return d_in, dg_norm.reshape(dm), dwi, dwo.reshape(N_CHIPS, ff // N_CHIPS, dm)


def _col(v):
    return v.reshape(SSM_COLS, 1)


def _layer_fwd(h, lw, p_l, seq):
    n, d = h.shape
    h1, ffn1_saved = _ffn_fwd("ffn1", h, lw["ffn1_norm"], lw["ffn1_wi"], lw["ffn1_wo"])

    xn2 = norm_fwd("mix_norm", h1, lw["mix_norm"])
    z = mm_nn("mix_in", xn2, lw["w_in"], F32)
    log_dt = jnp.repeat(lw["ssm_log_dt"], SSM_STATE)
    b_re, b_im = lw["ssm_b_re"].reshape(SSM_COLS, SSM_CH), lw["ssm_b_im"].reshape(SSM_COLS, SSM_CH)
    pw_re, pw_im, bb_re, bb_im = ssm_prep("ssm_prep", _col(lw["ssm_lambda_re"]), _col(lw["ssm_lambda_im"]), _col(log_dt), b_re, b_im)
    bbd = expand_b(bb_re, bb_im)
    cbd = expand_c(lw["ssm_c_re"], lw["ssm_c_im"])
    bu = ssm_in("ssm_in", z, bbd)
    s = ssm_scan("ssm_scan", scan_coefficients(pw_re, pw_im, False), bu, seq, False)
    y0c = ssm_out("ssm_out", s, cbd)

    def skip_gelu(yc, zs, dvec):
        y0 = yc + dvec * zs
        return y0, _gelu(y0)

    y0, y1 = _rowwise("ssm_gelu", skip_gelu, n, 512, [(y0c, SSM_WIDTH, 0), (z, SSM_WIDTH, 0)],
                      [lw["ssm_d"].reshape(1, SSM_WIDTH)], [(SSM_WIDTH, F32), (SSM_WIDTH, F32)])
    t = mm_nn("ssm_glu_mm", y1, lw["ssm_w_glu"], F32)
    y2 = _rowwise("ssm_glu", lambda a, b: a * jax.nn.sigmoid(b), n, 512, [(y1, SSM_WIDTH, 0), (t, SSM_WIDTH, 0)], [],
                  [(SSM_WIDTH, BF16)])[0]

    q = pool_window("pool_window", z, SSM_WIDTH // POOL_CH, seq, BF16, False)
    wp_eff = lw["pool_w"] * lw["pool_scale"].reshape(POOL_GROUPS, 1, POOL_CH)
    yp = pool_mm("pool_mm", q, wp_eff, BF16)
    m = jnp.concatenate([y2, yp], axis=1)
    h2 = mm_nn("mix_out", m, lw["w_out"], F32, res=h1)

    h3, ffn2_saved = _ffn_fwd("ffn2", h2, lw["ffn2_norm"], lw["ffn2_wi"], lw["ffn2_wo"])

    xn4 = norm_fwd("ple_norm", h3, lw["ple_norm"])
    tg = mm_nn("ple_gate", xn4, lw["ple_w_gate"], F32)
    e = mm_nn_colsharded("ple_proj", p_l, lw["ple_w_proj"], F32)
    h4 = _rowwise("ple_add", lambda a, b, c: a + jax.nn.sigmoid(b) * c, n, 512, [(h3, d, 0), (tg, d, 0), (e, d, 0)], [], [(d, F32)])[0]
    saved = dict(h=h, h1=h1, h2=h2, h3=h3, ffn1=ffn1_saved, ffn2=ffn2_saved, xn2=xn2, z=z, s=s, y0=y0, y1=y1, t=t, m=m, q=q,
                 xn4=xn4, tg=tg, e=e, pw_re=pw_re, pw_im=pw_im, bbd=bbd, cbd=cbd)
    return h4, saved


def _layer_bwd(d, lw, p_l, sv, seq):
    n, dm = d.shape
    g = {}
    def ple_bwd(dd, tg, e):
        gate = jax.nn.sigmoid(tg)
        return dd * e * gate * (1.0 - gate), dd * gate

    dtg, de = _rowwise("ple_dadd", ple_bwd, n, 512, [(d, dm, 0), (sv["tg"], dm, 0), (sv["e"], dm, 0)], [], [(dm, BF16), (dm, BF16)])
    g["ple_w_proj"] = mm_tn_colsharded("ple_dproj", p_l, de, F32)
    g["ple_w_gate"] = mm_tn("ple_dgate_w", sv["xn4"], dtg, F32).reshape(N_CHIPS, dm // N_CHIPS, dm)
    dxn4 = mm_nn("ple_dgate_x", dtg, lw["ple_w_gate_t"], F32)
    d, dg = norm_bwd("ple_dnorm", dxn4, sv["h3"], lw["ple_norm"], d)
    g["ple_norm"] = dg.reshape(dm)

    d, g["ffn2_norm"], g["ffn2_wi"], g["ffn2_wo"] = _ffn_bwd("ffn2b", d, sv["h2"], lw["ffn2_norm"], lw["ffn2_wi_t"], lw["ffn2_wo_t"], sv["ffn2"])

    dmix = mm_nn("mix_dout_x", d, lw["w_out_t"], F32)
    g["w_out"] = mm_tn("mix_dout_w", sv["m"], d, F32).reshape(N_CHIPS, dm // N_CHIPS, dm)
    pool_cb = SSM_WIDTH // POOL_CH
    wp_eff = lw["pool_w"] * lw["pool_scale"].reshape(POOL_GROUPS, 1, POOL_CH)
    dq = pool_mm_t("pool_dmm_x", dmix, pool_cb, wp_eff, F32)
    dwp_eff = pool_grad_w("pool_dmm_w", sv["q"], dmix, pool_cb)
    g["pool_w"] = dwp_eff * lw["pool_scale"].reshape(POOL_GROUPS, 1, POOL_CH)
    g["pool_scale"] = jnp.sum(dwp_eff * lw["pool_w"], axis=1).reshape(POOL_WIDTH)
    dzp = pool_window("pool_dwindow", dq, 0, seq, BF16, True)

    def glu_bwd(dy2, y1, t):
        sg = jax.nn.sigmoid(t)
        return dy2 * y1 * sg * (1.0 - sg), dy2 * sg

    dt_, dy1a = _rowwise("ssm_dglu", glu_bwd, n, 512, [(dmix, SSM_WIDTH, 0), (sv["y1"], SSM_WIDTH, 0), (sv["t"], SSM_WIDTH, 0)], [],
                         [(SSM_WIDTH, BF16), (SSM_WIDTH, F32)])
    g["ssm_w_glu"] = mm_tn("ssm_dglu_w", sv["y1"], dt_, F32).reshape(N_CHIPS, SSM_WIDTH // N_CHIPS, SSM_WIDTH)
    dy1b = mm_nn("ssm_dglu_x", dt_, lw["ssm_w_glu_t"], F32)

    def gelu_bwd(da, db, y0, zs, dvec):
        dy0 = (da + db) * _gelu_grad(y0)
        return dy0, dy0 * dvec, jnp.sum(dy0 * zs, axis=0, keepdims=True)

    dy0, dzs_a, dd = _rowwise("ssm_dgelu", gelu_bwd, n, 512,
                              [(dy1a, SSM_WIDTH, 0), (dy1b, SSM_WIDTH, 0), (sv["y0"], SSM_WIDTH, 0), (sv["z"], SSM_WIDTH, 0)],
                              [lw["ssm_d"].reshape(1, SSM_WIDTH)], [(SSM_WIDTH, F32), (SSM_WIDTH, F32)], [(1, SSM_WIDTH)])
    g["ssm_d"] = dd.reshape(SSM_WIDTH)
    g["ssm_c_re"], g["ssm_c_im"] = compact_c(ssm_grad_c("ssm_dc", sv["s"], dy0))
    v = ssm_out_t("ssm_dout", dy0, sv["cbd"])
    a, dlam = ssm_scan("ssm_scan_adj", scan_coefficients(sv["pw_re"], sv["pw_im"], True), v, seq, True, states=sv["s"])
    dbb_re, dbb_im = compact_b(ssm_grad_b("ssm_db", sv["z"], a))
    dzs = ssm_in_t("ssm_din", a, sv["bbd"], dzs_a)
    dlam = jnp.sum(dlam, axis=1)
    log_dt = jnp.repeat(lw["ssm_log_dt"], SSM_STATE)
    b_re, b_im = lw["ssm_b_re"].reshape(SSM_COLS, SSM_CH), lw["ssm_b_im"].reshape(SSM_COLS, SSM_CH)
    glr, gli, gld, gbr, gbi = ssm_prep_bwd("ssm_prep_bwd", _col(lw["ssm_lambda_re"]), _col(lw["ssm_lambda_im"]), _col(log_dt), b_re, b_im,
                                           _col(dlam[0]), _col(dlam[1]), dbb_re, dbb_im)
    g["ssm_lambda_re"] = glr.reshape(SSM_GROUPS, SSM_STATE)
    g["ssm_lambda_im"] = gli.reshape(SSM_GROUPS, SSM_STATE)
    g["ssm_log_dt"] = jnp.sum(gld.reshape(SSM_GROUPS, SSM_STATE), axis=1)
    g["ssm_b_re"] = gbr.reshape(SSM_GROUPS, SSM_STATE, SSM_CH)
    g["ssm_b_im"] = gbi.reshape(SSM_GROUPS, SSM_STATE, SSM_CH)

    dz = jnp.concatenate([dzs.astype(BF16), dzp], axis=1)
    g["w_in"] = mm_tn("mix_din_w", sv["xn2"], dz, F32).reshape(N_CHIPS, dm // N_CHIPS, dm)
    dxn2 = mm_nn("mix_din_x", dz, lw["w_in_t"], F32)
    d, dg = norm_bwd("mix_dnorm", dxn2, sv["h1"], lw["mix_norm"], d)
    g["mix_norm"] = dg.reshape(dm)

    d, g["ffn1_norm"], g["ffn1_wi"], g["ffn1_wo"] = _ffn_bwd("ffn1b", d, sv["h"], lw["ffn1_norm"], lw["ffn1_wi_t"], lw["ffn1_wo_t"], sv["ffn1"])
    return d, g


def _flatten_small(tensors):
    flat = jnp.concatenate([t.reshape(-1) for t in tensors])
    pad = (-flat.shape[0]) % (SUBLANE * LANE)
    return jnp.pad(flat, (0, pad)).reshape(-1, LANE)


def _unflatten_small(flat, like):
    flat = flat.reshape(-1)
    out, off = [], 0
    for t in like:
        out.append(flat[off:off + t.size].reshape(t.shape))
        off += t.size
    return out


def kernel(x, p, ffn1_norm, ffn1_wi, ffn1_wo, mix_norm, w_in, ssm_lambda_re, ssm_lambda_im, ssm_log_dt, ssm_b_re, ssm_b_im, ssm_c_re, ssm_c_im, ssm_d, ssm_w_glu, pool_w, pool_scale, w_out, ffn2_norm, ffn2_wi, ffn2_wo, ple_norm, ple_w_gate, ple_w_proj, final_norm, loss_target, m_ffn1_norm, m_ffn1_wi, m_ffn1_wo, m_mix_norm, m_w_in, m_ssm_lambda_re, m_ssm_lambda_im, m_ssm_log_dt, m_ssm_b_re, m_ssm_b_im, m_ssm_c_re, m_ssm_c_im, m_ssm_d, m_ssm_w_glu, m_pool_w, m_pool_scale, m_w_out, m_ffn2_norm, m_ffn2_wi, m_ffn2_wo, m_ple_norm, m_ple_w_gate, m_ple_w_proj, m_final_norm, v_ffn1_norm, v_ffn1_wi, v_ffn1_wo, v_mix_norm, v_w_in, v_ssm_lambda_re, v_ssm_lambda_im, v_ssm_log_dt, v_ssm_b_re, v_ssm_b_im, v_ssm_c_re, v_ssm_c_im, v_ssm_d, v_ssm_w_glu, v_pool_w, v_pool_scale, v_w_out, v_ffn2_norm, v_ffn2_wi, v_ffn2_wo, v_ple_norm, v_ple_w_gate, v_ple_w_proj, v_final_norm):
    given = dict(locals())
    w = {k: given[k] for k in WEIGHTS}
    mom = {k: given["m_" + k] for k in WEIGHTS}
    var = {k: given["v_" + k] for k in WEIGHTS}
    bsz, seq, dm = x.shape
    n = bsz * seq
    depth = ffn1_wi.shape[0]

    full = {}
    for k in BIG:
        a = gather_weight("gather_" + k, w[k].astype(MXU_DTYPE))
        if k in COL_SHARDED:
            full[k] = a
            if k != "ple_w_proj":
                full[k + "_t"] = transpose_colsharded(a)
        else:
            full[k] = a.reshape(depth, N_CHIPS * a.shape[2], a.shape[3])
            full[k + "_t"] = jnp.swapaxes(full[k], 1, 2)
    for k in SMALL:
        if k != "final_norm":
            full[k] = w[k]
    layers = [{k: v[l] for k, v in full.items()} for l in range(depth)]

    p2 = p.reshape(depth, n, p.shape[-1])
    h_last = x.reshape(n, dm)
    saved = []
    for l in range(depth):
        h_last, sv = _layer_fwd(h_last, layers[l], p2[l], seq)
        saved.append(sv)

    def head(hh, tgt, gf):
        r = lax.rsqrt(jnp.mean(hh * hh, axis=-1, keepdims=True) + NORM_EPS)
        xh = hh * r
        diff = xh * gf - tgt
        dy = diff * (1.0 / dm)
        dxh = dy * gf
        dx = r * (dxh - xh * jnp.mean(dxh * xh, axis=-1, keepdims=True))
        return dx, jnp.sum(diff * diff, axis=0, keepdims=True) * (0.5 / dm), jnp.sum(dy * xh, axis=0, keepdims=True)

    d_last, loss_cols, g_final = _rowwise("loss_head", head, n, 512, [(h_last, dm, 0), (loss_target.reshape(n, dm), dm, 0)],
                                          [final_norm.reshape(1, dm)], [(dm, F32)], [(1, dm), (1, dm)])
    loss = lax.psum(jnp.sum(loss_cols), ("x", "y", "c"))

    d_x = d_last
    layer_grads = [None] * depth
    for l in reversed(range(depth)):
        d_x, layer_grads[l] = _layer_bwd(d_x, layers[l], p2[l], saved[l], seq)
    grads = {k: jnp.stack([g[k] for g in layer_grads]) for k in layer_grads[0]}
    grad_x = d_x.reshape(bsz, seq, dm)

    big_sum = [reduce_scatter_streamed("rs_" + k, grads[k]) for k in BIG]
    small_keys = [k for k in SMALL]
    small_parts = [grads[k] if k != "final_norm" else g_final.reshape(dm) for k in small_keys]
    small_sum = _unflatten_small(all_reduce_small(_flatten_small(small_parts)), small_parts)
    g_out = dict(zip(BIG, big_sum))
    g_out.update(dict(zip(small_keys, small_sum)))
    for k in BIG:
        g_out[k] = g_out[k].reshape(w[k].shape)

    delta, new_m, new_v = {}, {}, {}
    for k in BIG:
        delta[k], new_m[k], new_v[k] = adamw("adamw_" + k, w[k], g_out[k], mom[k], var[k])
    sw = adamw("adamw_small", *[_flatten_small([t[k] for k in small_keys]) for t in (w, g_out, mom, var)])
    for name, flat in zip((delta, new_m, new_v), sw):
        for k, t in zip(small_keys, _unflatten_small(flat, [w[k] for k in small_keys])):
            name[k] = t

    return (loss, grad_x, *[g_out[k] for k in WEIGHTS], *[delta[k] for k in WEIGHTS],
            *[new_m[k] for k in WEIGHTS], *[new_v[k] for k in WEIGHTS])
```

```python
import functools
import math

import jax
import jax.numpy as jnp
from jax import lax
from jax.experimental import pallas as pl
from jax.experimental.pallas import tpu as pltpu

F32 = jnp.float32
BF16 = jnp.bfloat16
MXU_DTYPE = jnp.bfloat16
GRAD_WIRE_DTYPE = jnp.bfloat16
STATE_DTYPE = jnp.bfloat16
VMEM_LIMIT = 56 * 1024 * 1024
LANE = 128
SUBLANE = 8

N_CHIPS = 4
SSM_GROUPS = 32
SSM_STATE = 64
SSM_CH = 16
SSM_WIDTH = SSM_GROUPS * SSM_CH
SSM_COLS = SSM_GROUPS * SSM_STATE
POOL_GROUPS = 4
POOL_CH = 128
POOL_WIDTH = POOL_GROUPS * POOL_CH
SSM_TILE = 256
NORM_EPS = 1e-6
ADAM_LR = 0.001
ADAM_B1 = 0.9
ADAM_B2 = 0.999
ADAM_EPS = 1e-08
ADAM_WD = 0.01
ADAM_STEP = 10
MESH_ID = pl.DeviceIdType.MESH

BIG = ("ffn1_wi", "ffn1_wo", "w_in", "ssm_w_glu", "w_out", "ffn2_wi", "ffn2_wo", "ple_w_gate", "ple_w_proj")
COL_SHARDED = ("ffn1_wi", "ffn2_wi", "ple_w_proj")
SMALL = ("ffn1_norm", "mix_norm", "ssm_lambda_re", "ssm_lambda_im", "ssm_log_dt", "ssm_b_re", "ssm_b_im",
         "ssm_c_re", "ssm_c_im", "ssm_d", "pool_w", "pool_scale", "ffn2_norm", "ple_norm", "final_norm")
WEIGHTS = ("ffn1_norm", "ffn1_wi", "ffn1_wo", "mix_norm", "w_in", "ssm_lambda_re", "ssm_lambda_im", "ssm_log_dt",
           "ssm_b_re", "ssm_b_im", "ssm_c_re", "ssm_c_im", "ssm_d", "ssm_w_glu", "pool_w", "pool_scale", "w_out",
           "ffn2_norm", "ffn2_wi", "ffn2_wo", "ple_norm", "ple_w_gate", "ple_w_proj", "final_norm")


def _tile(dim, target):
    best = None
    t = LANE
    while t <= min(dim, target):
        if dim % t == 0:
            best = t
        t += LANE
    return best if best is not None else dim


def _params(sem):
    return pltpu.CompilerParams(dimension_semantics=sem, vmem_limit_bytes=VMEM_LIMIT)


def _mm(name, a, b, out_shape, out_dtype, grid, a_spec, b_spec, o_spec, contract, alpha=1.0, res=None):
    n_k = grid[2]
    acc_shape = tuple(d for d in o_spec.block_shape if d is not None)

    def body(*refs):
        a_ref, b_ref = refs[0], refs[1]
        r_ref = refs[2] if res is not None else None
        o_ref = refs[3] if res is not None else refs[2]

        def product():
            return lax.dot_general(a_ref[...].astype(MXU_DTYPE), b_ref[...].astype(MXU_DTYPE),
                                   (contract, ((), ())), preferred_element_type=F32)

        def finish(v):
            if alpha != 1.0:
                v = v * alpha
            if r_ref is not None:
                v = v + r_ref[...].astype(F32)
            o_ref[...] = v.astype(out_dtype)

        if n_k == 1:
            finish(product())
            return
        acc = refs[-1]
        k = pl.program_id(2)

        @pl.when(k == 0)
        def _():
            acc[...] = product()

        @pl.when(k > 0)
        def _():
            acc[...] += product()

        @pl.when(k == n_k - 1)
        def _():
            finish(acc[...])

    in_specs = [a_spec, b_spec]
    operands = [a, b]
    if res is not None:
        in_specs.append(o_spec)
        operands.append(res)
    return pl.pallas_call(
        body, name=name, grid=grid, in_specs=in_specs, out_specs=o_spec,
        out_shape=jax.ShapeDtypeStruct(out_shape, out_dtype),
        scratch_shapes=[pltpu.VMEM(acc_shape, F32)] if n_k > 1 else [],
        compiler_params=_params(("parallel", "parallel", "arbitrary")),
    )(*operands)


NN = ((1,), (0,))
NT = ((1,), (1,))
TN = ((0,), (0,))


def mm_nn(name, a, b, out_dtype, alpha=1.0, res=None, tm=1024, tn=1024, tk=1024):
    m, k = a.shape
    n = b.shape[1]
    tm, tn, tk = _tile(m, tm), _tile(n, tn), _tile(k, tk)
    return _mm(name, a, b, (m, n), out_dtype, (m // tm, n // tn, k // tk),
               pl.BlockSpec((tm, tk), lambda i, j, kk: (i, kk)),
               pl.BlockSpec((tk, tn), lambda i, j, kk: (kk, j)),
               pl.BlockSpec((tm, tn), lambda i, j, kk: (i, j)), NN, alpha, res)


def mm_nt(name, a, b, out_dtype, alpha=1.0, res=None, tm=1024, tn=512, tk=512):
    m, k = a.shape
    n = b.shape[0]
    tm, tn, tk = _tile(m, tm), _tile(n, tn), _tile(k, tk)
    return _mm(name, a, b, (m, n), out_dtype, (m // tm, n // tn, k // tk),
               pl.BlockSpec((tm, tk), lambda i, j, kk: (i, kk)),
               pl.BlockSpec((tn, tk), lambda i, j, kk: (j, kk)),
               pl.BlockSpec((tm, tn), lambda i, j, kk: (i, j)), NT, alpha, res)


def mm_tn(name, a, b, out_dtype, alpha=1.0, tm=1024, tn=1024, tk=1024):
    k, m = a.shape
    n = b.shape[1]
    tm, tn, tk = _tile(m, tm), _tile(n, tn), _tile(k, tk)
    return _mm(name, a, b, (m, n), out_dtype, (m // tm, n // tn, k // tk),
               pl.BlockSpec((tk, tm), lambda i, j, kk: (kk, i)),
               pl.BlockSpec((tk, tn), lambda i, j, kk: (kk, j)),
               pl.BlockSpec((tm, tn), lambda i, j, kk: (i, j)), TN, alpha)


def mm_nn_colsharded(name, a, w, out_dtype, tm=1024, tk=1024):
    m, k = a.shape
    c = w.shape[2]
    tm, tk = _tile(m, tm), _tile(k, tk)
    return _mm(name, a, w, (m, N_CHIPS * c), out_dtype, (m // tm, N_CHIPS, k // tk),
               pl.BlockSpec((tm, tk), lambda i, j, kk: (i, kk)),
               pl.BlockSpec((None, tk, c), lambda i, j, kk: (j, kk, 0)),
               pl.BlockSpec((tm, c), lambda i, j, kk: (i, j)), NN)


def transpose_colsharded(w):
    nl, _, k, c = w.shape
    return jnp.swapaxes(w, 2, 3).reshape(nl, N_CHIPS * c, k)


def mm_tn_colsharded(name, a, b, out_dtype, tm=1024, tk=1024):
    t, k = a.shape
    c = b.shape[1] // N_CHIPS
    tm, tk = _tile(k, tm), _tile(t, tk)
    return _mm(name, a, b, (N_CHIPS, k, c), out_dtype, (k // tm, N_CHIPS, t // tk),
               pl.BlockSpec((tk, tm), lambda i, j, kk: (kk, i)),
               pl.BlockSpec((tk, c), lambda i, j, kk: (kk, j)),
               pl.BlockSpec((None, tm, c), lambda i, j, kk: (j, i, 0)), TN)


def _rowwise(name, fn, n_rows, tm, row_ins, bcast_ins, outs, accs=()):
    tm = min(tm, n_rows)
    grid = (n_rows // tm,)
    n_row, n_b, n_out = len(row_ins), len(bcast_ins), len(outs)

    def body(*refs):
        ins = [r[...] for r in refs[:n_row + n_b]]
        out_refs = refs[n_row + n_b:n_row + n_b + n_out]
        acc_refs = refs[n_row + n_b + n_out:]
        res = fn(*ins)
        if not isinstance(res, (tuple, list)):
            res = (res,)
        for o_ref, v in zip(out_refs, res[:n_out]):
            o_ref[...] = v.astype(o_ref.dtype)
        if acc_refs:
            @pl.when(pl.program_id(0) == 0)
            def _():
                for a_ref in acc_refs:
                    a_ref[...] = jnp.zeros_like(a_ref)
            for a_ref, v in zip(acc_refs, res[n_out:]):
                a_ref[...] += v

    in_specs, operands = [], []
    for spec in row_ins:
        arr, width, cb = spec[0], spec[1], spec[2]
        rb = spec[3] if len(spec) > 3 else 0
        in_specs.append(pl.BlockSpec((tm, width), functools.partial(lambda i, cb, rb: (i + rb, cb), cb=cb, rb=rb)))
        operands.append(arr)
    for arr in bcast_ins:
        in_specs.append(pl.BlockSpec(arr.shape, functools.partial(lambda i, nd: (0,) * nd, nd=arr.ndim)))
        operands.append(arr)
    out_specs = [pl.BlockSpec((tm, w), lambda i: (i, 0)) for w, _ in outs]
    out_specs += [pl.BlockSpec((r, w), lambda i: (0, 0)) for r, w in accs]
    out_shape = [jax.ShapeDtypeStruct((n_rows, w), dt) for w, dt in outs]
    out_shape += [jax.ShapeDtypeStruct((r, w), F32) for r, w in accs]
    res = pl.pallas_call(
        body, name=name, grid=grid, in_specs=in_specs, out_specs=out_specs, out_shape=out_shape,
        compiler_params=_params(("arbitrary",) if accs else ("parallel",)),
    )(*operands)
    return res


def _rms(x, g):
    r = lax.rsqrt(jnp.mean(x * x, axis=-1, keepdims=True) + NORM_EPS)
    return x * r * g


def _rms_bwd(dy, x, g):
    r = lax.rsqrt(jnp.mean(x * x, axis=-1, keepdims=True) + NORM_EPS)
    xh = x * r
    dxh = dy * g
    dx = r * (dxh - xh * jnp.mean(dxh * xh, axis=-1, keepdims=True))
    return dx, jnp.sum(dy * xh, axis=0, keepdims=True)


def norm_fwd(name, h, g):
    n, d = h.shape
    return _rowwise(name, lambda x, gg: _rms(x, gg), n, 512, [(h, d, 0)], [g.reshape(1, d)], [(d, BF16)])[0]


def norm_bwd(name, dxn, h, g, d_res):
    n, d = h.shape

    def fn(dy, x, dr, gg):
        dx, dg = _rms_bwd(dy, x, gg)
        return dr + dx, dg

    return _rowwise(name, fn, n, 512, [(dxn, d, 0), (h, d, 0), (d_res, d, 0)], [g.reshape(1, d)], [(d, F32)], [(1, d)])


_GELU_C = math.sqrt(2.0 / math.pi)


def _gelu(x):
    return 0.5 * x * (1.0 + jnp.tanh(_GELU_C * (x + 0.044715 * (x * x * x))))


def _gelu_grad(x):
    th = jnp.tanh(_GELU_C * (x + 0.044715 * (x * x * x)))
    return 0.5 * (1.0 + th) + 0.5 * x * (1.0 - th * th) * (_GELU_C * (1.0 + 3.0 * 0.044715 * (x * x)))


def _ssm_discretize(lam_re, lam_im, log_dt, b_re, b_im):
    dt = jnp.exp(log_dt)
    e = jnp.exp(lam_re * dt)
    lb_re = e * jnp.cos(lam_im * dt)
    lb_im = e * jnp.sin(lam_im * dt)
    nr, ni = lb_re - 1.0, lb_im
    den = lam_re * lam_re + lam_im * lam_im
    cr = (nr * lam_re + ni * lam_im) / den
    ci = (ni * lam_re - nr * lam_im) / den
    return lb_re, lb_im, cr * b_re - ci * b_im, cr * b_im + ci * b_re


def ssm_prep(name, lam_re, lam_im, log_dt, b_re, b_im):
    def body(lr, li, ld, br, bi, pr_ref, pi_ref, bbr_ref, bbi_ref):
        lb_re, lb_im, bb_re, bb_im = _ssm_discretize(lr[...], li[...], ld[...], br[...], bi[...])
        bbr_ref[...] = bb_re
        bbi_ref[...] = bb_im
        pr, pi = lb_re, lb_im
        cols_r, cols_i = [pr], [pi]
        for _ in range(SUBLANE - 1):
            pr, pi = pr * lb_re - pi * lb_im, pr * lb_im + pi * lb_re
            cols_r.append(pr)
            cols_i.append(pi)
        lane = lax.broadcasted_iota(jnp.int32, (SSM_COLS, SUBLANE), 1)
        out_r = jnp.zeros((SSM_COLS, SUBLANE), F32)
        out_i = jnp.zeros((SSM_COLS, SUBLANE), F32)
        for r in range(SUBLANE):
            out_r = jnp.where(lane == r, cols_r[r], out_r)
            out_i = jnp.where(lane == r, cols_i[r], out_i)
        pr_ref[...] = out_r
        pi_ref[...] = out_i

    shapes = [jax.ShapeDtypeStruct((SSM_COLS, SUBLANE), F32)] * 2 + [jax.ShapeDtypeStruct((SSM_COLS, SSM_CH), F32)] * 2
    return pl.pallas_call(body, name=name, out_shape=shapes,
                          compiler_params=pltpu.CompilerParams(vmem_limit_bytes=VMEM_LIMIT))(lam_re, lam_im, log_dt, b_re, b_im)


def ssm_prep_bwd(name, lam_re, lam_im, log_dt, b_re, b_im, d_lb_re, d_lb_im, d_bb_re, d_bb_im):
    def body(lr, li, ld, br, bi, g0, g1, g2, g3, o0, o1, o2, o3, o4):
        _, vjp = jax.vjp(_ssm_discretize, lr[...], li[...], ld[...], br[...], bi[...])
        res = vjp((g0[...], g1[...], g2[...], g3[...]))
        for o, v in zip((o0, o1, o2, o3, o4), res):
            o[...] = v

    col = jax.ShapeDtypeStruct((SSM_COLS, 1), F32)
    mat = jax.ShapeDtypeStruct((SSM_COLS, SSM_CH), F32)
    return pl.pallas_call(body, name=name, out_shape=[col, col, col, mat, mat],
                          compiler_params=pltpu.CompilerParams(vmem_limit_bytes=VMEM_LIMIT))(
        lam_re, lam_im, log_dt, b_re, b_im, d_lb_re, d_lb_im, d_bb_re, d_bb_im)


def scan_coefficients(pw_re, pw_im, reverse):
    pr, pi = pw_re.T, pw_im.T
    if reverse:
        pi = -pi
    row = jnp.arange(SUBLANE)[:, None]
    out = []
    for d in (1, 2, 4):
        valid = (row < SUBLANE - d) if reverse else (row >= d)
        out.append(jnp.where(valid, pr[d - 1][None, :], 0.0))
        out.append(jnp.where(valid, pi[d - 1][None, :], 0.0))
    out.append(pr[::-1] if reverse else pr)
    out.append(pi[::-1] if reverse else pi)
    return jnp.stack(out)


def ssm_scan(name, coef, x, seq, reverse, states=None):
    n = x.shape[1]
    n_seq = n // seq
    cw = LANE
    n_cb = SSM_COLS // cw
    pair = 2 * SUBLANE
    n_pairs = seq // pair
    pairs_per_step = 2 if n_pairs % 2 == 0 else 1
    with_dlam = states is not None

    def body(*refs):
        if with_dlam:
            coef_ref, x_ref, s_ref, o_ref, dl_ref = refs
        else:
            coef_ref, x_ref, o_ref = refs
        c = [coef_ref[i] for i in range(8)]
        row16 = lax.broadcasted_iota(jnp.int32, (pair, cw), 0)
        zero = jnp.zeros((SUBLANE, cw), F32)

        edge = 0 if reverse else SUBLANE - 1

        def bcast_row(v, r, rows=SUBLANE):
            return jnp.broadcast_to(v[r:r + 1, :], (rows, cw))

        p8r, p8i = bcast_row(c[6], edge), bcast_row(c[7], edge)

        def local_scan(xr, xi):
            for si, d in enumerate((1, 2, 4)):
                sh = (SUBLANE - d) if reverse else d
                sr, sm = pltpu.roll(xr, sh, 0), pltpu.roll(xi, sh, 0)
                lre, lim = c[2 * si], c[2 * si + 1]
                xr, xi = xr + lre * sr - lim * sm, xi + lre * sm + lim * sr
            return xr, xi

        def step(it, carry):
            work = []
            for u in range(pairs_per_step):
                k = it * pairs_per_step + u
                pidx = (n_pairs - 1 - k) if reverse else k
                off = pl.multiple_of(pidx * pair, pair)
                xr16 = x_ref[0, pl.ds(off, pair), :].astype(F32)
                xi16 = x_ref[1, pl.ds(off, pair), :].astype(F32)
                halves = (1, 0) if reverse else (0, 1)
                tiles = {h: local_scan(xr16[h * SUBLANE:(h + 1) * SUBLANE], xi16[h * SUBLANE:(h + 1) * SUBLANE]) for h in halves}
                work.append((pidx, off, halves, tiles))
            cre, cim = carry[0], carry[1]
            acc = carry[2:]
            for pidx, off, halves, tiles in work:
                done = {}
                for h in halves:
                    lr, li = tiles[h]
                    done[h] = (lr + c[6] * cre - c[7] * cim, li + c[6] * cim + c[7] * cre)
                    cre, cim = (bcast_row(lr, edge) + p8r * cre - p8i * cim, bcast_row(li, edge) + p8r * cim + p8i * cre)
                or16 = jnp.concatenate([done[0][0], done[1][0]], axis=0)
                oi16 = jnp.concatenate([done[0][1], done[1][1]], axis=0)
                o_ref[0, pl.ds(off, pair), :] = or16.astype(o_ref.dtype)
                o_ref[1, pl.ds(off, pair), :] = oi16.astype(o_ref.dtype)
                if with_dlam:
                    poff = pl.multiple_of(jnp.maximum(pidx - 1, 0) * pair, pair)
                    first = pidx > 0
                    sr16 = s_ref[0, pl.ds(off, pair), :].astype(F32)
                    si16 = s_ref[1, pl.ds(off, pair), :].astype(F32)
                    pr_last = jnp.where(first, bcast_row(s_ref[0, pl.ds(poff, pair), :].astype(F32), pair - 1, pair), 0.0)
                    pi_last = jnp.where(first, bcast_row(s_ref[1, pl.ds(poff, pair), :].astype(F32), pair - 1, pair), 0.0)
                    spr = jnp.where(row16 == 0, pr_last, pltpu.roll(sr16, 1, 0))
                    spi = jnp.where(row16 == 0, pi_last, pltpu.roll(si16, 1, 0))
                    dre = or16 * spr + oi16 * spi
                    dim = oi16 * spr - or16 * spi
                    acc = (acc[0] + dre[:SUBLANE] + dre[SUBLANE:], acc[1] + dim[:SUBLANE] + dim[SUBLANE:])
            return (cre, cim) + tuple(acc)

        init = (zero, zero, zero, zero) if with_dlam else (zero, zero)
        fin = lax.fori_loop(0, n_pairs // pairs_per_step, step, init)
        if with_dlam:
            @pl.when(pl.program_id(1) == 0)
            def _():
                dl_ref[...] = jnp.zeros_like(dl_ref)
            dl_ref[0] += fin[2]
            dl_ref[1] += fin[3]

    blk = pl.BlockSpec((2, seq, cw), lambda j, b: (0, b, j))
    in_specs = [pl.BlockSpec((8, SUBLANE, cw), lambda j, b: (0, 0, j)), blk]
    operands = [coef, x]
    out_specs = [blk]
    out_shape = [jax.ShapeDtypeStruct(x.shape, STATE_DTYPE)]
    if with_dlam:
        in_specs.append(blk)
        operands.append(states)
        out_specs.append(pl.BlockSpec((2, SUBLANE, cw), lambda j, b: (0, 0, j)))
        out_shape.append(jax.ShapeDtypeStruct((2, SUBLANE, SSM_COLS), F32))
    res = pl.pallas_call(
        body, name=name, grid=(n_cb, n_seq), in_specs=in_specs, out_specs=out_specs, out_shape=out_shape,
        compiler_params=_params(("parallel", "arbitrary")),
    )(*operands)
    return res if with_dlam else res[0]


def _state_col(j, kk):
    return 2 * j + kk + 2 * (kk // 2)


SSM_WIDE = 4 * SSM_TILE


def ssm_in(name, z, bbd, tm=1024):
    n = z.shape[0]
    tm = _tile(n, tm)
    t, w = SSM_TILE, SSM_WIDE
    return _mm(name, z, bbd, (2, n, SSM_COLS), STATE_DTYPE, (n // tm, 2 * SSM_COLS // w, 1),
               pl.BlockSpec((tm, t), lambda i, j, kk: (i, j % 2)),
               pl.BlockSpec((t, w), lambda i, j, kk: (j % 2, j)),
               pl.BlockSpec((None, tm, w), lambda i, j, kk: (j // 2, i, j % 2)), NN)


def ssm_out(name, s, cbd, tm=1024):
    n = s.shape[1]
    tm = _tile(n, tm)
    t = SSM_TILE
    return _mm(name, s, cbd, (n, SSM_WIDTH), F32, (n // tm, SSM_WIDTH // t, 4),
               pl.BlockSpec((None, tm, 512), lambda i, j, kk: (kk // 2, i, 2 * j + kk % 2)),
               pl.BlockSpec((512, t), lambda i, j, kk: (_state_col(j, kk), j)),
               pl.BlockSpec((tm, t), lambda i, j, kk: (i, j)), NN)


def ssm_out_t(name, dy, cbd, tm=1024):
    n = dy.shape[0]
    tm = _tile(n, tm)
    t, w = SSM_TILE, SSM_WIDE
    return _mm(name, dy, cbd, (2, n, SSM_COLS), STATE_DTYPE, (n // tm, 2 * SSM_COLS // w, 1),
               pl.BlockSpec((tm, t), lambda i, j, kk: (i, j % 2)),
               pl.BlockSpec((w, t), lambda i, j, kk: (j, j % 2)),
               pl.BlockSpec((None, tm, w), lambda i, j, kk: (j // 2, i, j % 2)), NT)


def ssm_in_t(name, a, bbd, res, tm=1024):
    n = a.shape[1]
    tm = _tile(n, tm)
    t = SSM_TILE
    return _mm(name, a, bbd, (n, SSM_WIDTH), F32, (n // tm, SSM_WIDTH // t, 4),
               pl.BlockSpec((None, tm, 512), lambda i, j, kk: (kk // 2, i, 2 * j + kk % 2)),
               pl.BlockSpec((t, 512), lambda i, j, kk: (j, _state_col(j, kk))),
               pl.BlockSpec((tm, t), lambda i, j, kk: (i, j)), NT, 1.0, res)


def ssm_grad_c(name, s, dy, tk=1024):
    n = s.shape[1]
    tk = _tile(n, tk)
    t, w = SSM_TILE, SSM_WIDE
    return _mm(name, s, dy, (2 * SSM_COLS, t), F32, (2 * SSM_COLS // w, 1, n // tk),
               pl.BlockSpec((None, tk, w), lambda i, j, kk: (i // 2, kk, i % 2)),
               pl.BlockSpec((tk, t), lambda i, j, kk: (kk, i % 2)),
               pl.BlockSpec((w, t), lambda i, j, kk: (i, 0)), TN)


def ssm_grad_b(name, z, a, tk=1024):
    n = z.shape[0]
    tk = _tile(n, tk)
    t, w = SSM_TILE, SSM_WIDE
    return _mm(name, z, a, (t, 2 * SSM_COLS), F32, (1, 2 * SSM_COLS // w, n // tk),
               pl.BlockSpec((tk, t), lambda i, j, kk: (kk, j % 2)),
               pl.BlockSpec((None, tk, w), lambda i, j, kk: (j // 2, kk, j % 2)),
               pl.BlockSpec((t, w), lambda i, j, kk: (0, j)), TN)


_GROUP_TILE = SSM_TILE // SSM_CH


def expand_b(bb_re, bb_im):
    b = jnp.stack([bb_re, bb_im]).reshape(2, SSM_GROUPS, SSM_STATE, SSM_CH)
    eye = jnp.eye(SSM_GROUPS, dtype=F32)
    return jnp.einsum("rgph,gk->ghrkp", b, eye).reshape(SSM_WIDTH, 2 * SSM_COLS).astype(MXU_DTYPE)


def expand_c(c_re, c_im):
    c = jnp.stack([c_re, -c_im])
    eye = jnp.eye(SSM_GROUPS, dtype=F32)
    return jnp.einsum("rghp,gk->rgpkh", c, eye).reshape(2 * SSM_COLS, SSM_WIDTH).astype(MXU_DTYPE)


def _group_pick():
    return (jnp.arange(SSM_GROUPS)[:, None] % _GROUP_TILE == jnp.arange(_GROUP_TILE)[None, :]).astype(F32)


def compact_c(dc):
    x = dc.reshape(2, SSM_GROUPS, SSM_STATE, _GROUP_TILE, SSM_CH)
    g = jnp.einsum("rgpch,gc->rghp", x, _group_pick())
    return g[0], -g[1]


def compact_b(db):
    x = db.reshape(_GROUP_TILE, SSM_CH, 2, SSM_GROUPS, SSM_STATE)
    g = jnp.einsum("chrgp,gc->rgph", x, _group_pick()).reshape(2, SSM_COLS, SSM_CH)
    return g[0], g[1]


def pool_window(name, x, col_block0, seq, out_dtype, adjoint):
    n = x.shape[0]

    def body(x_ref, o_ref):
        win = 2 << pl.program_id(1)
        row = lax.broadcasted_iota(jnp.int32, (seq, POOL_CH), 0)
        v = x_ref[...].astype(F32)
        cnt = jnp.minimum(row + 1, win).astype(F32)
        s = v / cnt if adjoint else v
        for d in (1, 2, 4, 8):
            if adjoint:
                sh = jnp.where((row < seq - d) & (d < win), pltpu.roll(s, seq - d, 0), 0.0)
            else:
                sh = jnp.where((row >= d) & (d < win), pltpu.roll(s, d, 0), 0.0)
            s = s + sh
        o_ref[...] = ((s - v) if adjoint else (s / cnt - v)).astype(out_dtype)

    return pl.pallas_call(
        body, name=name, grid=(n // seq, POOL_GROUPS),
        in_specs=[pl.BlockSpec((seq, POOL_CH), lambda b, g: (b, col_block0 + g))],
        out_specs=pl.BlockSpec((seq, POOL_CH), lambda b, g: (b, g)),
        out_shape=jax.ShapeDtypeStruct((n, POOL_WIDTH), out_dtype),
        compiler_params=_params(("parallel", "parallel")),
    )(x)


def pool_mm(name, q, w, out_dtype, tm=1024):
    n = q.shape[0]
    tm = _tile(n, tm)
    return _mm(name, q, w, (n, POOL_WIDTH), out_dtype, (n // tm, POOL_GROUPS, 1),
               pl.BlockSpec((tm, POOL_CH), lambda i, j, kk: (i, j)),
               pl.BlockSpec((None, POOL_CH, POOL_CH), lambda i, j, kk: (j, 0, 0)),
               pl.BlockSpec((tm, POOL_CH), lambda i, j, kk: (i, j)), NN)


def pool_mm_t(name, dy, col_block0, w, out_dtype, tm=1024):
    n = dy.shape[0]
    tm = _tile(n, tm)
    return _mm(name, dy, w, (n, POOL_WIDTH), out_dtype, (n // tm, POOL_GROUPS, 1),
               pl.BlockSpec((tm, POOL_CH), lambda i, j, kk: (i, col_block0 + j)),
               pl.BlockSpec((None, POOL_CH, POOL_CH), lambda i, j, kk: (j, 0, 0)),
               pl.BlockSpec((tm, POOL_CH), lambda i, j, kk: (i, j)), NT)


def pool_grad_w(name, q, dy, col_block0, tk=1024):
    n = q.shape[0]
    tk = _tile(n, tk)
    return _mm(name, q, dy, (POOL_GROUPS, POOL_CH, POOL_CH), F32, (POOL_GROUPS, 1, n // tk),
               pl.BlockSpec((tk, POOL_CH), lambda i, j, kk: (kk, i)),
               pl.BlockSpec((tk, POOL_CH), lambda i, j, kk: (kk, col_block0 + i)),
               pl.BlockSpec((None, POOL_CH, POOL_CH), lambda i, j, kk: (i, 0, 0)), TN)


def _any_specs(n):
    return [pl.BlockSpec(memory_space=pl.ANY)] * n


def _place():
    x, y, c = lax.axis_index("x"), lax.axis_index("y"), lax.axis_index("c")
    return x, y, c


def _at_axis(ref, axis, start, size):
    return ref.at[(slice(None),) * axis + (pl.ds(start, size),)]


def sibling_swap_halves(name, arrays, axis):
    n = len(arrays)
    halves = [a.shape[axis] // 2 for a in arrays]

    def body(*refs):
        ins, own, got = refs[:n], refs[n:2 * n], refs[2 * n:3 * n]
        send_sems, recv_sems, local_sems = refs[3 * n:]
        x, y, c = _place()
        copies = []
        for i in range(n):
            h = halves[i]
            mine = pltpu.make_async_copy(_at_axis(ins[i], axis, c * h, h), own[i], local_sems.at[i])
            mine.start()
            away = pltpu.make_async_remote_copy(
                src_ref=_at_axis(ins[i], axis, (1 - c) * h, h), dst_ref=got[i],
                send_sem=send_sems.at[i], recv_sem=recv_sems.at[i], device_id=(x, y, 1 - c), device_id_type=MESH_ID)
            away.start()
            copies += [mine, away]
        for cp in copies:
            cp.wait()

    def half_shape(a, h):
        return jax.ShapeDtypeStruct(a.shape[:axis] + (h,) + a.shape[axis + 1:], a.dtype)

    shapes = [half_shape(a, h) for a, h in zip(arrays, halves)]
    res = pl.pallas_call(
        body, name=name, in_specs=_any_specs(n), out_specs=_any_specs(2 * n), out_shape=shapes + shapes,
        scratch_shapes=[pltpu.SemaphoreType.DMA((n,)), pltpu.SemaphoreType.DMA((n,)), pltpu.SemaphoreType.DMA((n,))],
    )(*arrays)
    return res[:n], res[n:]


def sibling_join_halves(name, arrays, axis):
    n = len(arrays)

    def body(*refs):
        ins, outs = refs[:n], refs[n:2 * n]
        send_sems, recv_sems, local_sems = refs[2 * n:]
        x, y, c = _place()
        copies = []
        for i in range(n):
            h = ins[i].shape[axis]
            dst = _at_axis(outs[i], axis, c * h, h)
            mine = pltpu.make_async_copy(ins[i], dst, local_sems.at[i])
            mine.start()
            away = pltpu.make_async_remote_copy(
                src_ref=ins[i], dst_ref=dst, send_sem=send_sems.at[i], recv_sem=recv_sems.at[i],
                device_id=(x, y, 1 - c), device_id_type=MESH_ID)
            away.start()
            copies += [mine, away]
        for cp in copies:
            cp.wait()

    shapes = [jax.ShapeDtypeStruct(a.shape[:axis] + (2 * a.shape[axis],) + a.shape[axis + 1:], a.dtype) for a in arrays]
    return pl.pallas_call(
        body, name=name, in_specs=_any_specs(n), out_specs=_any_specs(n), out_shape=shapes,
        scratch_shapes=[pltpu.SemaphoreType.DMA((n,)), pltpu.SemaphoreType.DMA((n,)), pltpu.SemaphoreType.DMA((n,))],
    )(*arrays)


def sibling_swap(name, arrays):
    n = len(arrays)

    def body(*refs):
        ins, outs = refs[:n], refs[n:2 * n]
        send_sems, recv_sems = refs[2 * n:]
        x, y, c = _place()
        copies = []
        for i in range(n):
            away = pltpu.make_async_remote_copy(
                src_ref=ins[i], dst_ref=outs[i], send_sem=send_sems.at[i], recv_sem=recv_sems.at[i],
                device_id=(x, y, 1 - c), device_id_type=MESH_ID)
            away.start()
            copies.append(away)
        for cp in copies:
            cp.wait()

    return pl.pallas_call(
        body, name=name, in_specs=_any_specs(n), out_specs=_any_specs(n),
        out_shape=[jax.ShapeDtypeStruct(a.shape, a.dtype) for a in arrays],
        scratch_shapes=[pltpu.SemaphoreType.DMA((n,)), pltpu.SemaphoreType.DMA((n,))],
    )(*arrays)


_FLIPS = ((1, 0), (0, 1), (1, 1))


def chip_exchange(name, arrays, axis, all_to_all):
    n = len(arrays)

    def body(*refs):
        ins, outs = refs[:n], refs[n:2 * n]
        send_sems, recv_sems, local_sems = refs[2 * n:]
        x, y, c = _place()
        me = 2 * x + y
        copies = []
        for i in range(n):
            dst = _at_axis(outs[i], axis, me, 1)
            mine = pltpu.make_async_copy(_at_axis(ins[i], axis, me, 1) if all_to_all else ins[i], dst, local_sems.at[i])
            mine.start()
            copies.append(mine)
            for f, (fx, fy) in enumerate(_FLIPS):
                px = (1 - x) if fx else x
                py = (1 - y) if fy else y
                src = _at_axis(ins[i], axis, 2 * px + py, 1) if all_to_all else ins[i]
                away = pltpu.make_async_remote_copy(
                    src_ref=src, dst_ref=dst, send_sem=send_sems.at[3 * i + f], recv_sem=recv_sems.at[3 * i + f],
                    device_id=(px, py, c), device_id_type=MESH_ID)
                away.start()
                copies.append(away)
        for cp in copies:
            cp.wait()

    shapes = [jax.ShapeDtypeStruct(a.shape[:axis] + (N_CHIPS,) + a.shape[axis + 1:], a.dtype) for a in arrays]
    return pl.pallas_call(
        body, name=name, in_specs=_any_specs(n), out_specs=_any_specs(n), out_shape=shapes,
        scratch_shapes=[pltpu.SemaphoreType.DMA((3 * n,)), pltpu.SemaphoreType.DMA((3 * n,)), pltpu.SemaphoreType.DMA((n,))],
    )(*arrays)


SIBLING = ((0, 0, 1),)
CHIPS = ((1, 0, 0), (0, 1, 0), (1, 1, 0))
ICI_CHUNK_BYTES = 2 * 1024 * 1024
D2D_CHUNK_BYTES = 4 * 1024 * 1024


def _peer(flip):
    x, y, c = _place()
    return tuple((1 - v) if f else v for v, f in zip((x, y, c), flip))


def _core():
    return lax.axis_index("c")


def _chip():
    return 2 * lax.axis_index("x") + lax.axis_index("y")


def _linear_step(grid):
    i = pl.program_id(0)
    for a in range(1, len(grid)):
        i = i * grid[a] + pl.program_id(a)
    return i


def stream_reduce(name, x, grid, block, own_map, send_maps, flips, out_shape, out_block, out_map, wire_dtype=None):
    n_steps = math.prod(grid)
    n_p = len(flips)
    vm_block = block = tuple(1 if d is None else d for d in block)
    out_block = tuple(1 if d is None else d for d in out_block)
    staged = wire_dtype is not None and wire_dtype != x.dtype
    slot_dtype = wire_dtype if staged else x.dtype

    def body(own_ref, *rest):
        send_refs = rest[:n_p]
        o_ref, recv, send_sems, recv_sems, credits = rest[n_p:n_p + 5]
        stage = rest[n_p + 5] if staged else None
        i = _linear_step(grid)
        s = i % 2
        copies = []
        for j, flip in enumerate(flips):
            src = send_refs[j]
            if staged:
                stage[j, s] = send_refs[j][...].astype(wire_dtype)
                src = stage.at[j, s]

            @pl.when(i >= 2)
            def _():
                pl.semaphore_wait(credits.at[j, s], 1)
            cp = pltpu.make_async_remote_copy(
                src_ref=src, dst_ref=recv.at[j, s], send_sem=send_sems.at[j, s], recv_sem=recv_sems.at[j, s],
                device_id=_peer(flip), device_id_type=MESH_ID)
            cp.start()
            copies.append(cp)
        acc = own_ref[...]
        for j, cp in enumerate(copies):
            cp.wait_recv()
            acc = acc + recv[j, s].astype(acc.dtype)
        o_ref[...] = acc.reshape(o_ref.shape)
        for cp in copies:
            cp.wait_send()
        for j, flip in enumerate(flips):
            @pl.when(i < n_steps - 2)
            def _():
                pl.semaphore_signal(credits.at[j, s], inc=1, device_id=_peer(flip), device_id_type=MESH_ID)

    in_specs = [pl.BlockSpec(block, own_map)] + [pl.BlockSpec(block, m) for m in send_maps]
    scratch = [pltpu.VMEM((n_p, 2) + vm_block, slot_dtype), pltpu.SemaphoreType.DMA((n_p, 2)),
               pltpu.SemaphoreType.DMA((n_p, 2)), pltpu.SemaphoreType.REGULAR((n_p, 2))]
    if staged:
        scratch.append(pltpu.VMEM((n_p, 2) + vm_block, slot_dtype))
    return pl.pallas_call(
        body, name=name, grid=grid, in_specs=in_specs, out_specs=pl.BlockSpec(out_block, out_map),
        out_shape=jax.ShapeDtypeStruct(out_shape, x.dtype), scratch_shapes=scratch,
        compiler_params=_params(("arbitrary",) * len(grid)),
    )(*([x] * (1 + n_p)))


def stream_gather(name, x, grid, block, in_map, flips, out_shape, out_block, out_map):
    n_p = len(flips)
    assert grid[-1] == n_p + 1
    n_steps = math.prod(grid[:-1])
    vm_block = block = tuple(1 if d is None else d for d in block)
    out_block = tuple(1 if d is None else d for d in out_block)

    def body(x_ref, o_ref, recv, send_sems, recv_sems, credits):
        i = _linear_step(grid[:-1])
        q = pl.program_id(len(grid) - 1)
        s = i % 2

        def copy(j):
            return pltpu.make_async_remote_copy(
                src_ref=x_ref, dst_ref=recv.at[j, s], send_sem=send_sems.at[j, s], recv_sem=recv_sems.at[j, s],
                device_id=_peer(flips[j]), device_id_type=MESH_ID)

        @pl.when(q == 0)
        def _():
            for j in range(n_p):
                @pl.when(i >= 2)
                def _():
                    pl.semaphore_wait(credits.at[j, s], 1)
                copy(j).start()
            o_ref[...] = x_ref[...].reshape(o_ref.shape)
            for j in range(n_p):
                copy(j).wait_send()

        for j in range(n_p):
            @pl.when(q == j + 1)
            def _():
                copy(j).wait_recv()
                o_ref[...] = recv[j, s].reshape(o_ref.shape)

                @pl.when(i < n_steps - 2)
                def _():
                    pl.semaphore_signal(credits.at[j, s], inc=1, device_id=_peer(flips[j]), device_id_type=MESH_ID)

    return pl.pallas_call(
        body, name=name, grid=grid, in_specs=[pl.BlockSpec(block, in_map)], out_specs=pl.BlockSpec(out_block, out_map),
        out_shape=jax.ShapeDtypeStruct(out_shape, x.dtype),
        scratch_shapes=[pltpu.VMEM((n_p, 2) + vm_block, x.dtype), pltpu.SemaphoreType.DMA((n_p, 2)),
                        pltpu.SemaphoreType.DMA((n_p, 2)), pltpu.SemaphoreType.REGULAR((n_p, 2))],
        compiler_params=_params(("arbitrary",) * len(grid)),
    )(x)


def _chip_of_substep(q):
    mask = jnp.where(q == 1, 2, jnp.where(q == 2, 1, jnp.where(q == 3, 3, 0)))
    return jnp.bitwise_xor(_chip(), mask)


def gather_weight(name, shard):
    nl, r, c = shard.shape
    r2 = r // 2
    f32_per_elem = 4 // shard.dtype.itemsize
    tr = _rows_tile(r2, c, budget=ICI_CHUNK_BYTES * f32_per_elem, step=16)
    nb = r2 // tr
    half = stream_gather(
        name + "_chips", shard, (nl, nb, N_CHIPS), (None, tr, c), lambda l, i, q: (l, _core() * nb + i, 0), CHIPS,
        (nl, N_CHIPS, r2, c), (None, None, tr, c), lambda l, i, q: (l, _chip_of_substep(q), i, 0))
    trd = _rows_tile(r2, c, budget=D2D_CHUNK_BYTES * f32_per_elem, step=16)
    nbd = r2 // trd
    both = stream_gather(
        name + "_cores", half, (nl, N_CHIPS, nbd, 2), (None, None, trd, c), lambda l, k, i, q: (l, k, i, 0), SIBLING,
        (nl, N_CHIPS, 2, r2, c), (None, None, None, trd, c), lambda l, k, i, q: (l, k, _core() + q - 2 * _core() * q, i, 0))
    return both.reshape(nl, N_CHIPS, r, c)


def reduce_scatter_streamed(name, g):
    nl, _, r, c = g.shape
    r2 = r // 2
    tr_d2d = _rows_tile(r2, c, budget=D2D_CHUNK_BYTES, step=16)
    nbd = r2 // tr_d2d
    chip_sum = stream_reduce(
        name + "_cores", g, (nl, N_CHIPS, nbd), (None, None, tr_d2d, c),
        lambda l, k, i: (l, k, _core() * nbd + i, 0), [lambda l, k, i: (l, k, (1 - _core()) * nbd + i, 0)], SIBLING,
        (nl, N_CHIPS, r2, c), (None, None, tr_d2d, c), lambda l, k, i: (l, k, i, 0))
    tr = _rows_tile(r2, c, budget=ICI_CHUNK_BYTES, step=16)
    nb = r2 // tr
    blk4 = (None, None, tr, c)
    masks = (2, 1, 3)
    mine = stream_reduce(
        name + "_chips", chip_sum, (nl, nb), blk4,
        lambda l, i: (l, _chip(), i, 0),
        [functools.partial(lambda l, i, m: (l, jnp.bitwise_xor(_chip(), m), i, 0), m=m) for m in masks], CHIPS,
        (nl, r2, c), (None, tr, c), lambda l, i: (l, i, 0), wire_dtype=GRAD_WIRE_DTYPE)
    both = stream_gather(
        name + "_join", mine, (nl, nbd, 2), (None, tr_d2d, c), lambda l, i, q: (l, i, 0), SIBLING,
        (nl, 2, r2, c), (None, None, tr_d2d, c), lambda l, i, q: (l, _core() + q - 2 * _core() * q, i, 0))
    return both.reshape(nl, r, c)


def add2(name, a, b):
    shape = a.shape
    a2, b2 = a.reshape(-1, shape[-1]), b.reshape(-1, shape[-1])
    rows, w = a2.shape
    tm = _rows_tile(rows, w)
    return _rowwise(name, lambda u, v: u + v, rows, tm, [(a2, w, 0), (b2, w, 0)], [], [(w, F32)])[0].reshape(shape)


def _rows_tile(rows, width, budget=2 * 1024 * 1024, step=SUBLANE):
    best = step
    t = step
    while t <= rows:
        if rows % t == 0 and t * width * 4 <= budget:
            best = t
        t += step
    return best


def sum_slots(name, a):
    nl, _, r, c = a.shape
    tr = _rows_tile(r, c)

    def body(s0, s1, s2, s3, o_ref):
        o_ref[...] = ((s0[...] + s1[...]) + s2[...]) + s3[...]

    specs = [pl.BlockSpec((None, None, tr, c), functools.partial(lambda l, i, k: (l, k, i, 0), k=k)) for k in range(N_CHIPS)]
    return pl.pallas_call(
        body, name=name, grid=(nl, r // tr), in_specs=specs,
        out_specs=pl.BlockSpec((None, tr, c), lambda l, i: (l, i, 0)),
        out_shape=jax.ShapeDtypeStruct((nl, r, c), F32),
        compiler_params=_params(("parallel", "parallel")),
    )(a, a, a, a)


def reduce_scatter_big(grads):
    own, got = sibling_swap_halves("rs_swap_halves", grads, 2)
    chip_sum = [add2("rs_add_cores", a, b) for a, b in zip(own, got)]
    spread = chip_exchange("rs_chips", chip_sum, 1, True)
    mine = [sum_slots("rs_sum_chips", a) for a in spread]
    return sibling_join_halves("rs_join_halves", mine, 1)


def all_reduce_small(flat):
    other = sibling_swap("ar_swap", [flat])[0]
    chip = add2("ar_add_cores", flat, other)
    slots = chip_exchange("ar_chips", [chip.reshape((1,) + chip.shape)], 0, False)[0]
    rows = flat.shape[0]
    tm = _rows_tile(rows, LANE)
    nb = rows // tm
    s2 = slots.reshape(N_CHIPS * rows, LANE)
    return _rowwise("ar_sum_chips", lambda a, b, c, d: ((a + b) + c) + d, rows, tm,
                    [(s2, LANE, 0, k * nb) for k in range(N_CHIPS)], [], [(LANE, F32)])[0]


def _adamw_math(w, g, m, v):
    m = ADAM_B1 * m + (1.0 - ADAM_B1) * g
    v = ADAM_B2 * v + (1.0 - ADAM_B2) * (g * g)
    m_hat = m / (1.0 - ADAM_B1 ** ADAM_STEP)
    v_hat = v / (1.0 - ADAM_B2 ** ADAM_STEP)
    delta = -ADAM_LR * (m_hat / (jnp.sqrt(v_hat) + ADAM_EPS) + ADAM_WD * w)
    return delta, m, v


def adamw(name, w, g, m, v):
    shape = w.shape
    width = shape[-1]
    flat = [t.reshape(-1, width) for t in (w, g, m, v)]
    rows = flat[0].shape[0]
    tm = _rows_tile(rows, width, budget=1024 * 1024)
    res = _rowwise(name, _adamw_math, rows, tm, [(t, width, 0) for t in flat], [], [(width, F32)] * 3)
    return [r.reshape(shape) for r in res]


def _ffn_fwd(tag, h, g_norm, wi, wo):
    n, d = h.shape
    ff = wo.shape[0]
    xn = norm_fwd(tag + "_norm", h, g_norm)
    gu = mm_nn_colsharded(tag + "_wi", xn, wi, BF16)
    act = _rowwise(tag + "_swiglu", lambda g, u: (g.astype(F32) * jax.nn.sigmoid(g.astype(F32))) * u.astype(F32),
                   n, 512, [(gu, ff, 0), (gu, ff, 1)], [], [(ff, BF16)])[0]
    out = mm_nn(tag + "_wo", act, wo, F32, alpha=0.5, res=h, tm=512, tk=2816)
    return out, (xn, gu, act)


def _ffn_bwd(tag, d, h, g_norm, wi_t, wo_t, saved):
    xn, gu, act = saved
    n, dm = h.shape
    ff = wo_t.shape[1]
    dact = mm_nn(tag + "_dact", d, wo_t, BF16, alpha=0.5, tn=1408)
    dwo = mm_tn(tag + "_dwo", act, d, F32, alpha=0.5, tm=1408)

    def swiglu_bwd(g, u, da):
        g, u, da = g.astype(F32), u.astype(F32), da.astype(F32)
        sg = jax.nn.sigmoid(g)
        dg = da * u * (sg * (1.0 + g * (1.0 - sg)))
        du = da * (g * sg)
        return jnp.concatenate([dg, du], axis=1)

    dgu = _rowwise(tag + "_dswiglu", swiglu_bwd, n, 512, [(gu, ff, 0), (gu, ff, 1), (dact, ff, 0)], [], [(2 * ff, BF16)])[0]
    dwi = mm_tn_colsharded(tag + "_dwi", xn, dgu, F32)
    dxn = mm_nn(tag + "_dxn", dgu, wi_t, F32, tk=1408)
    d_in, dg_norm = norm_bwd(tag + "_dnorm", dxn, h, g_norm, d)
    return d_in, dg_norm.reshape(dm), dwi, dwo.reshape(N_CHIPS, ff // N_CHIPS, dm)


def _col(v):
    return v.reshape(SSM_COLS, 1)


def _layer_fwd(h, lw, p_l, seq):
    n, d = h.shape
    h1, ffn1_saved = _ffn_fwd("ffn1", h, lw["ffn1_norm"], lw["ffn1_wi"], lw["ffn1_wo"])

    xn2 = norm_fwd("mix_norm", h1, lw["mix_norm"])
    z = mm_nn("mix_in", xn2, lw["w_in"], F32)
    log_dt = jnp.repeat(lw["ssm_log_dt"], SSM_STATE)
    b_re, b_im = lw["ssm_b_re"].reshape(SSM_COLS, SSM_CH), lw["ssm_b_im"].reshape(SSM_COLS, SSM_CH)
    pw_re, pw_im, bb_re, bb_im = ssm_prep("ssm_prep", _col(lw["ssm_lambda_re"]), _col(lw["ssm_lambda_im"]), _col(log_dt), b_re, b_im)
    bbd = expand_b(bb_re, bb_im)
    cbd = expand_c(lw["ssm_c_re"], lw["ssm_c_im"])
    bu = ssm_in("ssm_in", z, bbd)
    s = ssm_scan("ssm_scan", scan_coefficients(pw_re, pw_im, False), bu, seq, False)
    y0c = ssm_out("ssm_out", s, cbd)

    def skip_gelu(yc, zs, dvec):
        y0 = yc + dvec * zs
        return y0, _gelu(y0)

    y0, y1 = _rowwise("ssm_gelu", skip_gelu, n, 512, [(y0c, SSM_WIDTH, 0), (z, SSM_WIDTH, 0)],
                      [lw["ssm_d"].reshape(1, SSM_WIDTH)], [(SSM_WIDTH, F32), (SSM_WIDTH, F32)])
    t = mm_nn("ssm_glu_mm", y1, lw["ssm_w_glu"], F32)
    y2 = _rowwise("ssm_glu", lambda a, b: a * jax.nn.sigmoid(b), n, 512, [(y1, SSM_WIDTH, 0), (t, SSM_WIDTH, 0)], [],
                  [(SSM_WIDTH, BF16)])[0]

    q = pool_window("pool_window", z, SSM_WIDTH // POOL_CH, seq, BF16, False)
    wp_eff = lw["pool_w"] * lw["pool_scale"].reshape(POOL_GROUPS, 1, POOL_CH)
    yp = pool_mm("pool_mm", q, wp_eff, BF16)
    m = jnp.concatenate([y2, yp], axis=1)
    h2 = mm_nn("mix_out", m, lw["w_out"], F32, res=h1)

    h3, ffn2_saved = _ffn_fwd("ffn2", h2, lw["ffn2_norm"], lw["ffn2_wi"], lw["ffn2_wo"])

    xn4 = norm_fwd("ple_norm", h3, lw["ple_norm"])
    tg = mm_nn("ple_gate", xn4, lw["ple_w_gate"], F32)
    e = mm_nn_colsharded("ple_proj", p_l, lw["ple_w_proj"], F32)
    h4 = _rowwise("ple_add", lambda a, b, c: a + jax.nn.sigmoid(b) * c, n, 512, [(h3, d, 0), (tg, d, 0), (e, d, 0)], [], [(d, F32)])[0]
    saved = dict(h=h, h1=h1, h2=h2, h3=h3, ffn1=ffn1_saved, ffn2=ffn2_saved, xn2=xn2, z=z, s=s, y0=y0, y1=y1, t=t, m=m, q=q,
                 xn4=xn4, tg=tg, e=e, pw_re=pw_re, pw_im=pw_im, bbd=bbd, cbd=cbd)
    return h4, saved


def _layer_bwd(d, lw, p_l, sv, seq):
    n, dm = d.shape
    g = {}
    def ple_bwd(dd, tg, e):
        gate = jax.nn.sigmoid(tg)
        return dd * e * gate * (1.0 - gate), dd * gate

    dtg, de = _rowwise("ple_dadd", ple_bwd, n, 512, [(d, dm, 0), (sv["tg"], dm, 0), (sv["e"], dm, 0)], [], [(dm, BF16), (dm, BF16)])
    g["ple_w_proj"] = mm_tn_colsharded("ple_dproj", p_l, de, F32)
    g["ple_w_gate"] = mm_tn("ple_dgate_w", sv["xn4"], dtg, F32).reshape(N_CHIPS, dm // N_CHIPS, dm)
    dxn4 = mm_nn("ple_dgate_x", dtg, lw["ple_w_gate_t"], F32)
    d, dg = norm_bwd("ple_dnorm", dxn4, sv["h3"], lw["ple_norm"], d)
    g["ple_norm"] = dg.reshape(dm)

    d, g["ffn2_norm"], g["ffn2_wi"], g["ffn2_wo"] = _ffn_bwd("ffn2b", d, sv["h2"], lw["ffn2_norm"], lw["ffn2_wi_t"], lw["ffn2_wo_t"], sv["ffn2"])

    dmix = mm_nn("mix_dout_x", d, lw["w_out_t"], F32)
    g["w_out"] = mm_tn("mix_dout_w", sv["m"], d, F32).reshape(N_CHIPS, dm // N_CHIPS, dm)
    pool_cb = SSM_WIDTH // POOL_CH
    wp_eff = lw["pool_w"] * lw["pool_scale"].reshape(POOL_GROUPS, 1, POOL_CH)
    dq = pool_mm_t("pool_dmm_x", dmix, pool_cb, wp_eff, F32)
    dwp_eff = pool_grad_w("pool_dmm_w", sv["q"], dmix, pool_cb)
    g["pool_w"] = dwp_eff * lw["pool_scale"].reshape(POOL_GROUPS, 1, POOL_CH)
    g["pool_scale"] = jnp.sum(dwp_eff * lw["pool_w"], axis=1).reshape(POOL_WIDTH)
    dzp = pool_window("pool_dwindow", dq, 0, seq, BF16, True)

    def glu_bwd(dy2, y1, t):
        sg = jax.nn.sigmoid(t)
        return dy2 * y1 * sg * (1.0 - sg), dy2 * sg

    dt_, dy1a = _rowwise("ssm_dglu", glu_bwd, n, 512, [(dmix, SSM_WIDTH, 0), (sv["y1"], SSM_WIDTH, 0), (sv["t"], SSM_WIDTH, 0)], [],
                         [(SSM_WIDTH, BF16), (SSM_WIDTH, F32)])
    g["ssm_w_glu"] = mm_tn("ssm_dglu_w", sv["y1"], dt_, F32).reshape(N_CHIPS, SSM_WIDTH // N_CHIPS, SSM_WIDTH)
    dy1b = mm_nn("ssm_dglu_x", dt_, lw["ssm_w_glu_t"], F32)

    def gelu_bwd(da, db, y0, zs, dvec):
        dy0 = (da + db) * _gelu_grad(y0)
        return dy0, dy0 * dvec, jnp.sum(dy0 * zs, axis=0, keepdims=True)

    dy0, dzs_a, dd = _rowwise("ssm_dgelu", gelu_bwd, n, 512,
                              [(dy1a, SSM_WIDTH, 0), (dy1b, SSM_WIDTH, 0), (sv["y0"], SSM_WIDTH, 0), (sv["z"], SSM_WIDTH, 0)],
                              [lw["ssm_d"].reshape(1, SSM_WIDTH)], [(SSM_WIDTH, F32), (SSM_WIDTH, F32)], [(1, SSM_WIDTH)])
    g["ssm_d"] = dd.reshape(SSM_WIDTH)
    g["ssm_c_re"], g["ssm_c_im"] = compact_c(ssm_grad_c("ssm_dc", sv["s"], dy0))
    v = ssm_out_t("ssm_dout", dy0, sv["cbd"])
    a, dlam = ssm_scan("ssm_scan_adj", scan_coefficients(sv["pw_re"], sv["pw_im"], True), v, seq, True, states=sv["s"])
    dbb_re, dbb_im = compact_b(ssm_grad_b("ssm_db", sv["z"], a))
    dzs = ssm_in_t("ssm_din", a, sv["bbd"], dzs_a)
    dlam = jnp.sum(dlam, axis=1)
    log_dt = jnp.repeat(lw["ssm_log_dt"], SSM_STATE)
    b_re, b_im = lw["ssm_b_re"].reshape(SSM_COLS, SSM_CH), lw["ssm_b_im"].reshape(SSM_COLS, SSM_CH)
    glr, gli, gld, gbr, gbi = ssm_prep_bwd("ssm_prep_bwd", _col(lw["ssm_lambda_re"]), _col(lw["ssm_lambda_im"]), _col(log_dt), b_re, b_im,
                                           _col(dlam[0]), _col(dlam[1]), dbb_re, dbb_im)
    g["ssm_lambda_re"] = glr.reshape(SSM_GROUPS, SSM_STATE)
    g["ssm_lambda_im"] = gli.reshape(SSM_GROUPS, SSM_STATE)
    g["ssm_log_dt"] = jnp.sum(gld.reshape(SSM_GROUPS, SSM_STATE), axis=1)
    g["ssm_b_re"] = gbr.reshape(SSM_GROUPS, SSM_STATE, SSM_CH)
    g["ssm_b_im"] = gbi.reshape(SSM_GROUPS, SSM_STATE, SSM_CH)

    dz = jnp.concatenate([dzs.astype(BF16), dzp], axis=1)
    g["w_in"] = mm_tn("mix_din_w", sv["xn2"], dz, F32).reshape(N_CHIPS, dm // N_CHIPS, dm)
    dxn2 = mm_nn("mix_din_x", dz, lw["w_in_t"], F32)
    d, dg = norm_bwd("mix_dnorm", dxn2, sv["h1"], lw["mix_norm"], d)
    g["mix_norm"] = dg.reshape(dm)

    d, g["ffn1_norm"], g["ffn1_wi"], g["ffn1_wo"] = _ffn_bwd("ffn1b", d, sv["h"], lw["ffn1_norm"], lw["ffn1_wi_t"], lw["ffn1_wo_t"], sv["ffn1"])
    return d, g


def _flatten_small(tensors):
    flat = jnp.concatenate([t.reshape(-1) for t in tensors])
    pad = (-flat.shape[0]) % (SUBLANE * LANE)
    return jnp.pad(flat, (0, pad)).reshape(-1, LANE)


def _unflatten_small(flat, like):
    flat = flat.reshape(-1)
    out, off = [], 0
    for t in like:
        out.append(flat[off:off + t.size].reshape(t.shape))
        off += t.size
    return out


def kernel(x, p, ffn1_norm, ffn1_wi, ffn1_wo, mix_norm, w_in, ssm_lambda_re, ssm_lambda_im, ssm_log_dt, ssm_b_re, ssm_b_im, ssm_c_re, ssm_c_im, ssm_d, ssm_w_glu, pool_w, pool_scale, w_out, ffn2_norm, ffn2_wi, ffn2_wo, ple_norm, ple_w_gate, ple_w_proj, final_norm, loss_target, m_ffn1_norm, m_ffn1_wi, m_ffn1_wo, m_mix_norm, m_w_in, m_ssm_lambda_re, m_ssm_lambda_im, m_ssm_log_dt, m_ssm_b_re, m_ssm_b_im, m_ssm_c_re, m_ssm_c_im, m_ssm_d, m_ssm_w_glu, m_pool_w, m_pool_scale, m_w_out, m_ffn2_norm, m_ffn2_wi, m_ffn2_wo, m_ple_norm, m_ple_w_gate, m_ple_w_proj, m_final_norm, v_ffn1_norm, v_ffn1_wi, v_ffn1_wo, v_mix_norm, v_w_in, v_ssm_lambda_re, v_ssm_lambda_im, v_ssm_log_dt, v_ssm_b_re, v_ssm_b_im, v_ssm_c_re, v_ssm_c_im, v_ssm_d, v_ssm_w_glu, v_pool_w, v_pool_scale, v_w_out, v_ffn2_norm, v_ffn2_wi, v_ffn2_wo, v_ple_norm, v_ple_w_gate, v_ple_w_proj, v_final_norm):
    given = dict(locals())
    w = {k: given[k] for k in WEIGHTS}
    mom = {k: given["m_" + k] for k in WEIGHTS}
    var = {k: given["v_" + k] for k in WEIGHTS}
    bsz, seq, dm = x.shape
    n = bsz * seq
    depth = ffn1_wi.shape[0]

    full = {}
    for k in BIG:
        a = gather_weight("gather_" + k, w[k].astype(MXU_DTYPE))
        if k in COL_SHARDED:
            full[k] = a
            if k != "ple_w_proj":
                full[k + "_t"] = transpose_colsharded(a)
        else:
            full[k] = a.reshape(depth, N_CHIPS * a.shape[2], a.shape[3])
            full[k + "_t"] = jnp.swapaxes(full[k], 1, 2)
    for k in SMALL:
        if k != "final_norm":
            full[k] = w[k]
    layers = [{k: v[l] for k, v in full.items()} for l in range(depth)]

    p2 = p.reshape(depth, n, p.shape[-1])
    h_last = x.reshape(n, dm)
    saved = []
    for l in range(depth):
        h_last, sv = _layer_fwd(h_last, layers[l], p2[l], seq)
        saved.append(sv)

    def head(hh, tgt, gf):
        r = lax.rsqrt(jnp.mean(hh * hh, axis=-1, keepdims=True) + NORM_EPS)
        xh = hh * r
        diff = xh * gf - tgt
        dy = diff * (1.0 / dm)
        dxh = dy * gf
        dx = r * (dxh - xh * jnp.mean(dxh * xh, axis=-1, keepdims=True))
        return dx, jnp.sum(diff * diff, axis=0, keepdims=True) * (0.5 / dm), jnp.sum(dy * xh, axis=0, keepdims=True)

    d_last, loss_cols, g_final = _rowwise("loss_head", head, n, 512, [(h_last, dm, 0), (loss_target.reshape(n, dm), dm, 0)],
                                          [final_norm.reshape(1, dm)], [(dm, F32)], [(1, dm), (1, dm)])
    loss = lax.psum(jnp.sum(loss_cols), ("x", "y", "c"))

    d_x = d_last
    layer_grads = [None] * depth
    for l in reversed(range(depth)):
        d_x, layer_grads[l] = _layer_bwd(d_x, layers[l], p2[l], saved[l], seq)
    grads = {k: jnp.stack([g[k] for g in layer_grads]) for k in layer_grads[0]}
    grad_x = d_x.reshape(bsz, seq, dm)

    big_sum = [reduce_scatter_streamed("rs_" + k, grads[k]) for k in BIG]
    small_keys = [k for k in SMALL]
    small_parts = [grads[k] if k != "final_norm" else g_final.reshape(dm) for k in small_keys]
    small_sum = _unflatten_small(all_reduce_small(_flatten_small(small_parts)), small_parts)
    g_out = dict(zip(BIG, big_sum))
    g_out.update(dict(zip(small_keys, small_sum)))
    for k in BIG:
        g_out[k] = g_out[k].reshape(w[k].shape)

    delta, new_m, new_v = {}, {}, {}
    for k in BIG:
        delta[k], new_m[k], new_v[k] = adamw("adamw_" + k, w[k], g_out[k], mom[k], var[k])
    sw = adamw("adamw_small", *[_flatten_small([t[k] for k in small_keys]) for t in (w, g_out, mom, var)])
    for name, flat in zip((delta, new_m, new_v), sw):
        for k, t in zip(small_keys, _unflatten_small(flat, [w[k] for k in small_keys])):
            name[k] = t

    return (loss, grad_x, *[g_out[k] for k in WEIGHTS], *[delta[k] for k in WEIGHTS],
            *[new_m[k] for k in WEIGHTS], *[new_v[k] for k in WEIGHTS])
```

```python
import functools
import math

import jax
import jax.numpy as jnp
from jax import lax
from jax.experimental import pallas as pl
from jax.experimental.pallas import tpu as pltpu

F32 = jnp.float32
BF16 = jnp.bfloat16
MXU_DTYPE = jnp.bfloat16
GRAD_WIRE_DTYPE = jnp.bfloat16
STATE_DTYPE = jnp.bfloat16
VMEM_LIMIT = 56 * 1024 * 1024
LANE = 128
SUBLANE = 8

N_CHIPS = 4
SSM_GROUPS = 32
SSM_STATE = 64
SSM_CH = 16
SSM_WIDTH = SSM_GROUPS * SSM_CH
SSM_COLS = SSM_GROUPS * SSM_STATE
POOL_GROUPS = 4
POOL_CH = 128
POOL_WIDTH = POOL_GROUPS * POOL_CH
SSM_TILE = 256
NORM_EPS = 1e-6
ADAM_LR = 0.001
ADAM_B1 = 0.9
ADAM_B2 = 0.999
ADAM_EPS = 1e-08
ADAM_WD = 0.01
ADAM_STEP = 10
MESH_ID = pl.DeviceIdType.MESH

BIG = ("ffn1_wi", "ffn1_wo", "w_in", "ssm_w_glu", "w_out", "ffn2_wi", "ffn2_wo", "ple_w_gate", "ple_w_proj")
COL_SHARDED = ("ffn1_wi", "ffn2_wi", "ple_w_proj")
SMALL = ("ffn1_norm", "mix_norm", "ssm_lambda_re", "ssm_lambda_im", "ssm_log_dt", "ssm_b_re", "ssm_b_im",
         "ssm_c_re", "ssm_c_im", "ssm_d", "pool_w", "pool_scale", "ffn2_norm", "ple_norm", "final_norm")
WEIGHTS = ("ffn1_norm", "ffn1_wi", "ffn1_wo", "mix_norm", "w_in", "ssm_lambda_re", "ssm_lambda_im", "ssm_log_dt",
           "ssm_b_re", "ssm_b_im", "ssm_c_re", "ssm_c_im", "ssm_d", "ssm_w_glu", "pool_w", "pool_scale", "w_out",
           "ffn2_norm", "ffn2_wi", "ffn2_wo", "ple_norm", "ple_w_gate", "ple_w_proj", "final_norm")


def _tile(dim, target):
    best = None
    t = LANE
    while t <= min(dim, target):
        if dim % t == 0:
            best = t
        t += LANE
    return best if best is not None else dim


def _params(sem):
    return pltpu.CompilerParams(dimension_semantics=sem, vmem_limit_bytes=VMEM_LIMIT)


CHIPS = ((1, 0, 0), (0, 1, 0), (1, 1, 0))


def _ride_scratch(n):
    return [pltpu.SemaphoreType.DMA((3 * n,)), pltpu.SemaphoreType.DMA((3 * n,)), pltpu.SemaphoreType.DMA((n,))]


def _ride_copies(ins, outs, send_sems, recv_sems, local_sems):
    x, y, c = lax.axis_index("x"), lax.axis_index("y"), lax.axis_index("c")
    me = 2 * x + y
    copies = []
    for i in range(len(ins)):
        dst = outs[i].at[pl.ds(me, 1)]
        copies.append(pltpu.make_async_copy(ins[i], dst, local_sems.at[i]))
        for f, (fx, fy, _) in enumerate(CHIPS):
            peer = ((1 - x) if fx else x, (1 - y) if fy else y, c)
            copies.append(pltpu.make_async_remote_copy(
                src_ref=ins[i], dst_ref=dst, send_sem=send_sems.at[3 * i + f], recv_sem=recv_sems.at[3 * i + f],
                device_id=peer, device_id_type=pl.DeviceIdType.MESH))
    return copies


def _ride(grid, ins, outs, sems):
    if not ins:
        return
    ids = [pl.program_id(a) for a in range(len(grid))]
    first = functools.reduce(jnp.logical_and, [i == 0 for i in ids])
    last = functools.reduce(jnp.logical_and, [i == g - 1 for i, g in zip(ids, grid)])

    @pl.when(first)
    def _():
        for cp in _ride_copies(ins, outs, *sems):
            cp.start()

    @pl.when(last)
    def _():
        for cp in _ride_copies(ins, outs, *sems):
            cp.wait()


def _ride_out_shapes(riders):
    return [jax.ShapeDtypeStruct((N_CHIPS,) + r.shape[1:], r.dtype) for r in riders]


def _mm(name, a, b, out_shape, out_dtype, grid, a_spec, b_spec, o_spec, contract, alpha=1.0, res=None, riders=()):
    n_k = grid[2]
    acc_shape = tuple(d for d in o_spec.block_shape if d is not None)
    n_in = 2 + (res is not None)
    n_r = len(riders)

    def body(*refs):
        a_ref, b_ref = refs[0], refs[1]
        r_ref = refs[2] if res is not None else None
        o_ref = refs[n_in + n_r]
        if n_r:
            _ride(grid, refs[n_in:n_in + n_r], refs[n_in + n_r + 1:n_in + 2 * n_r + 1], refs[len(refs) - 3:])

        def product():
            return lax.dot_general(a_ref[...].astype(MXU_DTYPE), b_ref[...].astype(MXU_DTYPE),
                                   (contract, ((), ())), preferred_element_type=F32)

        def finish(v):
            if alpha != 1.0:
                v = v * alpha
            if r_ref is not None:
                v = v + r_ref[...].astype(F32)
            o_ref[...] = v.astype(out_dtype)

        if n_k == 1:
            finish(product())
            return
        acc = refs[n_in + 2 * n_r + 1]
        k = pl.program_id(2)

        @pl.when(k == 0)
        def _():
            acc[...] = product()

        @pl.when(k > 0)
        def _():
            acc[...] += product()

        @pl.when(k == n_k - 1)
        def _():
            finish(acc[...])

    in_specs = [a_spec, b_spec]
    operands = [a, b]
    if res is not None:
        in_specs.append(o_spec)
        operands.append(res)
    scratch = [pltpu.VMEM(acc_shape, F32)] if n_k > 1 else []
    if not n_r:
        return pl.pallas_call(
            body, name=name, grid=grid, in_specs=in_specs, out_specs=o_spec,
            out_shape=jax.ShapeDtypeStruct(out_shape, out_dtype), scratch_shapes=scratch,
            compiler_params=_params(("parallel", "parallel", "arbitrary")),
        )(*operands)
    any_spec = pl.BlockSpec(memory_space=pl.ANY)
    return pl.pallas_call(
        body, name=name, grid=grid, in_specs=in_specs + [any_spec] * n_r, out_specs=[o_spec] + [any_spec] * n_r,
        out_shape=[jax.ShapeDtypeStruct(out_shape, out_dtype)] + _ride_out_shapes(riders),
        scratch_shapes=scratch + _ride_scratch(n_r),
        compiler_params=_params(("arbitrary", "arbitrary", "arbitrary")),
    )(*operands, *riders)


NN = ((1,), (0,))
NT = ((1,), (1,))
TN = ((0,), (0,))


def mm_nn(name, a, b, out_dtype, alpha=1.0, res=None, tm=1024, tn=1024, tk=1024, riders=()):
    m, k = a.shape
    n = b.shape[1]
    tm, tn, tk = _tile(m, tm), _tile(n, tn), _tile(k, tk)
    return _mm(name, a, b, (m, n), out_dtype, (m // tm, n // tn, k // tk),
               pl.BlockSpec((tm, tk), lambda i, j, kk: (i, kk)),
               pl.BlockSpec((tk, tn), lambda i, j, kk: (kk, j)),
               pl.BlockSpec((tm, tn), lambda i, j, kk: (i, j)), NN, alpha, res, riders)


def mm_nt(name, a, b, out_dtype, alpha=1.0, res=None, tm=1024, tn=512, tk=512):
    m, k = a.shape
    n = b.shape[0]
    tm, tn, tk = _tile(m, tm), _tile(n, tn), _tile(k, tk)
    return _mm(name, a, b, (m, n), out_dtype, (m // tm, n // tn, k // tk),
               pl.BlockSpec((tm, tk), lambda i, j, kk: (i, kk)),
               pl.BlockSpec((tn, tk), lambda i, j, kk: (j, kk)),
               pl.BlockSpec((tm, tn), lambda i, j, kk: (i, j)), NT, alpha, res)


def mm_tn(name, a, b, out_dtype, alpha=1.0, tm=1024, tn=1024, tk=1024):
    k, m = a.shape
    n = b.shape[1]
    tm, tn, tk = _tile(m, tm), _tile(n, tn), _tile(k, tk)
    return _mm(name, a, b, (m, n), out_dtype, (m // tm, n // tn, k // tk),
               pl.BlockSpec((tk, tm), lambda i, j, kk: (kk, i)),
               pl.BlockSpec((tk, tn), lambda i, j, kk: (kk, j)),
               pl.BlockSpec((tm, tn), lambda i, j, kk: (i, j)), TN, alpha)


def mm_nn_colsharded(name, a, w, out_dtype, tm=1024, tk=1024, riders=()):
    m, k = a.shape
    c = w.shape[2]
    tm, tk = _tile(m, tm), _tile(k, tk)
    return _mm(name, a, w, (m, N_CHIPS * c), out_dtype, (m // tm, N_CHIPS, k // tk),
               pl.BlockSpec((tm, tk), lambda i, j, kk: (i, kk)),
               pl.BlockSpec((None, tk, c), lambda i, j, kk: (j, kk, 0)),
               pl.BlockSpec((tm, c), lambda i, j, kk: (i, j)), NN, riders=riders)


def transpose_colsharded(w):
    nl, _, k, c = w.shape
    return jnp.swapaxes(w, 2, 3).reshape(nl, N_CHIPS * c, k)


def mm_tn_colsharded(name, a, b, out_dtype, tm=1024, tk=1024):
    t, k = a.shape
    c = b.shape[1] // N_CHIPS
    tm, tk = _tile(k, tm), _tile(t, tk)
    return _mm(name, a, b, (N_CHIPS, k, c), out_dtype, (k // tm, N_CHIPS, t // tk),
               pl.BlockSpec((tk, tm), lambda i, j, kk: (kk, i)),
               pl.BlockSpec((tk, c), lambda i, j, kk: (kk, j)),
               pl.BlockSpec((None, tm, c), lambda i, j, kk: (j, i, 0)), TN)


def _rowwise(name, fn, n_rows, tm, row_ins, bcast_ins, outs, accs=()):
    tm = min(tm, n_rows)
    grid = (n_rows // tm,)
    n_row, n_b, n_out = len(row_ins), len(bcast_ins), len(outs)

    def body(*refs):
        ins = [r[...] for r in refs[:n_row + n_b]]
        out_refs = refs[n_row + n_b:n_row + n_b + n_out]
        acc_refs = refs[n_row + n_b + n_out:]
        res = fn(*ins)
        if not isinstance(res, (tuple, list)):
            res = (res,)
        for o_ref, v in zip(out_refs, res[:n_out]):
            o_ref[...] = v.astype(o_ref.dtype)
        if acc_refs:
            @pl.when(pl.program_id(0) == 0)
            def _():
                for a_ref in acc_refs:
                    a_ref[...] = jnp.zeros_like(a_ref)
            for a_ref, v in zip(acc_refs, res[n_out:]):
                a_ref[...] += v

    in_specs, operands = [], []
    for spec in row_ins:
        arr, width, cb = spec[0], spec[1], spec[2]
        rb = spec[3] if len(spec) > 3 else 0
        in_specs.append(pl.BlockSpec((tm, width), functools.partial(lambda i, cb, rb: (i + rb, cb), cb=cb, rb=rb)))
        operands.append(arr)
    for arr in bcast_ins:
        in_specs.append(pl.BlockSpec(arr.shape, functools.partial(lambda i, nd: (0,) * nd, nd=arr.ndim)))
        operands.append(arr)
    out_specs = [pl.BlockSpec((tm, w), lambda i: (i, 0)) for w, _ in outs]
    out_specs += [pl.BlockSpec((r, w), lambda i: (0, 0)) for r, w in accs]
    out_shape = [jax.ShapeDtypeStruct((n_rows, w), dt) for w, dt in outs]
    out_shape += [jax.ShapeDtypeStruct((r, w), F32) for r, w in accs]
    res = pl.pallas_call(
        body, name=name, grid=grid, in_specs=in_specs, out_specs=out_specs, out_shape=out_shape,
        compiler_params=_params(("arbitrary",) if accs else ("parallel",)),
    )(*operands)
    return res


def _rms(x, g):
    r = lax.rsqrt(jnp.mean(x * x, axis=-1, keepdims=True) + NORM_EPS)
    return x * r * g


def _rms_bwd(dy, x, g):
    r = lax.rsqrt(jnp.mean(x * x, axis=-1, keepdims=True) + NORM_EPS)
    xh = x * r
    dxh = dy * g
    dx = r * (dxh - xh * jnp.mean(dxh * xh, axis=-1, keepdims=True))
    return dx, jnp.sum(dy * xh, axis=0, keepdims=True)


def norm_fwd(name, h, g):
    n, d = h.shape
    return _rowwise(name, lambda x, gg: _rms(x, gg), n, 512, [(h, d, 0)], [g.reshape(1, d)], [(d, BF16)])[0]


def norm_bwd(name, dxn, h, g, d_res):
    n, d = h.shape

    def fn(dy, x, dr, gg):
        dx, dg = _rms_bwd(dy, x, gg)
        return dr + dx, dg

    return _rowwise(name, fn, n, 512, [(dxn, d, 0), (h, d, 0), (d_res, d, 0)], [g.reshape(1, d)], [(d, F32)], [(1, d)])


_GELU_C = math.sqrt(2.0 / math.pi)


def _gelu(x):
    return 0.5 * x * (1.0 + jnp.tanh(_GELU_C * (x + 0.044715 * (x * x * x))))


def _gelu_grad(x):
    th = jnp.tanh(_GELU_C * (x + 0.044715 * (x * x * x)))
    return 0.5 * (1.0 + th) + 0.5 * x * (1.0 - th * th) * (_GELU_C * (1.0 + 3.0 * 0.044715 * (x * x)))


def _ssm_discretize(lam_re, lam_im, log_dt, b_re, b_im):
    dt = jnp.exp(log_dt)
    e = jnp.exp(lam_re * dt)
    lb_re = e * jnp.cos(lam_im * dt)
    lb_im = e * jnp.sin(lam_im * dt)
    nr, ni = lb_re - 1.0, lb_im
    den = lam_re * lam_re + lam_im * lam_im
    cr = (nr * lam_re + ni * lam_im) / den
    ci = (ni * lam_re - nr * lam_im) / den
    return lb_re, lb_im, cr * b_re - ci * b_im, cr * b_im + ci * b_re


def ssm_prep(name, lam_re, lam_im, log_dt, b_re, b_im):
    def body(lr, li, ld, br, bi, pr_ref, pi_ref, bbr_ref, bbi_ref):
        lb_re, lb_im, bb_re, bb_im = _ssm_discretize(lr[...], li[...], ld[...], br[...], bi[...])
        bbr_ref[...] = bb_re
        bbi_ref[...] = bb_im
        pr, pi = lb_re, lb_im
        cols_r, cols_i = [pr], [pi]
        for _ in range(SUBLANE - 1):
            pr, pi = pr * lb_re - pi * lb_im, pr * lb_im + pi * lb_re
            cols_r.append(pr)
            cols_i.append(pi)
        lane = lax.broadcasted_iota(jnp.int32, (SSM_COLS, SUBLANE), 1)
        out_r = jnp.zeros((SSM_COLS, SUBLANE), F32)
        out_i = jnp.zeros((SSM_COLS, SUBLANE), F32)
        for r in range(SUBLANE):
            out_r = jnp.where(lane == r, cols_r[r], out_r)
            out_i = jnp.where(lane == r, cols_i[r], out_i)
        pr_ref[...] = out_r
        pi_ref[...] = out_i

    shapes = [jax.ShapeDtypeStruct((SSM_COLS, SUBLANE), F32)] * 2 + [jax.ShapeDtypeStruct((SSM_COLS, SSM_CH), F32)] * 2
    return pl.pallas_call(body, name=name, out_shape=shapes,
                          compiler_params=pltpu.CompilerParams(vmem_limit_bytes=VMEM_LIMIT))(lam_re, lam_im, log_dt, b_re, b_im)


def ssm_prep_bwd(name, lam_re, lam_im, log_dt, b_re, b_im, d_lb_re, d_lb_im, d_bb_re, d_bb_im):
    def body(lr, li, ld, br, bi, g0, g1, g2, g3, o0, o1, o2, o3, o4):
        _, vjp = jax.vjp(_ssm_discretize, lr[...], li[...], ld[...], br[...], bi[...])
        res = vjp((g0[...], g1[...], g2[...], g3[...]))
        for o, v in zip((o0, o1, o2, o3, o4), res):
            o[...] = v

    col = jax.ShapeDtypeStruct((SSM_COLS, 1), F32)
    mat = jax.ShapeDtypeStruct((SSM_COLS, SSM_CH), F32)
    return pl.pallas_call(body, name=name, out_shape=[col, col, col, mat, mat],
                          compiler_params=pltpu.CompilerParams(vmem_limit_bytes=VMEM_LIMIT))(
        lam_re, lam_im, log_dt, b_re, b_im, d_lb_re, d_lb_im, d_bb_re, d_bb_im)


def scan_coefficients(pw_re, pw_im, reverse):
    pr, pi = pw_re.T, pw_im.T
    if reverse:
        pi = -pi
    row = jnp.arange(SUBLANE)[:, None]
    out = []
    for d in (1, 2, 4):
        valid = (row < SUBLANE - d) if reverse else (row >= d)
        out.append(jnp.where(valid, pr[d - 1][None, :], 0.0))
        out.append(jnp.where(valid, pi[d - 1][None, :], 0.0))
    out.append(pr[::-1] if reverse else pr)
    out.append(pi[::-1] if reverse else pi)
    return jnp.stack(out)


def ssm_scan(name, coef, x, seq, reverse, states=None, riders=()):
    n = x.shape[1]
    n_seq = n // seq
    cw = LANE
    n_cb = SSM_COLS // cw
    pair = 2 * SUBLANE
    n_pairs = seq // pair
    pairs_per_step = 2 if n_pairs % 2 == 0 else 1
    with_dlam = states is not None
    n_r = len(riders)
    assert not (n_r and with_dlam)

    def body(*refs):
        if with_dlam:
            coef_ref, x_ref, s_ref, o_ref, dl_ref = refs
        else:
            coef_ref, x_ref, o_ref = refs[0], refs[1], refs[2 + n_r]
            if n_r:
                _ride((n_cb, n_seq), refs[2:2 + n_r], refs[3 + n_r:3 + 2 * n_r], refs[3 + 2 * n_r:])
        c = [coef_ref[i] for i in range(8)]
        row16 = lax.broadcasted_iota(jnp.int32, (pair, cw), 0)
        zero = jnp.zeros((SUBLANE, cw), F32)

        edge = 0 if reverse else SUBLANE - 1

        def bcast_row(v, r, rows=SUBLANE):
            return jnp.broadcast_to(v[r:r + 1, :], (rows, cw))

        p8r, p8i = bcast_row(c[6], edge), bcast_row(c[7], edge)

        def local_scan(xr, xi):
            for si, d in enumerate((1, 2, 4)):
                sh = (SUBLANE - d) if reverse else d
                sr, sm = pltpu.roll(xr, sh, 0), pltpu.roll(xi, sh, 0)
                lre, lim = c[2 * si], c[2 * si + 1]
                xr, xi = xr + lre * sr - lim * sm, xi + lre * sm + lim * sr
            return xr, xi

        def step(it, carry):
            work = []
            for u in range(pairs_per_step):
                k = it * pairs_per_step + u
                pidx = (n_pairs - 1 - k) if reverse else k
                off = pl.multiple_of(pidx * pair, pair)
                xr16 = x_ref[0, pl.ds(off, pair), :].astype(F32)
                xi16 = x_ref[1, pl.ds(off, pair), :].astype(F32)
                halves = (1, 0) if reverse else (0, 1)
                tiles = {h: local_scan(xr16[h * SUBLANE:(h + 1) * SUBLANE], xi16[h * SUBLANE:(h + 1) * SUBLANE]) for h in halves}
                work.append((pidx, off, halves, tiles))
            cre, cim = carry[0], carry[1]
            acc = carry[2:]
            for pidx, off, halves, tiles in work:
                done = {}
                for h in halves:
                    lr, li = tiles[h]
                    done[h] = (lr + c[6] * cre - c[7] * cim, li + c[6] * cim + c[7] * cre)
                    cre, cim = (bcast_row(lr, edge) + p8r * cre - p8i * cim, bcast_row(li, edge) + p8r * cim + p8i * cre)
                or16 = jnp.concatenate([done[0][0], done[1][0]], axis=0)
                oi16 = jnp.concatenate([done[0][1], done[1][1]], axis=0)
                o_ref[0, pl.ds(off, pair), :] = or16.astype(o_ref.dtype)
                o_ref[1, pl.ds(off, pair), :] = oi16.astype(o_ref.dtype)
                if with_dlam:
                    poff = pl.multiple_of(jnp.maximum(pidx - 1, 0) * pair, pair)
                    first = pidx > 0
                    sr16 = s_ref[0, pl.ds(off, pair), :].astype(F32)
                    si16 = s_ref[1, pl.ds(off, pair), :].astype(F32)
                    pr_last = jnp.where(first, bcast_row(s_ref[0, pl.ds(poff, pair), :].astype(F32), pair - 1, pair), 0.0)
                    pi_last = jnp.where(first, bcast_row(s_ref[1, pl.ds(poff, pair), :].astype(F32), pair - 1, pair), 0.0)
                    spr = jnp.where(row16 == 0, pr_last, pltpu.roll(sr16, 1, 0))
                    spi = jnp.where(row16 == 0, pi_last, pltpu.roll(si16, 1, 0))
                    dre = or16 * spr + oi16 * spi
                    dim = oi16 * spr - or16 * spi
                    acc = (acc[0] + dre[:SUBLANE] + dre[SUBLANE:], acc[1] + dim[:SUBLANE] + dim[SUBLANE:])
            return (cre, cim) + tuple(acc)

        init = (zero, zero, zero, zero) if with_dlam else (zero, zero)
        fin = lax.fori_loop(0, n_pairs // pairs_per_step, step, init)
        if with_dlam:
            @pl.when(pl.program_id(1) == 0)
            def _():
                dl_ref[...] = jnp.zeros_like(dl_ref)
            dl_ref[0] += fin[2]
            dl_ref[1] += fin[3]

    blk = pl.BlockSpec((2, seq, cw), lambda j, b: (0, b, j))
    in_specs = [pl.BlockSpec((8, SUBLANE, cw), lambda j, b: (0, 0, j)), blk]
    operands = [coef, x]
    out_specs = [blk]
    out_shape = [jax.ShapeDtypeStruct(x.shape, STATE_DTYPE)]
    if with_dlam:
        in_specs.append(blk)
        operands.append(states)
        out_specs.append(pl.BlockSpec((2, SUBLANE, cw), lambda j, b: (0, 0, j)))
        out_shape.append(jax.ShapeDtypeStruct((2, SUBLANE, SSM_COLS), F32))
    scratch = []
    if n_r:
        any_spec = pl.BlockSpec(memory_space=pl.ANY)
        in_specs += [any_spec] * n_r
        operands += list(riders)
        out_specs += [any_spec] * n_r
        out_shape += _ride_out_shapes(riders)
        scratch = _ride_scratch(n_r)
    res = pl.pallas_call(
        body, name=name, grid=(n_cb, n_seq), in_specs=in_specs, out_specs=out_specs, out_shape=out_shape,
        scratch_shapes=scratch, compiler_params=_params(("arbitrary" if n_r else "parallel", "arbitrary")),
    )(*operands)
    return res if (with_dlam or n_r) else res[0]


def _state_col(j, kk):
    return 2 * j + kk + 2 * (kk // 2)


SSM_WIDE = 4 * SSM_TILE


def ssm_in(name, z, bbd, tm=1024):
    n = z.shape[0]
    tm = _tile(n, tm)
    t, w = SSM_TILE, SSM_WIDE
    return _mm(name, z, bbd, (2, n, SSM_COLS), STATE_DTYPE, (n // tm, 2 * SSM_COLS // w, 1),
               pl.BlockSpec((tm, t), lambda i, j, kk: (i, j % 2)),
               pl.BlockSpec((t, w), lambda i, j, kk: (j % 2, j)),
               pl.BlockSpec((None, tm, w), lambda i, j, kk: (j // 2, i, j % 2)), NN)


def ssm_out(name, s, cbd, tm=1024):
    n = s.shape[1]
    tm = _tile(n, tm)
    t = SSM_TILE
    return _mm(name, s, cbd, (n, SSM_WIDTH), F32, (n // tm, SSM_WIDTH // t, 4),
               pl.BlockSpec((None, tm, 512), lambda i, j, kk: (kk // 2, i, 2 * j + kk % 2)),
               pl.BlockSpec((512, t), lambda i, j, kk: (_state_col(j, kk), j)),
               pl.BlockSpec((tm, t), lambda i, j, kk: (i, j)), NN)


def ssm_out_t(name, dy, cbd, tm=1024):
    n = dy.shape[0]
    tm = _tile(n, tm)
    t, w = SSM_TILE, SSM_WIDE
    return _mm(name, dy, cbd, (2, n, SSM_COLS), STATE_DTYPE, (n // tm, 2 * SSM_COLS // w, 1),
               pl.BlockSpec((tm, t), lambda i, j, kk: (i, j % 2)),
               pl.BlockSpec((w, t), lambda i, j, kk: (j, j % 2)),
               pl.BlockSpec((None, tm, w), lambda i, j, kk: (j // 2, i, j % 2)), NT)


def ssm_in_t(name, a, bbd, res, tm=1024):
    n = a.shape[1]
    tm = _tile(n, tm)
    t = SSM_TILE
    return _mm(name, a, bbd, (n, SSM_WIDTH), F32, (n // tm, SSM_WIDTH // t, 4),
               pl.BlockSpec((None, tm, 512), lambda i, j, kk: (kk // 2, i, 2 * j + kk % 2)),
               pl.BlockSpec((t, 512), lambda i, j, kk: (j, _state_col(j, kk))),
               pl.BlockSpec((tm, t), lambda i, j, kk: (i, j)), NT, 1.0, res)


def ssm_grad_c(name, s, dy, tk=1024):
    n = s.shape[1]
    tk = _tile(n, tk)
    t, w = SSM_TILE, SSM_WIDE
    return _mm(name, s, dy, (2 * SSM_COLS, t), F32, (2 * SSM_COLS // w, 1, n // tk),
               pl.BlockSpec((None, tk, w), lambda i, j, kk: (i // 2, kk, i % 2)),
               pl.BlockSpec((tk, t), lambda i, j, kk: (kk, i % 2)),
               pl.BlockSpec((w, t), lambda i, j, kk: (i, 0)), TN)


def ssm_grad_b(name, z, a, tk=1024):
    n = z.shape[0]
    tk = _tile(n, tk)
    t, w = SSM_TILE, SSM_WIDE
    return _mm(name, z, a, (t, 2 * SSM_COLS), F32, (1, 2 * SSM_COLS // w, n // tk),
               pl.BlockSpec((tk, t), lambda i, j, kk: (kk, j % 2)),
               pl.BlockSpec((None, tk, w), lambda i, j, kk: (j // 2, kk, j % 2)),
               pl.BlockSpec((t, w), lambda i, j, kk: (0, j)), TN)


_GROUP_TILE = SSM_TILE // SSM_CH


def expand_b(bb_re, bb_im):
    b = jnp.stack([bb_re, bb_im]).reshape(2, SSM_GROUPS, SSM_STATE, SSM_CH)
    eye = jnp.eye(SSM_GROUPS, dtype=F32)
    return jnp.einsum("rgph,gk->ghrkp", b, eye).reshape(SSM_WIDTH, 2 * SSM_COLS).astype(MXU_DTYPE)


def expand_c(c_re, c_im):
    c = jnp.stack([c_re, -c_im])
    eye = jnp.eye(SSM_GROUPS, dtype=F32)
    return jnp.einsum("rghp,gk->rgpkh", c, eye).reshape(2 * SSM_COLS, SSM_WIDTH).astype(MXU_DTYPE)


def _group_pick():
    return (jnp.arange(SSM_GROUPS)[:, None] % _GROUP_TILE == jnp.arange(_GROUP_TILE)[None, :]).astype(F32)


def compact_c(dc):
    x = dc.reshape(2, SSM_GROUPS, SSM_STATE, _GROUP_TILE, SSM_CH)
    g = jnp.einsum("rgpch,gc->rghp", x, _group_pick())
    return g[0], -g[1]


def compact_b(db):
    x = db.reshape(_GROUP_TILE, SSM_CH, 2, SSM_GROUPS, SSM_STATE)
    g = jnp.einsum("chrgp,gc->rgph", x, _group_pick()).reshape(2, SSM_COLS, SSM_CH)
    return g[0], g[1]


def pool_window(name, x, col_block0, seq, out_dtype, adjoint):
    n = x.shape[0]

    def body(x_ref, o_ref):
        win = 2 << pl.program_id(1)
        row = lax.broadcasted_iota(jnp.int32, (seq, POOL_CH), 0)
        v = x_ref[...].astype(F32)
        cnt = jnp.minimum(row + 1, win).astype(F32)
        s = v / cnt if adjoint else v
        for d in (1, 2, 4, 8):
            if adjoint:
                sh = jnp.where((row < seq - d) & (d < win), pltpu.roll(s, seq - d, 0), 0.0)
            else:
                sh = jnp.where((row >= d) & (d < win), pltpu.roll(s, d, 0), 0.0)
            s = s + sh
        o_ref[...] = ((s - v) if adjoint else (s / cnt - v)).astype(out_dtype)

    return pl.pallas_call(
        body, name=name, grid=(n // seq, POOL_GROUPS),
        in_specs=[pl.BlockSpec((seq, POOL_CH), lambda b, g: (b, col_block0 + g))],
        out_specs=pl.BlockSpec((seq, POOL_CH), lambda b, g: (b, g)),
        out_shape=jax.ShapeDtypeStruct((n, POOL_WIDTH), out_dtype),
        compiler_params=_params(("parallel", "parallel")),
    )(x)


def pool_mm(name, q, w, out_dtype, tm=1024):
    n = q.shape[0]
    tm = _tile(n, tm)
    return _mm(name, q, w, (n, POOL_WIDTH), out_dtype, (n // tm, POOL_GROUPS, 1),
               pl.BlockSpec((tm, POOL_CH), lambda i, j, kk: (i, j)),
               pl.BlockSpec((None, POOL_CH, POOL_CH), lambda i, j, kk: (j, 0, 0)),
               pl.BlockSpec((tm, POOL_CH), lambda i, j, kk: (i, j)), NN)


def pool_mm_t(name, dy, col_block0, w, out_dtype, tm=1024):
    n = dy.shape[0]
    tm = _tile(n, tm)
    return _mm(name, dy, w, (n, POOL_WIDTH), out_dtype, (n // tm, POOL_GROUPS, 1),
               pl.BlockSpec((tm, POOL_CH), lambda i, j, kk: (i, col_block0 + j)),
               pl.BlockSpec((None, POOL_CH, POOL_CH), lambda i, j, kk: (j, 0, 0)),
               pl.BlockSpec((tm, POOL_CH), lambda i, j, kk: (i, j)), NT)


def pool_grad_w(name, q, dy, col_block0, tk=1024):
    n = q.shape[0]
    tk = _tile(n, tk)
    return _mm(name, q, dy, (POOL_GROUPS, POOL_CH, POOL_CH), F32, (POOL_GROUPS, 1, n // tk),
               pl.BlockSpec((tk, POOL_CH), lambda i, j, kk: (kk, i)),
               pl.BlockSpec((tk, POOL_CH), lambda i, j, kk: (kk, col_block0 + i)),
               pl.BlockSpec((None, POOL_CH, POOL_CH), lambda i, j, kk: (i, 0, 0)), TN)


def _any_specs(n):
    return [pl.BlockSpec(memory_space=pl.ANY)] * n


def _place():
    x, y, c = lax.axis_index("x"), lax.axis_index("y"), lax.axis_index("c")
    return x, y, c


def _at_axis(ref, axis, start, size):
    return ref.at[(slice(None),) * axis + (pl.ds(start, size),)]


def sibling_swap_halves(name, arrays, axis):
    n = len(arrays)
    halves = [a.shape[axis] // 2 for a in arrays]

    def body(*refs):
        ins, own, got = refs[:n], refs[n:2 * n], refs[2 * n:3 * n]
        send_sems, recv_sems, local_sems = refs[3 * n:]
        x, y, c = _place()
        copies = []
        for i in range(n):
            h = halves[i]
            mine = pltpu.make_async_copy(_at_axis(ins[i], axis, c * h, h), own[i], local_sems.at[i])
            mine.start()
            away = pltpu.make_async_remote_copy(
                src_ref=_at_axis(ins[i], axis, (1 - c) * h, h), dst_ref=got[i],
                send_sem=send_sems.at[i], recv_sem=recv_sems.at[i], device_id=(x, y, 1 - c), device_id_type=MESH_ID)
            away.start()
            copies += [mine, away]
        for cp in copies:
            cp.wait()

    def half_shape(a, h):
        return jax.ShapeDtypeStruct(a.shape[:axis] + (h,) + a.shape[axis + 1:], a.dtype)

    shapes = [half_shape(a, h) for a, h in zip(arrays, halves)]
    res = pl.pallas_call(
        body, name=name, in_specs=_any_specs(n), out_specs=_any_specs(2 * n), out_shape=shapes + shapes,
        scratch_shapes=[pltpu.SemaphoreType.DMA((n,)), pltpu.SemaphoreType.DMA((n,)), pltpu.SemaphoreType.DMA((n,))],
    )(*arrays)
    return res[:n], res[n:]


def sibling_join_halves(name, arrays, axis):
    n = len(arrays)

    def body(*refs):
        ins, outs = refs[:n], refs[n:2 * n]
        send_sems, recv_sems, local_sems = refs[2 * n:]
        x, y, c = _place()
        copies = []
        for i in range(n):
            h = ins[i].shape[axis]
            dst = _at_axis(outs[i], axis, c * h, h)
            mine = pltpu.make_async_copy(ins[i], dst, local_sems.at[i])
            mine.start()
            away = pltpu.make_async_remote_copy(
                src_ref=ins[i], dst_ref=dst, send_sem=send_sems.at[i], recv_sem=recv_sems.at[i],
                device_id=(x, y, 1 - c), device_id_type=MESH_ID)
            away.start()
            copies += [mine, away]
        for cp in copies:
            cp.wait()

    shapes = [jax.ShapeDtypeStruct(a.shape[:axis] + (2 * a.shape[axis],) + a.shape[axis + 1:], a.dtype) for a in arrays]
    return pl.pallas_call(
        body, name=name, in_specs=_any_specs(n), out_specs=_any_specs(n), out_shape=shapes,
        scratch_shapes=[pltpu.SemaphoreType.DMA((n,)), pltpu.SemaphoreType.DMA((n,)), pltpu.SemaphoreType.DMA((n,))],
    )(*arrays)


def sibling_swap(name, arrays):
    n = len(arrays)

    def body(*refs):
        ins, outs = refs[:n], refs[n:2 * n]
        send_sems, recv_sems = refs[2 * n:]
        x, y, c = _place()
        copies = []
        for i in range(n):
            away = pltpu.make_async_remote_copy(
                src_ref=ins[i], dst_ref=outs[i], send_sem=send_sems.at[i], recv_sem=recv_sems.at[i],
                device_id=(x, y, 1 - c), device_id_type=MESH_ID)
            away.start()
            copies.append(away)
        for cp in copies:
            cp.wait()

    return pl.pallas_call(
        body, name=name, in_specs=_any_specs(n), out_specs=_any_specs(n),
        out_shape=[jax.ShapeDtypeStruct(a.shape, a.dtype) for a in arrays],
        scratch_shapes=[pltpu.SemaphoreType.DMA((n,)), pltpu.SemaphoreType.DMA((n,))],
    )(*arrays)


_FLIPS = ((1, 0), (0, 1), (1, 1))


def chip_exchange(name, arrays, axis, all_to_all):
    n = len(arrays)

    def body(*refs):
        ins, outs = refs[:n], refs[n:2 * n]
        send_sems, recv_sems, local_sems = refs[2 * n:]
        x, y, c = _place()
        me = 2 * x + y
        copies = []
        for i in range(n):
            dst = _at_axis(outs[i], axis, me, 1)
            mine = pltpu.make_async_copy(_at_axis(ins[i], axis, me, 1) if all_to_all else ins[i], dst, local_sems.at[i])
            mine.start()
            copies.append(mine)
            for f, (fx, fy) in enumerate(_FLIPS):
                px = (1 - x) if fx else x
                py = (1 - y) if fy else y
                src = _at_axis(ins[i], axis, 2 * px + py, 1) if all_to_all else ins[i]
                away = pltpu.make_async_remote_copy(
                    src_ref=src, dst_ref=dst, send_sem=send_sems.at[3 * i + f], recv_sem=recv_sems.at[3 * i + f],
                    device_id=(px, py, c), device_id_type=MESH_ID)
                away.start()
                copies.append(away)
        for cp in copies:
            cp.wait()

    shapes = [jax.ShapeDtypeStruct(a.shape[:axis] + (N_CHIPS,) + a.shape[axis + 1:], a.dtype) for a in arrays]
    return pl.pallas_call(
        body, name=name, in_specs=_any_specs(n), out_specs=_any_specs(n), out_shape=shapes,
        scratch_shapes=[pltpu.SemaphoreType.DMA((3 * n,)), pltpu.SemaphoreType.DMA((3 * n,)), pltpu.SemaphoreType.DMA((n,))],
    )(*arrays)


SIBLING = ((0, 0, 1),)
ICI_CHUNK_BYTES = 2 * 1024 * 1024
D2D_CHUNK_BYTES = 4 * 1024 * 1024


def _peer(flip):
    x, y, c = _place()
    return tuple((1 - v) if f else v for v, f in zip((x, y, c), flip))


def _core():
    return lax.axis_index("c")


def _chip():
    return 2 * lax.axis_index("x") + lax.axis_index("y")


def _linear_step(grid):
    i = pl.program_id(0)
    for a in range(1, len(grid)):
        i = i * grid[a] + pl.program_id(a)
    return i


def stream_reduce(name, x, grid, block, own_map, send_maps, flips, out_shape, out_block, out_map, wire_dtype=None):
    n_steps = math.prod(grid)
    n_p = len(flips)
    vm_block = block = tuple(1 if d is None else d for d in block)
    out_block = tuple(1 if d is None else d for d in out_block)
    staged = wire_dtype is not None and wire_dtype != x.dtype
    slot_dtype = wire_dtype if staged else x.dtype

    def body(own_ref, *rest):
        send_refs = rest[:n_p]
        o_ref, recv, send_sems, recv_sems, credits = rest[n_p:n_p + 5]
        stage = rest[n_p + 5] if staged else None
        i = _linear_step(grid)
        s = i % 2
        copies = []
        for j, flip in enumerate(flips):
            src = send_refs[j]
            if staged:
                stage[j, s] = send_refs[j][...].astype(wire_dtype)
                src = stage.at[j, s]

            @pl.when(i >= 2)
            def _():
                pl.semaphore_wait(credits.at[j, s], 1)
            cp = pltpu.make_async_remote_copy(
                src_ref=src, dst_ref=recv.at[j, s], send_sem=send_sems.at[j, s], recv_sem=recv_sems.at[j, s],
                device_id=_peer(flip), device_id_type=MESH_ID)
            cp.start()
            copies.append(cp)
        acc = own_ref[...]
        for j, cp in enumerate(copies):
            cp.wait_recv()
            acc = acc + recv[j, s].astype(acc.dtype)
        o_ref[...] = acc.reshape(o_ref.shape)
        for cp in copies:
            cp.wait_send()
        for j, flip in enumerate(flips):
            @pl.when(i < n_steps - 2)
            def _():
                pl.semaphore_signal(credits.at[j, s], inc=1, device_id=_peer(flip), device_id_type=MESH_ID)

    in_specs = [pl.BlockSpec(block, own_map)] + [pl.BlockSpec(block, m) for m in send_maps]
    scratch = [pltpu.VMEM((n_p, 2) + vm_block, slot_dtype), pltpu.SemaphoreType.DMA((n_p, 2)),
               pltpu.SemaphoreType.DMA((n_p, 2)), pltpu.SemaphoreType.REGULAR((n_p, 2))]
    if staged:
        scratch.append(pltpu.VMEM((n_p, 2) + vm_block, slot_dtype))
    return pl.pallas_call(
        body, name=name, grid=grid, in_specs=in_specs, out_specs=pl.BlockSpec(out_block, out_map),
        out_shape=jax.ShapeDtypeStruct(out_shape, x.dtype), scratch_shapes=scratch,
        compiler_params=_params(("arbitrary",) * len(grid)),
    )(*([x] * (1 + n_p)))


def stream_gather(name, x, grid, block, in_map, flips, out_shape, out_block, out_map):
    n_p = len(flips)
    assert grid[-1] == n_p + 1
    n_steps = math.prod(grid[:-1])
    vm_block = block = tuple(1 if d is None else d for d in block)
    out_block = tuple(1 if d is None else d for d in out_block)

    def body(x_ref, o_ref, recv, send_sems, recv_sems, credits):
        i = _linear_step(grid[:-1])
        q = pl.program_id(len(grid) - 1)
        s = i % 2

        def copy(j):
            return pltpu.make_async_remote_copy(
                src_ref=x_ref, dst_ref=recv.at[j, s], send_sem=send_sems.at[j, s], recv_sem=recv_sems.at[j, s],
                device_id=_peer(flips[j]), device_id_type=MESH_ID)

        @pl.when(q == 0)
        def _():
            for j in range(n_p):
                @pl.when(i >= 2)
                def _():
                    pl.semaphore_wait(credits.at[j, s], 1)
                copy(j).start()
            o_ref[...] = x_ref[...].reshape(o_ref.shape)
            for j in range(n_p):
                copy(j).wait_send()

        for j in range(n_p):
            @pl.when(q == j + 1)
            def _():
                copy(j).wait_recv()
                o_ref[...] = recv[j, s].reshape(o_ref.shape)

                @pl.when(i < n_steps - 2)
                def _():
                    pl.semaphore_signal(credits.at[j, s], inc=1, device_id=_peer(flips[j]), device_id_type=MESH_ID)

    return pl.pallas_call(
        body, name=name, grid=grid, in_specs=[pl.BlockSpec(block, in_map)], out_specs=pl.BlockSpec(out_block, out_map),
        out_shape=jax.ShapeDtypeStruct(out_shape, x.dtype),
        scratch_shapes=[pltpu.VMEM((n_p, 2) + vm_block, x.dtype), pltpu.SemaphoreType.DMA((n_p, 2)),
                        pltpu.SemaphoreType.DMA((n_p, 2)), pltpu.SemaphoreType.REGULAR((n_p, 2))],
        compiler_params=_params(("arbitrary",) * len(grid)),
    )(x)


def _chip_of_substep(q):
    mask = jnp.where(q == 1, 2, jnp.where(q == 2, 1, jnp.where(q == 3, 3, 0)))
    return jnp.bitwise_xor(_chip(), mask)


def gather_weight(name, shard):
    nl, r, c = shard.shape
    r2 = r // 2
    f32_per_elem = 4 // shard.dtype.itemsize
    tr = _rows_tile(r2, c, budget=ICI_CHUNK_BYTES * f32_per_elem, step=16)
    nb = r2 // tr
    half = stream_gather(
        name + "_chips", shard, (nl, nb, N_CHIPS), (None, tr, c), lambda l, i, q: (l, _core() * nb + i, 0), CHIPS,
        (nl, N_CHIPS, r2, c), (None, None, tr, c), lambda l, i, q: (l, _chip_of_substep(q), i, 0))
    trd = _rows_tile(r2, c, budget=D2D_CHUNK_BYTES * f32_per_elem, step=16)
    nbd = r2 // trd
    both = stream_gather(
        name + "_cores", half, (nl, N_CHIPS, nbd, 2), (None, None, trd, c), lambda l, k, i, q: (l, k, i, 0), SIBLING,
        (nl, N_CHIPS, 2, r2, c), (None, None, None, trd, c), lambda l, k, i, q: (l, k, _core() + q - 2 * _core() * q, i, 0))
    return both.reshape(nl, N_CHIPS, r, c)


def reduce_scatter_streamed(name, g):
    nl, _, r, c = g.shape
    r2 = r // 2
    tr_d2d = _rows_tile(r2, c, budget=D2D_CHUNK_BYTES, step=16)
    nbd = r2 // tr_d2d
    chip_sum = stream_reduce(
        name + "_cores", g, (nl, N_CHIPS, nbd), (None, None, tr_d2d, c),
        lambda l, k, i: (l, k, _core() * nbd + i, 0), [lambda l, k, i: (l, k, (1 - _core()) * nbd + i, 0)], SIBLING,
        (nl, N_CHIPS, r2, c), (None, None, tr_d2d, c), lambda l, k, i: (l, k, i, 0), wire_dtype=GRAD_WIRE_DTYPE)
    tr = _rows_tile(r2, c, budget=ICI_CHUNK_BYTES, step=16)
    nb = r2 // tr
    blk4 = (None, None, tr, c)
    masks = (2, 1, 3)
    mine = stream_reduce(
        name + "_chips", chip_sum, (nl, nb), blk4,
        lambda l, i: (l, _chip(), i, 0),
        [functools.partial(lambda l, i, m: (l, jnp.bitwise_xor(_chip(), m), i, 0), m=m) for m in masks], CHIPS,
        (nl, r2, c), (None, tr, c), lambda l, i: (l, i, 0), wire_dtype=GRAD_WIRE_DTYPE)
    both = stream_gather(
        name + "_join", mine, (nl, nbd, 2), (None, tr_d2d, c), lambda l, i, q: (l, i, 0), SIBLING,
        (nl, 2, r2, c), (None, None, tr_d2d, c), lambda l, i, q: (l, _core() + q - 2 * _core() * q, i, 0))
    return both.reshape(nl, r, c)


def add2(name, a, b):
    shape = a.shape
    a2, b2 = a.reshape(-1, shape[-1]), b.reshape(-1, shape[-1])
    rows, w = a2.shape
    tm = _rows_tile(rows, w)
    return _rowwise(name, lambda u, v: u + v, rows, tm, [(a2, w, 0), (b2, w, 0)], [], [(w, F32)])[0].reshape(shape)


def _rows_tile(rows, width, budget=2 * 1024 * 1024, step=SUBLANE):
    best = step
    t = step
    while t <= rows:
        if rows % t == 0 and t * width * 4 <= budget:
            best = t
        t += step
    return best


def sum_slots(name, a):
    nl, _, r, c = a.shape
    tr = _rows_tile(r, c)

    def body(s0, s1, s2, s3, o_ref):
        o_ref[...] = ((s0[...] + s1[...]) + s2[...]) + s3[...]

    specs = [pl.BlockSpec((None, None, tr, c), functools.partial(lambda l, i, k: (l, k, i, 0), k=k)) for k in range(N_CHIPS)]
    return pl.pallas_call(
        body, name=name, grid=(nl, r // tr), in_specs=specs,
        out_specs=pl.BlockSpec((None, tr, c), lambda l, i: (l, i, 0)),
        out_shape=jax.ShapeDtypeStruct((nl, r, c), F32),
        compiler_params=_params(("parallel", "parallel")),
    )(a, a, a, a)


def reduce_scatter_big(grads):
    own, got = sibling_swap_halves("rs_swap_halves", grads, 2)
    chip_sum = [add2("rs_add_cores", a, b) for a, b in zip(own, got)]
    spread = chip_exchange("rs_chips", chip_sum, 1, True)
    mine = [sum_slots("rs_sum_chips", a) for a in spread]
    return sibling_join_halves("rs_join_halves", mine, 1)


def all_reduce_small(flat):
    other = sibling_swap("ar_swap", [flat])[0]
    chip = add2("ar_add_cores", flat, other)
    slots = chip_exchange("ar_chips", [chip.reshape((1,) + chip.shape)], 0, False)[0]
    rows = flat.shape[0]
    tm = _rows_tile(rows, LANE)
    nb = rows // tm
    s2 = slots.reshape(N_CHIPS * rows, LANE)
    return _rowwise("ar_sum_chips", lambda a, b, c, d: ((a + b) + c) + d, rows, tm,
                    [(s2, LANE, 0, k * nb) for k in range(N_CHIPS)], [], [(LANE, F32)])[0]


def _adamw_math(w, g, m, v):
    m = ADAM_B1 * m + (1.0 - ADAM_B1) * g
    v = ADAM_B2 * v + (1.0 - ADAM_B2) * (g * g)
    m_hat = m / (1.0 - ADAM_B1 ** ADAM_STEP)
    v_hat = v / (1.0 - ADAM_B2 ** ADAM_STEP)
    delta = -ADAM_LR * (m_hat / (jnp.sqrt(v_hat) + ADAM_EPS) + ADAM_WD * w)
    return delta, m, v


def adamw(name, w, g, m, v):
    shape = w.shape
    width = shape[-1]
    flat = [t.reshape(-1, width) for t in (w, g, m, v)]
    rows = flat[0].shape[0]
    tm = _rows_tile(rows, width, budget=1024 * 1024)
    res = _rowwise(name, _adamw_math, rows, tm, [(t, width, 0) for t in flat], [], [(width, F32)] * 3)
    return [r.reshape(shape) for r in res]


def _ffn_fwd(tag, h, g_norm, wi, wo, next_wi=None, next_wo=None):
    n, d = h.shape
    ff = wo.shape[0]
    fetched = []
    xn = norm_fwd(tag + "_norm", h, g_norm)
    gu = mm_nn_colsharded(tag + "_wi", xn, wi, BF16, riders=() if next_wi is None else (next_wi,))
    if next_wi is not None:
        gu, got = gu
        fetched.append(got)
    act = _rowwise(tag + "_swiglu", lambda g, u: (g.astype(F32) * jax.nn.sigmoid(g.astype(F32))) * u.astype(F32),
                   n, 512, [(gu, ff, 0), (gu, ff, 1)], [], [(ff, BF16)])[0]
    out = mm_nn(tag + "_wo", act, wo, F32, alpha=0.5, res=h, tm=512, tk=2816, riders=() if next_wo is None else (next_wo,))
    if next_wo is not None:
        out, got = out
        fetched.append(got)
    return out, (xn, gu, act), fetched


def _ffn_bwd(tag, d, h, g_norm, wi_t, wo_t, saved):
    xn, gu, act = saved
    n, dm = h.shape
    ff = wo_t.shape[1]
    dact = mm_nn(tag + "_dact", d, wo_t, BF16, alpha=0.5, tn=1408)
    dwo = mm_tn(tag + "_dwo", act, d, F32, alpha=0.5, tm=1408)

    def swiglu_bwd(g, u, da):
        g, u, da = g.astype(F32), u.astype(F32), da.astype(F32)
        sg = jax.nn.sigmoid(g)
        dg = da * u * (sg * (1.0 + g * (1.0 - sg)))
        du = da * (g * sg)
        return jnp.concatenate([dg, du], axis=1)

    dgu = _rowwise(tag + "_dswiglu", swiglu_bwd, n, 512, [(gu, ff, 0), (gu, ff, 1), (dact, ff, 0)], [], [(2 * ff, BF16)])[0]
    dwi = mm_tn_colsharded(tag + "_dwi", xn, dgu, F32)
    dxn = mm_nn(tag + "_dxn", dgu, wi_t, F32, tk=1408)
    d_in, dg_norm = norm_bwd(tag + "_dnorm", dxn, h, g_norm, d)
    return d_in, dg_norm.reshape(dm), dwi, dwo.reshape(N_CHIPS, ff // N_CHIPS, dm)


def _col(v):
    return v.reshape(SSM_COLS, 1)


SCAN_RIDERS = ("w_in", "ssm_w_glu", "w_out", "ple_w_gate", "ple_w_proj")


def _layer_fwd(h, lw, p_l, seq, nxt=None):
    n, d = h.shape
    pre = {k: None for k in BIG} if nxt is None else nxt
    h1, ffn1_saved, got1 = _ffn_fwd("ffn1", h, lw["ffn1_norm"], lw["ffn1_wi"], lw["ffn1_wo"], pre["ffn1_wi"], pre["ffn1_wo"])

    xn2 = norm_fwd("mix_norm", h1, lw["mix_norm"])
    z = mm_nn("mix_in", xn2, lw["w_in"], F32)
    log_dt = jnp.repeat(lw["ssm_log_dt"], SSM_STATE)
    b_re, b_im = lw["ssm_b_re"].reshape(SSM_COLS, SSM_CH), lw["ssm_b_im"].reshape(SSM_COLS, SSM_CH)
    pw_re, pw_im, bb_re, bb_im = ssm_prep("ssm_prep", _col(lw["ssm_lambda_re"]), _col(lw["ssm_lambda_im"]), _col(log_dt), b_re, b_im)
    bbd = expand_b(bb_re, bb_im)
    cbd = expand_c(lw["ssm_c_re"], lw["ssm_c_im"])
    bu = ssm_in("ssm_in", z, bbd)
    s = ssm_scan("ssm_scan", scan_coefficients(pw_re, pw_im, False), bu, seq, False,
                 riders=() if nxt is None else tuple(nxt[k] for k in SCAN_RIDERS))
    got_scan = []
    if nxt is not None:
        s, got_scan = s[0], s[1:]
    y0c = ssm_out("ssm_out", s, cbd)

    def skip_gelu(yc, zs, dvec):
        y0 = yc + dvec * zs
        return y0, _gelu(y0)

    y0, y1 = _rowwise("ssm_gelu", skip_gelu, n, 512, [(y0c, SSM_WIDTH, 0), (z, SSM_WIDTH, 0)],
                      [lw["ssm_d"].reshape(1, SSM_WIDTH)], [(SSM_WIDTH, F32), (SSM_WIDTH, F32)])
    t = mm_nn("ssm_glu_mm", y1, lw["ssm_w_glu"], F32)
    y2 = _rowwise("ssm_glu", lambda a, b: a * jax.nn.sigmoid(b), n, 512, [(y1, SSM_WIDTH, 0), (t, SSM_WIDTH, 0)], [],
                  [(SSM_WIDTH, BF16)])[0]

    q = pool_window("pool_window", z, SSM_WIDTH // POOL_CH, seq, BF16, False)
    wp_eff = lw["pool_w"] * lw["pool_scale"].reshape(POOL_GROUPS, 1, POOL_CH)
    yp = pool_mm("pool_mm", q, wp_eff, BF16)
    m = jnp.concatenate([y2, yp], axis=1)
    h2 = mm_nn("mix_out", m, lw["w_out"], F32, res=h1)

    h3, ffn2_saved, got2 = _ffn_fwd("ffn2", h2, lw["ffn2_norm"], lw["ffn2_wi"], lw["ffn2_wo"], pre["ffn2_wi"], pre["ffn2_wo"])

    xn4 = norm_fwd("ple_norm", h3, lw["ple_norm"])
    tg = mm_nn("ple_gate", xn4, lw["ple_w_gate"], F32)
    e = mm_nn_colsharded("ple_proj", p_l, lw["ple_w_proj"], F32)
    h4 = _rowwise("ple_add", lambda a, b, c: a + jax.nn.sigmoid(b) * c, n, 512, [(h3, d, 0), (tg, d, 0), (e, d, 0)], [], [(d, F32)])[0]
    saved = dict(h=h, h1=h1, h2=h2, h3=h3, ffn1=ffn1_saved, ffn2=ffn2_saved, xn2=xn2, z=z, s=s, y0=y0, y1=y1, t=t, m=m, q=q,
                 xn4=xn4, tg=tg, e=e, pw_re=pw_re, pw_im=pw_im, bbd=bbd, cbd=cbd)
    fetched = None
    if nxt is not None:
        fetched = dict(zip(("ffn1_wi", "ffn1_wo"), got1))
        fetched.update(zip(("ffn2_wi", "ffn2_wo"), got2))
        fetched.update(zip(SCAN_RIDERS, got_scan))
    return h4, saved, fetched


def _layer_bwd(d, lw, p_l, sv, seq):
    n, dm = d.shape
    g = {}
    def ple_bwd(dd, tg, e):
        gate = jax.nn.sigmoid(tg)
        return dd * e * gate * (1.0 - gate), dd * gate

    dtg, de = _rowwise("ple_dadd", ple_bwd, n, 512, [(d, dm, 0), (sv["tg"], dm, 0), (sv["e"], dm, 0)], [], [(dm, BF16), (dm, BF16)])
    g["ple_w_proj"] = mm_tn_colsharded("ple_dproj", p_l, de, F32)
    g["ple_w_gate"] = mm_tn("ple_dgate_w", sv["xn4"], dtg, F32).reshape(N_CHIPS, dm // N_CHIPS, dm)
    dxn4 = mm_nn("ple_dgate_x", dtg, lw["ple_w_gate_t"], F32)
    d, dg = norm_bwd("ple_dnorm", dxn4, sv["h3"], lw["ple_norm"], d)
    g["ple_norm"] = dg.reshape(dm)

    d, g["ffn2_norm"], g["ffn2_wi"], g["ffn2_wo"] = _ffn_bwd("ffn2b", d, sv["h2"], lw["ffn2_norm"], lw["ffn2_wi_t"], lw["ffn2_wo_t"], sv["ffn2"])

    dmix = mm_nn("mix_dout_x", d, lw["w_out_t"], F32)
    g["w_out"] = mm_tn("mix_dout_w", sv["m"], d, F32).reshape(N_CHIPS, dm // N_CHIPS, dm)
    pool_cb = SSM_WIDTH // POOL_CH
    wp_eff = lw["pool_w"] * lw["pool_scale"].reshape(POOL_GROUPS, 1, POOL_CH)
    dq = pool_mm_t("pool_dmm_x", dmix, pool_cb, wp_eff, F32)
    dwp_eff = pool_grad_w("pool_dmm_w", sv["q"], dmix, pool_cb)
    g["pool_w"] = dwp_eff * lw["pool_scale"].reshape(POOL_GROUPS, 1, POOL_CH)
    g["pool_scale"] = jnp.sum(dwp_eff * lw["pool_w"], axis=1).reshape(POOL_WIDTH)
    dzp = pool_window("pool_dwindow", dq, 0, seq, BF16, True)

    def glu_bwd(dy2, y1, t):
        sg = jax.nn.sigmoid(t)
        return dy2 * y1 * sg * (1.0 - sg), dy2 * sg

    dt_, dy1a = _rowwise("ssm_dglu", glu_bwd, n, 512, [(dmix, SSM_WIDTH, 0), (sv["y1"], SSM_WIDTH, 0), (sv["t"], SSM_WIDTH, 0)], [],
                         [(SSM_WIDTH, BF16), (SSM_WIDTH, F32)])
    g["ssm_w_glu"] = mm_tn("ssm_dglu_w", sv["y1"], dt_, F32).reshape(N_CHIPS, SSM_WIDTH // N_CHIPS, SSM_WIDTH)
    dy1b = mm_nn("ssm_dglu_x", dt_, lw["ssm_w_glu_t"], F32)

    def gelu_bwd(da, db, y0, zs, dvec):
        dy0 = (da + db) * _gelu_grad(y0)
        return dy0, dy0 * dvec, jnp.sum(dy0 * zs, axis=0, keepdims=True)

    dy0, dzs_a, dd = _rowwise("ssm_dgelu", gelu_bwd, n, 512,
                              [(dy1a, SSM_WIDTH, 0), (dy1b, SSM_WIDTH, 0), (sv["y0"], SSM_WIDTH, 0), (sv["z"], SSM_WIDTH, 0)],
                              [lw["ssm_d"].reshape(1, SSM_WIDTH)], [(SSM_WIDTH, F32), (SSM_WIDTH, F32)], [(1, SSM_WIDTH)])
    g["ssm_d"] = dd.reshape(SSM_WIDTH)
    g["ssm_c_re"], g["ssm_c_im"] = compact_c(ssm_grad_c("ssm_dc", sv["s"], dy0))
    v = ssm_out_t("ssm_dout", dy0, sv["cbd"])
    a, dlam = ssm_scan("ssm_scan_adj", scan_coefficients(sv["pw_re"], sv["pw_im"], True), v, seq, True, states=sv["s"])
    dbb_re, dbb_im = compact_b(ssm_grad_b("ssm_db", sv["z"], a))
    dzs = ssm_in_t("ssm_din", a, sv["bbd"], dzs_a)
    dlam = jnp.sum(dlam, axis=1)
    log_dt = jnp.repeat(lw["ssm_log_dt"], SSM_STATE)
    b_re, b_im = lw["ssm_b_re"].reshape(SSM_COLS, SSM_CH), lw["ssm_b_im"].reshape(SSM_COLS, SSM_CH)
    glr, gli, gld, gbr, gbi = ssm_prep_bwd("ssm_prep_bwd", _col(lw["ssm_lambda_re"]), _col(lw["ssm_lambda_im"]), _col(log_dt), b_re, b_im,
                                           _col(dlam[0]), _col(dlam[1]), dbb_re, dbb_im)
    g["ssm_lambda_re"] = glr.reshape(SSM_GROUPS, SSM_STATE)
    g["ssm_lambda_im"] = gli.reshape(SSM_GROUPS, SSM_STATE)
    g["ssm_log_dt"] = jnp.sum(gld.reshape(SSM_GROUPS, SSM_STATE), axis=1)
    g["ssm_b_re"] = gbr.reshape(SSM_GROUPS, SSM_STATE, SSM_CH)
    g["ssm_b_im"] = gbi.reshape(SSM_GROUPS, SSM_STATE, SSM_CH)

    dz = jnp.concatenate([dzs.astype(BF16), dzp], axis=1)
    g["w_in"] = mm_tn("mix_din_w", sv["xn2"], dz, F32).reshape(N_CHIPS, dm // N_CHIPS, dm)
    dxn2 = mm_nn("mix_din_x", dz, lw["w_in_t"], F32)
    d, dg = norm_bwd("mix_dnorm", dxn2, sv["h1"], lw["mix_norm"], d)
    g["mix_norm"] = dg.reshape(dm)

    d, g["ffn1_norm"], g["ffn1_wi"], g["ffn1_wo"] = _ffn_bwd("ffn1b", d, sv["h"], lw["ffn1_norm"], lw["ffn1_wi_t"], lw["ffn1_wo_t"], sv["ffn1"])
    return d, g


def _flatten_small(tensors):
    flat = jnp.concatenate([t.reshape(-1) for t in tensors])
    pad = (-flat.shape[0]) % (SUBLANE * LANE)
    return jnp.pad(flat, (0, pad)).reshape(-1, LANE)


def _unflatten_small(flat, like):
    flat = flat.reshape(-1)
    out, off = [], 0
    for t in like:
        out.append(flat[off:off + t.size].reshape(t.shape))
        off += t.size
    return out


def kernel(x, p, ffn1_norm, ffn1_wi, ffn1_wo, mix_norm, w_in, ssm_lambda_re, ssm_lambda_im, ssm_log_dt, ssm_b_re, ssm_b_im, ssm_c_re, ssm_c_im, ssm_d, ssm_w_glu, pool_w, pool_scale, w_out, ffn2_norm, ffn2_wi, ffn2_wo, ple_norm, ple_w_gate, ple_w_proj, final_norm, loss_target, m_ffn1_norm, m_ffn1_wi, m_ffn1_wo, m_mix_norm, m_w_in, m_ssm_lambda_re, m_ssm_lambda_im, m_ssm_log_dt, m_ssm_b_re, m_ssm_b_im, m_ssm_c_re, m_ssm_c_im, m_ssm_d, m_ssm_w_glu, m_pool_w, m_pool_scale, m_w_out, m_ffn2_norm, m_ffn2_wi, m_ffn2_wo, m_ple_norm, m_ple_w_gate, m_ple_w_proj, m_final_norm, v_ffn1_norm, v_ffn1_wi, v_ffn1_wo, v_mix_norm, v_w_in, v_ssm_lambda_re, v_ssm_lambda_im, v_ssm_log_dt, v_ssm_b_re, v_ssm_b_im, v_ssm_c_re, v_ssm_c_im, v_ssm_d, v_ssm_w_glu, v_pool_w, v_pool_scale, v_w_out, v_ffn2_norm, v_ffn2_wi, v_ffn2_wo, v_ple_norm, v_ple_w_gate, v_ple_w_proj, v_final_norm):
    given = dict(locals())
    w = {k: given[k] for k in WEIGHTS}
    mom = {k: given["m_" + k] for k in WEIGHTS}
    var = {k: given["v_" + k] for k in WEIGHTS}
    bsz, seq, dm = x.shape
    n = bsz * seq
    depth = ffn1_wi.shape[0]

    shards = {k: w[k].astype(MXU_DTYPE) for k in BIG}

    def layer_weights(l, gathered):
        lw = {k: w[k][l] for k in SMALL if k != "final_norm"}
        for k, a in gathered.items():
            if k in COL_SHARDED:
                lw[k] = a
                if k != "ple_w_proj":
                    lw[k + "_t"] = transpose_colsharded(a[None])[0]
            else:
                lw[k] = a.reshape(N_CHIPS * a.shape[1], a.shape[2])
                lw[k + "_t"] = lw[k].T
        return lw

    gathered = {k: gather_weight("gather_" + k, shards[k][:1])[0] for k in BIG}

    p2 = p.reshape(depth, n, p.shape[-1])
    h_last = x.reshape(n, dm)
    saved, layers = [], []
    for l in range(depth):
        layers.append(layer_weights(l, gathered))
        nxt = {k: shards[k][l + 1:l + 2] for k in BIG} if l + 1 < depth else None
        h_last, sv, gathered = _layer_fwd(h_last, layers[l], p2[l], seq, nxt)
        saved.append(sv)

    def head(hh, tgt, gf):
        r = lax.rsqrt(jnp.mean(hh * hh, axis=-1, keepdims=True) + NORM_EPS)
        xh = hh * r
        diff = xh * gf - tgt
        dy = diff * (1.0 / dm)
        dxh = dy * gf
        dx = r * (dxh - xh * jnp.mean(dxh * xh, axis=-1, keepdims=True))
        return dx, jnp.sum(diff * diff, axis=0, keepdims=True) * (0.5 / dm), jnp.sum(dy * xh, axis=0, keepdims=True)

    d_last, loss_cols, g_final = _rowwise("loss_head", head, n, 512, [(h_last, dm, 0), (loss_target.reshape(n, dm), dm, 0)],
                                          [final_norm.reshape(1, dm)], [(dm, F32)], [(1, dm), (1, dm)])
    loss = lax.psum(jnp.sum(loss_cols), ("x", "y", "c"))

    d_x = d_last
    layer_grads = [None] * depth
    for l in reversed(range(depth)):
        d_x, layer_grads[l] = _layer_bwd(d_x, layers[l], p2[l], saved[l], seq)
    grads = {k: jnp.stack([g[k] for g in layer_grads]) for k in layer_grads[0]}
    grad_x = d_x.reshape(bsz, seq, dm)

    big_sum = [reduce_scatter_streamed("rs_" + k, grads[k]) for k in BIG]
    small_keys = [k for k in SMALL]
    small_parts = [grads[k] if k != "final_norm" else g_final.reshape(dm) for k in small_keys]
    small_sum = _unflatten_small(all_reduce_small(_flatten_small(small_parts)), small_parts)
    g_out = dict(zip(BIG, big_sum))
    g_out.update(dict(zip(small_keys, small_sum)))
    for k in BIG:
        g_out[k] = g_out[k].reshape(w[k].shape)

    delta, new_m, new_v = {}, {}, {}
    for k in BIG:
        delta[k], new_m[k], new_v[k] = adamw("adamw_" + k, w[k], g_out[k], mom[k], var[k])
    sw = adamw("adamw_small", *[_flatten_small([t[k] for k in small_keys]) for t in (w, g_out, mom, var)])
    for name, flat in zip((delta, new_m, new_v), sw):
        for k, t in zip(small_keys, _unflatten_small(flat, [w[k] for k in small_keys])):
            name[k] = t

    return (loss, grad_x, *[g_out[k] for k in WEIGHTS], *[delta[k] for k in WEIGHTS],
            *[new_m[k] for k in WEIGHTS], *[new_v[k] for k in WEIGHTS])
```

```python
import functools
import math

import jax
import jax.numpy as jnp
from jax import lax
from jax.experimental import pallas as pl
from jax.experimental.pallas import tpu as pltpu

F32 = jnp.float32
BF16 = jnp.bfloat16
MXU_DTYPE = jnp.bfloat16
GRAD_WIRE_DTYPE = jnp.bfloat16
STATE_DTYPE = jnp.bfloat16
VMEM_LIMIT = 56 * 1024 * 1024
LANE = 128
SUBLANE = 8

N_CHIPS = 4
SSM_GROUPS = 32
SSM_STATE = 64
SSM_CH = 16
SSM_WIDTH = SSM_GROUPS * SSM_CH
SSM_COLS = SSM_GROUPS * SSM_STATE
POOL_GROUPS = 4
POOL_CH = 128
POOL_WIDTH = POOL_GROUPS * POOL_CH
SSM_TILE = 256
NORM_EPS = 1e-6
ADAM_LR = 0.001
ADAM_B1 = 0.9
ADAM_B2 = 0.999
ADAM_EPS = 1e-08
ADAM_WD = 0.01
ADAM_STEP = 10
MESH_ID = pl.DeviceIdType.MESH

BIG = ("ffn1_wi", "ffn1_wo", "w_in", "ssm_w_glu", "w_out", "ffn2_wi", "ffn2_wo", "ple_w_gate", "ple_w_proj")
COL_SHARDED = ("ffn1_wi", "ffn2_wi", "ple_w_proj")
SMALL = ("ffn1_norm", "mix_norm", "ssm_lambda_re", "ssm_lambda_im", "ssm_log_dt", "ssm_b_re", "ssm_b_im",
         "ssm_c_re", "ssm_c_im", "ssm_d", "pool_w", "pool_scale", "ffn2_norm", "ple_norm", "final_norm")
WEIGHTS = ("ffn1_norm", "ffn1_wi", "ffn1_wo", "mix_norm", "w_in", "ssm_lambda_re", "ssm_lambda_im", "ssm_log_dt",
           "ssm_b_re", "ssm_b_im", "ssm_c_re", "ssm_c_im", "ssm_d", "ssm_w_glu", "pool_w", "pool_scale", "w_out",
           "ffn2_norm", "ffn2_wi", "ffn2_wo", "ple_norm", "ple_w_gate", "ple_w_proj", "final_norm")


def _tile(dim, target):
    best = None
    t = LANE
    while t <= min(dim, target):
        if dim % t == 0:
            best = t
        t += LANE
    return best if best is not None else dim


def _params(sem):
    return pltpu.CompilerParams(dimension_semantics=sem, vmem_limit_bytes=VMEM_LIMIT)


CHIPS = ((1, 0, 0), (0, 1, 0), (1, 1, 0))


def _ride_scratch(n):
    return [pltpu.SemaphoreType.DMA((3 * n,)), pltpu.SemaphoreType.DMA((3 * n,)), pltpu.SemaphoreType.DMA((n,))]


def _ride_copies(layers, ins, outs, send_sems, recv_sems, local_sems):
    x, y, c = lax.axis_index("x"), lax.axis_index("y"), lax.axis_index("c")
    me = 2 * x + y
    copies = []
    for i in range(len(ins)):
        src = ins[i].at[pl.ds(layers[i], 1)]
        dst = outs[i].at[pl.ds(me, 1)]
        copies.append(pltpu.make_async_copy(src, dst, local_sems.at[i]))
        for f, (fx, fy, _) in enumerate(CHIPS):
            peer = ((1 - x) if fx else x, (1 - y) if fy else y, c)
            copies.append(pltpu.make_async_remote_copy(
                src_ref=src, dst_ref=dst, send_sem=send_sems.at[3 * i + f], recv_sem=recv_sems.at[3 * i + f],
                device_id=peer, device_id_type=pl.DeviceIdType.MESH))
    return copies


def _ride(grid, layers, ins, outs, sems):
    if not ins:
        return
    ids = [pl.program_id(a) for a in range(len(grid))]
    first = functools.reduce(jnp.logical_and, [i == 0 for i in ids])
    last = functools.reduce(jnp.logical_and, [i == g - 1 for i, g in zip(ids, grid)])

    @pl.when(first)
    def _():
        for cp in _ride_copies(layers, ins, outs, *sems):
            cp.start()

    @pl.when(last)
    def _():
        for cp in _ride_copies(layers, ins, outs, *sems):
            cp.wait()


def _ride_out_shapes(riders):
    return [jax.ShapeDtypeStruct((N_CHIPS,) + r.shape[1:], r.dtype) for r, _ in riders]


def _mm(name, a, b, out_shape, out_dtype, grid, a_spec, b_spec, o_spec, contract, alpha=1.0, res=None, riders=(), into=None):
    n_k = grid[2]
    acc_shape = tuple(d for d in o_spec.block_shape if d is not None)
    n_in = 2 + (res is not None) + (into is not None)
    n_r = len(riders)
    ride_layers = [l for _, l in riders]

    def body(*refs):
        a_ref, b_ref = refs[0], refs[1]
        r_ref = refs[2] if res is not None else None
        o_ref = refs[n_in + n_r]
        if n_r:
            _ride(grid, ride_layers, refs[n_in:n_in + n_r], refs[n_in + n_r + 1:n_in + 2 * n_r + 1], refs[len(refs) - 3:])

        def product():
            return lax.dot_general(a_ref[...].astype(MXU_DTYPE), b_ref[...].astype(MXU_DTYPE),
                                   (contract, ((), ())), preferred_element_type=F32)

        def finish(v):
            if alpha != 1.0:
                v = v * alpha
            if r_ref is not None:
                v = v + r_ref[...].astype(F32)
            o_ref[...] = v.astype(out_dtype)

        if n_k == 1:
            finish(product())
            return
        acc = refs[n_in + 2 * n_r + 1]
        k = pl.program_id(2)

        @pl.when(k == 0)
        def _():
            acc[...] = product()

        @pl.when(k > 0)
        def _():
            acc[...] += product()

        @pl.when(k == n_k - 1)
        def _():
            finish(acc[...])

    in_specs = [a_spec, b_spec]
    operands = [a, b]
    if res is not None:
        in_specs.append(o_spec)
        operands.append(res)
    scratch = [pltpu.VMEM(acc_shape, F32)] if n_k > 1 else []
    any_spec = pl.BlockSpec(memory_space=pl.ANY)
    if into is not None:
        buf, layer = into
        assert buf.shape[1:] == tuple(out_shape) and buf.dtype == out_dtype and not n_r
        placed = pl.BlockSpec((None,) + tuple(o_spec.block_shape),
                              functools.partial(lambda i, j, kk, m, l: (l,) + tuple(m(i, j, kk)), m=o_spec.index_map, l=layer))
        return pl.pallas_call(
            body, name=name, grid=grid, in_specs=in_specs + [any_spec], out_specs=placed,
            out_shape=jax.ShapeDtypeStruct(buf.shape, out_dtype), scratch_shapes=scratch,
            input_output_aliases={len(operands): 0},
            compiler_params=_params(("parallel", "parallel", "arbitrary")),
        )(*operands, buf)
    if not n_r:
        return pl.pallas_call(
            body, name=name, grid=grid, in_specs=in_specs, out_specs=o_spec,
            out_shape=jax.ShapeDtypeStruct(out_shape, out_dtype), scratch_shapes=scratch,
            compiler_params=_params(("parallel", "parallel", "arbitrary")),
        )(*operands)
    return pl.pallas_call(
        body, name=name, grid=grid, in_specs=in_specs + [any_spec] * n_r, out_specs=[o_spec] + [any_spec] * n_r,
        out_shape=[jax.ShapeDtypeStruct(out_shape, out_dtype)] + _ride_out_shapes(riders),
        scratch_shapes=scratch + _ride_scratch(n_r),
        compiler_params=_params(("arbitrary", "arbitrary", "arbitrary")),
    )(*operands, *[r for r, _ in riders])


NN = ((1,), (0,))
NT = ((1,), (1,))
TN = ((0,), (0,))


def mm_nn(name, a, b, out_dtype, alpha=1.0, res=None, tm=1024, tn=1024, tk=1024, riders=()):
    m, k = a.shape
    n = b.shape[1]
    tm, tn, tk = _tile(m, tm), _tile(n, tn), _tile(k, tk)
    return _mm(name, a, b, (m, n), out_dtype, (m // tm, n // tn, k // tk),
               pl.BlockSpec((tm, tk), lambda i, j, kk: (i, kk)),
               pl.BlockSpec((tk, tn), lambda i, j, kk: (kk, j)),
               pl.BlockSpec((tm, tn), lambda i, j, kk: (i, j)), NN, alpha, res, riders)


def mm_nt(name, a, b, out_dtype, alpha=1.0, res=None, tm=1024, tn=512, tk=512):
    m, k = a.shape
    n = b.shape[0]
    tm, tn, tk = _tile(m, tm), _tile(n, tn), _tile(k, tk)
    return _mm(name, a, b, (m, n), out_dtype, (m // tm, n // tn, k // tk),
               pl.BlockSpec((tm, tk), lambda i, j, kk: (i, kk)),
               pl.BlockSpec((tn, tk), lambda i, j, kk: (j, kk)),
               pl.BlockSpec((tm, tn), lambda i, j, kk: (i, j)), NT, alpha, res)


def mm_tn(name, a, b, out_dtype, alpha=1.0, tm=1024, tn=1024, tk=1024, into=None):
    k, m = a.shape
    n = b.shape[1]
    tm, tn, tk = _tile(m, tm), _tile(n, tn), _tile(k, tk)
    if into is not None:
        shape = into[0].shape
        into = (into[0].reshape(shape[0], m, n), into[1])
    out = _mm(name, a, b, (m, n), out_dtype, (m // tm, n // tn, k // tk),
              pl.BlockSpec((tk, tm), lambda i, j, kk: (kk, i)),
              pl.BlockSpec((tk, tn), lambda i, j, kk: (kk, j)),
              pl.BlockSpec((tm, tn), lambda i, j, kk: (i, j)), TN, alpha, into=into)
    return out if into is None else out.reshape(shape)


def mm_nn_colsharded(name, a, w, out_dtype, tm=1024, tk=1024, riders=()):
    m, k = a.shape
    c = w.shape[2]
    tm, tk = _tile(m, tm), _tile(k, tk)
    return _mm(name, a, w, (m, N_CHIPS * c), out_dtype, (m // tm, N_CHIPS, k // tk),
               pl.BlockSpec((tm, tk), lambda i, j, kk: (i, kk)),
               pl.BlockSpec((None, tk, c), lambda i, j, kk: (j, kk, 0)),
               pl.BlockSpec((tm, c), lambda i, j, kk: (i, j)), NN, riders=riders)


def transpose_colsharded(w):
    nl, _, k, c = w.shape
    return jnp.swapaxes(w, 2, 3).reshape(nl, N_CHIPS * c, k)


def mm_tn_colsharded(name, a, b, out_dtype, tm=1024, tk=1024, into=None):
    t, k = a.shape
    c = b.shape[1] // N_CHIPS
    tm, tk = _tile(k, tm), _tile(t, tk)
    return _mm(name, a, b, (N_CHIPS, k, c), out_dtype, (k // tm, N_CHIPS, t // tk),
               pl.BlockSpec((tk, tm), lambda i, j, kk: (kk, i)),
               pl.BlockSpec((tk, c), lambda i, j, kk: (kk, j)),
               pl.BlockSpec((None, tm, c), lambda i, j, kk: (j, i, 0)), TN, into=into)


def _rowwise(name, fn, n_rows, tm, row_ins, bcast_ins, outs, accs=()):
    tm = min(tm, n_rows)
    grid = (n_rows // tm,)
    n_row, n_b, n_out = len(row_ins), len(bcast_ins), len(outs)

    def body(*refs):
        ins = [r[...] for r in refs[:n_row + n_b]]
        out_refs = refs[n_row + n_b:n_row + n_b + n_out]
        acc_refs = refs[n_row + n_b + n_out:]
        res = fn(*ins)
        if not isinstance(res, (tuple, list)):
            res = (res,)
        for o_ref, v in zip(out_refs, res[:n_out]):
            o_ref[...] = v.astype(o_ref.dtype)
        if acc_refs:
            @pl.when(pl.program_id(0) == 0)
            def _():
                for a_ref in acc_refs:
                    a_ref[...] = jnp.zeros_like(a_ref)
            for a_ref, v in zip(acc_refs, res[n_out:]):
                a_ref[...] += v

    in_specs, operands = [], []
    for spec in row_ins:
        arr, width, cb = spec[0], spec[1], spec[2]
        rb = spec[3] if len(spec) > 3 else 0
        in_specs.append(pl.BlockSpec((tm, width), functools.partial(lambda i, cb, rb: (i + rb, cb), cb=cb, rb=rb)))
        operands.append(arr)
    for arr in bcast_ins:
        in_specs.append(pl.BlockSpec(arr.shape, functools.partial(lambda i, nd: (0,) * nd, nd=arr.ndim)))
        operands.append(arr)
    out_specs = [pl.BlockSpec((tm, w), lambda i: (i, 0)) for w, _ in outs]
    out_specs += [pl.BlockSpec((r, w), lambda i: (0, 0)) for r, w in accs]
    out_shape = [jax.ShapeDtypeStruct((n_rows, w), dt) for w, dt in outs]
    out_shape += [jax.ShapeDtypeStruct((r, w), F32) for r, w in accs]
    res = pl.pallas_call(
        body, name=name, grid=grid, in_specs=in_specs, out_specs=out_specs, out_shape=out_shape,
        compiler_params=_params(("arbitrary",) if accs else ("parallel",)),
    )(*operands)
    return res


def _rms(x, g):
    r = lax.rsqrt(jnp.mean(x * x, axis=-1, keepdims=True) + NORM_EPS)
    return x * r * g


def _rms_bwd(dy, x, g):
    r = lax.rsqrt(jnp.mean(x * x, axis=-1, keepdims=True) + NORM_EPS)
    xh = x * r
    dxh = dy * g
    dx = r * (dxh - xh * jnp.mean(dxh * xh, axis=-1, keepdims=True))
    return dx, jnp.sum(dy * xh, axis=0, keepdims=True)


def norm_fwd(name, h, g):
    n, d = h.shape
    return _rowwise(name, lambda x, gg: _rms(x, gg), n, 512, [(h, d, 0)], [g.reshape(1, d)], [(d, BF16)])[0]


def norm_bwd(name, dxn, h, g, d_res):
    n, d = h.shape

    def fn(dy, x, dr, gg):
        dx, dg = _rms_bwd(dy, x, gg)
        return dr + dx, dg

    return _rowwise(name, fn, n, 512, [(dxn, d, 0), (h, d, 0), (d_res, d, 0)], [g.reshape(1, d)], [(d, F32)], [(1, d)])


_GELU_C = math.sqrt(2.0 / math.pi)


def _gelu(x):
    return 0.5 * x * (1.0 + jnp.tanh(_GELU_C * (x + 0.044715 * (x * x * x))))


def _gelu_grad(x):
    th = jnp.tanh(_GELU_C * (x + 0.044715 * (x * x * x)))
    return 0.5 * (1.0 + th) + 0.5 * x * (1.0 - th * th) * (_GELU_C * (1.0 + 3.0 * 0.044715 * (x * x)))


def _ssm_discretize(lam_re, lam_im, log_dt, b_re, b_im):
    dt = jnp.exp(log_dt)
    e = jnp.exp(lam_re * dt)
    lb_re = e * jnp.cos(lam_im * dt)
    lb_im = e * jnp.sin(lam_im * dt)
    nr, ni = lb_re - 1.0, lb_im
    den = lam_re * lam_re + lam_im * lam_im
    cr = (nr * lam_re + ni * lam_im) / den
    ci = (ni * lam_re - nr * lam_im) / den
    return lb_re, lb_im, cr * b_re - ci * b_im, cr * b_im + ci * b_re


def ssm_prep(name, lam_re, lam_im, log_dt, b_re, b_im):
    def body(lr, li, ld, br, bi, pr_ref, pi_ref, bbr_ref, bbi_ref):
        lb_re, lb_im, bb_re, bb_im = _ssm_discretize(lr[...], li[...], ld[...], br[...], bi[...])
        bbr_ref[...] = bb_re
        bbi_ref[...] = bb_im
        pr, pi = lb_re, lb_im
        cols_r, cols_i = [pr], [pi]
        for _ in range(SUBLANE - 1):
            pr, pi = pr * lb_re - pi * lb_im, pr * lb_im + pi * lb_re
            cols_r.append(pr)
            cols_i.append(pi)
        lane = lax.broadcasted_iota(jnp.int32, (SSM_COLS, SUBLANE), 1)
        out_r = jnp.zeros((SSM_COLS, SUBLANE), F32)
        out_i = jnp.zeros((SSM_COLS, SUBLANE), F32)
        for r in range(SUBLANE):
            out_r = jnp.where(lane == r, cols_r[r], out_r)
            out_i = jnp.where(lane == r, cols_i[r], out_i)
        pr_ref[...] = out_r
        pi_ref[...] = out_i

    shapes = [jax.ShapeDtypeStruct((SSM_COLS, SUBLANE), F32)] * 2 + [jax.ShapeDtypeStruct((SSM_COLS, SSM_CH), F32)] * 2
    return pl.pallas_call(body, name=name, out_shape=shapes,
                          compiler_params=pltpu.CompilerParams(vmem_limit_bytes=VMEM_LIMIT))(lam_re, lam_im, log_dt, b_re, b_im)


def ssm_prep_bwd(name, lam_re, lam_im, log_dt, b_re, b_im, d_lb_re, d_lb_im, d_bb_re, d_bb_im):
    def body(lr, li, ld, br, bi, g0, g1, g2, g3, o0, o1, o2, o3, o4):
        _, vjp = jax.vjp(_ssm_discretize, lr[...], li[...], ld[...], br[...], bi[...])
        res = vjp((g0[...], g1[...], g2[...], g3[...]))
        for o, v in zip((o0, o1, o2, o3, o4), res):
            o[...] = v

    col = jax.ShapeDtypeStruct((SSM_COLS, 1), F32)
    mat = jax.ShapeDtypeStruct((SSM_COLS, SSM_CH), F32)
    return pl.pallas_call(body, name=name, out_shape=[col, col, col, mat, mat],
                          compiler_params=pltpu.CompilerParams(vmem_limit_bytes=VMEM_LIMIT))(
        lam_re, lam_im, log_dt, b_re, b_im, d_lb_re, d_lb_im, d_bb_re, d_bb_im)


def scan_coefficients(pw_re, pw_im, reverse):
    pr, pi = pw_re.T, pw_im.T
    if reverse:
        pi = -pi
    row = jnp.arange(SUBLANE)[:, None]
    out = []
    for d in (1, 2, 4):
        valid = (row < SUBLANE - d) if reverse else (row >= d)
        out.append(jnp.where(valid, pr[d - 1][None, :], 0.0))
        out.append(jnp.where(valid, pi[d - 1][None, :], 0.0))
    out.append(pr[::-1] if reverse else pr)
    out.append(pi[::-1] if reverse else pi)
    return jnp.stack(out)


def ssm_scan(name, coef, x, seq, reverse, states=None, riders=()):
    n = x.shape[1]
    n_seq = n // seq
    cw = LANE
    n_cb = SSM_COLS // cw
    pair = 2 * SUBLANE
    n_pairs = seq // pair
    pairs_per_step = 2 if n_pairs % 2 == 0 else 1
    with_dlam = states is not None
    n_r = len(riders)
    assert not (n_r and with_dlam)

    def body(*refs):
        if with_dlam:
            coef_ref, x_ref, s_ref, o_ref, dl_ref = refs
        else:
            coef_ref, x_ref, o_ref = refs[0], refs[1], refs[2 + n_r]
            if n_r:
                _ride((n_cb, n_seq), [l for _, l in riders], refs[2:2 + n_r], refs[3 + n_r:3 + 2 * n_r], refs[3 + 2 * n_r:])
        c = [coef_ref[i] for i in range(8)]
        row16 = lax.broadcasted_iota(jnp.int32, (pair, cw), 0)
        zero = jnp.zeros((SUBLANE, cw), F32)

        edge = 0 if reverse else SUBLANE - 1

        def bcast_row(v, r, rows=SUBLANE):
            return jnp.broadcast_to(v[r:r + 1, :], (rows, cw))

        p8r, p8i = bcast_row(c[6], edge), bcast_row(c[7], edge)

        def local_scan(xr, xi):
            for si, d in enumerate((1, 2, 4)):
                sh = (SUBLANE - d) if reverse else d
                sr, sm = pltpu.roll(xr, sh, 0), pltpu.roll(xi, sh, 0)
                lre, lim = c[2 * si], c[2 * si + 1]
                xr, xi = xr + lre * sr - lim * sm, xi + lre * sm + lim * sr
            return xr, xi

        def step(it, carry):
            work = []
            for u in range(pairs_per_step):
                k = it * pairs_per_step + u
                pidx = (n_pairs - 1 - k) if reverse else k
                off = pl.multiple_of(pidx * pair, pair)
                xr16 = x_ref[0, pl.ds(off, pair), :].astype(F32)
                xi16 = x_ref[1, pl.ds(off, pair), :].astype(F32)
                halves = (1, 0) if reverse else (0, 1)
                tiles = {h: local_scan(xr16[h * SUBLANE:(h + 1) * SUBLANE], xi16[h * SUBLANE:(h + 1) * SUBLANE]) for h in halves}
                work.append((pidx, off, halves, tiles))
            cre, cim = carry[0], carry[1]
            acc = carry[2:]
            for pidx, off, halves, tiles in work:
                done = {}
                for h in halves:
                    lr, li = tiles[h]
                    done[h] = (lr + c[6] * cre - c[7] * cim, li + c[6] * cim + c[7] * cre)
                    cre, cim = (bcast_row(lr, edge) + p8r * cre - p8i * cim, bcast_row(li, edge) + p8r * cim + p8i * cre)
                or16 = jnp.concatenate([done[0][0], done[1][0]], axis=0)
                oi16 = jnp.concatenate([done[0][1], done[1][1]], axis=0)
                o_ref[0, pl.ds(off, pair), :] = or16.astype(o_ref.dtype)
                o_ref[1, pl.ds(off, pair), :] = oi16.astype(o_ref.dtype)
                if with_dlam:
                    poff = pl.multiple_of(jnp.maximum(pidx - 1, 0) * pair, pair)
                    first = pidx > 0
                    sr16 = s_ref[0, pl.ds(off, pair), :].astype(F32)
                    si16 = s_ref[1, pl.ds(off, pair), :].astype(F32)
                    pr_last = jnp.where(first, bcast_row(s_ref[0, pl.ds(poff, pair), :].astype(F32), pair - 1, pair), 0.0)
                    pi_last = jnp.where(first, bcast_row(s_ref[1, pl.ds(poff, pair), :].astype(F32), pair - 1, pair), 0.0)
                    spr = jnp.where(row16 == 0, pr_last, pltpu.roll(sr16, 1, 0))
                    spi = jnp.where(row16 == 0, pi_last, pltpu.roll(si16, 1, 0))
                    dre = or16 * spr + oi16 * spi
                    dim = oi16 * spr - or16 * spi
                    acc = (acc[0] + dre[:SUBLANE] + dre[SUBLANE:], acc[1] + dim[:SUBLANE] + dim[SUBLANE:])
            return (cre, cim) + tuple(acc)

        init = (zero, zero, zero, zero) if with_dlam else (zero, zero)
        fin = lax.fori_loop(0, n_pairs // pairs_per_step, step, init)
        if with_dlam:
            @pl.when(pl.program_id(1) == 0)
            def _():
                dl_ref[...] = jnp.zeros_like(dl_ref)
            dl_ref[0] += fin[2]
            dl_ref[1] += fin[3]

    blk = pl.BlockSpec((2, seq, cw), lambda j, b: (0, b, j))
    in_specs = [pl.BlockSpec((8, SUBLANE, cw), lambda j, b: (0, 0, j)), blk]
    operands = [coef, x]
    out_specs = [blk]
    out_shape = [jax.ShapeDtypeStruct(x.shape, STATE_DTYPE)]
    if with_dlam:
        in_specs.append(blk)
        operands.append(states)
        out_specs.append(pl.BlockSpec((2, SUBLANE, cw), lambda j, b: (0, 0, j)))
        out_shape.append(jax.ShapeDtypeStruct((2, SUBLANE, SSM_COLS), F32))
    scratch = []
    if n_r:
        any_spec = pl.BlockSpec(memory_space=pl.ANY)
        in_specs += [any_spec] * n_r
        operands += [r for r, _ in riders]
        out_specs += [any_spec] * n_r
        out_shape += _ride_out_shapes(riders)
        scratch = _ride_scratch(n_r)
    res = pl.pallas_call(
        body, name=name, grid=(n_cb, n_seq), in_specs=in_specs, out_specs=out_specs, out_shape=out_shape,
        scratch_shapes=scratch, compiler_params=_params(("arbitrary" if n_r else "parallel", "arbitrary")),
    )(*operands)
    return res if (with_dlam or n_r) else res[0]


def _state_col(j, kk):
    return 2 * j + kk + 2 * (kk // 2)


SSM_WIDE = 4 * SSM_TILE


def ssm_in(name, z, bbd, tm=1024):
    n = z.shape[0]
    tm = _tile(n, tm)
    t, w = SSM_TILE, SSM_WIDE
    return _mm(name, z, bbd, (2, n, SSM_COLS), STATE_DTYPE, (n // tm, 2 * SSM_COLS // w, 1),
               pl.BlockSpec((tm, t), lambda i, j, kk: (i, j % 2)),
               pl.BlockSpec((t, w), lambda i, j, kk: (j % 2, j)),
               pl.BlockSpec((None, tm, w), lambda i, j, kk: (j // 2, i, j % 2)), NN)


def ssm_out(name, s, cbd, tm=1024):
    n = s.shape[1]
    tm = _tile(n, tm)
    t = SSM_TILE
    return _mm(name, s, cbd, (n, SSM_WIDTH), F32, (n // tm, SSM_WIDTH // t, 4),
               pl.BlockSpec((None, tm, 512), lambda i, j, kk: (kk // 2, i, 2 * j + kk % 2)),
               pl.BlockSpec((512, t), lambda i, j, kk: (_state_col(j, kk), j)),
               pl.BlockSpec((tm, t), lambda i, j, kk: (i, j)), NN)


def ssm_out_t(name, dy, cbd, tm=1024):
    n = dy.shape[0]
    tm = _tile(n, tm)
    t, w = SSM_TILE, SSM_WIDE
    return _mm(name, dy, cbd, (2, n, SSM_COLS), STATE_DTYPE, (n // tm, 2 * SSM_COLS // w, 1),
               pl.BlockSpec((tm, t), lambda i, j, kk: (i, j % 2)),
               pl.BlockSpec((w, t), lambda i, j, kk: (j, j % 2)),
               pl.BlockSpec((None, tm, w), lambda i, j, kk: (j // 2, i, j % 2)), NT)


def ssm_in_t(name, a, bbd, res, tm=1024):
    n = a.shape[1]
    tm = _tile(n, tm)
    t = SSM_TILE
    return _mm(name, a, bbd, (n, SSM_WIDTH), F32, (n // tm, SSM_WIDTH // t, 4),
               pl.BlockSpec((None, tm, 512), lambda i, j, kk: (kk // 2, i, 2 * j + kk % 2)),
               pl.BlockSpec((t, 512), lambda i, j, kk: (j, _state_col(j, kk))),
               pl.BlockSpec((tm, t), lambda i, j, kk: (i, j)), NT, 1.0, res)


def ssm_grad_c(name, s, dy, tk=1024):
    n = s.shape[1]
    tk = _tile(n, tk)
    t, w = SSM_TILE, SSM_WIDE
    return _mm(name, s, dy, (2 * SSM_COLS, t), F32, (2 * SSM_COLS // w, 1, n // tk),
               pl.BlockSpec((None, tk, w), lambda i, j, kk: (i // 2, kk, i % 2)),
               pl.BlockSpec((tk, t), lambda i, j, kk: (kk, i % 2)),
               pl.BlockSpec((w, t), lambda i, j, kk: (i, 0)), TN)


def ssm_grad_b(name, z, a, tk=1024):
    n = z.shape[0]
    tk = _tile(n, tk)
    t, w = SSM_TILE, SSM_WIDE
    return _mm(name, z, a, (t, 2 * SSM_COLS), F32, (1, 2 * SSM_COLS // w, n // tk),
               pl.BlockSpec((tk, t), lambda i, j, kk: (kk, j % 2)),
               pl.BlockSpec((None, tk, w), lambda i, j, kk: (j // 2, kk, j % 2)),
               pl.BlockSpec((t, w), lambda i, j, kk: (0, j)), TN)


_GROUP_TILE = SSM_TILE // SSM_CH


def expand_b(bb_re, bb_im):
    b = jnp.stack([bb_re, bb_im]).reshape(2, SSM_GROUPS, SSM_STATE, SSM_CH)
    eye = jnp.eye(SSM_GROUPS, dtype=F32)
    return jnp.einsum("rgph,gk->ghrkp", b, eye).reshape(SSM_WIDTH, 2 * SSM_COLS).astype(MXU_DTYPE)


def expand_c(c_re, c_im):
    c = jnp.stack([c_re, -c_im])
    eye = jnp.eye(SSM_GROUPS, dtype=F32)
    return jnp.einsum("rghp,gk->rgpkh", c, eye).reshape(2 * SSM_COLS, SSM_WIDTH).astype(MXU_DTYPE)


def _group_pick():
    return (jnp.arange(SSM_GROUPS)[:, None] % _GROUP_TILE == jnp.arange(_GROUP_TILE)[None, :]).astype(F32)


def compact_c(dc):
    x = dc.reshape(2, SSM_GROUPS, SSM_STATE, _GROUP_TILE, SSM_CH)
    g = jnp.einsum("rgpch,gc->rghp", x, _group_pick())
    return g[0], -g[1]


def compact_b(db):
    x = db.reshape(_GROUP_TILE, SSM_CH, 2, SSM_GROUPS, SSM_STATE)
    g = jnp.einsum("chrgp,gc->rgph", x, _group_pick()).reshape(2, SSM_COLS, SSM_CH)
    return g[0], g[1]


def pool_window(name, x, col_block0, seq, out_dtype, adjoint):
    n = x.shape[0]

    def body(x_ref, o_ref):
        win = 2 << pl.program_id(1)
        row = lax.broadcasted_iota(jnp.int32, (seq, POOL_CH), 0)
        v = x_ref[...].astype(F32)
        cnt = jnp.minimum(row + 1, win).astype(F32)
        s = v / cnt if adjoint else v
        for d in (1, 2, 4, 8):
            if adjoint:
                sh = jnp.where((row < seq - d) & (d < win), pltpu.roll(s, seq - d, 0), 0.0)
            else:
                sh = jnp.where((row >= d) & (d < win), pltpu.roll(s, d, 0), 0.0)
            s = s + sh
        o_ref[...] = ((s - v) if adjoint else (s / cnt - v)).astype(out_dtype)

    return pl.pallas_call(
        body, name=name, grid=(n // seq, POOL_GROUPS),
        in_specs=[pl.BlockSpec((seq, POOL_CH), lambda b, g: (b, col_block0 + g))],
        out_specs=pl.BlockSpec((seq, POOL_CH), lambda b, g: (b, g)),
        out_shape=jax.ShapeDtypeStruct((n, POOL_WIDTH), out_dtype),
        compiler_params=_params(("parallel", "parallel")),
    )(x)


def pool_mm(name, q, w, out_dtype, tm=1024):
    n = q.shape[0]
    tm = _tile(n, tm)
    return _mm(name, q, w, (n, POOL_WIDTH), out_dtype, (n // tm, POOL_GROUPS, 1),
               pl.BlockSpec((tm, POOL_CH), lambda i, j, kk: (i, j)),
               pl.BlockSpec((None, POOL_CH, POOL_CH), lambda i, j, kk: (j, 0, 0)),
               pl.BlockSpec((tm, POOL_CH), lambda i, j, kk: (i, j)), NN)


def pool_mm_t(name, dy, col_block0, w, out_dtype, tm=1024):
    n = dy.shape[0]
    tm = _tile(n, tm)
    return _mm(name, dy, w, (n, POOL_WIDTH), out_dtype, (n // tm, POOL_GROUPS, 1),
               pl.BlockSpec((tm, POOL_CH), lambda i, j, kk: (i, col_block0 + j)),
               pl.BlockSpec((None, POOL_CH, POOL_CH), lambda i, j, kk: (j, 0, 0)),
               pl.BlockSpec((tm, POOL_CH), lambda i, j, kk: (i, j)), NT)


def pool_grad_w(name, q, dy, col_block0, tk=1024):
    n = q.shape[0]
    tk = _tile(n, tk)
    return _mm(name, q, dy, (POOL_GROUPS, POOL_CH, POOL_CH), F32, (POOL_GROUPS, 1, n // tk),
               pl.BlockSpec((tk, POOL_CH), lambda i, j, kk: (kk, i)),
               pl.BlockSpec((tk, POOL_CH), lambda i, j, kk: (kk, col_block0 + i)),
               pl.BlockSpec((None, POOL_CH, POOL_CH), lambda i, j, kk: (i, 0, 0)), TN)


def _any_specs(n):
    return [pl.BlockSpec(memory_space=pl.ANY)] * n


def _place():
    x, y, c = lax.axis_index("x"), lax.axis_index("y"), lax.axis_index("c")
    return x, y, c


def _at_axis(ref, axis, start, size):
    return ref.at[(slice(None),) * axis + (pl.ds(start, size),)]


def sibling_swap_halves(name, arrays, axis):
    n = len(arrays)
    halves = [a.shape[axis] // 2 for a in arrays]

    def body(*refs):
        ins, own, got = refs[:n], refs[n:2 * n], refs[2 * n:3 * n]
        send_sems, recv_sems, local_sems = refs[3 * n:]
        x, y, c = _place()
        copies = []
        for i in range(n):
            h = halves[i]
            mine = pltpu.make_async_copy(_at_axis(ins[i], axis, c * h, h), own[i], local_sems.at[i])
            mine.start()
            away = pltpu.make_async_remote_copy(
                src_ref=_at_axis(ins[i], axis, (1 - c) * h, h), dst_ref=got[i],
                send_sem=send_sems.at[i], recv_sem=recv_sems.at[i], device_id=(x, y, 1 - c), device_id_type=MESH_ID)
            away.start()
            copies += [mine, away]
        for cp in copies:
            cp.wait()

    def half_shape(a, h):
        return jax.ShapeDtypeStruct(a.shape[:axis] + (h,) + a.shape[axis + 1:], a.dtype)

    shapes = [half_shape(a, h) for a, h in zip(arrays, halves)]
    res = pl.pallas_call(
        body, name=name, in_specs=_any_specs(n), out_specs=_any_specs(2 * n), out_shape=shapes + shapes,
        scratch_shapes=[pltpu.SemaphoreType.DMA((n,)), pltpu.SemaphoreType.DMA((n,)), pltpu.SemaphoreType.DMA((n,))],
    )(*arrays)
    return res[:n], res[n:]


def sibling_join_halves(name, arrays, axis):
    n = len(arrays)

    def body(*refs):
        ins, outs = refs[:n], refs[n:2 * n]
        send_sems, recv_sems, local_sems = refs[2 * n:]
        x, y, c = _place()
        copies = []
        for i in range(n):
            h = ins[i].shape[axis]
            dst = _at_axis(outs[i], axis, c * h, h)
            mine = pltpu.make_async_copy(ins[i], dst, local_sems.at[i])
            mine.start()
            away = pltpu.make_async_remote_copy(
                src_ref=ins[i], dst_ref=dst, send_sem=send_sems.at[i], recv_sem=recv_sems.at[i],
                device_id=(x, y, 1 - c), device_id_type=MESH_ID)
            away.start()
            copies += [mine, away]
        for cp in copies:
            cp.wait()

    shapes = [jax.ShapeDtypeStruct(a.shape[:axis] + (2 * a.shape[axis],) + a.shape[axis + 1:], a.dtype) for a in arrays]
    return pl.pallas_call(
        body, name=name, in_specs=_any_specs(n), out_specs=_any_specs(n), out_shape=shapes,
        scratch_shapes=[pltpu.SemaphoreType.DMA((n,)), pltpu.SemaphoreType.DMA((n,)), pltpu.SemaphoreType.DMA((n,))],
    )(*arrays)


def sibling_swap(name, arrays):
    n = len(arrays)

    def body(*refs):
        ins, outs = refs[:n], refs[n:2 * n]
        send_sems, recv_sems = refs[2 * n:]
        x, y, c = _place()
        copies = []
        for i in range(n):
            away = pltpu.make_async_remote_copy(
                src_ref=ins[i], dst_ref=outs[i], send_sem=send_sems.at[i], recv_sem=recv_sems.at[i],
                device_id=(x, y, 1 - c), device_id_type=MESH_ID)
            away.start()
            copies.append(away)
        for cp in copies:
            cp.wait()

    return pl.pallas_call(
        body, name=name, in_specs=_any_specs(n), out_specs=_any_specs(n),
        out_shape=[jax.ShapeDtypeStruct(a.shape, a.dtype) for a in arrays],
        scratch_shapes=[pltpu.SemaphoreType.DMA((n,)), pltpu.SemaphoreType.DMA((n,))],
    )(*arrays)


_FLIPS = ((1, 0), (0, 1), (1, 1))


def chip_exchange(name, arrays, axis, all_to_all):
    n = len(arrays)

    def body(*refs):
        ins, outs = refs[:n], refs[n:2 * n]
        send_sems, recv_sems, local_sems = refs[2 * n:]
        x, y, c = _place()
        me = 2 * x + y
        copies = []
        for i in range(n):
            dst = _at_axis(outs[i], axis, me, 1)
            mine = pltpu.make_async_copy(_at_axis(ins[i], axis, me, 1) if all_to_all else ins[i], dst, local_sems.at[i])
            mine.start()
            copies.append(mine)
            for f, (fx, fy) in enumerate(_FLIPS):
                px = (1 - x) if fx else x
                py = (1 - y) if fy else y
                src = _at_axis(ins[i], axis, 2 * px + py, 1) if all_to_all else ins[i]
                away = pltpu.make_async_remote_copy(
                    src_ref=src, dst_ref=dst, send_sem=send_sems.at[3 * i + f], recv_sem=recv_sems.at[3 * i + f],
                    device_id=(px, py, c), device_id_type=MESH_ID)
                away.start()
                copies.append(away)
        for cp in copies:
            cp.wait()

    shapes = [jax.ShapeDtypeStruct(a.shape[:axis] + (N_CHIPS,) + a.shape[axis + 1:], a.dtype) for a in arrays]
    return pl.pallas_call(
        body, name=name, in_specs=_any_specs(n), out_specs=_any_specs(n), out_shape=shapes,
        scratch_shapes=[pltpu.SemaphoreType.DMA((3 * n,)), pltpu.SemaphoreType.DMA((3 * n,)), pltpu.SemaphoreType.DMA((n,))],
    )(*arrays)


SIBLING = ((0, 0, 1),)
ICI_CHUNK_BYTES = 2 * 1024 * 1024
D2D_CHUNK_BYTES = 4 * 1024 * 1024


def _peer(flip):
    x, y, c = _place()
    return tuple((1 - v) if f else v for v, f in zip((x, y, c), flip))


def _core():
    return lax.axis_index("c")


def _chip():
    return 2 * lax.axis_index("x") + lax.axis_index("y")


def _linear_step(grid):
    i = pl.program_id(0)
    for a in range(1, len(grid)):
        i = i * grid[a] + pl.program_id(a)
    return i


def stream_reduce(name, x, grid, block, own_map, send_maps, flips, out_shape, out_block, out_map, wire_dtype=None):
    n_steps = math.prod(grid)
    n_p = len(flips)
    vm_block = block = tuple(1 if d is None else d for d in block)
    out_block = tuple(1 if d is None else d for d in out_block)
    staged = wire_dtype is not None and wire_dtype != x.dtype
    slot_dtype = wire_dtype if staged else x.dtype

    def body(own_ref, *rest):
        send_refs = rest[:n_p]
        o_ref, recv, send_sems, recv_sems, credits = rest[n_p:n_p + 5]
        stage = rest[n_p + 5] if staged else None
        i = _linear_step(grid)
        s = i % 2
        copies = []
        for j, flip in enumerate(flips):
            src = send_refs[j]
            if staged:
                stage[j, s] = send_refs[j][...].astype(wire_dtype)
                src = stage.at[j, s]

            @pl.when(i >= 2)
            def _():
                pl.semaphore_wait(credits.at[j, s], 1)
            cp = pltpu.make_async_remote_copy(
                src_ref=src, dst_ref=recv.at[j, s], send_sem=send_sems.at[j, s], recv_sem=recv_sems.at[j, s],
                device_id=_peer(flip), device_id_type=MESH_ID)
            cp.start()
            copies.append(cp)
        acc = own_ref[...]
        for j, cp in enumerate(copies):
            cp.wait_recv()
            acc = acc + recv[j, s].astype(acc.dtype)
        o_ref[...] = acc.reshape(o_ref.shape)
        for cp in copies:
            cp.wait_send()
        for j, flip in enumerate(flips):
            @pl.when(i < n_steps - 2)
            def _():
                pl.semaphore_signal(credits.at[j, s], inc=1, device_id=_peer(flip), device_id_type=MESH_ID)

    in_specs = [pl.BlockSpec(block, own_map)] + [pl.BlockSpec(block, m) for m in send_maps]
    scratch = [pltpu.VMEM((n_p, 2) + vm_block, slot_dtype), pltpu.SemaphoreType.DMA((n_p, 2)),
               pltpu.SemaphoreType.DMA((n_p, 2)), pltpu.SemaphoreType.REGULAR((n_p, 2))]
    if staged:
        scratch.append(pltpu.VMEM((n_p, 2) + vm_block, slot_dtype))
    return pl.pallas_call(
        body, name=name, grid=grid, in_specs=in_specs, out_specs=pl.BlockSpec(out_block, out_map),
        out_shape=jax.ShapeDtypeStruct(out_shape, x.dtype), scratch_shapes=scratch,
        compiler_params=_params(("arbitrary",) * len(grid)),
    )(*([x] * (1 + n_p)))


def stream_gather(name, x, grid, block, in_map, flips, out_shape, out_block, out_map):
    n_p = len(flips)
    assert grid[-1] == n_p + 1
    n_steps = math.prod(grid[:-1])
    vm_block = block = tuple(1 if d is None else d for d in block)
    out_block = tuple(1 if d is None else d for d in out_block)

    def body(x_ref, o_ref, recv, send_sems, recv_sems, credits):
        i = _linear_step(grid[:-1])
        q = pl.program_id(len(grid) - 1)
        s = i % 2

        def copy(j):
            return pltpu.make_async_remote_copy(
                src_ref=x_ref, dst_ref=recv.at[j, s], send_sem=send_sems.at[j, s], recv_sem=recv_sems.at[j, s],
                device_id=_peer(flips[j]), device_id_type=MESH_ID)

        @pl.when(q == 0)
        def _():
            for j in range(n_p):
                @pl.when(i >= 2)
                def _():
                    pl.semaphore_wait(credits.at[j, s], 1)
                copy(j).start()
            o_ref[...] = x_ref[...].reshape(o_ref.shape)
            for j in range(n_p):
                copy(j).wait_send()

        for j in range(n_p):
            @pl.when(q == j + 1)
            def _():
                copy(j).wait_recv()
                o_ref[...] = recv[j, s].reshape(o_ref.shape)

                @pl.when(i < n_steps - 2)
                def _():
                    pl.semaphore_signal(credits.at[j, s], inc=1, device_id=_peer(flips[j]), device_id_type=MESH_ID)

    return pl.pallas_call(
        body, name=name, grid=grid, in_specs=[pl.BlockSpec(block, in_map)], out_specs=pl.BlockSpec(out_block, out_map),
        out_shape=jax.ShapeDtypeStruct(out_shape, x.dtype),
        scratch_shapes=[pltpu.VMEM((n_p, 2) + vm_block, x.dtype), pltpu.SemaphoreType.DMA((n_p, 2)),
                        pltpu.SemaphoreType.DMA((n_p, 2)), pltpu.SemaphoreType.REGULAR((n_p, 2))],
        compiler_params=_params(("arbitrary",) * len(grid)),
    )(x)


def _chip_of_substep(q):
    mask = jnp.where(q == 1, 2, jnp.where(q == 2, 1, jnp.where(q == 3, 3, 0)))
    return jnp.bitwise_xor(_chip(), mask)


def gather_weight(name, shard):
    nl, r, c = shard.shape
    r2 = r // 2
    f32_per_elem = 4 // shard.dtype.itemsize
    tr = _rows_tile(r2, c, budget=ICI_CHUNK_BYTES * f32_per_elem, step=16)
    nb = r2 // tr
    half = stream_gather(
        name + "_chips", shard, (nl, nb, N_CHIPS), (None, tr, c), lambda l, i, q: (l, _core() * nb + i, 0), CHIPS,
        (nl, N_CHIPS, r2, c), (None, None, tr, c), lambda l, i, q: (l, _chip_of_substep(q), i, 0))
    trd = _rows_tile(r2, c, budget=D2D_CHUNK_BYTES * f32_per_elem, step=16)
    nbd = r2 // trd
    both = stream_gather(
        name + "_cores", half, (nl, N_CHIPS, nbd, 2), (None, None, trd, c), lambda l, k, i, q: (l, k, i, 0), SIBLING,
        (nl, N_CHIPS, 2, r2, c), (None, None, None, trd, c), lambda l, k, i, q: (l, k, _core() + q - 2 * _core() * q, i, 0))
    return both.reshape(nl, N_CHIPS, r, c)


def reduce_scatter_streamed(name, g):
    nl, _, r, c = g.shape
    r2 = r // 2
    tr_d2d = _rows_tile(r2, c, budget=D2D_CHUNK_BYTES, step=16)
    nbd = r2 // tr_d2d
    chip_sum = stream_reduce(
        name + "_cores", g, (nl, N_CHIPS, nbd), (None, None, tr_d2d, c),
        lambda l, k, i: (l, k, _core() * nbd + i, 0), [lambda l, k, i: (l, k, (1 - _core()) * nbd + i, 0)], SIBLING,
        (nl, N_CHIPS, r2, c), (None, None, tr_d2d, c), lambda l, k, i: (l, k, i, 0), wire_dtype=GRAD_WIRE_DTYPE)
    tr = _rows_tile(r2, c, budget=ICI_CHUNK_BYTES, step=16)
    nb = r2 // tr
    blk4 = (None, None, tr, c)
    masks = (2, 1, 3)
    mine = stream_reduce(
        name + "_chips", chip_sum, (nl, nb), blk4,
        lambda l, i: (l, _chip(), i, 0),
        [functools.partial(lambda l, i, m: (l, jnp.bitwise_xor(_chip(), m), i, 0), m=m) for m in masks], CHIPS,
        (nl, r2, c), (None, tr, c), lambda l, i: (l, i, 0), wire_dtype=GRAD_WIRE_DTYPE)
    both = stream_gather(
        name + "_join", mine, (nl, nbd, 2), (None, tr_d2d, c), lambda l, i, q: (l, i, 0), SIBLING,
        (nl, 2, r2, c), (None, None, tr_d2d, c), lambda l, i, q: (l, _core() + q - 2 * _core() * q, i, 0))
    return both.reshape(nl, r, c)


def add2(name, a, b):
    shape = a.shape
    a2, b2 = a.reshape(-1, shape[-1]), b.reshape(-1, shape[-1])
    rows, w = a2.shape
    tm = _rows_tile(rows, w)
    return _rowwise(name, lambda u, v: u + v, rows, tm, [(a2, w, 0), (b2, w, 0)], [], [(w, F32)])[0].reshape(shape)


def _rows_tile(rows, width, budget=2 * 1024 * 1024, step=SUBLANE):
    best = step
    t = step
    while t <= rows:
        if rows % t == 0 and t * width * 4 <= budget:
            best = t
        t += step
    return best


def sum_slots(name, a):
    nl, _, r, c = a.shape
    tr = _rows_tile(r, c)

    def body(s0, s1, s2, s3, o_ref):
        o_ref[...] = ((s0[...] + s1[...]) + s2[...]) + s3[...]

    specs = [pl.BlockSpec((None, None, tr, c), functools.partial(lambda l, i, k: (l, k, i, 0), k=k)) for k in range(N_CHIPS)]
    return pl.pallas_call(
        body, name=name, grid=(nl, r // tr), in_specs=specs,
        out_specs=pl.BlockSpec((None, tr, c), lambda l, i: (l, i, 0)),
        out_shape=jax.ShapeDtypeStruct((nl, r, c), F32),
        compiler_params=_params(("parallel", "parallel")),
    )(a, a, a, a)


def reduce_scatter_big(grads):
    own, got = sibling_swap_halves("rs_swap_halves", grads, 2)
    chip_sum = [add2("rs_add_cores", a, b) for a, b in zip(own, got)]
    spread = chip_exchange("rs_chips", chip_sum, 1, True)
    mine = [sum_slots("rs_sum_chips", a) for a in spread]
    return sibling_join_halves("rs_join_halves", mine, 1)


def all_reduce_small(flat):
    other = sibling_swap("ar_swap", [flat])[0]
    chip = add2("ar_add_cores", flat, other)
    slots = chip_exchange("ar_chips", [chip.reshape((1,) + chip.shape)], 0, False)[0]
    rows = flat.shape[0]
    tm = _rows_tile(rows, LANE)
    nb = rows // tm
    s2 = slots.reshape(N_CHIPS * rows, LANE)
    return _rowwise("ar_sum_chips", lambda a, b, c, d: ((a + b) + c) + d, rows, tm,
                    [(s2, LANE, 0, k * nb) for k in range(N_CHIPS)], [], [(LANE, F32)])[0]


def _adamw_math(w, g, m, v):
    m = ADAM_B1 * m + (1.0 - ADAM_B1) * g
    v = ADAM_B2 * v + (1.0 - ADAM_B2) * (g * g)
    m_hat = m / (1.0 - ADAM_B1 ** ADAM_STEP)
    v_hat = v / (1.0 - ADAM_B2 ** ADAM_STEP)
    delta = -ADAM_LR * (m_hat / (jnp.sqrt(v_hat) + ADAM_EPS) + ADAM_WD * w)
    return delta, m, v


def adamw(name, w, g, m, v):
    shape = w.shape
    width = shape[-1]
    flat = [t.reshape(-1, width) for t in (w, g, m, v)]
    rows = flat[0].shape[0]
    tm = _rows_tile(rows, width, budget=1024 * 1024)
    res = _rowwise(name, _adamw_math, rows, tm, [(t, width, 0) for t in flat], [], [(width, F32)] * 3)
    return [r.reshape(shape) for r in res]


def ffn_in(name, xn, wi, riders=(), tm=512):
    n, d = xn.shape
    c = wi.shape[2]
    tm = _tile(n, tm)
    grid = (n // tm, 2)
    n_r = len(riders)

    def body(*refs):
        x_ref, wg_ref, wu_ref = refs[:3]
        g_ref, u_ref, a_ref = refs[3 + n_r:6 + n_r]
        if n_r:
            _ride(grid, [l for _, l in riders], refs[3:3 + n_r], refs[6 + n_r:6 + 2 * n_r], refs[6 + 2 * n_r:])
        x = x_ref[...].astype(MXU_DTYPE)
        g = jnp.dot(x, wg_ref[...].astype(MXU_DTYPE), preferred_element_type=F32)
        u = jnp.dot(x, wu_ref[...].astype(MXU_DTYPE), preferred_element_type=F32)
        g_ref[...] = g.astype(g_ref.dtype)
        u_ref[...] = u.astype(u_ref.dtype)
        a_ref[...] = (g * jax.nn.sigmoid(g) * u).astype(a_ref.dtype)

    any_spec = pl.BlockSpec(memory_space=pl.ANY)
    out_blk = pl.BlockSpec((tm, c), lambda i, j: (i, j))
    res = pl.pallas_call(
        body, name=name, grid=grid,
        in_specs=[pl.BlockSpec((tm, d), lambda i, j: (i, 0)), pl.BlockSpec((None, d, c), lambda i, j: (j, 0, 0)),
                  pl.BlockSpec((None, d, c), lambda i, j: (j + 2, 0, 0))] + [any_spec] * n_r,
        out_specs=[out_blk] * 3 + [any_spec] * n_r,
        out_shape=[jax.ShapeDtypeStruct((n, 2 * c), BF16)] * 3 + _ride_out_shapes(riders),
        scratch_shapes=_ride_scratch(n_r) if n_r else [],
        compiler_params=_params(("arbitrary", "arbitrary") if n_r else ("parallel", "parallel")),
    )(xn, wi, wi, *[r for r, _ in riders])
    return res[:3], list(res[3:])


def _ffn_fwd(tag, h, g_norm, wi, wo, next_wi=None, next_wo=None):
    fetched = []
    xn = norm_fwd(tag + "_norm", h, g_norm)
    (g, u, act), got = ffn_in(tag + "_wi", xn, wi, riders=() if next_wi is None else (next_wi,))
    fetched += got
    out = mm_nn(tag + "_wo", act, wo, F32, alpha=0.5, res=h, tm=512, tk=2816, riders=() if next_wo is None else (next_wo,))
    if next_wo is not None:
        out, got = out
        fetched.append(got)
    return out, (xn, g, u, act), fetched


def _ffn_bwd(tag, d, h, g_norm, wi_t, wo_t, saved, into_wi, into_wo):
    xn, gate, up, act = saved
    n, dm = h.shape
    ff = wo_t.shape[1]
    dact = mm_nn(tag + "_dact", d, wo_t, BF16, alpha=0.5, tn=1408)
    dwo = mm_tn(tag + "_dwo", act, d, F32, alpha=0.5, tm=1408, into=into_wo)

    def swiglu_bwd(g, u, da):
        g, u, da = g.astype(F32), u.astype(F32), da.astype(F32)
        sg = jax.nn.sigmoid(g)
        dg = da * u * (sg * (1.0 + g * (1.0 - sg)))
        du = da * (g * sg)
        return jnp.concatenate([dg, du], axis=1)

    dgu = _rowwise(tag + "_dswiglu", swiglu_bwd, n, 512, [(gate, ff, 0), (up, ff, 0), (dact, ff, 0)], [], [(2 * ff, BF16)])[0]
    dwi = mm_tn_colsharded(tag + "_dwi", xn, dgu, F32, into=into_wi)
    dxn = mm_nn(tag + "_dxn", dgu, wi_t, F32, tk=1408)
    d_in, dg_norm = norm_bwd(tag + "_dnorm", dxn, h, g_norm, d)
    return d_in, dg_norm.reshape(dm), dwi, dwo


def _col(v):
    return v.reshape(SSM_COLS, 1)


SCAN_RIDERS = ("w_in", "ssm_w_glu", "w_out", "ple_w_gate", "ple_w_proj")


def _layer_fwd(h, lw, p_l, seq, nxt=None):
    n, d = h.shape
    pre = {k: None for k in BIG} if nxt is None else nxt
    h1, ffn1_saved, got1 = _ffn_fwd("ffn1", h, lw["ffn1_norm"], lw["ffn1_wi"], lw["ffn1_wo"], pre["ffn1_wi"], pre["ffn1_wo"])

    xn2 = norm_fwd("mix_norm", h1, lw["mix_norm"])
    z = mm_nn("mix_in", xn2, lw["w_in"], F32)
    log_dt = jnp.repeat(lw["ssm_log_dt"], SSM_STATE)
    b_re, b_im = lw["ssm_b_re"].reshape(SSM_COLS, SSM_CH), lw["ssm_b_im"].reshape(SSM_COLS, SSM_CH)
    pw_re, pw_im, bb_re, bb_im = ssm_prep("ssm_prep", _col(lw["ssm_lambda_re"]), _col(lw["ssm_lambda_im"]), _col(log_dt), b_re, b_im)
    bbd = expand_b(bb_re, bb_im)
    cbd = expand_c(lw["ssm_c_re"], lw["ssm_c_im"])
    bu = ssm_in("ssm_in", z, bbd)
    s = ssm_scan("ssm_scan", scan_coefficients(pw_re, pw_im, False), bu, seq, False,
                 riders=() if nxt is None else tuple(nxt[k] for k in SCAN_RIDERS))
    got_scan = []
    if nxt is not None:
        s, got_scan = s[0], s[1:]
    y0c = ssm_out("ssm_out", s, cbd)

    def skip_gelu(yc, zs, dvec):
        y0 = yc + dvec * zs
        return y0, _gelu(y0)

    y0, y1 = _rowwise("ssm_gelu", skip_gelu, n, 512, [(y0c, SSM_WIDTH, 0), (z, SSM_WIDTH, 0)],
                      [lw["ssm_d"].reshape(1, SSM_WIDTH)], [(SSM_WIDTH, F32), (SSM_WIDTH, F32)])
    t = mm_nn("ssm_glu_mm", y1, lw["ssm_w_glu"], F32)
    y2 = _rowwise("ssm_glu", lambda a, b: a * jax.nn.sigmoid(b), n, 512, [(y1, SSM_WIDTH, 0), (t, SSM_WIDTH, 0)], [],
                  [(SSM_WIDTH, BF16)])[0]

    q = pool_window("pool_window", z, SSM_WIDTH // POOL_CH, seq, BF16, False)
    wp_eff = lw["pool_w"] * lw["pool_scale"].reshape(POOL_GROUPS, 1, POOL_CH)
    yp = pool_mm("pool_mm", q, wp_eff, BF16)
    m = jnp.concatenate([y2, yp], axis=1)
    h2 = mm_nn("mix_out", m, lw["w_out"], F32, res=h1)

    h3, ffn2_saved, got2 = _ffn_fwd("ffn2", h2, lw["ffn2_norm"], lw["ffn2_wi"], lw["ffn2_wo"], pre["ffn2_wi"], pre["ffn2_wo"])

    xn4 = norm_fwd("ple_norm", h3, lw["ple_norm"])
    tg = mm_nn("ple_gate", xn4, lw["ple_w_gate"], F32)
    e = mm_nn_colsharded("ple_proj", p_l, lw["ple_w_proj"], F32)
    h4 = _rowwise("ple_add", lambda a, b, c: a + jax.nn.sigmoid(b) * c, n, 512, [(h3, d, 0), (tg, d, 0), (e, d, 0)], [], [(d, F32)])[0]
    saved = dict(h=h, h1=h1, h2=h2, h3=h3, ffn1=ffn1_saved, ffn2=ffn2_saved, xn2=xn2, z=z, s=s, y0=y0, y1=y1, t=t, m=m, q=q,
                 xn4=xn4, tg=tg, e=e, pw_re=pw_re, pw_im=pw_im, bbd=bbd, cbd=cbd)
    fetched = None
    if nxt is not None:
        fetched = dict(zip(("ffn1_wi", "ffn1_wo"), got1))
        fetched.update(zip(("ffn2_wi", "ffn2_wo"), got2))
        fetched.update(zip(SCAN_RIDERS, got_scan))
    return h4, saved, fetched


def _layer_bwd(d, lw, p_l, sv, seq, bufs, layer):
    n, dm = d.shape
    g = {}
    def ple_bwd(dd, tg, e):
        gate = jax.nn.sigmoid(tg)
        return dd * e * gate * (1.0 - gate), dd * gate

    dtg, de = _rowwise("ple_dadd", ple_bwd, n, 512, [(d, dm, 0), (sv["tg"], dm, 0), (sv["e"], dm, 0)], [], [(dm, BF16), (dm, BF16)])
    g["ple_w_proj"] = mm_tn_colsharded("ple_dproj", p_l, de, F32, into=(bufs["ple_w_proj"], layer))
    g["ple_w_gate"] = mm_tn("ple_dgate_w", sv["xn4"], dtg, F32, into=(bufs["ple_w_gate"], layer))
    dxn4 = mm_nn("ple_dgate_x", dtg, lw["ple_w_gate_t"], F32)
    d, dg = norm_bwd("ple_dnorm", dxn4, sv["h3"], lw["ple_norm"], d)
    g["ple_norm"] = dg.reshape(dm)

    d, g["ffn2_norm"], g["ffn2_wi"], g["ffn2_wo"] = _ffn_bwd("ffn2b", d, sv["h2"], lw["ffn2_norm"], lw["ffn2_wi_t"], lw["ffn2_wo_t"], sv["ffn2"],
                                                            (bufs["ffn2_wi"], layer), (bufs["ffn2_wo"], layer))

    dmix = mm_nn("mix_dout_x", d, lw["w_out_t"], F32)
    g["w_out"] = mm_tn("mix_dout_w", sv["m"], d, F32, into=(bufs["w_out"], layer))
    pool_cb = SSM_WIDTH // POOL_CH
    wp_eff = lw["pool_w"] * lw["pool_scale"].reshape(POOL_GROUPS, 1, POOL_CH)
    dq = pool_mm_t("pool_dmm_x", dmix, pool_cb, wp_eff, F32)
    dwp_eff = pool_grad_w("pool_dmm_w", sv["q"], dmix, pool_cb)
    g["pool_w"] = dwp_eff * lw["pool_scale"].reshape(POOL_GROUPS, 1, POOL_CH)
    g["pool_scale"] = jnp.sum(dwp_eff * lw["pool_w"], axis=1).reshape(POOL_WIDTH)
    dzp = pool_window("pool_dwindow", dq, 0, seq, BF16, True)

    def glu_bwd(dy2, y1, t):
        sg = jax.nn.sigmoid(t)
        return dy2 * y1 * sg * (1.0 - sg), dy2 * sg

    dt_, dy1a = _rowwise("ssm_dglu", glu_bwd, n, 512, [(dmix, SSM_WIDTH, 0), (sv["y1"], SSM_WIDTH, 0), (sv["t"], SSM_WIDTH, 0)], [],
                         [(SSM_WIDTH, BF16), (SSM_WIDTH, F32)])
    g["ssm_w_glu"] = mm_tn("ssm_dglu_w", sv["y1"], dt_, F32, into=(bufs["ssm_w_glu"], layer))
    dy1b = mm_nn("ssm_dglu_x", dt_, lw["ssm_w_glu_t"], F32)

    def gelu_bwd(da, db, y0, zs, dvec):
        dy0 = (da + db) * _gelu_grad(y0)
        return dy0, dy0 * dvec, jnp.sum(dy0 * zs, axis=0, keepdims=True)

    dy0, dzs_a, dd = _rowwise("ssm_dgelu", gelu_bwd, n, 512,
                              [(dy1a, SSM_WIDTH, 0), (dy1b, SSM_WIDTH, 0), (sv["y0"], SSM_WIDTH, 0), (sv["z"], SSM_WIDTH, 0)],
                              [lw["ssm_d"].reshape(1, SSM_WIDTH)], [(SSM_WIDTH, F32), (SSM_WIDTH, F32)], [(1, SSM_WIDTH)])
    g["ssm_d"] = dd.reshape(SSM_WIDTH)
    g["ssm_c_re"], g["ssm_c_im"] = compact_c(ssm_grad_c("ssm_dc", sv["s"], dy0))
    v = ssm_out_t("ssm_dout", dy0, sv["cbd"])
    a, dlam = ssm_scan("ssm_scan_adj", scan_coefficients(sv["pw_re"], sv["pw_im"], True), v, seq, True, states=sv["s"])
    dbb_re, dbb_im = compact_b(ssm_grad_b("ssm_db", sv["z"], a))
    dzs = ssm_in_t("ssm_din", a, sv["bbd"], dzs_a)
    dlam = jnp.sum(dlam, axis=1)
    log_dt = jnp.repeat(lw["ssm_log_dt"], SSM_STATE)
    b_re, b_im = lw["ssm_b_re"].reshape(SSM_COLS, SSM_CH), lw["ssm_b_im"].reshape(SSM_COLS, SSM_CH)
    glr, gli, gld, gbr, gbi = ssm_prep_bwd("ssm_prep_bwd", _col(lw["ssm_lambda_re"]), _col(lw["ssm_lambda_im"]), _col(log_dt), b_re, b_im,
                                           _col(dlam[0]), _col(dlam[1]), dbb_re, dbb_im)
    g["ssm_lambda_re"] = glr.reshape(SSM_GROUPS, SSM_STATE)
    g["ssm_lambda_im"] = gli.reshape(SSM_GROUPS, SSM_STATE)
    g["ssm_log_dt"] = jnp.sum(gld.reshape(SSM_GROUPS, SSM_STATE), axis=1)
    g["ssm_b_re"] = gbr.reshape(SSM_GROUPS, SSM_STATE, SSM_CH)
    g["ssm_b_im"] = gbi.reshape(SSM_GROUPS, SSM_STATE, SSM_CH)

    dz = jnp.concatenate([dzs.astype(BF16), dzp], axis=1)
    g["w_in"] = mm_tn("mix_din_w", sv["xn2"], dz, F32, into=(bufs["w_in"], layer))
    dxn2 = mm_nn("mix_din_x", dz, lw["w_in_t"], F32)
    d, dg = norm_bwd("mix_dnorm", dxn2, sv["h1"], lw["mix_norm"], d)
    g["mix_norm"] = dg.reshape(dm)

    d, g["ffn1_norm"], g["ffn1_wi"], g["ffn1_wo"] = _ffn_bwd("ffn1b", d, sv["h"], lw["ffn1_norm"], lw["ffn1_wi_t"], lw["ffn1_wo_t"], sv["ffn1"],
                                                            (bufs["ffn1_wi"], layer), (bufs["ffn1_wo"], layer))
    return d, g


def _flatten_small(tensors):
    flat = jnp.concatenate([t.reshape(-1) for t in tensors])
    pad = (-flat.shape[0]) % (SUBLANE * LANE)
    return jnp.pad(flat, (0, pad)).reshape(-1, LANE)


def _unflatten_small(flat, like):
    flat = flat.reshape(-1)
    out, off = [], 0
    for t in like:
        out.append(flat[off:off + t.size].reshape(t.shape))
        off += t.size
    return out


def kernel(x, p, ffn1_norm, ffn1_wi, ffn1_wo, mix_norm, w_in, ssm_lambda_re, ssm_lambda_im, ssm_log_dt, ssm_b_re, ssm_b_im, ssm_c_re, ssm_c_im, ssm_d, ssm_w_glu, pool_w, pool_scale, w_out, ffn2_norm, ffn2_wi, ffn2_wo, ple_norm, ple_w_gate, ple_w_proj, final_norm, loss_target, m_ffn1_norm, m_ffn1_wi, m_ffn1_wo, m_mix_norm, m_w_in, m_ssm_lambda_re, m_ssm_lambda_im, m_ssm_log_dt, m_ssm_b_re, m_ssm_b_im, m_ssm_c_re, m_ssm_c_im, m_ssm_d, m_ssm_w_glu, m_pool_w, m_pool_scale, m_w_out, m_ffn2_norm, m_ffn2_wi, m_ffn2_wo, m_ple_norm, m_ple_w_gate, m_ple_w_proj, m_final_norm, v_ffn1_norm, v_ffn1_wi, v_ffn1_wo, v_mix_norm, v_w_in, v_ssm_lambda_re, v_ssm_lambda_im, v_ssm_log_dt, v_ssm_b_re, v_ssm_b_im, v_ssm_c_re, v_ssm_c_im, v_ssm_d, v_ssm_w_glu, v_pool_w, v_pool_scale, v_w_out, v_ffn2_norm, v_ffn2_wi, v_ffn2_wo, v_ple_norm, v_ple_w_gate, v_ple_w_proj, v_final_norm):
    given = dict(locals())
    w = {k: given[k] for k in WEIGHTS}
    mom = {k: given["m_" + k] for k in WEIGHTS}
    var = {k: given["v_" + k] for k in WEIGHTS}
    bsz, seq, dm = x.shape
    n = bsz * seq
    depth = ffn1_wi.shape[0]

    shards = {k: w[k].astype(MXU_DTYPE) for k in BIG}

    def layer_weights(l, gathered):
        lw = {k: w[k][l] for k in SMALL if k != "final_norm"}
        for k, a in gathered.items():
            if k in COL_SHARDED:
                lw[k] = a
                if k != "ple_w_proj":
                    lw[k + "_t"] = transpose_colsharded(a[None])[0]
            else:
                lw[k] = a.reshape(N_CHIPS * a.shape[1], a.shape[2])
                lw[k + "_t"] = lw[k].T
        return lw

    gathered = {k: gather_weight("gather_" + k, shards[k][:1])[0] for k in BIG}

    p2 = p.reshape(depth, n, p.shape[-1])
    h_last = x.reshape(n, dm)
    saved, layers = [], []
    for l in range(depth):
        layers.append(layer_weights(l, gathered))
        nxt = {k: (shards[k], l + 1) for k in BIG} if l + 1 < depth else None
        h_last, sv, gathered = _layer_fwd(h_last, layers[l], p2[l], seq, nxt)
        saved.append(sv)

    def head(hh, tgt, gf):
        r = lax.rsqrt(jnp.mean(hh * hh, axis=-1, keepdims=True) + NORM_EPS)
        xh = hh * r
        diff = xh * gf - tgt
        dy = diff * (1.0 / dm)
        dxh = dy * gf
        dx = r * (dxh - xh * jnp.mean(dxh * xh, axis=-1, keepdims=True))
        return dx, jnp.sum(diff * diff, axis=0, keepdims=True) * (0.5 / dm), jnp.sum(dy * xh, axis=0, keepdims=True)

    d_last, loss_cols, g_final = _rowwise("loss_head", head, n, 512, [(h_last, dm, 0), (loss_target.reshape(n, dm), dm, 0)],
                                          [final_norm.reshape(1, dm)], [(dm, F32)], [(1, dm), (1, dm)])
    loss = lax.psum(jnp.sum(loss_cols), ("x", "y", "c"))

    d_x = d_last
    layer_grads = [None] * depth
    bufs = {k: lax.empty((depth, N_CHIPS) + w[k].shape[1:], F32) for k in BIG}
    for l in reversed(range(depth)):
        d_x, layer_grads[l] = _layer_bwd(d_x, layers[l], p2[l], saved[l], seq, bufs, l)
        bufs = {k: layer_grads[l][k] for k in BIG}
    grads = {k: jnp.stack([g[k] for g in layer_grads]) for k in layer_grads[0] if k not in BIG}
    grads.update(bufs)
    grad_x = d_x.reshape(bsz, seq, dm)

    big_sum = [reduce_scatter_streamed("rs_" + k, grads[k]) for k in BIG]
    small_keys = [k for k in SMALL]
    small_parts = [grads[k] if k != "final_norm" else g_final.reshape(dm) for k in small_keys]
    small_sum = _unflatten_small(all_reduce_small(_flatten_small(small_parts)), small_parts)
    g_out = dict(zip(BIG, big_sum))
    g_out.update(dict(zip(small_keys, small_sum)))
    for k in BIG:
        g_out[k] = g_out[k].reshape(w[k].shape)

    delta, new_m, new_v = {}, {}, {}
    for k in BIG:
        delta[k], new_m[k], new_v[k] = adamw("adamw_" + k, w[k], g_out[k], mom[k], var[k])
    sw = adamw("adamw_small", *[_flatten_small([t[k] for k in small_keys]) for t in (w, g_out, mom, var)])
    for name, flat in zip((delta, new_m, new_v), sw):
        for k, t in zip(small_keys, _unflatten_small(flat, [w[k] for k in small_keys])):
            name[k] = t

    return (loss, grad_x, *[g_out[k] for k in WEIGHTS], *[delta[k] for k in WEIGHTS],
            *[new_m[k] for k in WEIGHTS], *[new_v[k] for k in WEIGHTS])
```

```python
import functools
import math

import jax
import jax.numpy as jnp
from jax import lax
from jax.experimental import pallas as pl
from jax.experimental.pallas import tpu as pltpu

F32 = jnp.float32
BF16 = jnp.bfloat16
MXU_DTYPE = jnp.bfloat16
GRAD_WIRE_DTYPE = jnp.bfloat16
STATE_DTYPE = jnp.bfloat16
VMEM_LIMIT = 56 * 1024 * 1024
LANE = 128
SUBLANE = 8

N_CHIPS = 4
SSM_GROUPS = 32
SSM_STATE = 64
SSM_CH = 16
SSM_WIDTH = SSM_GROUPS * SSM_CH
SSM_COLS = SSM_GROUPS * SSM_STATE
POOL_GROUPS = 4
POOL_CH = 128
POOL_WIDTH = POOL_GROUPS * POOL_CH
SSM_TILE = 256
NORM_EPS = 1e-6
ADAM_LR = 0.001
ADAM_B1 = 0.9
ADAM_B2 = 0.999
ADAM_EPS = 1e-08
ADAM_WD = 0.01
ADAM_STEP = 10
MESH_ID = pl.DeviceIdType.MESH

BIG = ("ffn1_wi", "ffn1_wo", "w_in", "ssm_w_glu", "w_out", "ffn2_wi", "ffn2_wo", "ple_w_gate", "ple_w_proj")
COL_SHARDED = ("ffn1_wi", "ffn2_wi", "ple_w_proj")
SMALL = ("ffn1_norm", "mix_norm", "ssm_lambda_re", "ssm_lambda_im", "ssm_log_dt", "ssm_b_re", "ssm_b_im",
         "ssm_c_re", "ssm_c_im", "ssm_d", "pool_w", "pool_scale", "ffn2_norm", "ple_norm", "final_norm")
WEIGHTS = ("ffn1_norm", "ffn1_wi", "ffn1_wo", "mix_norm", "w_in", "ssm_lambda_re", "ssm_lambda_im", "ssm_log_dt",
           "ssm_b_re", "ssm_b_im", "ssm_c_re", "ssm_c_im", "ssm_d", "ssm_w_glu", "pool_w", "pool_scale", "w_out",
           "ffn2_norm", "ffn2_wi", "ffn2_wo", "ple_norm", "ple_w_gate", "ple_w_proj", "final_norm")


def _tile(dim, target):
    best = None
    t = LANE
    while t <= min(dim, target):
        if dim % t == 0:
            best = t
        t += LANE
    return best if best is not None else dim


def _params(sem):
    return pltpu.CompilerParams(dimension_semantics=sem, vmem_limit_bytes=VMEM_LIMIT)


CHIPS = ((1, 0, 0), (0, 1, 0), (1, 1, 0))


def _ride_scratch(n):
    return [pltpu.SemaphoreType.DMA((3 * n,)), pltpu.SemaphoreType.DMA((3 * n,)), pltpu.SemaphoreType.DMA((n,))]


def _ride_copies(layers, ins, outs, send_sems, recv_sems, local_sems):
    x, y, c = lax.axis_index("x"), lax.axis_index("y"), lax.axis_index("c")
    me = 2 * x + y
    copies = []
    for i in range(len(ins)):
        src = ins[i].at[pl.ds(layers[i], 1)]
        dst = outs[i].at[pl.ds(me, 1)]
        copies.append(pltpu.make_async_copy(src, dst, local_sems.at[i]))
        for f, (fx, fy, _) in enumerate(CHIPS):
            peer = ((1 - x) if fx else x, (1 - y) if fy else y, c)
            copies.append(pltpu.make_async_remote_copy(
                src_ref=src, dst_ref=dst, send_sem=send_sems.at[3 * i + f], recv_sem=recv_sems.at[3 * i + f],
                device_id=peer, device_id_type=pl.DeviceIdType.MESH))
    return copies


def _ride(grid, layers, ins, outs, sems):
    if not ins:
        return
    ids = [pl.program_id(a) for a in range(len(grid))]
    first = functools.reduce(jnp.logical_and, [i == 0 for i in ids])
    last = functools.reduce(jnp.logical_and, [i == g - 1 for i, g in zip(ids, grid)])

    @pl.when(first)
    def _():
        for cp in _ride_copies(layers, ins, outs, *sems):
            cp.start()

    @pl.when(last)
    def _():
        for cp in _ride_copies(layers, ins, outs, *sems):
            cp.wait()


def _ride_out_shapes(riders):
    return [jax.ShapeDtypeStruct((N_CHIPS,) + r.shape[1:], r.dtype) for r, _ in riders]


def _mm(name, a, b, out_shape, out_dtype, grid, a_spec, b_spec, o_spec, contract, alpha=1.0, res=None, riders=(), into=None):
    n_k = grid[2]
    acc_shape = tuple(d for d in o_spec.block_shape if d is not None)
    n_in = 2 + (res is not None) + (into is not None)
    n_r = len(riders)
    ride_layers = [l for _, l in riders]

    def body(*refs):
        a_ref, b_ref = refs[0], refs[1]
        r_ref = refs[2] if res is not None else None
        o_ref = refs[n_in + n_r]
        if n_r:
            _ride(grid, ride_layers, refs[n_in:n_in + n_r], refs[n_in + n_r + 1:n_in + 2 * n_r + 1], refs[len(refs) - 3:])

        def product():
            return lax.dot_general(a_ref[...].astype(MXU_DTYPE), b_ref[...].astype(MXU_DTYPE),
                                   (contract, ((), ())), preferred_element_type=F32)

        def finish(v):
            if alpha != 1.0:
                v = v * alpha
            if r_ref is not None:
                v = v + r_ref[...].astype(F32)
            o_ref[...] = v.astype(out_dtype)

        if n_k == 1:
            finish(product())
            return
        acc = refs[n_in + 2 * n_r + 1]
        k = pl.program_id(2)

        @pl.when(k == 0)
        def _():
            acc[...] = product()

        @pl.when(k > 0)
        def _():
            acc[...] += product()

        @pl.when(k == n_k - 1)
        def _():
            finish(acc[...])

    in_specs = [a_spec, b_spec]
    operands = [a, b]
    if res is not None:
        in_specs.append(o_spec)
        operands.append(res)
    scratch = [pltpu.VMEM(acc_shape, F32)] if n_k > 1 else []
    any_spec = pl.BlockSpec(memory_space=pl.ANY)
    if into is not None:
        buf, layer = into
        assert buf.shape[1:] == tuple(out_shape) and buf.dtype == out_dtype and not n_r
        placed = pl.BlockSpec((None,) + tuple(o_spec.block_shape),
                              functools.partial(lambda i, j, kk, m, l: (l,) + tuple(m(i, j, kk)), m=o_spec.index_map, l=layer))
        return pl.pallas_call(
            body, name=name, grid=grid, in_specs=in_specs + [any_spec], out_specs=placed,
            out_shape=jax.ShapeDtypeStruct(buf.shape, out_dtype), scratch_shapes=scratch,
            input_output_aliases={len(operands): 0},
            compiler_params=_params(("parallel", "parallel", "arbitrary")),
        )(*operands, buf)
    if not n_r:
        return pl.pallas_call(
            body, name=name, grid=grid, in_specs=in_specs, out_specs=o_spec,
            out_shape=jax.ShapeDtypeStruct(out_shape, out_dtype), scratch_shapes=scratch,
            compiler_params=_params(("parallel", "parallel", "arbitrary")),
        )(*operands)
    return pl.pallas_call(
        body, name=name, grid=grid, in_specs=in_specs + [any_spec] * n_r, out_specs=[o_spec] + [any_spec] * n_r,
        out_shape=[jax.ShapeDtypeStruct(out_shape, out_dtype)] + _ride_out_shapes(riders),
        scratch_shapes=scratch + _ride_scratch(n_r),
        compiler_params=_params(("arbitrary", "arbitrary", "arbitrary")),
    )(*operands, *[r for r, _ in riders])


NN = ((1,), (0,))
NT = ((1,), (1,))
TN = ((0,), (0,))


def mm_nn(name, a, b, out_dtype, alpha=1.0, res=None, tm=1024, tn=1024, tk=1024, riders=()):
    m, k = a.shape
    n = b.shape[1]
    tm, tn, tk = _tile(m, tm), _tile(n, tn), _tile(k, tk)
    return _mm(name, a, b, (m, n), out_dtype, (m // tm, n // tn, k // tk),
               pl.BlockSpec((tm, tk), lambda i, j, kk: (i, kk)),
               pl.BlockSpec((tk, tn), lambda i, j, kk: (kk, j)),
               pl.BlockSpec((tm, tn), lambda i, j, kk: (i, j)), NN, alpha, res, riders)


def mm_nt(name, a, b, out_dtype, alpha=1.0, res=None, tm=1024, tn=512, tk=512):
    m, k = a.shape
    n = b.shape[0]
    tm, tn, tk = _tile(m, tm), _tile(n, tn), _tile(k, tk)
    return _mm(name, a, b, (m, n), out_dtype, (m // tm, n // tn, k // tk),
               pl.BlockSpec((tm, tk), lambda i, j, kk: (i, kk)),
               pl.BlockSpec((tn, tk), lambda i, j, kk: (j, kk)),
               pl.BlockSpec((tm, tn), lambda i, j, kk: (i, j)), NT, alpha, res)


def mm_tn(name, a, b, out_dtype, alpha=1.0, tm=1024, tn=1024, tk=1024, into=None):
    k, m = a.shape
    n = b.shape[1]
    tm, tn, tk = _tile(m, tm), _tile(n, tn), _tile(k, tk)
    if into is not None:
        shape = into[0].shape
        into = (into[0].reshape(shape[0], m, n), into[1])
    out = _mm(name, a, b, (m, n), out_dtype, (m // tm, n // tn, k // tk),
              pl.BlockSpec((tk, tm), lambda i, j, kk: (kk, i)),
              pl.BlockSpec((tk, tn), lambda i, j, kk: (kk, j)),
              pl.BlockSpec((tm, tn), lambda i, j, kk: (i, j)), TN, alpha, into=into)
    return out if into is None else out.reshape(shape)


def mm_nn_colsharded(name, a, w, out_dtype, tm=1024, tk=1024, riders=()):
    m, k = a.shape
    c = w.shape[2]
    tm, tk = _tile(m, tm), _tile(k, tk)
    return _mm(name, a, w, (m, N_CHIPS * c), out_dtype, (m // tm, N_CHIPS, k // tk),
               pl.BlockSpec((tm, tk), lambda i, j, kk: (i, kk)),
               pl.BlockSpec((None, tk, c), lambda i, j, kk: (j, kk, 0)),
               pl.BlockSpec((tm, c), lambda i, j, kk: (i, j)), NN, riders=riders)


def transpose_colsharded(w):
    nl, _, k, c = w.shape
    return jnp.swapaxes(w, 2, 3).reshape(nl, N_CHIPS * c, k)


def mm_tn_colsharded(name, a, b, out_dtype, tm=1024, tk=1024, into=None):
    t, k = a.shape
    c = b.shape[1] // N_CHIPS
    tm, tk = _tile(k, tm), _tile(t, tk)
    return _mm(name, a, b, (N_CHIPS, k, c), out_dtype, (k // tm, N_CHIPS, t // tk),
               pl.BlockSpec((tk, tm), lambda i, j, kk: (kk, i)),
               pl.BlockSpec((tk, c), lambda i, j, kk: (kk, j)),
               pl.BlockSpec((None, tm, c), lambda i, j, kk: (j, i, 0)), TN, into=into)


def _rowwise(name, fn, n_rows, tm, row_ins, bcast_ins, outs, accs=()):
    tm = min(tm, n_rows)
    grid = (n_rows // tm,)
    n_row, n_b, n_out = len(row_ins), len(bcast_ins), len(outs)

    def body(*refs):
        ins = [r[...] for r in refs[:n_row + n_b]]
        out_refs = refs[n_row + n_b:n_row + n_b + n_out]
        acc_refs = refs[n_row + n_b + n_out:]
        res = fn(*ins)
        if not isinstance(res, (tuple, list)):
            res = (res,)
        for o_ref, v in zip(out_refs, res[:n_out]):
            o_ref[...] = v.astype(o_ref.dtype)
        if acc_refs:
            @pl.when(pl.program_id(0) == 0)
            def _():
                for a_ref in acc_refs:
                    a_ref[...] = jnp.zeros_like(a_ref)
            for a_ref, v in zip(acc_refs, res[n_out:]):
                a_ref[...] += v

    in_specs, operands = [], []
    for spec in row_ins:
        arr, width, cb = spec[0], spec[1], spec[2]
        rb = spec[3] if len(spec) > 3 else 0
        in_specs.append(pl.BlockSpec((tm, width), functools.partial(lambda i, cb, rb: (i + rb, cb), cb=cb, rb=rb)))
        operands.append(arr)
    for arr in bcast_ins:
        in_specs.append(pl.BlockSpec(arr.shape, functools.partial(lambda i, nd: (0,) * nd, nd=arr.ndim)))
        operands.append(arr)
    out_specs = [pl.BlockSpec((tm, w), lambda i: (i, 0)) for w, _ in outs]
    out_specs += [pl.BlockSpec((r, w), lambda i: (0, 0)) for r, w in accs]
    out_shape = [jax.ShapeDtypeStruct((n_rows, w), dt) for w, dt in outs]
    out_shape += [jax.ShapeDtypeStruct((r, w), F32) for r, w in accs]
    res = pl.pallas_call(
        body, name=name, grid=grid, in_specs=in_specs, out_specs=out_specs, out_shape=out_shape,
        compiler_params=_params(("arbitrary",) if accs else ("parallel",)),
    )(*operands)
    return res


def _rms(x, g):
    r = lax.rsqrt(jnp.mean(x * x, axis=-1, keepdims=True) + NORM_EPS)
    return x * r * g


def _rms_bwd(dy, x, g):
    r = lax.rsqrt(jnp.mean(x * x, axis=-1, keepdims=True) + NORM_EPS)
    xh = x * r
    dxh = dy * g
    dx = r * (dxh - xh * jnp.mean(dxh * xh, axis=-1, keepdims=True))
    return dx, jnp.sum(dy * xh, axis=0, keepdims=True)


def norm_fwd(name, h, g):
    n, d = h.shape
    return _rowwise(name, lambda x, gg: _rms(x, gg), n, 512, [(h, d, 0)], [g.reshape(1, d)], [(d, BF16)])[0]


def norm_bwd(name, dxn, h, g, d_res):
    n, d = h.shape

    def fn(dy, x, dr, gg):
        dx, dg = _rms_bwd(dy, x, gg)
        return dr + dx, dg

    return _rowwise(name, fn, n, 512, [(dxn, d, 0), (h, d, 0), (d_res, d, 0)], [g.reshape(1, d)], [(d, F32)], [(1, d)])


_GELU_C = math.sqrt(2.0 / math.pi)


def _gelu(x):
    return 0.5 * x * (1.0 + jnp.tanh(_GELU_C * (x + 0.044715 * (x * x * x))))


def _gelu_grad(x):
    th = jnp.tanh(_GELU_C * (x + 0.044715 * (x * x * x)))
    return 0.5 * (1.0 + th) + 0.5 * x * (1.0 - th * th) * (_GELU_C * (1.0 + 3.0 * 0.044715 * (x * x)))


def _ssm_discretize(lam_re, lam_im, log_dt, b_re, b_im):
    dt = jnp.exp(log_dt)
    e = jnp.exp(lam_re * dt)
    lb_re = e * jnp.cos(lam_im * dt)
    lb_im = e * jnp.sin(lam_im * dt)
    nr, ni = lb_re - 1.0, lb_im
    den = lam_re * lam_re + lam_im * lam_im
    cr = (nr * lam_re + ni * lam_im) / den
    ci = (ni * lam_re - nr * lam_im) / den
    return lb_re, lb_im, cr * b_re - ci * b_im, cr * b_im + ci * b_re


def ssm_prep(name, lam_re, lam_im, log_dt, b_re, b_im):
    def body(lr, li, ld, br, bi, pr_ref, pi_ref, bbr_ref, bbi_ref):
        lb_re, lb_im, bb_re, bb_im = _ssm_discretize(lr[...], li[...], ld[...], br[...], bi[...])
        bbr_ref[...] = bb_re
        bbi_ref[...] = bb_im
        pr, pi = lb_re, lb_im
        cols_r, cols_i = [pr], [pi]
        for _ in range(SUBLANE - 1):
            pr, pi = pr * lb_re - pi * lb_im, pr * lb_im + pi * lb_re
            cols_r.append(pr)
            cols_i.append(pi)
        lane = lax.broadcasted_iota(jnp.int32, (SSM_COLS, SUBLANE), 1)
        out_r = jnp.zeros((SSM_COLS, SUBLANE), F32)
        out_i = jnp.zeros((SSM_COLS, SUBLANE), F32)
        for r in range(SUBLANE):
            out_r = jnp.where(lane == r, cols_r[r], out_r)
            out_i = jnp.where(lane == r, cols_i[r], out_i)
        pr_ref[...] = out_r
        pi_ref[...] = out_i

    shapes = [jax.ShapeDtypeStruct((SSM_COLS, SUBLANE), F32)] * 2 + [jax.ShapeDtypeStruct((SSM_COLS, SSM_CH), F32)] * 2
    return pl.pallas_call(body, name=name, out_shape=shapes,
                          compiler_params=pltpu.CompilerParams(vmem_limit_bytes=VMEM_LIMIT))(lam_re, lam_im, log_dt, b_re, b_im)


def ssm_prep_bwd(name, lam_re, lam_im, log_dt, b_re, b_im, d_lb_re, d_lb_im, d_bb_re, d_bb_im):
    def body(lr, li, ld, br, bi, g0, g1, g2, g3, o0, o1, o2, o3, o4):
        _, vjp = jax.vjp(_ssm_discretize, lr[...], li[...], ld[...], br[...], bi[...])
        res = vjp((g0[...], g1[...], g2[...], g3[...]))
        for o, v in zip((o0, o1, o2, o3, o4), res):
            o[...] = v

    col = jax.ShapeDtypeStruct((SSM_COLS, 1), F32)
    mat = jax.ShapeDtypeStruct((SSM_COLS, SSM_CH), F32)
    return pl.pallas_call(body, name=name, out_shape=[col, col, col, mat, mat],
                          compiler_params=pltpu.CompilerParams(vmem_limit_bytes=VMEM_LIMIT))(
        lam_re, lam_im, log_dt, b_re, b_im, d_lb_re, d_lb_im, d_bb_re, d_bb_im)


def scan_coefficients(pw_re, pw_im, reverse):
    pr, pi = pw_re.T, pw_im.T
    if reverse:
        pi = -pi
    row = jnp.arange(SUBLANE)[:, None]
    out = []
    for d in (1, 2, 4):
        valid = (row < SUBLANE - d) if reverse else (row >= d)
        out.append(jnp.where(valid, pr[d - 1][None, :], 0.0))
        out.append(jnp.where(valid, pi[d - 1][None, :], 0.0))
    out.append(pr[::-1] if reverse else pr)
    out.append(pi[::-1] if reverse else pi)
    return jnp.stack(out)


def ssm_scan(name, coef, x, seq, reverse, states=None, riders=()):
    n = x.shape[1]
    n_seq = n // seq
    cw = LANE
    n_cb = SSM_COLS // cw
    pair = 2 * SUBLANE
    n_pairs = seq // pair
    pairs_per_step = 2 if n_pairs % 2 == 0 else 1
    with_dlam = states is not None
    n_r = len(riders)
    assert not (n_r and with_dlam)

    def body(*refs):
        if with_dlam:
            coef_ref, x_ref, s_ref, o_ref, dl_ref = refs
        else:
            coef_ref, x_ref, o_ref = refs[0], refs[1], refs[2 + n_r]
            if n_r:
                _ride((n_cb, n_seq), [l for _, l in riders], refs[2:2 + n_r], refs[3 + n_r:3 + 2 * n_r], refs[3 + 2 * n_r:])
        c = [coef_ref[i] for i in range(8)]
        row16 = lax.broadcasted_iota(jnp.int32, (pair, cw), 0)
        zero = jnp.zeros((SUBLANE, cw), F32)

        edge = 0 if reverse else SUBLANE - 1

        def bcast_row(v, r, rows=SUBLANE):
            return jnp.broadcast_to(v[r:r + 1, :], (rows, cw))

        p8r, p8i = bcast_row(c[6], edge), bcast_row(c[7], edge)

        def local_scan(xr, xi):
            for si, d in enumerate((1, 2, 4)):
                sh = (SUBLANE - d) if reverse else d
                sr, sm = pltpu.roll(xr, sh, 0), pltpu.roll(xi, sh, 0)
                lre, lim = c[2 * si], c[2 * si + 1]
                xr, xi = xr + lre * sr - lim * sm, xi + lre * sm + lim * sr
            return xr, xi

        def step(it, carry):
            work = []
            for u in range(pairs_per_step):
                k = it * pairs_per_step + u
                pidx = (n_pairs - 1 - k) if reverse else k
                off = pl.multiple_of(pidx * pair, pair)
                xr16 = x_ref[0, pl.ds(off, pair), :].astype(F32)
                xi16 = x_ref[1, pl.ds(off, pair), :].astype(F32)
                halves = (1, 0) if reverse else (0, 1)
                tiles = {h: local_scan(xr16[h * SUBLANE:(h + 1) * SUBLANE], xi16[h * SUBLANE:(h + 1) * SUBLANE]) for h in halves}
                work.append((pidx, off, halves, tiles))
            cre, cim = carry[0], carry[1]
            acc = carry[2:]
            for pidx, off, halves, tiles in work:
                done = {}
                for h in halves:
                    lr, li = tiles[h]
                    done[h] = (lr + c[6] * cre - c[7] * cim, li + c[6] * cim + c[7] * cre)
                    cre, cim = (bcast_row(lr, edge) + p8r * cre - p8i * cim, bcast_row(li, edge) + p8r * cim + p8i * cre)
                or16 = jnp.concatenate([done[0][0], done[1][0]], axis=0)
                oi16 = jnp.concatenate([done[0][1], done[1][1]], axis=0)
                o_ref[0, pl.ds(off, pair), :] = or16.astype(o_ref.dtype)
                o_ref[1, pl.ds(off, pair), :] = oi16.astype(o_ref.dtype)
                if with_dlam:
                    poff = pl.multiple_of(jnp.maximum(pidx - 1, 0) * pair, pair)
                    first = pidx > 0
                    sr16 = s_ref[0, pl.ds(off, pair), :].astype(F32)
                    si16 = s_ref[1, pl.ds(off, pair), :].astype(F32)
                    pr_last = jnp.where(first, bcast_row(s_ref[0, pl.ds(poff, pair), :].astype(F32), pair - 1, pair), 0.0)
                    pi_last = jnp.where(first, bcast_row(s_ref[1, pl.ds(poff, pair), :].astype(F32), pair - 1, pair), 0.0)
                    spr = jnp.where(row16 == 0, pr_last, pltpu.roll(sr16, 1, 0))
                    spi = jnp.where(row16 == 0, pi_last, pltpu.roll(si16, 1, 0))
                    dre = or16 * spr + oi16 * spi
                    dim = oi16 * spr - or16 * spi
                    acc = (acc[0] + dre[:SUBLANE] + dre[SUBLANE:], acc[1] + dim[:SUBLANE] + dim[SUBLANE:])
            return (cre, cim) + tuple(acc)

        init = (zero, zero, zero, zero) if with_dlam else (zero, zero)
        fin = lax.fori_loop(0, n_pairs // pairs_per_step, step, init)
        if with_dlam:
            @pl.when(pl.program_id(1) == 0)
            def _():
                dl_ref[...] = jnp.zeros_like(dl_ref)
            dl_ref[0] += fin[2]
            dl_ref[1] += fin[3]

    blk = pl.BlockSpec((2, seq, cw), lambda j, b: (0, b, j))
    in_specs = [pl.BlockSpec((8, SUBLANE, cw), lambda j, b: (0, 0, j)), blk]
    operands = [coef, x]
    out_specs = [blk]
    out_shape = [jax.ShapeDtypeStruct(x.shape, STATE_DTYPE)]
    if with_dlam:
        in_specs.append(blk)
        operands.append(states)
        out_specs.append(pl.BlockSpec((2, SUBLANE, cw), lambda j, b: (0, 0, j)))
        out_shape.append(jax.ShapeDtypeStruct((2, SUBLANE, SSM_COLS), F32))
    scratch = []
    if n_r:
        any_spec = pl.BlockSpec(memory_space=pl.ANY)
        in_specs += [any_spec] * n_r
        operands += [r for r, _ in riders]
        out_specs += [any_spec] * n_r
        out_shape += _ride_out_shapes(riders)
        scratch = _ride_scratch(n_r)
    res = pl.pallas_call(
        body, name=name, grid=(n_cb, n_seq), in_specs=in_specs, out_specs=out_specs, out_shape=out_shape,
        scratch_shapes=scratch, compiler_params=_params(("arbitrary" if n_r else "parallel", "arbitrary")),
    )(*operands)
    return res if (with_dlam or n_r) else res[0]


def _state_col(j, kk):
    return 2 * j + kk + 2 * (kk // 2)


SSM_WIDE = 4 * SSM_TILE


def ssm_in(name, z, bbd, tm=1024, riders=()):
    n = z.shape[0]
    tm = _tile(n, tm)
    t, w = SSM_TILE, SSM_WIDE
    return _mm(name, z, bbd, (2, n, SSM_COLS), STATE_DTYPE, (n // tm, 2 * SSM_COLS // w, 1),
               pl.BlockSpec((tm, t), lambda i, j, kk: (i, j % 2)),
               pl.BlockSpec((t, w), lambda i, j, kk: (j % 2, j)),
               pl.BlockSpec((None, tm, w), lambda i, j, kk: (j // 2, i, j % 2)), NN, riders=riders)


def ssm_out(name, s, cbd, tm=1024, riders=()):
    n = s.shape[1]
    tm = _tile(n, tm)
    t = SSM_TILE
    return _mm(name, s, cbd, (n, SSM_WIDTH), F32, (n // tm, SSM_WIDTH // t, 4),
               pl.BlockSpec((None, tm, 512), lambda i, j, kk: (kk // 2, i, 2 * j + kk % 2)),
               pl.BlockSpec((512, t), lambda i, j, kk: (_state_col(j, kk), j)),
               pl.BlockSpec((tm, t), lambda i, j, kk: (i, j)), NN, riders=riders)


def ssm_out_t(name, dy, cbd, tm=1024):
    n = dy.shape[0]
    tm = _tile(n, tm)
    t, w = SSM_TILE, SSM_WIDE
    return _mm(name, dy, cbd, (2, n, SSM_COLS), STATE_DTYPE, (n // tm, 2 * SSM_COLS // w, 1),
               pl.BlockSpec((tm, t), lambda i, j, kk: (i, j % 2)),
               pl.BlockSpec((w, t), lambda i, j, kk: (j, j % 2)),
               pl.BlockSpec((None, tm, w), lambda i, j, kk: (j // 2, i, j % 2)), NT)


def ssm_in_t(name, a, bbd, res, tm=1024):
    n = a.shape[1]
    tm = _tile(n, tm)
    t = SSM_TILE
    return _mm(name, a, bbd, (n, SSM_WIDTH), F32, (n // tm, SSM_WIDTH // t, 4),
               pl.BlockSpec((None, tm, 512), lambda i, j, kk: (kk // 2, i, 2 * j + kk % 2)),
               pl.BlockSpec((t, 512), lambda i, j, kk: (j, _state_col(j, kk))),
               pl.BlockSpec((tm, t), lambda i, j, kk: (i, j)), NT, 1.0, res)


def ssm_grad_c(name, s, dy, tk=1024):
    n = s.shape[1]
    tk = _tile(n, tk)
    t, w = SSM_TILE, SSM_WIDE
    return _mm(name, s, dy, (2 * SSM_COLS, t), F32, (2 * SSM_COLS // w, 1, n // tk),
               pl.BlockSpec((None, tk, w), lambda i, j, kk: (i // 2, kk, i % 2)),
               pl.BlockSpec((tk, t), lambda i, j, kk: (kk, i % 2)),
               pl.BlockSpec((w, t), lambda i, j, kk: (i, 0)), TN)


def ssm_grad_b(name, z, a, tk=1024):
    n = z.shape[0]
    tk = _tile(n, tk)
    t, w = SSM_TILE, SSM_WIDE
    return _mm(name, z, a, (t, 2 * SSM_COLS), F32, (1, 2 * SSM_COLS // w, n // tk),
               pl.BlockSpec((tk, t), lambda i, j, kk: (kk, j % 2)),
               pl.BlockSpec((None, tk, w), lambda i, j, kk: (j // 2, kk, j % 2)),
               pl.BlockSpec((t, w), lambda i, j, kk: (0, j)), TN)


_GROUP_TILE = SSM_TILE // SSM_CH


def expand_b(bb_re, bb_im):
    b = jnp.stack([bb_re, bb_im]).reshape(2, SSM_GROUPS, SSM_STATE, SSM_CH)
    eye = jnp.eye(SSM_GROUPS, dtype=F32)
    return jnp.einsum("rgph,gk->ghrkp", b, eye).reshape(SSM_WIDTH, 2 * SSM_COLS).astype(MXU_DTYPE)


def expand_c(c_re, c_im):
    c = jnp.stack([c_re, -c_im])
    eye = jnp.eye(SSM_GROUPS, dtype=F32)
    return jnp.einsum("rghp,gk->rgpkh", c, eye).reshape(2 * SSM_COLS, SSM_WIDTH).astype(MXU_DTYPE)


def _group_pick():
    return (jnp.arange(SSM_GROUPS)[:, None] % _GROUP_TILE == jnp.arange(_GROUP_TILE)[None, :]).astype(F32)


def compact_c(dc):
    x = dc.reshape(2, SSM_GROUPS, SSM_STATE, _GROUP_TILE, SSM_CH)
    g = jnp.einsum("rgpch,gc->rghp", x, _group_pick())
    return g[0], -g[1]


def compact_b(db):
    x = db.reshape(_GROUP_TILE, SSM_CH, 2, SSM_GROUPS, SSM_STATE)
    g = jnp.einsum("chrgp,gc->rgph", x, _group_pick()).reshape(2, SSM_COLS, SSM_CH)
    return g[0], g[1]


def pool_window(name, x, col_block0, seq, out_dtype, adjoint):
    n = x.shape[0]

    def body(x_ref, o_ref):
        win = 2 << pl.program_id(1)
        row = lax.broadcasted_iota(jnp.int32, (seq, POOL_CH), 0)
        v = x_ref[...].astype(F32)
        cnt = jnp.minimum(row + 1, win).astype(F32)
        s = v / cnt if adjoint else v
        for d in (1, 2, 4, 8):
            if adjoint:
                sh = jnp.where((row < seq - d) & (d < win), pltpu.roll(s, seq - d, 0), 0.0)
            else:
                sh = jnp.where((row >= d) & (d < win), pltpu.roll(s, d, 0), 0.0)
            s = s + sh
        o_ref[...] = ((s - v) if adjoint else (s / cnt - v)).astype(out_dtype)

    return pl.pallas_call(
        body, name=name, grid=(n // seq, POOL_GROUPS),
        in_specs=[pl.BlockSpec((seq, POOL_CH), lambda b, g: (b, col_block0 + g))],
        out_specs=pl.BlockSpec((seq, POOL_CH), lambda b, g: (b, g)),
        out_shape=jax.ShapeDtypeStruct((n, POOL_WIDTH), out_dtype),
        compiler_params=_params(("parallel", "parallel")),
    )(x)


def pool_mm(name, q, w, out_dtype, tm=1024):
    n = q.shape[0]
    tm = _tile(n, tm)
    return _mm(name, q, w, (n, POOL_WIDTH), out_dtype, (n // tm, POOL_GROUPS, 1),
               pl.BlockSpec((tm, POOL_CH), lambda i, j, kk: (i, j)),
               pl.BlockSpec((None, POOL_CH, POOL_CH), lambda i, j, kk: (j, 0, 0)),
               pl.BlockSpec((tm, POOL_CH), lambda i, j, kk: (i, j)), NN)


def pool_mm_t(name, dy, col_block0, w, out_dtype, tm=1024):
    n = dy.shape[0]
    tm = _tile(n, tm)
    return _mm(name, dy, w, (n, POOL_WIDTH), out_dtype, (n // tm, POOL_GROUPS, 1),
               pl.BlockSpec((tm, POOL_CH), lambda i, j, kk: (i, col_block0 + j)),
               pl.BlockSpec((None, POOL_CH, POOL_CH), lambda i, j, kk: (j, 0, 0)),
               pl.BlockSpec((tm, POOL_CH), lambda i, j, kk: (i, j)), NT)


def pool_grad_w(name, q, dy, col_block0, tk=1024):
    n = q.shape[0]
    tk = _tile(n, tk)
    return _mm(name, q, dy, (POOL_GROUPS, POOL_CH, POOL_CH), F32, (POOL_GROUPS, 1, n // tk),
               pl.BlockSpec((tk, POOL_CH), lambda i, j, kk: (kk, i)),
               pl.BlockSpec((tk, POOL_CH), lambda i, j, kk: (kk, col_block0 + i)),
               pl.BlockSpec((None, POOL_CH, POOL_CH), lambda i, j, kk: (i, 0, 0)), TN)


def _any_specs(n):
    return [pl.BlockSpec(memory_space=pl.ANY)] * n


def _place():
    x, y, c = lax.axis_index("x"), lax.axis_index("y"), lax.axis_index("c")
    return x, y, c


def _at_axis(ref, axis, start, size):
    return ref.at[(slice(None),) * axis + (pl.ds(start, size),)]


def sibling_swap_halves(name, arrays, axis):
    n = len(arrays)
    halves = [a.shape[axis] // 2 for a in arrays]

    def body(*refs):
        ins, own, got = refs[:n], refs[n:2 * n], refs[2 * n:3 * n]
        send_sems, recv_sems, local_sems = refs[3 * n:]
        x, y, c = _place()
        copies = []
        for i in range(n):
            h = halves[i]
            mine = pltpu.make_async_copy(_at_axis(ins[i], axis, c * h, h), own[i], local_sems.at[i])
            mine.start()
            away = pltpu.make_async_remote_copy(
                src_ref=_at_axis(ins[i], axis, (1 - c) * h, h), dst_ref=got[i],
                send_sem=send_sems.at[i], recv_sem=recv_sems.at[i], device_id=(x, y, 1 - c), device_id_type=MESH_ID)
            away.start()
            copies += [mine, away]
        for cp in copies:
            cp.wait()

    def half_shape(a, h):
        return jax.ShapeDtypeStruct(a.shape[:axis] + (h,) + a.shape[axis + 1:], a.dtype)

    shapes = [half_shape(a, h) for a, h in zip(arrays, halves)]
    res = pl.pallas_call(
        body, name=name, in_specs=_any_specs(n), out_specs=_any_specs(2 * n), out_shape=shapes + shapes,
        scratch_shapes=[pltpu.SemaphoreType.DMA((n,)), pltpu.SemaphoreType.DMA((n,)), pltpu.SemaphoreType.DMA((n,))],
    )(*arrays)
    return res[:n], res[n:]


def sibling_join_halves(name, arrays, axis):
    n = len(arrays)

    def body(*refs):
        ins, outs = refs[:n], refs[n:2 * n]
        send_sems, recv_sems, local_sems = refs[2 * n:]
        x, y, c = _place()
        copies = []
        for i in range(n):
            h = ins[i].shape[axis]
            dst = _at_axis(outs[i], axis, c * h, h)
            mine = pltpu.make_async_copy(ins[i], dst, local_sems.at[i])
            mine.start()
            away = pltpu.make_async_remote_copy(
                src_ref=ins[i], dst_ref=dst, send_sem=send_sems.at[i], recv_sem=recv_sems.at[i],
                device_id=(x, y, 1 - c), device_id_type=MESH_ID)
            away.start()
            copies += [mine, away]
        for cp in copies:
            cp.wait()

    shapes = [jax.ShapeDtypeStruct(a.shape[:axis] + (2 * a.shape[axis],) + a.shape[axis + 1:], a.dtype) for a in arrays]
    return pl.pallas_call(
        body, name=name, in_specs=_any_specs(n), out_specs=_any_specs(n), out_shape=shapes,
        scratch_shapes=[pltpu.SemaphoreType.DMA((n,)), pltpu.SemaphoreType.DMA((n,)), pltpu.SemaphoreType.DMA((n,))],
    )(*arrays)


def sibling_swap(name, arrays):
    n = len(arrays)

    def body(*refs):
        ins, outs = refs[:n], refs[n:2 * n]
        send_sems, recv_sems = refs[2 * n:]
        x, y, c = _place()
        copies = []
        for i in range(n):
            away = pltpu.make_async_remote_copy(
                src_ref=ins[i], dst_ref=outs[i], send_sem=send_sems.at[i], recv_sem=recv_sems.at[i],
                device_id=(x, y, 1 - c), device_id_type=MESH_ID)
            away.start()
            copies.append(away)
        for cp in copies:
            cp.wait()

    return pl.pallas_call(
        body, name=name, in_specs=_any_specs(n), out_specs=_any_specs(n),
        out_shape=[jax.ShapeDtypeStruct(a.shape, a.dtype) for a in arrays],
        scratch_shapes=[pltpu.SemaphoreType.DMA((n,)), pltpu.SemaphoreType.DMA((n,))],
    )(*arrays)


_FLIPS = ((1, 0), (0, 1), (1, 1))


def chip_exchange(name, arrays, axis, all_to_all):
    n = len(arrays)

    def body(*refs):
        ins, outs = refs[:n], refs[n:2 * n]
        send_sems, recv_sems, local_sems = refs[2 * n:]
        x, y, c = _place()
        me = 2 * x + y
        copies = []
        for i in range(n):
            dst = _at_axis(outs[i], axis, me, 1)
            mine = pltpu.make_async_copy(_at_axis(ins[i], axis, me, 1) if all_to_all else ins[i], dst, local_sems.at[i])
            mine.start()
            copies.append(mine)
            for f, (fx, fy) in enumerate(_FLIPS):
                px = (1 - x) if fx else x
                py = (1 - y) if fy else y
                src = _at_axis(ins[i], axis, 2 * px + py, 1) if all_to_all else ins[i]
                away = pltpu.make_async_remote_copy(
                    src_ref=src, dst_ref=dst, send_sem=send_sems.at[3 * i + f], recv_sem=recv_sems.at[3 * i + f],
                    device_id=(px, py, c), device_id_type=MESH_ID)
                away.start()
                copies.append(away)
        for cp in copies:
            cp.wait()

    shapes = [jax.ShapeDtypeStruct(a.shape[:axis] + (N_CHIPS,) + a.shape[axis + 1:], a.dtype) for a in arrays]
    return pl.pallas_call(
        body, name=name, in_specs=_any_specs(n), out_specs=_any_specs(n), out_shape=shapes,
        scratch_shapes=[pltpu.SemaphoreType.DMA((3 * n,)), pltpu.SemaphoreType.DMA((3 * n,)), pltpu.SemaphoreType.DMA((n,))],
    )(*arrays)


SIBLING = ((0, 0, 1),)
ICI_CHUNK_BYTES = 2 * 1024 * 1024
D2D_CHUNK_BYTES = 4 * 1024 * 1024


def _peer(flip):
    x, y, c = _place()
    return tuple((1 - v) if f else v for v, f in zip((x, y, c), flip))


def _core():
    return lax.axis_index("c")


def _chip():
    return 2 * lax.axis_index("x") + lax.axis_index("y")


def _linear_step(grid):
    i = pl.program_id(0)
    for a in range(1, len(grid)):
        i = i * grid[a] + pl.program_id(a)
    return i


def stream_reduce(name, x, grid, block, own_map, send_maps, flips, out_shape, out_block, out_map, wire_dtype=None):
    n_steps = math.prod(grid)
    n_p = len(flips)
    vm_block = block = tuple(1 if d is None else d for d in block)
    out_block = tuple(1 if d is None else d for d in out_block)
    staged = wire_dtype is not None and wire_dtype != x.dtype
    slot_dtype = wire_dtype if staged else x.dtype

    def body(own_ref, *rest):
        send_refs = rest[:n_p]
        o_ref, recv, send_sems, recv_sems, credits = rest[n_p:n_p + 5]
        stage = rest[n_p + 5] if staged else None
        i = _linear_step(grid)
        s = i % 2
        copies = []
        for j, flip in enumerate(flips):
            src = send_refs[j]
            if staged:
                stage[j, s] = send_refs[j][...].astype(wire_dtype)
                src = stage.at[j, s]

            @pl.when(i >= 2)
            def _():
                pl.semaphore_wait(credits.at[j, s], 1)
            cp = pltpu.make_async_remote_copy(
                src_ref=src, dst_ref=recv.at[j, s], send_sem=send_sems.at[j, s], recv_sem=recv_sems.at[j, s],
                device_id=_peer(flip), device_id_type=MESH_ID)
            cp.start()
            copies.append(cp)
        acc = own_ref[...]
        for j, cp in enumerate(copies):
            cp.wait_recv()
            acc = acc + recv[j, s].astype(acc.dtype)
        o_ref[...] = acc.reshape(o_ref.shape)
        for cp in copies:
            cp.wait_send()
        for j, flip in enumerate(flips):
            @pl.when(i < n_steps - 2)
            def _():
                pl.semaphore_signal(credits.at[j, s], inc=1, device_id=_peer(flip), device_id_type=MESH_ID)

    in_specs = [pl.BlockSpec(block, own_map)] + [pl.BlockSpec(block, m) for m in send_maps]
    scratch = [pltpu.VMEM((n_p, 2) + vm_block, slot_dtype), pltpu.SemaphoreType.DMA((n_p, 2)),
               pltpu.SemaphoreType.DMA((n_p, 2)), pltpu.SemaphoreType.REGULAR((n_p, 2))]
    if staged:
        scratch.append(pltpu.VMEM((n_p, 2) + vm_block, slot_dtype))
    return pl.pallas_call(
        body, name=name, grid=grid, in_specs=in_specs, out_specs=pl.BlockSpec(out_block, out_map),
        out_shape=jax.ShapeDtypeStruct(out_shape, x.dtype), scratch_shapes=scratch,
        compiler_params=_params(("arbitrary",) * len(grid)),
    )(*([x] * (1 + n_p)))


def stream_gather(name, x, grid, block, in_map, flips, out_shape, out_block, out_map):
    n_p = len(flips)
    assert grid[-1] == n_p + 1
    n_steps = math.prod(grid[:-1])
    vm_block = block = tuple(1 if d is None else d for d in block)
    out_block = tuple(1 if d is None else d for d in out_block)

    def body(x_ref, o_ref, recv, send_sems, recv_sems, credits):
        i = _linear_step(grid[:-1])
        q = pl.program_id(len(grid) - 1)
        s = i % 2

        def copy(j):
            return pltpu.make_async_remote_copy(
                src_ref=x_ref, dst_ref=recv.at[j, s], send_sem=send_sems.at[j, s], recv_sem=recv_sems.at[j, s],
                device_id=_peer(flips[j]), device_id_type=MESH_ID)

        @pl.when(q == 0)
        def _():
            for j in range(n_p):
                @pl.when(i >= 2)
                def _():
                    pl.semaphore_wait(credits.at[j, s], 1)
                copy(j).start()
            o_ref[...] = x_ref[...].reshape(o_ref.shape)
            for j in range(n_p):
                copy(j).wait_send()

        for j in range(n_p):
            @pl.when(q == j + 1)
            def _():
                copy(j).wait_recv()
                o_ref[...] = recv[j, s].reshape(o_ref.shape)

                @pl.when(i < n_steps - 2)
                def _():
                    pl.semaphore_signal(credits.at[j, s], inc=1, device_id=_peer(flips[j]), device_id_type=MESH_ID)

    return pl.pallas_call(
        body, name=name, grid=grid, in_specs=[pl.BlockSpec(block, in_map)], out_specs=pl.BlockSpec(out_block, out_map),
        out_shape=jax.ShapeDtypeStruct(out_shape, x.dtype),
        scratch_shapes=[pltpu.VMEM((n_p, 2) + vm_block, x.dtype), pltpu.SemaphoreType.DMA((n_p, 2)),
                        pltpu.SemaphoreType.DMA((n_p, 2)), pltpu.SemaphoreType.REGULAR((n_p, 2))],
        compiler_params=_params(("arbitrary",) * len(grid)),
    )(x)


def _chip_of_substep(q):
    mask = jnp.where(q == 1, 2, jnp.where(q == 2, 1, jnp.where(q == 3, 3, 0)))
    return jnp.bitwise_xor(_chip(), mask)


def gather_weight(name, shard):
    nl, r, c = shard.shape
    r2 = r // 2
    f32_per_elem = 4 // shard.dtype.itemsize
    tr = _rows_tile(r2, c, budget=ICI_CHUNK_BYTES * f32_per_elem, step=16)
    nb = r2 // tr
    half = stream_gather(
        name + "_chips", shard, (nl, nb, N_CHIPS), (None, tr, c), lambda l, i, q: (l, _core() * nb + i, 0), CHIPS,
        (nl, N_CHIPS, r2, c), (None, None, tr, c), lambda l, i, q: (l, _chip_of_substep(q), i, 0))
    trd = _rows_tile(r2, c, budget=D2D_CHUNK_BYTES * f32_per_elem, step=16)
    nbd = r2 // trd
    both = stream_gather(
        name + "_cores", half, (nl, N_CHIPS, nbd, 2), (None, None, trd, c), lambda l, k, i, q: (l, k, i, 0), SIBLING,
        (nl, N_CHIPS, 2, r2, c), (None, None, None, trd, c), lambda l, k, i, q: (l, k, _core() + q - 2 * _core() * q, i, 0))
    return both.reshape(nl, N_CHIPS, r, c)


def reduce_scatter_streamed(name, g):
    nl, _, r, c = g.shape
    r2 = r // 2
    tr_d2d = _rows_tile(r2, c, budget=D2D_CHUNK_BYTES, step=16)
    nbd = r2 // tr_d2d
    chip_sum = stream_reduce(
        name + "_cores", g, (nl, N_CHIPS, nbd), (None, None, tr_d2d, c),
        lambda l, k, i: (l, k, _core() * nbd + i, 0), [lambda l, k, i: (l, k, (1 - _core()) * nbd + i, 0)], SIBLING,
        (nl, N_CHIPS, r2, c), (None, None, tr_d2d, c), lambda l, k, i: (l, k, i, 0), wire_dtype=GRAD_WIRE_DTYPE)
    tr = _rows_tile(r2, c, budget=ICI_CHUNK_BYTES, step=16)
    nb = r2 // tr
    blk4 = (None, None, tr, c)
    masks = (2, 1, 3)
    mine = stream_reduce(
        name + "_chips", chip_sum, (nl, nb), blk4,
        lambda l, i: (l, _chip(), i, 0),
        [functools.partial(lambda l, i, m: (l, jnp.bitwise_xor(_chip(), m), i, 0), m=m) for m in masks], CHIPS,
        (nl, r2, c), (None, tr, c), lambda l, i: (l, i, 0), wire_dtype=GRAD_WIRE_DTYPE)
    both = stream_gather(
        name + "_join", mine, (nl, nbd, 2), (None, tr_d2d, c), lambda l, i, q: (l, i, 0), SIBLING,
        (nl, 2, r2, c), (None, None, tr_d2d, c), lambda l, i, q: (l, _core() + q - 2 * _core() * q, i, 0))
    return both.reshape(nl, r, c)


def add2(name, a, b):
    shape = a.shape
    a2, b2 = a.reshape(-1, shape[-1]), b.reshape(-1, shape[-1])
    rows, w = a2.shape
    tm = _rows_tile(rows, w)
    return _rowwise(name, lambda u, v: u + v, rows, tm, [(a2, w, 0), (b2, w, 0)], [], [(w, F32)])[0].reshape(shape)


def _rows_tile(rows, width, budget=2 * 1024 * 1024, step=SUBLANE):
    best = step
    t = step
    while t <= rows:
        if rows % t == 0 and t * width * 4 <= budget:
            best = t
        t += step
    return best


def sum_slots(name, a):
    nl, _, r, c = a.shape
    tr = _rows_tile(r, c)

    def body(s0, s1, s2, s3, o_ref):
        o_ref[...] = ((s0[...] + s1[...]) + s2[...]) + s3[...]

    specs = [pl.BlockSpec((None, None, tr, c), functools.partial(lambda l, i, k: (l, k, i, 0), k=k)) for k in range(N_CHIPS)]
    return pl.pallas_call(
        body, name=name, grid=(nl, r // tr), in_specs=specs,
        out_specs=pl.BlockSpec((None, tr, c), lambda l, i: (l, i, 0)),
        out_shape=jax.ShapeDtypeStruct((nl, r, c), F32),
        compiler_params=_params(("parallel", "parallel")),
    )(a, a, a, a)


def reduce_scatter_big(grads):
    own, got = sibling_swap_halves("rs_swap_halves", grads, 2)
    chip_sum = [add2("rs_add_cores", a, b) for a, b in zip(own, got)]
    spread = chip_exchange("rs_chips", chip_sum, 1, True)
    mine = [sum_slots("rs_sum_chips", a) for a in spread]
    return sibling_join_halves("rs_join_halves", mine, 1)


def all_reduce_small(flat):
    other = sibling_swap("ar_swap", [flat])[0]
    chip = add2("ar_add_cores", flat, other)
    slots = chip_exchange("ar_chips", [chip.reshape((1,) + chip.shape)], 0, False)[0]
    rows = flat.shape[0]
    tm = _rows_tile(rows, LANE)
    nb = rows // tm
    s2 = slots.reshape(N_CHIPS * rows, LANE)
    return _rowwise("ar_sum_chips", lambda a, b, c, d: ((a + b) + c) + d, rows, tm,
                    [(s2, LANE, 0, k * nb) for k in range(N_CHIPS)], [], [(LANE, F32)])[0]


def _adamw_math(w, g, m, v):
    m = ADAM_B1 * m + (1.0 - ADAM_B1) * g
    v = ADAM_B2 * v + (1.0 - ADAM_B2) * (g * g)
    m_hat = m / (1.0 - ADAM_B1 ** ADAM_STEP)
    v_hat = v / (1.0 - ADAM_B2 ** ADAM_STEP)
    delta = -ADAM_LR * (m_hat / (jnp.sqrt(v_hat) + ADAM_EPS) + ADAM_WD * w)
    return delta, m, v


def adamw(name, w, g, m, v):
    shape = w.shape
    width = shape[-1]
    flat = [t.reshape(-1, width) for t in (w, g, m, v)]
    rows = flat[0].shape[0]
    tm = _rows_tile(rows, width, budget=1024 * 1024)
    res = _rowwise(name, _adamw_math, rows, tm, [(t, width, 0) for t in flat], [], [(width, F32)] * 3)
    return [r.reshape(shape) for r in res]


def ffn_in(name, xn, wi, riders=(), tm=512):
    n, d = xn.shape
    c = wi.shape[2]
    tm = _tile(n, tm)
    grid = (n // tm, 2)
    n_r = len(riders)

    def body(*refs):
        x_ref, wg_ref, wu_ref = refs[:3]
        g_ref, u_ref, a_ref = refs[3 + n_r:6 + n_r]
        if n_r:
            _ride(grid, [l for _, l in riders], refs[3:3 + n_r], refs[6 + n_r:6 + 2 * n_r], refs[6 + 2 * n_r:])
        x = x_ref[...].astype(MXU_DTYPE)
        g = jnp.dot(x, wg_ref[...].astype(MXU_DTYPE), preferred_element_type=F32)
        u = jnp.dot(x, wu_ref[...].astype(MXU_DTYPE), preferred_element_type=F32)
        g_ref[...] = g.astype(g_ref.dtype)
        u_ref[...] = u.astype(u_ref.dtype)
        a_ref[...] = (g * jax.nn.sigmoid(g) * u).astype(a_ref.dtype)

    any_spec = pl.BlockSpec(memory_space=pl.ANY)
    out_blk = pl.BlockSpec((tm, c), lambda i, j: (i, j))
    res = pl.pallas_call(
        body, name=name, grid=grid,
        in_specs=[pl.BlockSpec((tm, d), lambda i, j: (i, 0)), pl.BlockSpec((None, d, c), lambda i, j: (j, 0, 0)),
                  pl.BlockSpec((None, d, c), lambda i, j: (j + 2, 0, 0))] + [any_spec] * n_r,
        out_specs=[out_blk] * 3 + [any_spec] * n_r,
        out_shape=[jax.ShapeDtypeStruct((n, 2 * c), BF16)] * 3 + _ride_out_shapes(riders),
        scratch_shapes=_ride_scratch(n_r) if n_r else [],
        compiler_params=_params(("arbitrary", "arbitrary") if n_r else ("parallel", "parallel")),
    )(xn, wi, wi, *[r for r, _ in riders])
    return res[:3], list(res[3:])


def _rows(a):
    return a.reshape(N_CHIPS * a.shape[1], a.shape[2])


def _ffn_fwd(tag, h, g_norm, wi, wo, ride_in=(), ride_out=()):
    xn = norm_fwd(tag + "_norm", h, g_norm)
    (g, u, act), got_in = ffn_in(tag + "_wi", xn, wi, riders=ride_in)
    if wo is None:
        wo = _rows(got_in[0])
    out = mm_nn(tag + "_wo", act, wo, F32, alpha=0.5, res=h, tm=512, tk=2816, riders=ride_out)
    got_out = []
    if ride_out:
        out, got_out = out[0], list(out[1:])
    return out, (xn, g, u, act), got_in, got_out


def _ffn_bwd(tag, d, h, g_norm, wi_t, wo_t, saved, into_wi, into_wo):
    xn, gate, up, act = saved
    n, dm = h.shape
    ff = wo_t.shape[1]
    dact = mm_nn(tag + "_dact", d, wo_t, BF16, alpha=0.5, tn=1408)
    dwo = mm_tn(tag + "_dwo", act, d, F32, alpha=0.5, tm=1408, into=into_wo)

    def swiglu_bwd(g, u, da):
        g, u, da = g.astype(F32), u.astype(F32), da.astype(F32)
        sg = jax.nn.sigmoid(g)
        dg = da * u * (sg * (1.0 + g * (1.0 - sg)))
        du = da * (g * sg)
        return jnp.concatenate([dg, du], axis=1)

    dgu = _rowwise(tag + "_dswiglu", swiglu_bwd, n, 512, [(gate, ff, 0), (up, ff, 0), (dact, ff, 0)], [], [(2 * ff, BF16)])[0]
    dwi = mm_tn_colsharded(tag + "_dwi", xn, dgu, F32, into=into_wi)
    dxn = mm_nn(tag + "_dxn", dgu, wi_t, F32, tk=1408)
    d_in, dg_norm = norm_bwd(tag + "_dnorm", dxn, h, g_norm, d)
    return d_in, dg_norm.reshape(dm), dwi, dwo


def _col(v):
    return v.reshape(SSM_COLS, 1)


def _layer_fwd(h, lw, ffn1_wi, ride, next_wi, p_l, seq):
    n, d = h.shape
    got = {"ffn1_wi": ffn1_wi}
    h1, ffn1_saved, got_in, got_out = _ffn_fwd("ffn1", h, lw["ffn1_norm"], ffn1_wi, None, (ride("ffn1_wo"),),
                                               (ride("w_in"), ride("ssm_w_glu"), ride("w_out")))
    got["ffn1_wo"] = got_in[0]
    got["w_in"], got["ssm_w_glu"], got["w_out"] = got_out

    xn2 = norm_fwd("mix_norm", h1, lw["mix_norm"])
    z = mm_nn("mix_in", xn2, _rows(got["w_in"]), F32)
    log_dt = jnp.repeat(lw["ssm_log_dt"], SSM_STATE)
    b_re, b_im = lw["ssm_b_re"].reshape(SSM_COLS, SSM_CH), lw["ssm_b_im"].reshape(SSM_COLS, SSM_CH)
    pw_re, pw_im, bb_re, bb_im = ssm_prep("ssm_prep", _col(lw["ssm_lambda_re"]), _col(lw["ssm_lambda_im"]), _col(log_dt), b_re, b_im)
    bbd = expand_b(bb_re, bb_im)
    cbd = expand_c(lw["ssm_c_re"], lw["ssm_c_im"])
    bu, got["ple_w_gate"], got["ple_w_proj"] = ssm_in("ssm_in", z, bbd, riders=(ride("ple_w_gate"), ride("ple_w_proj")))
    s, got["ffn2_wi"] = ssm_scan("ssm_scan", scan_coefficients(pw_re, pw_im, False), bu, seq, False, riders=(ride("ffn2_wi"),))
    y0c, got["ffn2_wo"] = ssm_out("ssm_out", s, cbd, riders=(ride("ffn2_wo"),))

    def skip_gelu(yc, zs, dvec):
        y0 = yc + dvec * zs
        return y0, _gelu(y0)

    y0, y1 = _rowwise("ssm_gelu", skip_gelu, n, 512, [(y0c, SSM_WIDTH, 0), (z, SSM_WIDTH, 0)],
                      [lw["ssm_d"].reshape(1, SSM_WIDTH)], [(SSM_WIDTH, F32), (SSM_WIDTH, F32)])
    t = mm_nn("ssm_glu_mm", y1, _rows(got["ssm_w_glu"]), F32)
    y2 = _rowwise("ssm_glu", lambda a, b: a * jax.nn.sigmoid(b), n, 512, [(y1, SSM_WIDTH, 0), (t, SSM_WIDTH, 0)], [],
                  [(SSM_WIDTH, BF16)])[0]

    q = pool_window("pool_window", z, SSM_WIDTH // POOL_CH, seq, BF16, False)
    wp_eff = lw["pool_w"] * lw["pool_scale"].reshape(POOL_GROUPS, 1, POOL_CH)
    yp = pool_mm("pool_mm", q, wp_eff, BF16)
    m = jnp.concatenate([y2, yp], axis=1)
    h2 = mm_nn("mix_out", m, _rows(got["w_out"]), F32, res=h1)

    h3, ffn2_saved, got_in, _ = _ffn_fwd("ffn2", h2, lw["ffn2_norm"], got["ffn2_wi"], _rows(got["ffn2_wo"]),
                                         () if next_wi is None else (next_wi,))
    next_ffn1_wi = got_in[0] if got_in else None

    xn4 = norm_fwd("ple_norm", h3, lw["ple_norm"])
    tg = mm_nn("ple_gate", xn4, _rows(got["ple_w_gate"]), F32)
    e = mm_nn_colsharded("ple_proj", p_l, got["ple_w_proj"], F32)
    h4 = _rowwise("ple_add", lambda a, b, c: a + jax.nn.sigmoid(b) * c, n, 512, [(h3, d, 0), (tg, d, 0), (e, d, 0)], [], [(d, F32)])[0]
    saved = dict(h=h, h1=h1, h2=h2, h3=h3, ffn1=ffn1_saved, ffn2=ffn2_saved, xn2=xn2, z=z, s=s, y0=y0, y1=y1, t=t, m=m, q=q,
                 xn4=xn4, tg=tg, e=e, pw_re=pw_re, pw_im=pw_im, bbd=bbd, cbd=cbd)
    return h4, saved, got, next_ffn1_wi


def _layer_bwd(d, lw, p_l, sv, seq, bufs, layer):
    n, dm = d.shape
    g = {}
    def ple_bwd(dd, tg, e):
        gate = jax.nn.sigmoid(tg)
        return dd * e * gate * (1.0 - gate), dd * gate

    dtg, de = _rowwise("ple_dadd", ple_bwd, n, 512, [(d, dm, 0), (sv["tg"], dm, 0), (sv["e"], dm, 0)], [], [(dm, BF16), (dm, BF16)])
    g["ple_w_proj"] = mm_tn_colsharded("ple_dproj", p_l, de, F32, into=(bufs["ple_w_proj"], layer))
    g["ple_w_gate"] = mm_tn("ple_dgate_w", sv["xn4"], dtg, F32, into=(bufs["ple_w_gate"], layer))
    dxn4 = mm_nn("ple_dgate_x", dtg, lw["ple_w_gate_t"], F32)
    d, dg = norm_bwd("ple_dnorm", dxn4, sv["h3"], lw["ple_norm"], d)
    g["ple_norm"] = dg.reshape(dm)

    d, g["ffn2_norm"], g["ffn2_wi"], g["ffn2_wo"] = _ffn_bwd("ffn2b", d, sv["h2"], lw["ffn2_norm"], lw["ffn2_wi_t"], lw["ffn2_wo_t"], sv["ffn2"],
                                                            (bufs["ffn2_wi"], layer), (bufs["ffn2_wo"], layer))

    dmix = mm_nn("mix_dout_x", d, lw["w_out_t"], F32)
    g["w_out"] = mm_tn("mix_dout_w", sv["m"], d, F32, into=(bufs["w_out"], layer))
    pool_cb = SSM_WIDTH // POOL_CH
    wp_eff = lw["pool_w"] * lw["pool_scale"].reshape(POOL_GROUPS, 1, POOL_CH)
    dq = pool_mm_t("pool_dmm_x", dmix, pool_cb, wp_eff, F32)
    dwp_eff = pool_grad_w("pool_dmm_w", sv["q"], dmix, pool_cb)
    g["pool_w"] = dwp_eff * lw["pool_scale"].reshape(POOL_GROUPS, 1, POOL_CH)
    g["pool_scale"] = jnp.sum(dwp_eff * lw["pool_w"], axis=1).reshape(POOL_WIDTH)
    dzp = pool_window("pool_dwindow", dq, 0, seq, BF16, True)

    def glu_bwd(dy2, y1, t):
        sg = jax.nn.sigmoid(t)
        return dy2 * y1 * sg * (1.0 - sg), dy2 * sg

    dt_, dy1a = _rowwise("ssm_dglu", glu_bwd, n, 512, [(dmix, SSM_WIDTH, 0), (sv["y1"], SSM_WIDTH, 0), (sv["t"], SSM_WIDTH, 0)], [],
                         [(SSM_WIDTH, BF16), (SSM_WIDTH, F32)])
    g["ssm_w_glu"] = mm_tn("ssm_dglu_w", sv["y1"], dt_, F32, into=(bufs["ssm_w_glu"], layer))
    dy1b = mm_nn("ssm_dglu_x", dt_, lw["ssm_w_glu_t"], F32)

    def gelu_bwd(da, db, y0, zs, dvec):
        dy0 = (da + db) * _gelu_grad(y0)
        return dy0, dy0 * dvec, jnp.sum(dy0 * zs, axis=0, keepdims=True)

    dy0, dzs_a, dd = _rowwise("ssm_dgelu", gelu_bwd, n, 512,
                              [(dy1a, SSM_WIDTH, 0), (dy1b, SSM_WIDTH, 0), (sv["y0"], SSM_WIDTH, 0), (sv["z"], SSM_WIDTH, 0)],
                              [lw["ssm_d"].reshape(1, SSM_WIDTH)], [(SSM_WIDTH, F32), (SSM_WIDTH, F32)], [(1, SSM_WIDTH)])
    g["ssm_d"] = dd.reshape(SSM_WIDTH)
    g["ssm_c_re"], g["ssm_c_im"] = compact_c(ssm_grad_c("ssm_dc", sv["s"], dy0))
    v = ssm_out_t("ssm_dout", dy0, sv["cbd"])
    a, dlam = ssm_scan("ssm_scan_adj", scan_coefficients(sv["pw_re"], sv["pw_im"], True), v, seq, True, states=sv["s"])
    dbb_re, dbb_im = compact_b(ssm_grad_b("ssm_db", sv["z"], a))
    dzs = ssm_in_t("ssm_din", a, sv["bbd"], dzs_a)
    dlam = jnp.sum(dlam, axis=1)
    log_dt = jnp.repeat(lw["ssm_log_dt"], SSM_STATE)
    b_re, b_im = lw["ssm_b_re"].reshape(SSM_COLS, SSM_CH), lw["ssm_b_im"].reshape(SSM_COLS, SSM_CH)
    glr, gli, gld, gbr, gbi = ssm_prep_bwd("ssm_prep_bwd", _col(lw["ssm_lambda_re"]), _col(lw["ssm_lambda_im"]), _col(log_dt), b_re, b_im,
                                           _col(dlam[0]), _col(dlam[1]), dbb_re, dbb_im)
    g["ssm_lambda_re"] = glr.reshape(SSM_GROUPS, SSM_STATE)
    g["ssm_lambda_im"] = gli.reshape(SSM_GROUPS, SSM_STATE)
    g["ssm_log_dt"] = jnp.sum(gld.reshape(SSM_GROUPS, SSM_STATE), axis=1)
    g["ssm_b_re"] = gbr.reshape(SSM_GROUPS, SSM_STATE, SSM_CH)
    g["ssm_b_im"] = gbi.reshape(SSM_GROUPS, SSM_STATE, SSM_CH)

    dz = jnp.concatenate([dzs.astype(BF16), dzp], axis=1)
    g["w_in"] = mm_tn("mix_din_w", sv["xn2"], dz, F32, into=(bufs["w_in"], layer))
    dxn2 = mm_nn("mix_din_x", dz, lw["w_in_t"], F32)
    d, dg = norm_bwd("mix_dnorm", dxn2, sv["h1"], lw["mix_norm"], d)
    g["mix_norm"] = dg.reshape(dm)

    d, g["ffn1_norm"], g["ffn1_wi"], g["ffn1_wo"] = _ffn_bwd("ffn1b", d, sv["h"], lw["ffn1_norm"], lw["ffn1_wi_t"], lw["ffn1_wo_t"], sv["ffn1"],
                                                            (bufs["ffn1_wi"], layer), (bufs["ffn1_wo"], layer))
    return d, g


def _flatten_small(tensors):
    flat = jnp.concatenate([t.reshape(-1) for t in tensors])
    pad = (-flat.shape[0]) % (SUBLANE * LANE)
    return jnp.pad(flat, (0, pad)).reshape(-1, LANE)


def _unflatten_small(flat, like):
    flat = flat.reshape(-1)
    out, off = [], 0
    for t in like:
        out.append(flat[off:off + t.size].reshape(t.shape))
        off += t.size
    return out


def kernel(x, p, ffn1_norm, ffn1_wi, ffn1_wo, mix_norm, w_in, ssm_lambda_re, ssm_lambda_im, ssm_log_dt, ssm_b_re, ssm_b_im, ssm_c_re, ssm_c_im, ssm_d, ssm_w_glu, pool_w, pool_scale, w_out, ffn2_norm, ffn2_wi, ffn2_wo, ple_norm, ple_w_gate, ple_w_proj, final_norm, loss_target, m_ffn1_norm, m_ffn1_wi, m_ffn1_wo, m_mix_norm, m_w_in, m_ssm_lambda_re, m_ssm_lambda_im, m_ssm_log_dt, m_ssm_b_re, m_ssm_b_im, m_ssm_c_re, m_ssm_c_im, m_ssm_d, m_ssm_w_glu, m_pool_w, m_pool_scale, m_w_out, m_ffn2_norm, m_ffn2_wi, m_ffn2_wo, m_ple_norm, m_ple_w_gate, m_ple_w_proj, m_final_norm, v_ffn1_norm, v_ffn1_wi, v_ffn1_wo, v_mix_norm, v_w_in, v_ssm_lambda_re, v_ssm_lambda_im, v_ssm_log_dt, v_ssm_b_re, v_ssm_b_im, v_ssm_c_re, v_ssm_c_im, v_ssm_d, v_ssm_w_glu, v_pool_w, v_pool_scale, v_w_out, v_ffn2_norm, v_ffn2_wi, v_ffn2_wo, v_ple_norm, v_ple_w_gate, v_ple_w_proj, v_final_norm):
    given = dict(locals())
    w = {k: given[k] for k in WEIGHTS}
    mom = {k: given["m_" + k] for k in WEIGHTS}
    var = {k: given["v_" + k] for k in WEIGHTS}
    bsz, seq, dm = x.shape
    n = bsz * seq
    depth = ffn1_wi.shape[0]

    shards = {k: w[k].astype(MXU_DTYPE) for k in BIG}

    def layer_weights(l, gathered):
        lw = {k: w[k][l] for k in SMALL if k != "final_norm"}
        for k, a in gathered.items():
            if k in COL_SHARDED:
                lw[k] = a
                if k != "ple_w_proj":
                    lw[k + "_t"] = transpose_colsharded(a[None])[0]
            else:
                lw[k] = a.reshape(N_CHIPS * a.shape[1], a.shape[2])
                lw[k + "_t"] = lw[k].T
        return lw

    ffn1_wi_l = gather_weight("gather_ffn1_wi", shards["ffn1_wi"][:1])[0]

    p2 = p.reshape(depth, n, p.shape[-1])
    h_last = x.reshape(n, dm)
    saved, layers = [], []
    for l in range(depth):
        small = {k: w[k][l] for k in SMALL if k != "final_norm"}
        next_wi = (shards["ffn1_wi"], l + 1) if l + 1 < depth else None
        h_last, sv, gathered, ffn1_wi_l = _layer_fwd(h_last, small, ffn1_wi_l, functools.partial(lambda k, l: (shards[k], l), l=l),
                                                      next_wi, p2[l], seq)
        layers.append(layer_weights(l, gathered))
        saved.append(sv)

    def head(hh, tgt, gf):
        r = lax.rsqrt(jnp.mean(hh * hh, axis=-1, keepdims=True) + NORM_EPS)
        xh = hh * r
        diff = xh * gf - tgt
        dy = diff * (1.0 / dm)
        dxh = dy * gf
        dx = r * (dxh - xh * jnp.mean(dxh * xh, axis=-1, keepdims=True))
        return dx, jnp.sum(diff * diff, axis=0, keepdims=True) * (0.5 / dm), jnp.sum(dy * xh, axis=0, keepdims=True)

    d_last, loss_cols, g_final = _rowwise("loss_head", head, n, 512, [(h_last, dm, 0), (loss_target.reshape(n, dm), dm, 0)],
                                          [final_norm.reshape(1, dm)], [(dm, F32)], [(1, dm), (1, dm)])
    loss = lax.psum(jnp.sum(loss_cols), ("x", "y", "c"))

    d_x = d_last
    layer_grads = [None] * depth
    bufs = {k: lax.empty((depth, N_CHIPS) + w[k].shape[1:], F32) for k in BIG}
    for l in reversed(range(depth)):
        d_x, layer_grads[l] = _layer_bwd(d_x, layers[l], p2[l], saved[l], seq, bufs, l)
        bufs = {k: layer_grads[l][k] for k in BIG}
    grads = {k: jnp.stack([g[k] for g in layer_grads]) for k in layer_grads[0] if k not in BIG}
    grads.update(bufs)
    grad_x = d_x.reshape(bsz, seq, dm)

    big_sum = [reduce_scatter_streamed("rs_" + k, grads[k]) for k in BIG]
    small_keys = [k for k in SMALL]
    small_parts = [grads[k] if k != "final_norm" else g_final.reshape(dm) for k in small_keys]
    small_sum = _unflatten_small(all_reduce_small(_flatten_small(small_parts)), small_parts)
    g_out = dict(zip(BIG, big_sum))
    g_out.update(dict(zip(small_keys, small_sum)))
    for k in BIG:
        g_out[k] = g_out[k].reshape(w[k].shape)

    delta, new_m, new_v = {}, {}, {}
    for k in BIG:
        delta[k], new_m[k], new_v[k] = adamw("adamw_" + k, w[k], g_out[k], mom[k], var[k])
    sw = adamw("adamw_small", *[_flatten_small([t[k] for k in small_keys]) for t in (w, g_out, mom, var)])
    for name, flat in zip((delta, new_m, new_v), sw):
        for k, t in zip(small_keys, _unflatten_small(flat, [w[k] for k in small_keys])):
            name[k] = t

    return (loss, grad_x, *[g_out[k] for k in WEIGHTS], *[delta[k] for k in WEIGHTS],
            *[new_m[k] for k in WEIGHTS], *[new_v[k] for k in WEIGHTS])
```

```python
import functools
import math

import jax
import jax.numpy as jnp
from jax import lax
from jax.experimental import pallas as pl
from jax.experimental.pallas import tpu as pltpu

F32 = jnp.float32
BF16 = jnp.bfloat16
MXU_DTYPE = jnp.bfloat16
GRAD_WIRE_DTYPE = jnp.bfloat16
STATE_DTYPE = jnp.bfloat16
VMEM_LIMIT = 56 * 1024 * 1024
LANE = 128
SUBLANE = 8

N_CHIPS = 4
SSM_GROUPS = 32
SSM_STATE = 64
SSM_CH = 16
SSM_WIDTH = SSM_GROUPS * SSM_CH
SSM_COLS = SSM_GROUPS * SSM_STATE
POOL_GROUPS = 4
POOL_CH = 128
POOL_WIDTH = POOL_GROUPS * POOL_CH
SSM_TILE = 256
NORM_EPS = 1e-6
ADAM_LR = 0.001
ADAM_B1 = 0.9
ADAM_B2 = 0.999
ADAM_EPS = 1e-08
ADAM_WD = 0.01
ADAM_STEP = 10
MESH_ID = pl.DeviceIdType.MESH

BIG = ("ffn1_wi", "ffn1_wo", "w_in", "ssm_w_glu", "w_out", "ffn2_wi", "ffn2_wo", "ple_w_gate", "ple_w_proj")
COL_SHARDED = ("ffn1_wi", "ffn2_wi", "ple_w_proj")
SMALL = ("ffn1_norm", "mix_norm", "ssm_lambda_re", "ssm_lambda_im", "ssm_log_dt", "ssm_b_re", "ssm_b_im",
         "ssm_c_re", "ssm_c_im", "ssm_d", "pool_w", "pool_scale", "ffn2_norm", "ple_norm", "final_norm")
WEIGHTS = ("ffn1_norm", "ffn1_wi", "ffn1_wo", "mix_norm", "w_in", "ssm_lambda_re", "ssm_lambda_im", "ssm_log_dt",
           "ssm_b_re", "ssm_b_im", "ssm_c_re", "ssm_c_im", "ssm_d", "ssm_w_glu", "pool_w", "pool_scale", "w_out",
           "ffn2_norm", "ffn2_wi", "ffn2_wo", "ple_norm", "ple_w_gate", "ple_w_proj", "final_norm")


def _tile(dim, target):
    best = None
    t = LANE
    while t <= min(dim, target):
        if dim % t == 0:
            best = t
        t += LANE
    return best if best is not None else dim


def _params(sem):
    return pltpu.CompilerParams(dimension_semantics=sem, vmem_limit_bytes=VMEM_LIMIT)


CHIPS = ((1, 0, 0), (0, 1, 0), (1, 1, 0))


def _ride_scratch(n):
    return [pltpu.SemaphoreType.DMA((3 * n,)), pltpu.SemaphoreType.DMA((3 * n,)), pltpu.SemaphoreType.DMA((n,))]


def _ride_copies(layers, ins, outs, send_sems, recv_sems, local_sems):
    x, y, c = lax.axis_index("x"), lax.axis_index("y"), lax.axis_index("c")
    me = 2 * x + y
    copies = []
    for i in range(len(ins)):
        src = ins[i].at[pl.ds(layers[i], 1)]
        dst = outs[i].at[pl.ds(me, 1)]
        copies.append(pltpu.make_async_copy(src, dst, local_sems.at[i]))
        for f, (fx, fy, _) in enumerate(CHIPS):
            peer = ((1 - x) if fx else x, (1 - y) if fy else y, c)
            copies.append(pltpu.make_async_remote_copy(
                src_ref=src, dst_ref=dst, send_sem=send_sems.at[3 * i + f], recv_sem=recv_sems.at[3 * i + f],
                device_id=peer, device_id_type=pl.DeviceIdType.MESH))
    return copies


def _ride(grid, layers, ins, outs, sems):
    if not ins:
        return
    ids = [pl.program_id(a) for a in range(len(grid))]
    first = functools.reduce(jnp.logical_and, [i == 0 for i in ids])
    last = functools.reduce(jnp.logical_and, [i == g - 1 for i, g in zip(ids, grid)])

    @pl.when(first)
    def _():
        for cp in _ride_copies(layers, ins, outs, *sems):
            cp.start()

    @pl.when(last)
    def _():
        for cp in _ride_copies(layers, ins, outs, *sems):
            cp.wait()


def _ride_out_shapes(riders):
    return [jax.ShapeDtypeStruct((N_CHIPS,) + r.shape[1:], r.dtype) for r, _ in riders]


def _mm(name, a, b, out_shape, out_dtype, grid, a_spec, b_spec, o_spec, contract, alpha=1.0, res=None, riders=(), into=None):
    n_k = grid[2]
    acc_shape = tuple(d for d in o_spec.block_shape if d is not None)
    n_in = 2 + (res is not None) + (into is not None)
    n_r = len(riders)
    ride_layers = [l for _, l in riders]

    def body(*refs):
        a_ref, b_ref = refs[0], refs[1]
        r_ref = refs[2] if res is not None else None
        o_ref = refs[n_in + n_r]
        if n_r:
            _ride(grid, ride_layers, refs[n_in:n_in + n_r], refs[n_in + n_r + 1:n_in + 2 * n_r + 1], refs[len(refs) - 3:])

        def product():
            return lax.dot_general(a_ref[...].astype(MXU_DTYPE), b_ref[...].astype(MXU_DTYPE),
                                   (contract, ((), ())), preferred_element_type=F32)

        def finish(v):
            if alpha != 1.0:
                v = v * alpha
            if r_ref is not None:
                v = v + r_ref[...].astype(F32)
            o_ref[...] = v.astype(out_dtype)

        if n_k == 1:
            finish(product())
            return
        acc = refs[n_in + 2 * n_r + 1]
        k = pl.program_id(2)

        @pl.when(k == 0)
        def _():
            acc[...] = product()

        @pl.when(k > 0)
        def _():
            acc[...] += product()

        @pl.when(k == n_k - 1)
        def _():
            finish(acc[...])

    in_specs = [a_spec, b_spec]
    operands = [a, b]
    if res is not None:
        in_specs.append(o_spec)
        operands.append(res)
    scratch = [pltpu.VMEM(acc_shape, F32)] if n_k > 1 else []
    any_spec = pl.BlockSpec(memory_space=pl.ANY)
    if into is not None:
        buf, layer = into
        assert buf.shape[1:] == tuple(out_shape) and buf.dtype == out_dtype and not n_r
        placed = pl.BlockSpec((None,) + tuple(o_spec.block_shape),
                              functools.partial(lambda i, j, kk, m, l: (l,) + tuple(m(i, j, kk)), m=o_spec.index_map, l=layer))
        return pl.pallas_call(
            body, name=name, grid=grid, in_specs=in_specs + [any_spec], out_specs=placed,
            out_shape=jax.ShapeDtypeStruct(buf.shape, out_dtype), scratch_shapes=scratch,
            input_output_aliases={len(operands): 0},
            compiler_params=_params(("parallel", "parallel", "arbitrary")),
        )(*operands, buf)
    if not n_r:
        return pl.pallas_call(
            body, name=name, grid=grid, in_specs=in_specs, out_specs=o_spec,
            out_shape=jax.ShapeDtypeStruct(out_shape, out_dtype), scratch_shapes=scratch,
            compiler_params=_params(("parallel", "parallel", "arbitrary")),
        )(*operands)
    return pl.pallas_call(
        body, name=name, grid=grid, in_specs=in_specs + [any_spec] * n_r, out_specs=[o_spec] + [any_spec] * n_r,
        out_shape=[jax.ShapeDtypeStruct(out_shape, out_dtype)] + _ride_out_shapes(riders),
        scratch_shapes=scratch + _ride_scratch(n_r),
        compiler_params=_params(("arbitrary", "arbitrary", "arbitrary")),
    )(*operands, *[r for r, _ in riders])


NN = ((1,), (0,))
NT = ((1,), (1,))
TN = ((0,), (0,))


def mm_nn(name, a, b, out_dtype, alpha=1.0, res=None, tm=1024, tn=1024, tk=1024, riders=()):
    m, k = a.shape
    n = b.shape[1]
    tm, tn, tk = _tile(m, tm), _tile(n, tn), _tile(k, tk)
    return _mm(name, a, b, (m, n), out_dtype, (m // tm, n // tn, k // tk),
               pl.BlockSpec((tm, tk), lambda i, j, kk: (i, kk)),
               pl.BlockSpec((tk, tn), lambda i, j, kk: (kk, j)),
               pl.BlockSpec((tm, tn), lambda i, j, kk: (i, j)), NN, alpha, res, riders)


def mm_nt(name, a, b, out_dtype, alpha=1.0, res=None, tm=1024, tn=512, tk=512):
    m, k = a.shape
    n = b.shape[0]
    tm, tn, tk = _tile(m, tm), _tile(n, tn), _tile(k, tk)
    return _mm(name, a, b, (m, n), out_dtype, (m // tm, n // tn, k // tk),
               pl.BlockSpec((tm, tk), lambda i, j, kk: (i, kk)),
               pl.BlockSpec((tn, tk), lambda i, j, kk: (j, kk)),
               pl.BlockSpec((tm, tn), lambda i, j, kk: (i, j)), NT, alpha, res)


def mm_tn(name, a, b, out_dtype, alpha=1.0, tm=1024, tn=1024, tk=1024, into=None):
    k, m = a.shape
    n = b.shape[1]
    tm, tn, tk = _tile(m, tm), _tile(n, tn), _tile(k, tk)
    if into is not None:
        shape = into[0].shape
        into = (into[0].reshape(shape[0], m, n), into[1])
    out = _mm(name, a, b, (m, n), out_dtype, (m // tm, n // tn, k // tk),
              pl.BlockSpec((tk, tm), lambda i, j, kk: (kk, i)),
              pl.BlockSpec((tk, tn), lambda i, j, kk: (kk, j)),
              pl.BlockSpec((tm, tn), lambda i, j, kk: (i, j)), TN, alpha, into=into)
    return out if into is None else out.reshape(shape)


def mm_nn_colsharded(name, a, w, out_dtype, tm=1024, tk=1024, riders=()):
    m, k = a.shape
    c = w.shape[2]
    tm, tk = _tile(m, tm), _tile(k, tk)
    return _mm(name, a, w, (m, N_CHIPS * c), out_dtype, (m // tm, N_CHIPS, k // tk),
               pl.BlockSpec((tm, tk), lambda i, j, kk: (i, kk)),
               pl.BlockSpec((None, tk, c), lambda i, j, kk: (j, kk, 0)),
               pl.BlockSpec((tm, c), lambda i, j, kk: (i, j)), NN, riders=riders)


def transpose_colsharded(w):
    nl, _, k, c = w.shape
    return jnp.swapaxes(w, 2, 3).reshape(nl, N_CHIPS * c, k)


def mm_tn_colsharded(name, a, b, out_dtype, tm=1024, tk=1024, into=None):
    t, k = a.shape
    c = b.shape[1] // N_CHIPS
    tm, tk = _tile(k, tm), _tile(t, tk)
    return _mm(name, a, b, (N_CHIPS, k, c), out_dtype, (k // tm, N_CHIPS, t // tk),
               pl.BlockSpec((tk, tm), lambda i, j, kk: (kk, i)),
               pl.BlockSpec((tk, c), lambda i, j, kk: (kk, j)),
               pl.BlockSpec((None, tm, c), lambda i, j, kk: (j, i, 0)), TN, into=into)


def _rowwise(name, fn, n_rows, tm, row_ins, bcast_ins, outs, accs=()):
    tm = min(tm, n_rows)
    grid = (n_rows // tm,)
    n_row, n_b, n_out = len(row_ins), len(bcast_ins), len(outs)

    def body(*refs):
        ins = [r[...] for r in refs[:n_row + n_b]]
        out_refs = refs[n_row + n_b:n_row + n_b + n_out]
        acc_refs = refs[n_row + n_b + n_out:]
        res = fn(*ins)
        if not isinstance(res, (tuple, list)):
            res = (res,)
        for o_ref, v in zip(out_refs, res[:n_out]):
            o_ref[...] = v.astype(o_ref.dtype)
        if acc_refs:
            @pl.when(pl.program_id(0) == 0)
            def _():
                for a_ref in acc_refs:
                    a_ref[...] = jnp.zeros_like(a_ref)
            for a_ref, v in zip(acc_refs, res[n_out:]):
                a_ref[...] += v

    in_specs, operands = [], []
    for spec in row_ins:
        arr, width, cb = spec[0], spec[1], spec[2]
        rb = spec[3] if len(spec) > 3 else 0
        in_specs.append(pl.BlockSpec((tm, width), functools.partial(lambda i, cb, rb: (i + rb, cb), cb=cb, rb=rb)))
        operands.append(arr)
    for arr in bcast_ins:
        in_specs.append(pl.BlockSpec(arr.shape, functools.partial(lambda i, nd: (0,) * nd, nd=arr.ndim)))
        operands.append(arr)
    out_specs = [pl.BlockSpec((tm, w), lambda i: (i, 0)) for w, _ in outs]
    out_specs += [pl.BlockSpec((r, w), lambda i: (0, 0)) for r, w in accs]
    out_shape = [jax.ShapeDtypeStruct((n_rows, w), dt) for w, dt in outs]
    out_shape += [jax.ShapeDtypeStruct((r, w), F32) for r, w in accs]
    res = pl.pallas_call(
        body, name=name, grid=grid, in_specs=in_specs, out_specs=out_specs, out_shape=out_shape,
        compiler_params=_params(("arbitrary",) if accs else ("parallel",)),
    )(*operands)
    return res


def _rms(x, g):
    r = lax.rsqrt(jnp.mean(x * x, axis=-1, keepdims=True) + NORM_EPS)
    return x * r * g


def _rms_bwd(dy, x, g):
    r = lax.rsqrt(jnp.mean(x * x, axis=-1, keepdims=True) + NORM_EPS)
    xh = x * r
    dxh = dy * g
    dx = r * (dxh - xh * jnp.mean(dxh * xh, axis=-1, keepdims=True))
    return dx, jnp.sum(dy * xh, axis=0, keepdims=True)


def norm_fwd(name, h, g):
    n, d = h.shape
    return _rowwise(name, lambda x, gg: _rms(x, gg), n, 512, [(h, d, 0)], [g.reshape(1, d)], [(d, BF16)])[0]


def norm_bwd(name, dxn, h, g, d_res):
    n, d = h.shape

    def fn(dy, x, dr, gg):
        dx, dg = _rms_bwd(dy, x, gg)
        return dr + dx, dg

    return _rowwise(name, fn, n, 512, [(dxn, d, 0), (h, d, 0), (d_res, d, 0)], [g.reshape(1, d)], [(d, F32)], [(1, d)])


_GELU_C = math.sqrt(2.0 / math.pi)


def _gelu(x):
    return 0.5 * x * (1.0 + jnp.tanh(_GELU_C * (x + 0.044715 * (x * x * x))))


def _gelu_grad(x):
    th = jnp.tanh(_GELU_C * (x + 0.044715 * (x * x * x)))
    return 0.5 * (1.0 + th) + 0.5 * x * (1.0 - th * th) * (_GELU_C * (1.0 + 3.0 * 0.044715 * (x * x)))


def _ssm_discretize(lam_re, lam_im, log_dt, b_re, b_im):
    dt = jnp.exp(log_dt)
    e = jnp.exp(lam_re * dt)
    lb_re = e * jnp.cos(lam_im * dt)
    lb_im = e * jnp.sin(lam_im * dt)
    nr, ni = lb_re - 1.0, lb_im
    den = lam_re * lam_re + lam_im * lam_im
    cr = (nr * lam_re + ni * lam_im) / den
    ci = (ni * lam_re - nr * lam_im) / den
    return lb_re, lb_im, cr * b_re - ci * b_im, cr * b_im + ci * b_re


def ssm_prep(name, lam_re, lam_im, log_dt, b_re, b_im):
    def body(lr, li, ld, br, bi, pr_ref, pi_ref, bbr_ref, bbi_ref):
        lb_re, lb_im, bb_re, bb_im = _ssm_discretize(lr[...], li[...], ld[...], br[...], bi[...])
        bbr_ref[...] = bb_re
        bbi_ref[...] = bb_im
        pr, pi = lb_re, lb_im
        cols_r, cols_i = [pr], [pi]
        for _ in range(SUBLANE - 1):
            pr, pi = pr * lb_re - pi * lb_im, pr * lb_im + pi * lb_re
            cols_r.append(pr)
            cols_i.append(pi)
        lane = lax.broadcasted_iota(jnp.int32, (SSM_COLS, SUBLANE), 1)
        out_r = jnp.zeros((SSM_COLS, SUBLANE), F32)
        out_i = jnp.zeros((SSM_COLS, SUBLANE), F32)
        for r in range(SUBLANE):
            out_r = jnp.where(lane == r, cols_r[r], out_r)
            out_i = jnp.where(lane == r, cols_i[r], out_i)
        pr_ref[...] = out_r
        pi_ref[...] = out_i

    shapes = [jax.ShapeDtypeStruct((SSM_COLS, SUBLANE), F32)] * 2 + [jax.ShapeDtypeStruct((SSM_COLS, SSM_CH), F32)] * 2
    return pl.pallas_call(body, name=name, out_shape=shapes,
                          compiler_params=pltpu.CompilerParams(vmem_limit_bytes=VMEM_LIMIT))(lam_re, lam_im, log_dt, b_re, b_im)


def ssm_prep_bwd(name, lam_re, lam_im, log_dt, b_re, b_im, d_lb_re, d_lb_im, d_bb_re, d_bb_im):
    def body(lr, li, ld, br, bi, g0, g1, g2, g3, o0, o1, o2, o3, o4):
        _, vjp = jax.vjp(_ssm_discretize, lr[...], li[...], ld[...], br[...], bi[...])
        res = vjp((g0[...], g1[...], g2[...], g3[...]))
        for o, v in zip((o0, o1, o2, o3, o4), res):
            o[...] = v

    col = jax.ShapeDtypeStruct((SSM_COLS, 1), F32)
    mat = jax.ShapeDtypeStruct((SSM_COLS, SSM_CH), F32)
    return pl.pallas_call(body, name=name, out_shape=[col, col, col, mat, mat],
                          compiler_params=pltpu.CompilerParams(vmem_limit_bytes=VMEM_LIMIT))(
        lam_re, lam_im, log_dt, b_re, b_im, d_lb_re, d_lb_im, d_bb_re, d_bb_im)


def scan_coefficients(pw_re, pw_im, reverse):
    pr, pi = pw_re.T, pw_im.T
    if reverse:
        pi = -pi
    row = jnp.arange(SUBLANE)[:, None]
    out = []
    for d in (1, 2, 4):
        valid = (row < SUBLANE - d) if reverse else (row >= d)
        out.append(jnp.where(valid, pr[d - 1][None, :], 0.0))
        out.append(jnp.where(valid, pi[d - 1][None, :], 0.0))
    out.append(pr[::-1] if reverse else pr)
    out.append(pi[::-1] if reverse else pi)
    return jnp.stack(out)


def ssm_scan(name, coef, x, seq, reverse, states=None, riders=()):
    n = x.shape[1]
    n_seq = n // seq
    cw = LANE
    n_cb = SSM_COLS // cw
    pair = 2 * SUBLANE
    n_pairs = seq // pair
    pairs_per_step = 2 if n_pairs % 2 == 0 else 1
    with_dlam = states is not None
    n_r = len(riders)
    assert not (n_r and with_dlam)

    def body(*refs):
        if with_dlam:
            coef_ref, x_ref, s_ref, o_ref, dl_ref = refs
        else:
            coef_ref, x_ref, o_ref = refs[0], refs[1], refs[2 + n_r]
            if n_r:
                _ride((n_cb, n_seq), [l for _, l in riders], refs[2:2 + n_r], refs[3 + n_r:3 + 2 * n_r], refs[3 + 2 * n_r:])
        c = [coef_ref[i] for i in range(8)]
        row16 = lax.broadcasted_iota(jnp.int32, (pair, cw), 0)
        zero = jnp.zeros((SUBLANE, cw), F32)

        edge = 0 if reverse else SUBLANE - 1

        def bcast_row(v, r, rows=SUBLANE):
            return jnp.broadcast_to(v[r:r + 1, :], (rows, cw))

        p8r, p8i = bcast_row(c[6], edge), bcast_row(c[7], edge)

        def local_scan(xr, xi):
            for si, d in enumerate((1, 2, 4)):
                sh = (SUBLANE - d) if reverse else d
                sr, sm = pltpu.roll(xr, sh, 0), pltpu.roll(xi, sh, 0)
                lre, lim = c[2 * si], c[2 * si + 1]
                xr, xi = xr + lre * sr - lim * sm, xi + lre * sm + lim * sr
            return xr, xi

        def step(it, carry):
            work = []
            for u in range(pairs_per_step):
                k = it * pairs_per_step + u
                pidx = (n_pairs - 1 - k) if reverse else k
                off = pl.multiple_of(pidx * pair, pair)
                xr16 = x_ref[0, pl.ds(off, pair), :].astype(F32)
                xi16 = x_ref[1, pl.ds(off, pair), :].astype(F32)
                halves = (1, 0) if reverse else (0, 1)
                tiles = {h: local_scan(xr16[h * SUBLANE:(h + 1) * SUBLANE], xi16[h * SUBLANE:(h + 1) * SUBLANE]) for h in halves}
                work.append((pidx, off, halves, tiles))
            cre, cim = carry[0], carry[1]
            acc = carry[2:]
            for pidx, off, halves, tiles in work:
                done = {}
                for h in halves:
                    lr, li = tiles[h]
                    done[h] = (lr + c[6] * cre - c[7] * cim, li + c[6] * cim + c[7] * cre)
                    cre, cim = (bcast_row(lr, edge) + p8r * cre - p8i * cim, bcast_row(li, edge) + p8r * cim + p8i * cre)
                or16 = jnp.concatenate([done[0][0], done[1][0]], axis=0)
                oi16 = jnp.concatenate([done[0][1], done[1][1]], axis=0)
                o_ref[0, pl.ds(off, pair), :] = or16.astype(o_ref.dtype)
                o_ref[1, pl.ds(off, pair), :] = oi16.astype(o_ref.dtype)
                if with_dlam:
                    poff = pl.multiple_of(jnp.maximum(pidx - 1, 0) * pair, pair)
                    first = pidx > 0
                    sr16 = s_ref[0, pl.ds(off, pair), :].astype(F32)
                    si16 = s_ref[1, pl.ds(off, pair), :].astype(F32)
                    pr_last = jnp.where(first, bcast_row(s_ref[0, pl.ds(poff, pair), :].astype(F32), pair - 1, pair), 0.0)
                    pi_last = jnp.where(first, bcast_row(s_ref[1, pl.ds(poff, pair), :].astype(F32), pair - 1, pair), 0.0)
                    spr = jnp.where(row16 == 0, pr_last, pltpu.roll(sr16, 1, 0))
                    spi = jnp.where(row16 == 0, pi_last, pltpu.roll(si16, 1, 0))
                    dre = or16 * spr + oi16 * spi
                    dim = oi16 * spr - or16 * spi
                    acc = (acc[0] + dre[:SUBLANE] + dre[SUBLANE:], acc[1] + dim[:SUBLANE] + dim[SUBLANE:])
            return (cre, cim) + tuple(acc)

        init = (zero, zero, zero, zero) if with_dlam else (zero, zero)
        fin = lax.fori_loop(0, n_pairs // pairs_per_step, step, init)
        if with_dlam:
            @pl.when(pl.program_id(1) == 0)
            def _():
                dl_ref[...] = jnp.zeros_like(dl_ref)
            dl_ref[0] += fin[2]
            dl_ref[1] += fin[3]

    blk = pl.BlockSpec((2, seq, cw), lambda j, b: (0, b, j))
    in_specs = [pl.BlockSpec((8, SUBLANE, cw), lambda j, b: (0, 0, j)), blk]
    operands = [coef, x]
    out_specs = [blk]
    out_shape = [jax.ShapeDtypeStruct(x.shape, STATE_DTYPE)]
    if with_dlam:
        in_specs.append(blk)
        operands.append(states)
        out_specs.append(pl.BlockSpec((2, SUBLANE, cw), lambda j, b: (0, 0, j)))
        out_shape.append(jax.ShapeDtypeStruct((2, SUBLANE, SSM_COLS), F32))
    scratch = []
    if n_r:
        any_spec = pl.BlockSpec(memory_space=pl.ANY)
        in_specs += [any_spec] * n_r
        operands += [r for r, _ in riders]
        out_specs += [any_spec] * n_r
        out_shape += _ride_out_shapes(riders)
        scratch = _ride_scratch(n_r)
    res = pl.pallas_call(
        body, name=name, grid=(n_cb, n_seq), in_specs=in_specs, out_specs=out_specs, out_shape=out_shape,
        scratch_shapes=scratch, compiler_params=_params(("arbitrary" if n_r else "parallel", "arbitrary")),
    )(*operands)
    return res if (with_dlam or n_r) else res[0]


def _state_col(j, kk):
    return 2 * j + kk + 2 * (kk // 2)


SSM_WIDE = 4 * SSM_TILE


def ssm_in(name, z, bbd, tm=1024, riders=()):
    n = z.shape[0]
    tm = _tile(n, tm)
    t, w = SSM_TILE, SSM_WIDE
    return _mm(name, z, bbd, (2, n, SSM_COLS), STATE_DTYPE, (n // tm, 2 * SSM_COLS // w, 1),
               pl.BlockSpec((tm, t), lambda i, j, kk: (i, j % 2)),
               pl.BlockSpec((t, w), lambda i, j, kk: (j % 2, j)),
               pl.BlockSpec((None, tm, w), lambda i, j, kk: (j // 2, i, j % 2)), NN, riders=riders)


def ssm_out(name, s, cbd, tm=1024, riders=()):
    n = s.shape[1]
    tm = _tile(n, tm)
    t = SSM_TILE
    return _mm(name, s, cbd, (n, SSM_WIDTH), F32, (n // tm, SSM_WIDTH // t, 4),
               pl.BlockSpec((None, tm, 512), lambda i, j, kk: (kk // 2, i, 2 * j + kk % 2)),
               pl.BlockSpec((512, t), lambda i, j, kk: (_state_col(j, kk), j)),
               pl.BlockSpec((tm, t), lambda i, j, kk: (i, j)), NN, riders=riders)


def ssm_out_t(name, dy, cbd, tm=1024):
    n = dy.shape[0]
    tm = _tile(n, tm)
    t, w = SSM_TILE, SSM_WIDE
    return _mm(name, dy, cbd, (2, n, SSM_COLS), STATE_DTYPE, (n // tm, 2 * SSM_COLS // w, 1),
               pl.BlockSpec((tm, t), lambda i, j, kk: (i, j % 2)),
               pl.BlockSpec((w, t), lambda i, j, kk: (j, j % 2)),
               pl.BlockSpec((None, tm, w), lambda i, j, kk: (j // 2, i, j % 2)), NT)


def ssm_in_t(name, a, bbd, res, tm=1024):
    n = a.shape[1]
    tm = _tile(n, tm)
    t = SSM_TILE
    return _mm(name, a, bbd, (n, SSM_WIDTH), F32, (n // tm, SSM_WIDTH // t, 4),
               pl.BlockSpec((None, tm, 512), lambda i, j, kk: (kk // 2, i, 2 * j + kk % 2)),
               pl.BlockSpec((t, 512), lambda i, j, kk: (j, _state_col(j, kk))),
               pl.BlockSpec((tm, t), lambda i, j, kk: (i, j)), NT, 1.0, res)


def ssm_grad_c(name, s, dy, tk=1024):
    n = s.shape[1]
    tk = _tile(n, tk)
    t, w = SSM_TILE, SSM_WIDE
    return _mm(name, s, dy, (2 * SSM_COLS, t), F32, (2 * SSM_COLS // w, 1, n // tk),
               pl.BlockSpec((None, tk, w), lambda i, j, kk: (i // 2, kk, i % 2)),
               pl.BlockSpec((tk, t), lambda i, j, kk: (kk, i % 2)),
               pl.BlockSpec((w, t), lambda i, j, kk: (i, 0)), TN)


def ssm_grad_b(name, z, a, tk=1024):
    n = z.shape[0]
    tk = _tile(n, tk)
    t, w = SSM_TILE, SSM_WIDE
    return _mm(name, z, a, (t, 2 * SSM_COLS), F32, (1, 2 * SSM_COLS // w, n // tk),
               pl.BlockSpec((tk, t), lambda i, j, kk: (kk, j % 2)),
               pl.BlockSpec((None, tk, w), lambda i, j, kk: (j // 2, kk, j % 2)),
               pl.BlockSpec((t, w), lambda i, j, kk: (0, j)), TN)


_GROUP_TILE = SSM_TILE // SSM_CH


def expand_b(bb_re, bb_im):
    b = jnp.stack([bb_re, bb_im]).reshape(2, SSM_GROUPS, SSM_STATE, SSM_CH)
    eye = jnp.eye(SSM_GROUPS, dtype=F32)
    return jnp.einsum("rgph,gk->ghrkp", b, eye).reshape(SSM_WIDTH, 2 * SSM_COLS).astype(MXU_DTYPE)


def expand_c(c_re, c_im):
    c = jnp.stack([c_re, -c_im])
    eye = jnp.eye(SSM_GROUPS, dtype=F32)
    return jnp.einsum("rghp,gk->rgpkh", c, eye).reshape(2 * SSM_COLS, SSM_WIDTH).astype(MXU_DTYPE)


def _group_pick():
    return (jnp.arange(SSM_GROUPS)[:, None] % _GROUP_TILE == jnp.arange(_GROUP_TILE)[None, :]).astype(F32)


def compact_c(dc):
    x = dc.reshape(2, SSM_GROUPS, SSM_STATE, _GROUP_TILE, SSM_CH)
    g = jnp.einsum("rgpch,gc->rghp", x, _group_pick())
    return g[0], -g[1]


def compact_b(db):
    x = db.reshape(_GROUP_TILE, SSM_CH, 2, SSM_GROUPS, SSM_STATE)
    g = jnp.einsum("chrgp,gc->rgph", x, _group_pick()).reshape(2, SSM_COLS, SSM_CH)
    return g[0], g[1]


def pool_window(name, x, col_block0, seq, out_dtype, adjoint):
    n = x.shape[0]

    def body(x_ref, o_ref):
        win = 2 << pl.program_id(1)
        row = lax.broadcasted_iota(jnp.int32, (seq, POOL_CH), 0)
        v = x_ref[...].astype(F32)
        cnt = jnp.minimum(row + 1, win).astype(F32)
        s = v / cnt if adjoint else v
        for d in (1, 2, 4, 8):
            if adjoint:
                sh = jnp.where((row < seq - d) & (d < win), pltpu.roll(s, seq - d, 0), 0.0)
            else:
                sh = jnp.where((row >= d) & (d < win), pltpu.roll(s, d, 0), 0.0)
            s = s + sh
        o_ref[...] = ((s - v) if adjoint else (s / cnt - v)).astype(out_dtype)

    return pl.pallas_call(
        body, name=name, grid=(n // seq, POOL_GROUPS),
        in_specs=[pl.BlockSpec((seq, POOL_CH), lambda b, g: (b, col_block0 + g))],
        out_specs=pl.BlockSpec((seq, POOL_CH), lambda b, g: (b, g)),
        out_shape=jax.ShapeDtypeStruct((n, POOL_WIDTH), out_dtype),
        compiler_params=_params(("parallel", "parallel")),
    )(x)


def pool_mm(name, q, w, out_dtype, tm=1024):
    n = q.shape[0]
    tm = _tile(n, tm)
    return _mm(name, q, w, (n, POOL_WIDTH), out_dtype, (n // tm, POOL_GROUPS, 1),
               pl.BlockSpec((tm, POOL_CH), lambda i, j, kk: (i, j)),
               pl.BlockSpec((None, POOL_CH, POOL_CH), lambda i, j, kk: (j, 0, 0)),
               pl.BlockSpec((tm, POOL_CH), lambda i, j, kk: (i, j)), NN)


def pool_mm_t(name, dy, col_block0, w, out_dtype, tm=1024):
    n = dy.shape[0]
    tm = _tile(n, tm)
    return _mm(name, dy, w, (n, POOL_WIDTH), out_dtype, (n // tm, POOL_GROUPS, 1),
               pl.BlockSpec((tm, POOL_CH), lambda i, j, kk: (i, col_block0 + j)),
               pl.BlockSpec((None, POOL_CH, POOL_CH), lambda i, j, kk: (j, 0, 0)),
               pl.BlockSpec((tm, POOL_CH), lambda i, j, kk: (i, j)), NT)


def pool_grad_w(name, q, dy, col_block0, tk=1024):
    n = q.shape[0]
    tk = _tile(n, tk)
    return _mm(name, q, dy, (POOL_GROUPS, POOL_CH, POOL_CH), F32, (POOL_GROUPS, 1, n // tk),
               pl.BlockSpec((tk, POOL_CH), lambda i, j, kk: (kk, i)),
               pl.BlockSpec((tk, POOL_CH), lambda i, j, kk: (kk, col_block0 + i)),
               pl.BlockSpec((None, POOL_CH, POOL_CH), lambda i, j, kk: (i, 0, 0)), TN)


def _any_specs(n):
    return [pl.BlockSpec(memory_space=pl.ANY)] * n


def _place():
    x, y, c = lax.axis_index("x"), lax.axis_index("y"), lax.axis_index("c")
    return x, y, c


def _at_axis(ref, axis, start, size):
    return ref.at[(slice(None),) * axis + (pl.ds(start, size),)]


def sibling_swap_halves(name, arrays, axis):
    n = len(arrays)
    halves = [a.shape[axis] // 2 for a in arrays]

    def body(*refs):
        ins, own, got = refs[:n], refs[n:2 * n], refs[2 * n:3 * n]
        send_sems, recv_sems, local_sems = refs[3 * n:]
        x, y, c = _place()
        copies = []
        for i in range(n):
            h = halves[i]
            mine = pltpu.make_async_copy(_at_axis(ins[i], axis, c * h, h), own[i], local_sems.at[i])
            mine.start()
            away = pltpu.make_async_remote_copy(
                src_ref=_at_axis(ins[i], axis, (1 - c) * h, h), dst_ref=got[i],
                send_sem=send_sems.at[i], recv_sem=recv_sems.at[i], device_id=(x, y, 1 - c), device_id_type=MESH_ID)
            away.start()
            copies += [mine, away]
        for cp in copies:
            cp.wait()

    def half_shape(a, h):
        return jax.ShapeDtypeStruct(a.shape[:axis] + (h,) + a.shape[axis + 1:], a.dtype)

    shapes = [half_shape(a, h) for a, h in zip(arrays, halves)]
    res = pl.pallas_call(
        body, name=name, in_specs=_any_specs(n), out_specs=_any_specs(2 * n), out_shape=shapes + shapes,
        scratch_shapes=[pltpu.SemaphoreType.DMA((n,)), pltpu.SemaphoreType.DMA((n,)), pltpu.SemaphoreType.DMA((n,))],
    )(*arrays)
    return res[:n], res[n:]


def sibling_join_halves(name, arrays, axis):
    n = len(arrays)

    def body(*refs):
        ins, outs = refs[:n], refs[n:2 * n]
        send_sems, recv_sems, local_sems = refs[2 * n:]
        x, y, c = _place()
        copies = []
        for i in range(n):
            h = ins[i].shape[axis]
            dst = _at_axis(outs[i], axis, c * h, h)
            mine = pltpu.make_async_copy(ins[i], dst, local_sems.at[i])
            mine.start()
            away = pltpu.make_async_remote_copy(
                src_ref=ins[i], dst_ref=dst, send_sem=send_sems.at[i], recv_sem=recv_sems.at[i],
                device_id=(x, y, 1 - c), device_id_type=MESH_ID)
            away.start()
            copies += [mine, away]
        for cp in copies:
            cp.wait()

    shapes = [jax.ShapeDtypeStruct(a.shape[:axis] + (2 * a.shape[axis],) + a.shape[axis + 1:], a.dtype) for a in arrays]
    return pl.pallas_call(
        body, name=name, in_specs=_any_specs(n), out_specs=_any_specs(n), out_shape=shapes,
        scratch_shapes=[pltpu.SemaphoreType.DMA((n,)), pltpu.SemaphoreType.DMA((n,)), pltpu.SemaphoreType.DMA((n,))],
    )(*arrays)


def sibling_swap(name, arrays):
    n = len(arrays)

    def body(*refs):
        ins, outs = refs[:n], refs[n:2 * n]
        send_sems, recv_sems = refs[2 * n:]
        x, y, c = _place()
        copies = []
        for i in range(n):
            away = pltpu.make_async_remote_copy(
                src_ref=ins[i], dst_ref=outs[i], send_sem=send_sems.at[i], recv_sem=recv_sems.at[i],
                device_id=(x, y, 1 - c), device_id_type=MESH_ID)
            away.start()
            copies.append(away)
        for cp in copies:
            cp.wait()

    return pl.pallas_call(
        body, name=name, in_specs=_any_specs(n), out_specs=_any_specs(n),
        out_shape=[jax.ShapeDtypeStruct(a.shape, a.dtype) for a in arrays],
        scratch_shapes=[pltpu.SemaphoreType.DMA((n,)), pltpu.SemaphoreType.DMA((n,))],
    )(*arrays)


_FLIPS = ((1, 0), (0, 1), (1, 1))


def chip_exchange(name, arrays, axis, all_to_all):
    n = len(arrays)

    def body(*refs):
        ins, outs = refs[:n], refs[n:2 * n]
        send_sems, recv_sems, local_sems = refs[2 * n:]
        x, y, c = _place()
        me = 2 * x + y
        copies = []
        for i in range(n):
            dst = _at_axis(outs[i], axis, me, 1)
            mine = pltpu.make_async_copy(_at_axis(ins[i], axis, me, 1) if all_to_all else ins[i], dst, local_sems.at[i])
            mine.start()
            copies.append(mine)
            for f, (fx, fy) in enumerate(_FLIPS):
                px = (1 - x) if fx else x
                py = (1 - y) if fy else y
                src = _at_axis(ins[i], axis, 2 * px + py, 1) if all_to_all else ins[i]
                away = pltpu.make_async_remote_copy(
                    src_ref=src, dst_ref=dst, send_sem=send_sems.at[3 * i + f], recv_sem=recv_sems.at[3 * i + f],
                    device_id=(px, py, c), device_id_type=MESH_ID)
                away.start()
                copies.append(away)
        for cp in copies:
            cp.wait()

    shapes = [jax.ShapeDtypeStruct(a.shape[:axis] + (N_CHIPS,) + a.shape[axis + 1:], a.dtype) for a in arrays]
    return pl.pallas_call(
        body, name=name, in_specs=_any_specs(n), out_specs=_any_specs(n), out_shape=shapes,
        scratch_shapes=[pltpu.SemaphoreType.DMA((3 * n,)), pltpu.SemaphoreType.DMA((3 * n,)), pltpu.SemaphoreType.DMA((n,))],
    )(*arrays)


SIBLING = ((0, 0, 1),)
ICI_CHUNK_BYTES = 2 * 1024 * 1024
D2D_CHUNK_BYTES = 4 * 1024 * 1024


def _peer(flip):
    x, y, c = _place()
    return tuple((1 - v) if f else v for v, f in zip((x, y, c), flip))


def _core():
    return lax.axis_index("c")


def _chip():
    return 2 * lax.axis_index("x") + lax.axis_index("y")


def _linear_step(grid):
    i = pl.program_id(0)
    for a in range(1, len(grid)):
        i = i * grid[a] + pl.program_id(a)
    return i


def stream_reduce(name, x, grid, block, own_map, send_maps, flips, out_shape, out_block, out_map, wire_dtype=None):
    n_steps = math.prod(grid)
    n_p = len(flips)
    vm_block = block = tuple(1 if d is None else d for d in block)
    out_block = tuple(1 if d is None else d for d in out_block)
    staged = wire_dtype is not None and wire_dtype != x.dtype
    slot_dtype = wire_dtype if staged else x.dtype

    def body(own_ref, *rest):
        send_refs = rest[:n_p]
        o_ref, recv, send_sems, recv_sems, credits = rest[n_p:n_p + 5]
        stage = rest[n_p + 5] if staged else None
        i = _linear_step(grid)
        s = i % 2
        copies = []
        for j, flip in enumerate(flips):
            src = send_refs[j]
            if staged:
                stage[j, s] = send_refs[j][...].astype(wire_dtype)
                src = stage.at[j, s]

            @pl.when(i >= 2)
            def _():
                pl.semaphore_wait(credits.at[j, s], 1)
            cp = pltpu.make_async_remote_copy(
                src_ref=src, dst_ref=recv.at[j, s], send_sem=send_sems.at[j, s], recv_sem=recv_sems.at[j, s],
                device_id=_peer(flip), device_id_type=MESH_ID)
            cp.start()
            copies.append(cp)
        acc = own_ref[...]
        for j, cp in enumerate(copies):
            cp.wait_recv()
            acc = acc + recv[j, s].astype(acc.dtype)
        o_ref[...] = acc.reshape(o_ref.shape)
        for cp in copies:
            cp.wait_send()
        for j, flip in enumerate(flips):
            @pl.when(i < n_steps - 2)
            def _():
                pl.semaphore_signal(credits.at[j, s], inc=1, device_id=_peer(flip), device_id_type=MESH_ID)

    in_specs = [pl.BlockSpec(block, own_map)] + [pl.BlockSpec(block, m) for m in send_maps]
    scratch = [pltpu.VMEM((n_p, 2) + vm_block, slot_dtype), pltpu.SemaphoreType.DMA((n_p, 2)),
               pltpu.SemaphoreType.DMA((n_p, 2)), pltpu.SemaphoreType.REGULAR((n_p, 2))]
    if staged:
        scratch.append(pltpu.VMEM((n_p, 2) + vm_block, slot_dtype))
    return pl.pallas_call(
        body, name=name, grid=grid, in_specs=in_specs, out_specs=pl.BlockSpec(out_block, out_map),
        out_shape=jax.ShapeDtypeStruct(out_shape, x.dtype), scratch_shapes=scratch,
        compiler_params=_params(("arbitrary",) * len(grid)),
    )(*([x] * (1 + n_p)))


def stream_gather(name, x, grid, block, in_map, flips, out_shape, out_block, out_map):
    n_p = len(flips)
    assert grid[-1] == n_p + 1
    n_steps = math.prod(grid[:-1])
    vm_block = block = tuple(1 if d is None else d for d in block)
    out_block = tuple(1 if d is None else d for d in out_block)

    def body(x_ref, o_ref, recv, send_sems, recv_sems, credits):
        i = _linear_step(grid[:-1])
        q = pl.program_id(len(grid) - 1)
        s = i % 2

        def copy(j):
            return pltpu.make_async_remote_copy(
                src_ref=x_ref, dst_ref=recv.at[j, s], send_sem=send_sems.at[j, s], recv_sem=recv_sems.at[j, s],
                device_id=_peer(flips[j]), device_id_type=MESH_ID)

        @pl.when(q == 0)
        def _():
            for j in range(n_p):
                @pl.when(i >= 2)
                def _():
                    pl.semaphore_wait(credits.at[j, s], 1)
                copy(j).start()
            o_ref[...] = x_ref[...].reshape(o_ref.shape)
            for j in range(n_p):
                copy(j).wait_send()

        for j in range(n_p):
            @pl.when(q == j + 1)
            def _():
                copy(j).wait_recv()
                o_ref[...] = recv[j, s].reshape(o_ref.shape)

                @pl.when(i < n_steps - 2)
                def _():
                    pl.semaphore_signal(credits.at[j, s], inc=1, device_id=_peer(flips[j]), device_id_type=MESH_ID)

    return pl.pallas_call(
        body, name=name, grid=grid, in_specs=[pl.BlockSpec(block, in_map)], out_specs=pl.BlockSpec(out_block, out_map),
        out_shape=jax.ShapeDtypeStruct(out_shape, x.dtype),
        scratch_shapes=[pltpu.VMEM((n_p, 2) + vm_block, x.dtype), pltpu.SemaphoreType.DMA((n_p, 2)),
                        pltpu.SemaphoreType.DMA((n_p, 2)), pltpu.SemaphoreType.REGULAR((n_p, 2))],
        compiler_params=_params(("arbitrary",) * len(grid)),
    )(x)


def _chip_of_substep(q):
    mask = jnp.where(q == 1, 2, jnp.where(q == 2, 1, jnp.where(q == 3, 3, 0)))
    return jnp.bitwise_xor(_chip(), mask)


def gather_weight(name, shard):
    nl, r, c = shard.shape
    r2 = r // 2
    f32_per_elem = 4 // shard.dtype.itemsize
    tr = _rows_tile(r2, c, budget=ICI_CHUNK_BYTES * f32_per_elem, step=16)
    nb = r2 // tr
    half = stream_gather(
        name + "_chips", shard, (nl, nb, N_CHIPS), (None, tr, c), lambda l, i, q: (l, _core() * nb + i, 0), CHIPS,
        (nl, N_CHIPS, r2, c), (None, None, tr, c), lambda l, i, q: (l, _chip_of_substep(q), i, 0))
    trd = _rows_tile(r2, c, budget=D2D_CHUNK_BYTES * f32_per_elem, step=16)
    nbd = r2 // trd
    both = stream_gather(
        name + "_cores", half, (nl, N_CHIPS, nbd, 2), (None, None, trd, c), lambda l, k, i, q: (l, k, i, 0), SIBLING,
        (nl, N_CHIPS, 2, r2, c), (None, None, None, trd, c), lambda l, k, i, q: (l, k, _core() + q - 2 * _core() * q, i, 0))
    return both.reshape(nl, N_CHIPS, r, c)


def reduce_scatter_streamed(name, g):
    nl, _, r, c = g.shape
    r2 = r // 2
    tr_d2d = _rows_tile(r2, c, budget=D2D_CHUNK_BYTES, step=16)
    nbd = r2 // tr_d2d
    chip_sum = stream_reduce(
        name + "_cores", g, (nl, N_CHIPS, nbd), (None, None, tr_d2d, c),
        lambda l, k, i: (l, k, _core() * nbd + i, 0), [lambda l, k, i: (l, k, (1 - _core()) * nbd + i, 0)], SIBLING,
        (nl, N_CHIPS, r2, c), (None, None, tr_d2d, c), lambda l, k, i: (l, k, i, 0), wire_dtype=GRAD_WIRE_DTYPE)
    tr = _rows_tile(r2, c, budget=ICI_CHUNK_BYTES, step=16)
    nb = r2 // tr
    blk4 = (None, None, tr, c)
    masks = (2, 1, 3)
    mine = stream_reduce(
        name + "_chips", chip_sum, (nl, nb), blk4,
        lambda l, i: (l, _chip(), i, 0),
        [functools.partial(lambda l, i, m: (l, jnp.bitwise_xor(_chip(), m), i, 0), m=m) for m in masks], CHIPS,
        (nl, r2, c), (None, tr, c), lambda l, i: (l, i, 0), wire_dtype=GRAD_WIRE_DTYPE)
    both = stream_gather(
        name + "_join", mine, (nl, nbd, 2), (None, tr_d2d, c), lambda l, i, q: (l, i, 0), SIBLING,
        (nl, 2, r2, c), (None, None, tr_d2d, c), lambda l, i, q: (l, _core() + q - 2 * _core() * q, i, 0))
    return both.reshape(nl, r, c)


def add2(name, a, b):
    shape = a.shape
    a2, b2 = a.reshape(-1, shape[-1]), b.reshape(-1, shape[-1])
    rows, w = a2.shape
    tm = _rows_tile(rows, w)
    return _rowwise(name, lambda u, v: u + v, rows, tm, [(a2, w, 0), (b2, w, 0)], [], [(w, F32)])[0].reshape(shape)


def _rows_tile(rows, width, budget=2 * 1024 * 1024, step=SUBLANE):
    best = step
    t = step
    while t <= rows:
        if rows % t == 0 and t * width * 4 <= budget:
            best = t
        t += step
    return best


def sum_slots(name, a):
    nl, _, r, c = a.shape
    tr = _rows_tile(r, c)

    def body(s0, s1, s2, s3, o_ref):
        o_ref[...] = ((s0[...] + s1[...]) + s2[...]) + s3[...]

    specs = [pl.BlockSpec((None, None, tr, c), functools.partial(lambda l, i, k: (l, k, i, 0), k=k)) for k in range(N_CHIPS)]
    return pl.pallas_call(
        body, name=name, grid=(nl, r // tr), in_specs=specs,
        out_specs=pl.BlockSpec((None, tr, c), lambda l, i: (l, i, 0)),
        out_shape=jax.ShapeDtypeStruct((nl, r, c), F32),
        compiler_params=_params(("parallel", "parallel")),
    )(a, a, a, a)


def reduce_scatter_big(grads):
    own, got = sibling_swap_halves("rs_swap_halves", grads, 2)
    chip_sum = [add2("rs_add_cores", a, b) for a, b in zip(own, got)]
    spread = chip_exchange("rs_chips", chip_sum, 1, True)
    mine = [sum_slots("rs_sum_chips", a) for a in spread]
    return sibling_join_halves("rs_join_halves", mine, 1)


def all_reduce_small(flat):
    other = sibling_swap("ar_swap", [flat])[0]
    chip = add2("ar_add_cores", flat, other)
    slots = chip_exchange("ar_chips", [chip.reshape((1,) + chip.shape)], 0, False)[0]
    rows = flat.shape[0]
    tm = _rows_tile(rows, LANE)
    nb = rows // tm
    s2 = slots.reshape(N_CHIPS * rows, LANE)
    return _rowwise("ar_sum_chips", lambda a, b, c, d: ((a + b) + c) + d, rows, tm,
                    [(s2, LANE, 0, k * nb) for k in range(N_CHIPS)], [], [(LANE, F32)])[0]


def _adamw_math(w, g, m, v):
    m = ADAM_B1 * m + (1.0 - ADAM_B1) * g
    v = ADAM_B2 * v + (1.0 - ADAM_B2) * (g * g)
    m_hat = m / (1.0 - ADAM_B1 ** ADAM_STEP)
    v_hat = v / (1.0 - ADAM_B2 ** ADAM_STEP)
    delta = -ADAM_LR * (m_hat / (jnp.sqrt(v_hat) + ADAM_EPS) + ADAM_WD * w)
    return delta, m, v


def adamw(name, w, g, m, v):
    shape = w.shape
    width = shape[-1]
    flat = [t.reshape(-1, width) for t in (w, g, m, v)]
    rows = flat[0].shape[0]
    tm = _rows_tile(rows, width, budget=1024 * 1024)
    res = _rowwise(name, _adamw_math, rows, tm, [(t, width, 0) for t in flat], [], [(width, F32)] * 3)
    return [r.reshape(shape) for r in res]


def ffn_in(name, xn, wi, riders=(), tm=512):
    n, d = xn.shape
    c = wi.shape[2]
    tm = _tile(n, tm)
    grid = (n // tm, 2)
    n_r = len(riders)

    def body(*refs):
        x_ref, wg_ref, wu_ref = refs[:3]
        g_ref, u_ref, a_ref = refs[3 + n_r:6 + n_r]
        if n_r:
            _ride(grid, [l for _, l in riders], refs[3:3 + n_r], refs[6 + n_r:6 + 2 * n_r], refs[6 + 2 * n_r:])
        x = x_ref[...].astype(MXU_DTYPE)
        g = jnp.dot(x, wg_ref[...].astype(MXU_DTYPE), preferred_element_type=F32)
        u = jnp.dot(x, wu_ref[...].astype(MXU_DTYPE), preferred_element_type=F32)
        g_ref[...] = g.astype(g_ref.dtype)
        u_ref[...] = u.astype(u_ref.dtype)
        a_ref[...] = (g * jax.nn.sigmoid(g) * u).astype(a_ref.dtype)

    any_spec = pl.BlockSpec(memory_space=pl.ANY)
    out_blk = pl.BlockSpec((tm, c), lambda i, j: (i, j))
    res = pl.pallas_call(
        body, name=name, grid=grid,
        in_specs=[pl.BlockSpec((tm, d), lambda i, j: (i, 0)), pl.BlockSpec((None, d, c), lambda i, j: (j, 0, 0)),
                  pl.BlockSpec((None, d, c), lambda i, j: (j + 2, 0, 0))] + [any_spec] * n_r,
        out_specs=[out_blk] * 3 + [any_spec] * n_r,
        out_shape=[jax.ShapeDtypeStruct((n, 2 * c), BF16)] * 3 + _ride_out_shapes(riders),
        scratch_shapes=_ride_scratch(n_r) if n_r else [],
        compiler_params=_params(("arbitrary", "arbitrary") if n_r else ("parallel", "parallel")),
    )(xn, wi, wi, *[r for r, _ in riders])
    return res[:3], list(res[3:])


def _rows(a):
    return a.reshape(N_CHIPS * a.shape[1], a.shape[2])


def _ffn_fwd(tag, h, g_norm, wi, wo, ride_in=(), ride_out=()):
    xn = norm_fwd(tag + "_norm", h, g_norm)
    (g, u, act), got_in = ffn_in(tag + "_wi", xn, wi, riders=ride_in)
    if wo is None:
        wo = _rows(got_in[0])
    out = mm_nn(tag + "_wo", act, wo, F32, alpha=0.5, res=h, tm=512, tk=2816, riders=ride_out)
    got_out = []
    if ride_out:
        out, got_out = out[0], list(out[1:])
    return out, (xn, g, u, act), got_in, got_out


HALF_SHARD_ORDER = (0, 2, 1, 3)


def ffn_dact(name, d, wo_t, gate, up, tm=512):
    n, dm = d.shape
    ff = wo_t.shape[1]
    c = ff // 2
    tm = _tile(n, tm)

    def body(d_ref, w_ref, g_ref, u_ref, o_ref):
        da = 0.5 * jnp.dot(d_ref[...].astype(MXU_DTYPE), w_ref[...].astype(MXU_DTYPE), preferred_element_type=F32)
        g, u = g_ref[...].astype(F32), u_ref[...].astype(F32)
        sg = jax.nn.sigmoid(g)
        dg = da * u * (sg * (1.0 + g * (1.0 - sg)))
        du = da * (g * sg)
        o_ref[...] = jnp.concatenate([dg, du], axis=1).astype(o_ref.dtype)

    half = pl.BlockSpec((tm, c), lambda i, j: (i, j))
    return pl.pallas_call(
        body, name=name, grid=(n // tm, 2),
        in_specs=[pl.BlockSpec((tm, dm), lambda i, j: (i, 0)), pl.BlockSpec((dm, c), lambda i, j: (0, j)), half, half],
        out_specs=pl.BlockSpec((tm, 2 * c), lambda i, j: (i, j)),
        out_shape=jax.ShapeDtypeStruct((n, 2 * ff), BF16),
        compiler_params=_params(("parallel", "parallel")),
    )(d, wo_t, gate, up)


def _ffn_bwd(tag, d, h, g_norm, wi_t, wo_t, saved, into_wi, into_wo):
    xn, gate, up, act = saved
    n, dm = h.shape
    dwo = mm_tn(tag + "_dwo", act, d, F32, alpha=0.5, tm=1408, into=into_wo)
    dgu = ffn_dact(tag + "_dact", d, wo_t, gate, up)
    t, k = xn.shape
    c = dgu.shape[1] // N_CHIPS
    tm_, tk_ = _tile(k, 1024), _tile(t, 1024)
    dwi = _mm(tag + "_dwi", xn, dgu, (N_CHIPS, k, c), F32, (k // tm_, N_CHIPS, t // tk_),
              pl.BlockSpec((tk_, tm_), lambda i, j, kk: (kk, i)),
              pl.BlockSpec((tk_, c), lambda i, j, kk: (kk, j)),
              pl.BlockSpec((None, tm_, c), lambda i, j, kk: ((j % 2) * 2 + j // 2, i, 0)), TN, into=into_wi)
    tmx, tnx = _tile(n, 1024), _tile(dm, 1024)
    dxn = _mm(tag + "_dxn", dgu, wi_t, (n, dm), F32, (n // tmx, dm // tnx, N_CHIPS),
              pl.BlockSpec((tmx, c), lambda i, j, kk: (i, kk)),
              pl.BlockSpec((None, c, tnx), lambda i, j, kk: ((kk % 2) * 2 + kk // 2, 0, j)),
              pl.BlockSpec((tmx, tnx), lambda i, j, kk: (i, j)), NN)
    d_in, dg_norm = norm_bwd(tag + "_dnorm", dxn, h, g_norm, d)
    return d_in, dg_norm.reshape(dm), dwi, dwo


def _col(v):
    return v.reshape(SSM_COLS, 1)


def _layer_fwd(h, lw, ffn1_wi, ride, next_wi, p_l, seq):
    n, d = h.shape
    got = {"ffn1_wi": ffn1_wi}
    h1, ffn1_saved, got_in, got_out = _ffn_fwd("ffn1", h, lw["ffn1_norm"], ffn1_wi, None, (ride("ffn1_wo"),),
                                               (ride("w_in"), ride("ssm_w_glu"), ride("w_out")))
    got["ffn1_wo"] = got_in[0]
    got["w_in"], got["ssm_w_glu"], got["w_out"] = got_out

    xn2 = norm_fwd("mix_norm", h1, lw["mix_norm"])
    z = mm_nn("mix_in", xn2, _rows(got["w_in"]), F32)
    log_dt = jnp.repeat(lw["ssm_log_dt"], SSM_STATE)
    b_re, b_im = lw["ssm_b_re"].reshape(SSM_COLS, SSM_CH), lw["ssm_b_im"].reshape(SSM_COLS, SSM_CH)
    pw_re, pw_im, bb_re, bb_im = ssm_prep("ssm_prep", _col(lw["ssm_lambda_re"]), _col(lw["ssm_lambda_im"]), _col(log_dt), b_re, b_im)
    bbd = expand_b(bb_re, bb_im)
    cbd = expand_c(lw["ssm_c_re"], lw["ssm_c_im"])
    bu, got["ple_w_gate"], got["ple_w_proj"] = ssm_in("ssm_in", z, bbd, riders=(ride("ple_w_gate"), ride("ple_w_proj")))
    s, got["ffn2_wi"] = ssm_scan("ssm_scan", scan_coefficients(pw_re, pw_im, False), bu, seq, False, riders=(ride("ffn2_wi"),))
    y0c, got["ffn2_wo"] = ssm_out("ssm_out", s, cbd, riders=(ride("ffn2_wo"),))

    def skip_gelu(yc, zs, dvec):
        y0 = yc + dvec * zs
        return y0, _gelu(y0)

    y0, y1 = _rowwise("ssm_gelu", skip_gelu, n, 512, [(y0c, SSM_WIDTH, 0), (z, SSM_WIDTH, 0)],
                      [lw["ssm_d"].reshape(1, SSM_WIDTH)], [(SSM_WIDTH, F32), (SSM_WIDTH, F32)])
    t = mm_nn("ssm_glu_mm", y1, _rows(got["ssm_w_glu"]), F32)
    y2 = _rowwise("ssm_glu", lambda a, b: a * jax.nn.sigmoid(b), n, 512, [(y1, SSM_WIDTH, 0), (t, SSM_WIDTH, 0)], [],
                  [(SSM_WIDTH, BF16)])[0]

    q = pool_window("pool_window", z, SSM_WIDTH // POOL_CH, seq, BF16, False)
    wp_eff = lw["pool_w"] * lw["pool_scale"].reshape(POOL_GROUPS, 1, POOL_CH)
    yp = pool_mm("pool_mm", q, wp_eff, BF16)
    m = jnp.concatenate([y2, yp], axis=1)
    h2 = mm_nn("mix_out", m, _rows(got["w_out"]), F32, res=h1)

    h3, ffn2_saved, got_in, _ = _ffn_fwd("ffn2", h2, lw["ffn2_norm"], got["ffn2_wi"], _rows(got["ffn2_wo"]),
                                         () if next_wi is None else (next_wi,))
    next_ffn1_wi = got_in[0] if got_in else None

    xn4 = norm_fwd("ple_norm", h3, lw["ple_norm"])
    tg = mm_nn("ple_gate", xn4, _rows(got["ple_w_gate"]), F32)
    e = mm_nn_colsharded("ple_proj", p_l, got["ple_w_proj"], F32)
    h4 = _rowwise("ple_add", lambda a, b, c: a + jax.nn.sigmoid(b) * c, n, 512, [(h3, d, 0), (tg, d, 0), (e, d, 0)], [], [(d, F32)])[0]
    saved = dict(h=h, h1=h1, h2=h2, h3=h3, ffn1=ffn1_saved, ffn2=ffn2_saved, xn2=xn2, z=z, s=s, y0=y0, y1=y1, t=t, m=m, q=q,
                 xn4=xn4, tg=tg, e=e, pw_re=pw_re, pw_im=pw_im, bbd=bbd, cbd=cbd)
    return h4, saved, got, next_ffn1_wi


def _layer_bwd(d, lw, p_l, sv, seq, bufs, layer):
    n, dm = d.shape
    g = {}
    def ple_bwd(dd, tg, e):
        gate = jax.nn.sigmoid(tg)
        return dd * e * gate * (1.0 - gate), dd * gate

    dtg, de = _rowwise("ple_dadd", ple_bwd, n, 512, [(d, dm, 0), (sv["tg"], dm, 0), (sv["e"], dm, 0)], [], [(dm, BF16), (dm, BF16)])
    g["ple_w_proj"] = mm_tn_colsharded("ple_dproj", p_l, de, F32, into=(bufs["ple_w_proj"], layer))
    g["ple_w_gate"] = mm_tn("ple_dgate_w", sv["xn4"], dtg, F32, into=(bufs["ple_w_gate"], layer))
    dxn4 = mm_nn("ple_dgate_x", dtg, lw["ple_w_gate_t"], F32)
    d, dg = norm_bwd("ple_dnorm", dxn4, sv["h3"], lw["ple_norm"], d)
    g["ple_norm"] = dg.reshape(dm)

    d, g["ffn2_norm"], g["ffn2_wi"], g["ffn2_wo"] = _ffn_bwd("ffn2b", d, sv["h2"], lw["ffn2_norm"], lw["ffn2_wi_t"], lw["ffn2_wo_t"], sv["ffn2"],
                                                            (bufs["ffn2_wi"], layer), (bufs["ffn2_wo"], layer))

    dmix = mm_nn("mix_dout_x", d, lw["w_out_t"], F32)
    g["w_out"] = mm_tn("mix_dout_w", sv["m"], d, F32, into=(bufs["w_out"], layer))
    pool_cb = SSM_WIDTH // POOL_CH
    wp_eff = lw["pool_w"] * lw["pool_scale"].reshape(POOL_GROUPS, 1, POOL_CH)
    dq = pool_mm_t("pool_dmm_x", dmix, pool_cb, wp_eff, F32)
    dwp_eff = pool_grad_w("pool_dmm_w", sv["q"], dmix, pool_cb)
    g["pool_w"] = dwp_eff * lw["pool_scale"].reshape(POOL_GROUPS, 1, POOL_CH)
    g["pool_scale"] = jnp.sum(dwp_eff * lw["pool_w"], axis=1).reshape(POOL_WIDTH)
    dzp = pool_window("pool_dwindow", dq, 0, seq, BF16, True)

    def glu_bwd(dy2, y1, t):
        sg = jax.nn.sigmoid(t)
        return dy2 * y1 * sg * (1.0 - sg), dy2 * sg

    dt_, dy1a = _rowwise("ssm_dglu", glu_bwd, n, 512, [(dmix, SSM_WIDTH, 0), (sv["y1"], SSM_WIDTH, 0), (sv["t"], SSM_WIDTH, 0)], [],
                         [(SSM_WIDTH, BF16), (SSM_WIDTH, F32)])
    g["ssm_w_glu"] = mm_tn("ssm_dglu_w", sv["y1"], dt_, F32, into=(bufs["ssm_w_glu"], layer))
    dy1b = mm_nn("ssm_dglu_x", dt_, lw["ssm_w_glu_t"], F32)

    def gelu_bwd(da, db, y0, zs, dvec):
        dy0 = (da + db) * _gelu_grad(y0)
        return dy0, dy0 * dvec, jnp.sum(dy0 * zs, axis=0, keepdims=True)

    dy0, dzs_a, dd = _rowwise("ssm_dgelu", gelu_bwd, n, 512,
                              [(dy1a, SSM_WIDTH, 0), (dy1b, SSM_WIDTH, 0), (sv["y0"], SSM_WIDTH, 0), (sv["z"], SSM_WIDTH, 0)],
                              [lw["ssm_d"].reshape(1, SSM_WIDTH)], [(SSM_WIDTH, F32), (SSM_WIDTH, F32)], [(1, SSM_WIDTH)])
    g["ssm_d"] = dd.reshape(SSM_WIDTH)
    g["ssm_c_re"], g["ssm_c_im"] = compact_c(ssm_grad_c("ssm_dc", sv["s"], dy0))
    v = ssm_out_t("ssm_dout", dy0, sv["cbd"])
    a, dlam = ssm_scan("ssm_scan_adj", scan_coefficients(sv["pw_re"], sv["pw_im"], True), v, seq, True, states=sv["s"])
    dbb_re, dbb_im = compact_b(ssm_grad_b("ssm_db", sv["z"], a))
    dzs = ssm_in_t("ssm_din", a, sv["bbd"], dzs_a)
    dlam = jnp.sum(dlam, axis=1)
    log_dt = jnp.repeat(lw["ssm_log_dt"], SSM_STATE)
    b_re, b_im = lw["ssm_b_re"].reshape(SSM_COLS, SSM_CH), lw["ssm_b_im"].reshape(SSM_COLS, SSM_CH)
    glr, gli, gld, gbr, gbi = ssm_prep_bwd("ssm_prep_bwd", _col(lw["ssm_lambda_re"]), _col(lw["ssm_lambda_im"]), _col(log_dt), b_re, b_im,
                                           _col(dlam[0]), _col(dlam[1]), dbb_re, dbb_im)
    g["ssm_lambda_re"] = glr.reshape(SSM_GROUPS, SSM_STATE)
    g["ssm_lambda_im"] = gli.reshape(SSM_GROUPS, SSM_STATE)
    g["ssm_log_dt"] = jnp.sum(gld.reshape(SSM_GROUPS, SSM_STATE), axis=1)
    g["ssm_b_re"] = gbr.reshape(SSM_GROUPS, SSM_STATE, SSM_CH)
    g["ssm_b_im"] = gbi.reshape(SSM_GROUPS, SSM_STATE, SSM_CH)

    dz = jnp.concatenate([dzs.astype(BF16), dzp], axis=1)
    g["w_in"] = mm_tn("mix_din_w", sv["xn2"], dz, F32, into=(bufs["w_in"], layer))
    dxn2 = mm_nn("mix_din_x", dz, lw["w_in_t"], F32)
    d, dg = norm_bwd("mix_dnorm", dxn2, sv["h1"], lw["mix_norm"], d)
    g["mix_norm"] = dg.reshape(dm)

    d, g["ffn1_norm"], g["ffn1_wi"], g["ffn1_wo"] = _ffn_bwd("ffn1b", d, sv["h"], lw["ffn1_norm"], lw["ffn1_wi_t"], lw["ffn1_wo_t"], sv["ffn1"],
                                                            (bufs["ffn1_wi"], layer), (bufs["ffn1_wo"], layer))
    return d, g


def _flatten_small(tensors):
    flat = jnp.concatenate([t.reshape(-1) for t in tensors])
    pad = (-flat.shape[0]) % (SUBLANE * LANE)
    return jnp.pad(flat, (0, pad)).reshape(-1, LANE)


def _unflatten_small(flat, like):
    flat = flat.reshape(-1)
    out, off = [], 0
    for t in like:
        out.append(flat[off:off + t.size].reshape(t.shape))
        off += t.size
    return out


def kernel(x, p, ffn1_norm, ffn1_wi, ffn1_wo, mix_norm, w_in, ssm_lambda_re, ssm_lambda_im, ssm_log_dt, ssm_b_re, ssm_b_im, ssm_c_re, ssm_c_im, ssm_d, ssm_w_glu, pool_w, pool_scale, w_out, ffn2_norm, ffn2_wi, ffn2_wo, ple_norm, ple_w_gate, ple_w_proj, final_norm, loss_target, m_ffn1_norm, m_ffn1_wi, m_ffn1_wo, m_mix_norm, m_w_in, m_ssm_lambda_re, m_ssm_lambda_im, m_ssm_log_dt, m_ssm_b_re, m_ssm_b_im, m_ssm_c_re, m_ssm_c_im, m_ssm_d, m_ssm_w_glu, m_pool_w, m_pool_scale, m_w_out, m_ffn2_norm, m_ffn2_wi, m_ffn2_wo, m_ple_norm, m_ple_w_gate, m_ple_w_proj, m_final_norm, v_ffn1_norm, v_ffn1_wi, v_ffn1_wo, v_mix_norm, v_w_in, v_ssm_lambda_re, v_ssm_lambda_im, v_ssm_log_dt, v_ssm_b_re, v_ssm_b_im, v_ssm_c_re, v_ssm_c_im, v_ssm_d, v_ssm_w_glu, v_pool_w, v_pool_scale, v_w_out, v_ffn2_norm, v_ffn2_wi, v_ffn2_wo, v_ple_norm, v_ple_w_gate, v_ple_w_proj, v_final_norm):
    given = dict(locals())
    w = {k: given[k] for k in WEIGHTS}
    mom = {k: given["m_" + k] for k in WEIGHTS}
    var = {k: given["v_" + k] for k in WEIGHTS}
    bsz, seq, dm = x.shape
    n = bsz * seq
    depth = ffn1_wi.shape[0]

    shards = {k: w[k].astype(MXU_DTYPE) for k in BIG}

    def layer_weights(l, gathered):
        lw = {k: w[k][l] for k in SMALL if k != "final_norm"}
        for k, a in gathered.items():
            if k in COL_SHARDED:
                lw[k] = a
                if k != "ple_w_proj":
                    lw[k + "_t"] = jnp.swapaxes(a, 1, 2)
            else:
                lw[k] = a.reshape(N_CHIPS * a.shape[1], a.shape[2])
                lw[k + "_t"] = lw[k].T
        return lw

    ffn1_wi_l = gather_weight("gather_ffn1_wi", shards["ffn1_wi"][:1])[0]

    p2 = p.reshape(depth, n, p.shape[-1])
    h_last = x.reshape(n, dm)
    saved, layers = [], []
    for l in range(depth):
        small = {k: w[k][l] for k in SMALL if k != "final_norm"}
        next_wi = (shards["ffn1_wi"], l + 1) if l + 1 < depth else None
        h_last, sv, gathered, ffn1_wi_l = _layer_fwd(h_last, small, ffn1_wi_l, functools.partial(lambda k, l: (shards[k], l), l=l),
                                                      next_wi, p2[l], seq)
        layers.append(layer_weights(l, gathered))
        saved.append(sv)

    def head(hh, tgt, gf):
        r = lax.rsqrt(jnp.mean(hh * hh, axis=-1, keepdims=True) + NORM_EPS)
        xh = hh * r
        diff = xh * gf - tgt
        dy = diff * (1.0 / dm)
        dxh = dy * gf
        dx = r * (dxh - xh * jnp.mean(dxh * xh, axis=-1, keepdims=True))
        return dx, jnp.sum(diff * diff, axis=0, keepdims=True) * (0.5 / dm), jnp.sum(dy * xh, axis=0, keepdims=True)

    d_last, loss_cols, g_final = _rowwise("loss_head", head, n, 512, [(h_last, dm, 0), (loss_target.reshape(n, dm), dm, 0)],
                                          [final_norm.reshape(1, dm)], [(dm, F32)], [(1, dm), (1, dm)])
    loss = lax.psum(jnp.sum(loss_cols), ("x", "y", "c"))

    d_x = d_last
    layer_grads = [None] * depth
    bufs = {k: lax.empty((depth, N_CHIPS) + w[k].shape[1:], F32) for k in BIG}
    for l in reversed(range(depth)):
        d_x, layer_grads[l] = _layer_bwd(d_x, layers[l], p2[l], saved[l], seq, bufs, l)
        bufs = {k: layer_grads[l][k] for k in BIG}
    grads = {k: jnp.stack([g[k] for g in layer_grads]) for k in layer_grads[0] if k not in BIG}
    grads.update(bufs)
    grad_x = d_x.reshape(bsz, seq, dm)

    big_sum = [reduce_scatter_streamed("rs_" + k, grads[k]) for k in BIG]
    small_keys = [k for k in SMALL]
    small_parts = [grads[k] if k != "final_norm" else g_final.reshape(dm) for k in small_keys]
    small_sum = _unflatten_small(all_reduce_small(_flatten_small(small_parts)), small_parts)
    g_out = dict(zip(BIG, big_sum))
    g_out.update(dict(zip(small_keys, small_sum)))
    for k in BIG:
        g_out[k] = g_out[k].reshape(w[k].shape)

    delta, new_m, new_v = {}, {}, {}
    for k in BIG:
        delta[k], new_m[k], new_v[k] = adamw("adamw_" + k, w[k], g_out[k], mom[k], var[k])
    sw = adamw("adamw_small", *[_flatten_small([t[k] for k in small_keys]) for t in (w, g_out, mom, var)])
    for name, flat in zip((delta, new_m, new_v), sw):
        for k, t in zip(small_keys, _unflatten_small(flat, [w[k] for k in small_keys])):
            name[k] = t

    return (loss, grad_x, *[g_out[k] for k in WEIGHTS], *[delta[k] for k in WEIGHTS],
            *[new_m[k] for k in WEIGHTS], *[new_v[k] for k in WEIGHTS])
```

```python
import functools
import math

import jax
import jax.numpy as jnp
from jax import lax
from jax.experimental import pallas as pl
from jax.experimental.pallas import tpu as pltpu

F32 = jnp.float32
BF16 = jnp.bfloat16
MXU_DTYPE = jnp.bfloat16
GRAD_WIRE_DTYPE = jnp.bfloat16
STATE_DTYPE = jnp.bfloat16
VMEM_LIMIT = 56 * 1024 * 1024
LANE = 128
SUBLANE = 8

N_CHIPS = 4
SSM_GROUPS = 32
SSM_STATE = 64
SSM_CH = 16
SSM_WIDTH = SSM_GROUPS * SSM_CH
SSM_COLS = SSM_GROUPS * SSM_STATE
POOL_GROUPS = 4
POOL_CH = 128
POOL_WIDTH = POOL_GROUPS * POOL_CH
SSM_TILE = 256
NORM_EPS = 1e-6
ADAM_LR = 0.001
ADAM_B1 = 0.9
ADAM_B2 = 0.999
ADAM_EPS = 1e-08
ADAM_WD = 0.01
ADAM_STEP = 10
MESH_ID = pl.DeviceIdType.MESH

BIG = ("ffn1_wi", "ffn1_wo", "w_in", "ssm_w_glu", "w_out", "ffn2_wi", "ffn2_wo", "ple_w_gate", "ple_w_proj")
COL_SHARDED = ("ffn1_wi", "ffn2_wi", "ple_w_proj")
SMALL = ("ffn1_norm", "mix_norm", "ssm_lambda_re", "ssm_lambda_im", "ssm_log_dt", "ssm_b_re", "ssm_b_im",
         "ssm_c_re", "ssm_c_im", "ssm_d", "pool_w", "pool_scale", "ffn2_norm", "ple_norm", "final_norm")
WEIGHTS = ("ffn1_norm", "ffn1_wi", "ffn1_wo", "mix_norm", "w_in", "ssm_lambda_re", "ssm_lambda_im", "ssm_log_dt",
           "ssm_b_re", "ssm_b_im", "ssm_c_re", "ssm_c_im", "ssm_d", "ssm_w_glu", "pool_w", "pool_scale", "w_out",
           "ffn2_norm", "ffn2_wi", "ffn2_wo", "ple_norm", "ple_w_gate", "ple_w_proj", "final_norm")


def _tile(dim, target):
    best = None
    t = LANE
    while t <= min(dim, target):
        if dim % t == 0:
            best = t
        t += LANE
    return best if best is not None else dim


def _params(sem):
    return pltpu.CompilerParams(dimension_semantics=sem, vmem_limit_bytes=VMEM_LIMIT)


CHIPS = ((1, 0, 0), (0, 1, 0), (1, 1, 0))


def _ride_scratch(n):
    return [pltpu.SemaphoreType.DMA((3 * n,)), pltpu.SemaphoreType.DMA((3 * n,)), pltpu.SemaphoreType.DMA((n,))]


def _ride_copies(layers, ins, outs, send_sems, recv_sems, local_sems):
    x, y, c = lax.axis_index("x"), lax.axis_index("y"), lax.axis_index("c")
    me = 2 * x + y
    copies = []
    for i in range(len(ins)):
        src = ins[i].at[pl.ds(layers[i], 1)]
        dst = outs[i].at[pl.ds(me, 1)]
        copies.append(pltpu.make_async_copy(src, dst, local_sems.at[i]))
        for f, (fx, fy, _) in enumerate(CHIPS):
            peer = ((1 - x) if fx else x, (1 - y) if fy else y, c)
            copies.append(pltpu.make_async_remote_copy(
                src_ref=src, dst_ref=dst, send_sem=send_sems.at[3 * i + f], recv_sem=recv_sems.at[3 * i + f],
                device_id=peer, device_id_type=pl.DeviceIdType.MESH))
    return copies


def _ride(grid, layers, ins, outs, sems):
    if not ins:
        return
    ids = [pl.program_id(a) for a in range(len(grid))]
    first = functools.reduce(jnp.logical_and, [i == 0 for i in ids])
    last = functools.reduce(jnp.logical_and, [i == g - 1 for i, g in zip(ids, grid)])

    @pl.when(first)
    def _():
        for cp in _ride_copies(layers, ins, outs, *sems):
            cp.start()

    @pl.when(last)
    def _():
        for cp in _ride_copies(layers, ins, outs, *sems):
            cp.wait()


def _ride_out_shapes(riders):
    return [jax.ShapeDtypeStruct((N_CHIPS,) + r.shape[1:], r.dtype) for r, _ in riders]


def _mm(name, a, b, out_shape, out_dtype, grid, a_spec, b_spec, o_spec, contract, alpha=1.0, res=None, riders=(), into=None):
    n_k = grid[2]
    acc_shape = tuple(d for d in o_spec.block_shape if d is not None)
    n_in = 2 + (res is not None) + (into is not None)
    n_r = len(riders)
    ride_layers = [l for _, l in riders]

    def body(*refs):
        a_ref, b_ref = refs[0], refs[1]
        r_ref = refs[2] if res is not None else None
        o_ref = refs[n_in + n_r]
        if n_r:
            _ride(grid, ride_layers, refs[n_in:n_in + n_r], refs[n_in + n_r + 1:n_in + 2 * n_r + 1], refs[len(refs) - 3:])

        def product():
            return lax.dot_general(a_ref[...].astype(MXU_DTYPE), b_ref[...].astype(MXU_DTYPE),
                                   (contract, ((), ())), preferred_element_type=F32)

        def finish(v):
            if alpha != 1.0:
                v = v * alpha
            if r_ref is not None:
                v = v + r_ref[...].astype(F32)
            o_ref[...] = v.astype(out_dtype)

        if n_k == 1:
            finish(product())
            return
        acc = refs[n_in + 2 * n_r + 1]
        k = pl.program_id(2)

        @pl.when(k == 0)
        def _():
            acc[...] = product()

        @pl.when(k > 0)
        def _():
            acc[...] += product()

        @pl.when(k == n_k - 1)
        def _():
            finish(acc[...])

    in_specs = [a_spec, b_spec]
    operands = [a, b]
    if res is not None:
        in_specs.append(o_spec)
        operands.append(res)
    scratch = [pltpu.VMEM(acc_shape, F32)] if n_k > 1 else []
    any_spec = pl.BlockSpec(memory_space=pl.ANY)
    if into is not None:
        buf, layer = into
        assert buf.shape[1:] == tuple(out_shape) and buf.dtype == out_dtype and not n_r
        placed = pl.BlockSpec((None,) + tuple(o_spec.block_shape),
                              functools.partial(lambda i, j, kk, m, l: (l,) + tuple(m(i, j, kk)), m=o_spec.index_map, l=layer))
        return pl.pallas_call(
            body, name=name, grid=grid, in_specs=in_specs + [any_spec], out_specs=placed,
            out_shape=jax.ShapeDtypeStruct(buf.shape, out_dtype), scratch_shapes=scratch,
            input_output_aliases={len(operands): 0},
            compiler_params=_params(("parallel", "parallel", "arbitrary")),
        )(*operands, buf)
    if not n_r:
        return pl.pallas_call(
            body, name=name, grid=grid, in_specs=in_specs, out_specs=o_spec,
            out_shape=jax.ShapeDtypeStruct(out_shape, out_dtype), scratch_shapes=scratch,
            compiler_params=_params(("parallel", "parallel", "arbitrary")),
        )(*operands)
    return pl.pallas_call(
        body, name=name, grid=grid, in_specs=in_specs + [any_spec] * n_r, out_specs=[o_spec] + [any_spec] * n_r,
        out_shape=[jax.ShapeDtypeStruct(out_shape, out_dtype)] + _ride_out_shapes(riders),
        scratch_shapes=scratch + _ride_scratch(n_r),
        compiler_params=_params(("arbitrary", "arbitrary", "arbitrary")),
    )(*operands, *[r for r, _ in riders])


NN = ((1,), (0,))
NT = ((1,), (1,))
TN = ((0,), (0,))


def mm_nn(name, a, b, out_dtype, alpha=1.0, res=None, tm=1024, tn=1024, tk=1024, riders=()):
    m, k = a.shape
    n = b.shape[1]
    tm, tn, tk = _tile(m, tm), _tile(n, tn), _tile(k, tk)
    return _mm(name, a, b, (m, n), out_dtype, (m // tm, n // tn, k // tk),
               pl.BlockSpec((tm, tk), lambda i, j, kk: (i, kk)),
               pl.BlockSpec((tk, tn), lambda i, j, kk: (kk, j)),
               pl.BlockSpec((tm, tn), lambda i, j, kk: (i, j)), NN, alpha, res, riders)


def mm_tn(name, a, b, out_dtype, alpha=1.0, tm=1024, tn=1024, tk=1024, into=None):
    k, m = a.shape
    n = b.shape[1]
    tm, tn, tk = _tile(m, tm), _tile(n, tn), _tile(k, tk)
    if into is not None:
        shape = into[0].shape
        into = (into[0].reshape(shape[0], m, n), into[1])
    out = _mm(name, a, b, (m, n), out_dtype, (m // tm, n // tn, k // tk),
              pl.BlockSpec((tk, tm), lambda i, j, kk: (kk, i)),
              pl.BlockSpec((tk, tn), lambda i, j, kk: (kk, j)),
              pl.BlockSpec((tm, tn), lambda i, j, kk: (i, j)), TN, alpha, into=into)
    return out if into is None else out.reshape(shape)


def mm_nn_colsharded(name, a, w, out_dtype, tm=1024, tk=1024, riders=()):
    m, k = a.shape
    c = w.shape[2]
    tm, tk = _tile(m, tm), _tile(k, tk)
    return _mm(name, a, w, (m, N_CHIPS * c), out_dtype, (m // tm, N_CHIPS, k // tk),
               pl.BlockSpec((tm, tk), lambda i, j, kk: (i, kk)),
               pl.BlockSpec((None, tk, c), lambda i, j, kk: (j, kk, 0)),
               pl.BlockSpec((tm, c), lambda i, j, kk: (i, j)), NN, riders=riders)


def mm_tn_colsharded(name, a, b, out_dtype, tm=1024, tk=1024, into=None):
    t, k = a.shape
    c = b.shape[1] // N_CHIPS
    tm, tk = _tile(k, tm), _tile(t, tk)
    return _mm(name, a, b, (N_CHIPS, k, c), out_dtype, (k // tm, N_CHIPS, t // tk),
               pl.BlockSpec((tk, tm), lambda i, j, kk: (kk, i)),
               pl.BlockSpec((tk, c), lambda i, j, kk: (kk, j)),
               pl.BlockSpec((None, tm, c), lambda i, j, kk: (j, i, 0)), TN, into=into)


def _rowwise(name, fn, n_rows, tm, row_ins, bcast_ins, outs, accs=()):
    tm = min(tm, n_rows)
    grid = (n_rows // tm,)
    n_row, n_b, n_out = len(row_ins), len(bcast_ins), len(outs)

    def body(*refs):
        ins = [r[...] for r in refs[:n_row + n_b]]
        out_refs = refs[n_row + n_b:n_row + n_b + n_out]
        acc_refs = refs[n_row + n_b + n_out:]
        res = fn(*ins)
        if not isinstance(res, (tuple, list)):
            res = (res,)
        for o_ref, v in zip(out_refs, res[:n_out]):
            o_ref[...] = v.astype(o_ref.dtype)
        if acc_refs:
            @pl.when(pl.program_id(0) == 0)
            def _():
                for a_ref in acc_refs:
                    a_ref[...] = jnp.zeros_like(a_ref)
            for a_ref, v in zip(acc_refs, res[n_out:]):
                a_ref[...] += v

    in_specs, operands = [], []
    for spec in row_ins:
        arr, width, cb = spec[0], spec[1], spec[2]
        rb = spec[3] if len(spec) > 3 else 0
        in_specs.append(pl.BlockSpec((tm, width), functools.partial(lambda i, cb, rb: (i + rb, cb), cb=cb, rb=rb)))
        operands.append(arr)
    for arr in bcast_ins:
        in_specs.append(pl.BlockSpec(arr.shape, functools.partial(lambda i, nd: (0,) * nd, nd=arr.ndim)))
        operands.append(arr)
    out_specs = [pl.BlockSpec((tm, w), lambda i: (i, 0)) for w, _ in outs]
    out_specs += [pl.BlockSpec((r, w), lambda i: (0, 0)) for r, w in accs]
    out_shape = [jax.ShapeDtypeStruct((n_rows, w), dt) for w, dt in outs]
    out_shape += [jax.ShapeDtypeStruct((r, w), F32) for r, w in accs]
    res = pl.pallas_call(
        body, name=name, grid=grid, in_specs=in_specs, out_specs=out_specs, out_shape=out_shape,
        compiler_params=_params(("arbitrary",) if accs else ("parallel",)),
    )(*operands)
    return res


def _rms(x, g):
    r = lax.rsqrt(jnp.mean(x * x, axis=-1, keepdims=True) + NORM_EPS)
    return x * r * g


def _rms_bwd(dy, x, g):
    r = lax.rsqrt(jnp.mean(x * x, axis=-1, keepdims=True) + NORM_EPS)
    xh = x * r
    dxh = dy * g
    dx = r * (dxh - xh * jnp.mean(dxh * xh, axis=-1, keepdims=True))
    return dx, jnp.sum(dy * xh, axis=0, keepdims=True)


def norm_fwd(name, h, g):
    n, d = h.shape
    return _rowwise(name, lambda x, gg: _rms(x, gg), n, 512, [(h, d, 0)], [g.reshape(1, d)], [(d, BF16)])[0]


def norm_bwd(name, dxn, h, g, d_res):
    n, d = h.shape

    def fn(dy, x, dr, gg):
        dx, dg = _rms_bwd(dy, x, gg)
        return dr + dx, dg

    return _rowwise(name, fn, n, 512, [(dxn, d, 0), (h, d, 0), (d_res, d, 0)], [g.reshape(1, d)], [(d, F32)], [(1, d)])


_GELU_C = math.sqrt(2.0 / math.pi)


def _gelu(x):
    return 0.5 * x * (1.0 + jnp.tanh(_GELU_C * (x + 0.044715 * (x * x * x))))


def _gelu_grad(x):
    th = jnp.tanh(_GELU_C * (x + 0.044715 * (x * x * x)))
    return 0.5 * (1.0 + th) + 0.5 * x * (1.0 - th * th) * (_GELU_C * (1.0 + 3.0 * 0.044715 * (x * x)))


def _ssm_discretize(lam_re, lam_im, log_dt, b_re, b_im):
    dt = jnp.exp(log_dt)
    e = jnp.exp(lam_re * dt)
    lb_re = e * jnp.cos(lam_im * dt)
    lb_im = e * jnp.sin(lam_im * dt)
    nr, ni = lb_re - 1.0, lb_im
    den = lam_re * lam_re + lam_im * lam_im
    cr = (nr * lam_re + ni * lam_im) / den
    ci = (ni * lam_re - nr * lam_im) / den
    return lb_re, lb_im, cr * b_re - ci * b_im, cr * b_im + ci * b_re


def ssm_prep(name, lam_re, lam_im, log_dt, b_re, b_im):
    def body(lr, li, ld, br, bi, pr_ref, pi_ref, bbr_ref, bbi_ref):
        lb_re, lb_im, bb_re, bb_im = _ssm_discretize(lr[...], li[...], ld[...], br[...], bi[...])
        bbr_ref[...] = bb_re
        bbi_ref[...] = bb_im
        pr, pi = lb_re, lb_im
        cols_r, cols_i = [pr], [pi]
        for _ in range(SUBLANE - 1):
            pr, pi = pr * lb_re - pi * lb_im, pr * lb_im + pi * lb_re
            cols_r.append(pr)
            cols_i.append(pi)
        lane = lax.broadcasted_iota(jnp.int32, (SSM_COLS, SUBLANE), 1)
        out_r = jnp.zeros((SSM_COLS, SUBLANE), F32)
        out_i = jnp.zeros((SSM_COLS, SUBLANE), F32)
        for r in range(SUBLANE):
            out_r = jnp.where(lane == r, cols_r[r], out_r)
            out_i = jnp.where(lane == r, cols_i[r], out_i)
        pr_ref[...] = out_r
        pi_ref[...] = out_i

    shapes = [jax.ShapeDtypeStruct((SSM_COLS, SUBLANE), F32)] * 2 + [jax.ShapeDtypeStruct((SSM_COLS, SSM_CH), F32)] * 2
    return pl.pallas_call(body, name=name, out_shape=shapes,
                          compiler_params=pltpu.CompilerParams(vmem_limit_bytes=VMEM_LIMIT))(lam_re, lam_im, log_dt, b_re, b_im)


def ssm_prep_bwd(name, lam_re, lam_im, log_dt, b_re, b_im, d_lb_re, d_lb_im, d_bb_re, d_bb_im):
    def body(lr, li, ld, br, bi, g0, g1, g2, g3, o0, o1, o2, o3, o4):
        _, vjp = jax.vjp(_ssm_discretize, lr[...], li[...], ld[...], br[...], bi[...])
        res = vjp((g0[...], g1[...], g2[...], g3[...]))
        for o, v in zip((o0, o1, o2, o3, o4), res):
            o[...] = v

    col = jax.ShapeDtypeStruct((SSM_COLS, 1), F32)
    mat = jax.ShapeDtypeStruct((SSM_COLS, SSM_CH), F32)
    return pl.pallas_call(body, name=name, out_shape=[col, col, col, mat, mat],
                          compiler_params=pltpu.CompilerParams(vmem_limit_bytes=VMEM_LIMIT))(
        lam_re, lam_im, log_dt, b_re, b_im, d_lb_re, d_lb_im, d_bb_re, d_bb_im)


def scan_coefficients(pw_re, pw_im, reverse):
    pr, pi = pw_re.T, pw_im.T
    if reverse:
        pi = -pi
    row = jnp.arange(SUBLANE)[:, None]
    out = []
    for d in (1, 2, 4):
        valid = (row < SUBLANE - d) if reverse else (row >= d)
        out.append(jnp.where(valid, pr[d - 1][None, :], 0.0))
        out.append(jnp.where(valid, pi[d - 1][None, :], 0.0))
    out.append(pr[::-1] if reverse else pr)
    out.append(pi[::-1] if reverse else pi)
    return jnp.stack(out)


def ssm_scan(name, coef, x, seq, reverse, states=None, riders=()):
    n = x.shape[1]
    n_seq = n // seq
    cw = LANE
    n_cb = SSM_COLS // cw
    pair = 2 * SUBLANE
    n_pairs = seq // pair
    pairs_per_step = 2 if n_pairs % 2 == 0 else 1
    with_dlam = states is not None
    n_r = len(riders)
    assert not (n_r and with_dlam)

    def body(*refs):
        if with_dlam:
            coef_ref, x_ref, s_ref, o_ref, dl_ref = refs
        else:
            coef_ref, x_ref, o_ref = refs[0], refs[1], refs[2 + n_r]
            if n_r:
                _ride((n_cb, n_seq), [l for _, l in riders], refs[2:2 + n_r], refs[3 + n_r:3 + 2 * n_r], refs[3 + 2 * n_r:])
        c = [coef_ref[i] for i in range(8)]
        row16 = lax.broadcasted_iota(jnp.int32, (pair, cw), 0)
        zero = jnp.zeros((SUBLANE, cw), F32)

        edge = 0 if reverse else SUBLANE - 1

        def bcast_row(v, r, rows=SUBLANE):
            return jnp.broadcast_to(v[r:r + 1, :], (rows, cw))

        p8r, p8i = bcast_row(c[6], edge), bcast_row(c[7], edge)

        def local_scan(xr, xi):
            for si, d in enumerate((1, 2, 4)):
                sh = (SUBLANE - d) if reverse else d
                sr, sm = pltpu.roll(xr, sh, 0), pltpu.roll(xi, sh, 0)
                lre, lim = c[2 * si], c[2 * si + 1]
                xr, xi = xr + lre * sr - lim * sm, xi + lre * sm + lim * sr
            return xr, xi

        def step(it, carry):
            work = []
            for u in range(pairs_per_step):
                k = it * pairs_per_step + u
                pidx = (n_pairs - 1 - k) if reverse else k
                off = pl.multiple_of(pidx * pair, pair)
                xr16 = x_ref[0, pl.ds(off, pair), :].astype(F32)
                xi16 = x_ref[1, pl.ds(off, pair), :].astype(F32)
                halves = (1, 0) if reverse else (0, 1)
                tiles = {h: local_scan(xr16[h * SUBLANE:(h + 1) * SUBLANE], xi16[h * SUBLANE:(h + 1) * SUBLANE]) for h in halves}
                work.append((pidx, off, halves, tiles))
            cre, cim = carry[0], carry[1]
            acc = carry[2:]
            for pidx, off, halves, tiles in work:
                done = {}
                for h in halves:
                    lr, li = tiles[h]
                    done[h] = (lr + c[6] * cre - c[7] * cim, li + c[6] * cim + c[7] * cre)
                    cre, cim = (bcast_row(lr, edge) + p8r * cre - p8i * cim, bcast_row(li, edge) + p8r * cim + p8i * cre)
                or16 = jnp.concatenate([done[0][0], done[1][0]], axis=0)
                oi16 = jnp.concatenate([done[0][1], done[1][1]], axis=0)
                o_ref[0, pl.ds(off, pair), :] = or16.astype(o_ref.dtype)
                o_ref[1, pl.ds(off, pair), :] = oi16.astype(o_ref.dtype)
                if with_dlam:
                    poff = pl.multiple_of(jnp.maximum(pidx - 1, 0) * pair, pair)
                    first = pidx > 0
                    sr16 = s_ref[0, pl.ds(off, pair), :].astype(F32)
                    si16 = s_ref[1, pl.ds(off, pair), :].astype(F32)
                    pr_last = jnp.where(first, bcast_row(s_ref[0, pl.ds(poff, pair), :].astype(F32), pair - 1, pair), 0.0)
                    pi_last = jnp.where(first, bcast_row(s_ref[1, pl.ds(poff, pair), :].astype(F32), pair - 1, pair), 0.0)
                    spr = jnp.where(row16 == 0, pr_last, pltpu.roll(sr16, 1, 0))
                    spi = jnp.where(row16 == 0, pi_last, pltpu.roll(si16, 1, 0))
                    dre = or16 * spr + oi16 * spi
                    dim = oi16 * spr - or16 * spi
                    acc = (acc[0] + dre[:SUBLANE] + dre[SUBLANE:], acc[1] + dim[:SUBLANE] + dim[SUBLANE:])
            return (cre, cim) + tuple(acc)

        init = (zero, zero, zero, zero) if with_dlam else (zero, zero)
        fin = lax.fori_loop(0, n_pairs // pairs_per_step, step, init)
        if with_dlam:
            @pl.when(pl.program_id(1) == 0)
            def _():
                dl_ref[...] = jnp.zeros_like(dl_ref)
            dl_ref[0] += fin[2]
            dl_ref[1] += fin[3]

    blk = pl.BlockSpec((2, seq, cw), lambda j, b: (0, b, j))
    in_specs = [pl.BlockSpec((8, SUBLANE, cw), lambda j, b: (0, 0, j)), blk]
    operands = [coef, x]
    out_specs = [blk]
    out_shape = [jax.ShapeDtypeStruct(x.shape, STATE_DTYPE)]
    if with_dlam:
        in_specs.append(blk)
        operands.append(states)
        out_specs.append(pl.BlockSpec((2, SUBLANE, cw), lambda j, b: (0, 0, j)))
        out_shape.append(jax.ShapeDtypeStruct((2, SUBLANE, SSM_COLS), F32))
    scratch = []
    if n_r:
        any_spec = pl.BlockSpec(memory_space=pl.ANY)
        in_specs += [any_spec] * n_r
        operands += [r for r, _ in riders]
        out_specs += [any_spec] * n_r
        out_shape += _ride_out_shapes(riders)
        scratch = _ride_scratch(n_r)
    res = pl.pallas_call(
        body, name=name, grid=(n_cb, n_seq), in_specs=in_specs, out_specs=out_specs, out_shape=out_shape,
        scratch_shapes=scratch, compiler_params=_params(("arbitrary" if n_r else "parallel", "arbitrary")),
    )(*operands)
    return res if (with_dlam or n_r) else res[0]


SSM_WIDE = 4 * SSM_TILE


def ssm_in(name, z, bbd, tm=1024, riders=()):
    n = z.shape[0]
    tm = _tile(n, tm)
    t, w = SSM_TILE, SSM_WIDE
    return _mm(name, z, bbd, (2, n, SSM_COLS), STATE_DTYPE, (n // tm, 2 * SSM_COLS // w, 1),
               pl.BlockSpec((tm, t), lambda i, j, kk: (i, j % 2)),
               pl.BlockSpec((t, w), lambda i, j, kk: (j % 2, j)),
               pl.BlockSpec((None, tm, w), lambda i, j, kk: (j // 2, i, j % 2)), NN, riders=riders)


def ssm_out(name, s, cbd, tm=1024, riders=()):
    n = s.shape[1]
    tm = _tile(n, tm)
    t = SSM_TILE
    w = SSM_WIDE
    return _mm(name, s, cbd, (n, SSM_WIDTH), F32, (n // tm, SSM_WIDTH // t, 2),
               pl.BlockSpec((None, tm, w), lambda i, j, kk: (kk, i, j)),
               pl.BlockSpec((w, t), lambda i, j, kk: (2 * kk + j, j)),
               pl.BlockSpec((tm, t), lambda i, j, kk: (i, j)), NN, riders=riders)


def ssm_out_t(name, dy, cbd, tm=1024):
    n = dy.shape[0]
    tm = _tile(n, tm)
    t, w = SSM_TILE, SSM_WIDE
    return _mm(name, dy, cbd, (2, n, SSM_COLS), STATE_DTYPE, (n // tm, 2 * SSM_COLS // w, 1),
               pl.BlockSpec((tm, t), lambda i, j, kk: (i, j % 2)),
               pl.BlockSpec((w, t), lambda i, j, kk: (j, j % 2)),
               pl.BlockSpec((None, tm, w), lambda i, j, kk: (j // 2, i, j % 2)), NT)


def ssm_in_t(name, a, bbd, res, tm=1024):
    n = a.shape[1]
    tm = _tile(n, tm)
    t, w = SSM_TILE, SSM_WIDE
    return _mm(name, a, bbd, (n, SSM_WIDTH), F32, (n // tm, SSM_WIDTH // t, 2),
               pl.BlockSpec((None, tm, w), lambda i, j, kk: (kk, i, j)),
               pl.BlockSpec((t, w), lambda i, j, kk: (j, 2 * kk + j)),
               pl.BlockSpec((tm, t), lambda i, j, kk: (i, j)), NT, 1.0, res)


def ssm_grad_c(name, s, dy, tk=1024):
    n = s.shape[1]
    tk = _tile(n, tk)
    t, w = SSM_TILE, SSM_WIDE
    return _mm(name, s, dy, (2 * SSM_COLS, t), F32, (2 * SSM_COLS // w, 1, n // tk),
               pl.BlockSpec((None, tk, w), lambda i, j, kk: (i // 2, kk, i % 2)),
               pl.BlockSpec((tk, t), lambda i, j, kk: (kk, i % 2)),
               pl.BlockSpec((w, t), lambda i, j, kk: (i, 0)), TN)


def ssm_grad_b(name, z, a, tk=1024):
    n = z.shape[0]
    tk = _tile(n, tk)
    t, w = SSM_TILE, SSM_WIDE
    return _mm(name, z, a, (t, 2 * SSM_COLS), F32, (1, 2 * SSM_COLS // w, n // tk),
               pl.BlockSpec((tk, t), lambda i, j, kk: (kk, j % 2)),
               pl.BlockSpec((None, tk, w), lambda i, j, kk: (j // 2, kk, j % 2)),
               pl.BlockSpec((t, w), lambda i, j, kk: (0, j)), TN)


_GROUP_TILE = SSM_TILE // SSM_CH


def expand_b(bb_re, bb_im):
    b = jnp.stack([bb_re, bb_im]).reshape(2, SSM_GROUPS, SSM_STATE, SSM_CH)
    eye = jnp.eye(SSM_GROUPS, dtype=F32)
    return jnp.einsum("rgph,gk->ghrkp", b, eye).reshape(SSM_WIDTH, 2 * SSM_COLS).astype(MXU_DTYPE)


def expand_c(c_re, c_im):
    c = jnp.stack([c_re, -c_im])
    eye = jnp.eye(SSM_GROUPS, dtype=F32)
    return jnp.einsum("rghp,gk->rgpkh", c, eye).reshape(2 * SSM_COLS, SSM_WIDTH).astype(MXU_DTYPE)


def _group_pick():
    return (jnp.arange(SSM_GROUPS)[:, None] % _GROUP_TILE == jnp.arange(_GROUP_TILE)[None, :]).astype(F32)


def compact_c(dc):
    x = dc.reshape(2, SSM_GROUPS, SSM_STATE, _GROUP_TILE, SSM_CH)
    g = jnp.einsum("rgpch,gc->rghp", x, _group_pick())
    return g[0], -g[1]


def compact_b(db):
    x = db.reshape(_GROUP_TILE, SSM_CH, 2, SSM_GROUPS, SSM_STATE)
    g = jnp.einsum("chrgp,gc->rgph", x, _group_pick()).reshape(2, SSM_COLS, SSM_CH)
    return g[0], g[1]


def pool_window(name, x, col_block0, seq, out_dtype, adjoint):
    n = x.shape[0]

    def body(x_ref, o_ref):
        win = 2 << pl.program_id(1)
        row = lax.broadcasted_iota(jnp.int32, (seq, POOL_CH), 0)
        v = x_ref[...].astype(F32)
        cnt = jnp.minimum(row + 1, win).astype(F32)
        s = v / cnt if adjoint else v
        for d in (1, 2, 4, 8):
            if adjoint:
                sh = jnp.where((row < seq - d) & (d < win), pltpu.roll(s, seq - d, 0), 0.0)
            else:
                sh = jnp.where((row >= d) & (d < win), pltpu.roll(s, d, 0), 0.0)
            s = s + sh
        o_ref[...] = ((s - v) if adjoint else (s / cnt - v)).astype(out_dtype)

    return pl.pallas_call(
        body, name=name, grid=(n // seq, POOL_GROUPS),
        in_specs=[pl.BlockSpec((seq, POOL_CH), lambda b, g: (b, col_block0 + g))],
        out_specs=pl.BlockSpec((seq, POOL_CH), lambda b, g: (b, g)),
        out_shape=jax.ShapeDtypeStruct((n, POOL_WIDTH), out_dtype),
        compiler_params=_params(("parallel", "parallel")),
    )(x)


def pool_mm(name, q, w, out_dtype, tm=1024):
    n = q.shape[0]
    tm = _tile(n, tm)
    return _mm(name, q, w, (n, POOL_WIDTH), out_dtype, (n // tm, POOL_GROUPS, 1),
               pl.BlockSpec((tm, POOL_CH), lambda i, j, kk: (i, j)),
               pl.BlockSpec((None, POOL_CH, POOL_CH), lambda i, j, kk: (j, 0, 0)),
               pl.BlockSpec((tm, POOL_CH), lambda i, j, kk: (i, j)), NN)


def pool_mm_t(name, dy, col_block0, w, out_dtype, tm=1024):
    n = dy.shape[0]
    tm = _tile(n, tm)
    return _mm(name, dy, w, (n, POOL_WIDTH), out_dtype, (n // tm, POOL_GROUPS, 1),
               pl.BlockSpec((tm, POOL_CH), lambda i, j, kk: (i, col_block0 + j)),
               pl.BlockSpec((None, POOL_CH, POOL_CH), lambda i, j, kk: (j, 0, 0)),
               pl.BlockSpec((tm, POOL_CH), lambda i, j, kk: (i, j)), NT)


def pool_grad_w(name, q, dy, col_block0, tk=1024):
    n = q.shape[0]
    tk = _tile(n, tk)
    return _mm(name, q, dy, (POOL_GROUPS, POOL_CH, POOL_CH), F32, (POOL_GROUPS, 1, n // tk),
               pl.BlockSpec((tk, POOL_CH), lambda i, j, kk: (kk, i)),
               pl.BlockSpec((tk, POOL_CH), lambda i, j, kk: (kk, col_block0 + i)),
               pl.BlockSpec((None, POOL_CH, POOL_CH), lambda i, j, kk: (i, 0, 0)), TN)


def _any_specs(n):
    return [pl.BlockSpec(memory_space=pl.ANY)] * n


def _place():
    x, y, c = lax.axis_index("x"), lax.axis_index("y"), lax.axis_index("c")
    return x, y, c


def sibling_swap(name, arrays):
    n = len(arrays)

    def body(*refs):
        ins, outs = refs[:n], refs[n:2 * n]
        send_sems, recv_sems = refs[2 * n:]
        x, y, c = _place()
        copies = []
        for i in range(n):
            away = pltpu.make_async_remote_copy(
                src_ref=ins[i], dst_ref=outs[i], send_sem=send_sems.at[i], recv_sem=recv_sems.at[i],
                device_id=(x, y, 1 - c), device_id_type=MESH_ID)
            away.start()
            copies.append(away)
        for cp in copies:
            cp.wait()

    return pl.pallas_call(
        body, name=name, in_specs=_any_specs(n), out_specs=_any_specs(n),
        out_shape=[jax.ShapeDtypeStruct(a.shape, a.dtype) for a in arrays],
        scratch_shapes=[pltpu.SemaphoreType.DMA((n,)), pltpu.SemaphoreType.DMA((n,))],
    )(*arrays)


def chip_gather(name, a):
    def body(in_ref, out_ref, send_sems, recv_sems, local_sems):
        copies = _ride_copies([0], [in_ref], [out_ref], send_sems, recv_sems, local_sems)
        for cp in copies:
            cp.start()
        for cp in copies:
            cp.wait()

    return pl.pallas_call(
        body, name=name, in_specs=_any_specs(1), out_specs=_any_specs(1)[0],
        out_shape=jax.ShapeDtypeStruct((N_CHIPS,) + a.shape[1:], a.dtype), scratch_shapes=_ride_scratch(1),
    )(a)


SIBLING = ((0, 0, 1),)
ICI_CHUNK_BYTES = 2 * 1024 * 1024
D2D_CHUNK_BYTES = 4 * 1024 * 1024


def _peer(flip):
    x, y, c = _place()
    return tuple((1 - v) if f else v for v, f in zip((x, y, c), flip))


def _core():
    return lax.axis_index("c")


def _chip():
    return 2 * lax.axis_index("x") + lax.axis_index("y")


def _linear_step(grid):
    i = pl.program_id(0)
    for a in range(1, len(grid)):
        i = i * grid[a] + pl.program_id(a)
    return i


def stream_reduce(name, x, grid, block, own_map, send_maps, flips, out_shape, out_block, out_map, wire_dtype=None):
    n_steps = math.prod(grid)
    n_p = len(flips)
    vm_block = block = tuple(1 if d is None else d for d in block)
    out_block = tuple(1 if d is None else d for d in out_block)
    staged = wire_dtype is not None and wire_dtype != x.dtype
    slot_dtype = wire_dtype if staged else x.dtype

    def body(own_ref, *rest):
        send_refs = rest[:n_p]
        o_ref, recv, send_sems, recv_sems, credits = rest[n_p:n_p + 5]
        stage = rest[n_p + 5] if staged else None
        i = _linear_step(grid)
        s = i % 2
        copies = []
        for j, flip in enumerate(flips):
            src = send_refs[j]
            if staged:
                stage[j, s] = send_refs[j][...].astype(wire_dtype)
                src = stage.at[j, s]

            @pl.when(i >= 2)
            def _():
                pl.semaphore_wait(credits.at[j, s], 1)
            cp = pltpu.make_async_remote_copy(
                src_ref=src, dst_ref=recv.at[j, s], send_sem=send_sems.at[j, s], recv_sem=recv_sems.at[j, s],
                device_id=_peer(flip), device_id_type=MESH_ID)
            cp.start()
            copies.append(cp)
        acc = own_ref[...]
        for j, cp in enumerate(copies):
            cp.wait_recv()
            acc = acc + recv[j, s].astype(acc.dtype)
        o_ref[...] = acc.reshape(o_ref.shape)
        for cp in copies:
            cp.wait_send()
        for j, flip in enumerate(flips):
            @pl.when(i < n_steps - 2)
            def _():
                pl.semaphore_signal(credits.at[j, s], inc=1, device_id=_peer(flip), device_id_type=MESH_ID)

    in_specs = [pl.BlockSpec(block, own_map)] + [pl.BlockSpec(block, m) for m in send_maps]
    scratch = [pltpu.VMEM((n_p, 2) + vm_block, slot_dtype), pltpu.SemaphoreType.DMA((n_p, 2)),
               pltpu.SemaphoreType.DMA((n_p, 2)), pltpu.SemaphoreType.REGULAR((n_p, 2))]
    if staged:
        scratch.append(pltpu.VMEM((n_p, 2) + vm_block, slot_dtype))
    return pl.pallas_call(
        body, name=name, grid=grid, in_specs=in_specs, out_specs=pl.BlockSpec(out_block, out_map),
        out_shape=jax.ShapeDtypeStruct(out_shape, x.dtype), scratch_shapes=scratch,
        compiler_params=_params(("arbitrary",) * len(grid)),
    )(*([x] * (1 + n_p)))


def stream_gather(name, x, grid, block, in_map, flips, out_shape, out_block, out_map):
    n_p = len(flips)
    assert grid[-1] == n_p + 1
    n_steps = math.prod(grid[:-1])
    vm_block = block = tuple(1 if d is None else d for d in block)
    out_block = tuple(1 if d is None else d for d in out_block)

    def body(x_ref, o_ref, recv, send_sems, recv_sems, credits):
        i = _linear_step(grid[:-1])
        q = pl.program_id(len(grid) - 1)
        s = i % 2

        def copy(j):
            return pltpu.make_async_remote_copy(
                src_ref=x_ref, dst_ref=recv.at[j, s], send_sem=send_sems.at[j, s], recv_sem=recv_sems.at[j, s],
                device_id=_peer(flips[j]), device_id_type=MESH_ID)

        @pl.when(q == 0)
        def _():
            for j in range(n_p):
                @pl.when(i >= 2)
                def _():
                    pl.semaphore_wait(credits.at[j, s], 1)
                copy(j).start()
            o_ref[...] = x_ref[...].reshape(o_ref.shape)
            for j in range(n_p):
                copy(j).wait_send()

        for j in range(n_p):
            @pl.when(q == j + 1)
            def _():
                copy(j).wait_recv()
                o_ref[...] = recv[j, s].reshape(o_ref.shape)

                @pl.when(i < n_steps - 2)
                def _():
                    pl.semaphore_signal(credits.at[j, s], inc=1, device_id=_peer(flips[j]), device_id_type=MESH_ID)

    return pl.pallas_call(
        body, name=name, grid=grid, in_specs=[pl.BlockSpec(block, in_map)], out_specs=pl.BlockSpec(out_block, out_map),
        out_shape=jax.ShapeDtypeStruct(out_shape, x.dtype),
        scratch_shapes=[pltpu.VMEM((n_p, 2) + vm_block, x.dtype), pltpu.SemaphoreType.DMA((n_p, 2)),
                        pltpu.SemaphoreType.DMA((n_p, 2)), pltpu.SemaphoreType.REGULAR((n_p, 2))],
        compiler_params=_params(("arbitrary",) * len(grid)),
    )(x)


def _chip_of_substep(q):
    mask = jnp.where(q == 1, 2, jnp.where(q == 2, 1, jnp.where(q == 3, 3, 0)))
    return jnp.bitwise_xor(_chip(), mask)


def gather_weight(name, shard):
    nl, r, c = shard.shape
    r2 = r // 2
    f32_per_elem = 4 // shard.dtype.itemsize
    tr = _rows_tile(r2, c, budget=ICI_CHUNK_BYTES * f32_per_elem, step=16)
    nb = r2 // tr
    half = stream_gather(
        name + "_chips", shard, (nl, nb, N_CHIPS), (None, tr, c), lambda l, i, q: (l, _core() * nb + i, 0), CHIPS,
        (nl, N_CHIPS, r2, c), (None, None, tr, c), lambda l, i, q: (l, _chip_of_substep(q), i, 0))
    trd = _rows_tile(r2, c, budget=D2D_CHUNK_BYTES * f32_per_elem, step=16)
    nbd = r2 // trd
    both = stream_gather(
        name + "_cores", half, (nl, N_CHIPS, nbd, 2), (None, None, trd, c), lambda l, k, i, q: (l, k, i, 0), SIBLING,
        (nl, N_CHIPS, 2, r2, c), (None, None, None, trd, c), lambda l, k, i, q: (l, k, _core() + q - 2 * _core() * q, i, 0))
    return both.reshape(nl, N_CHIPS, r, c)


def reduce_scatter_streamed(name, g):
    nl, _, r, c = g.shape
    r2 = r // 2
    tr_d2d = _rows_tile(r2, c, budget=D2D_CHUNK_BYTES, step=16)
    nbd = r2 // tr_d2d
    chip_sum = stream_reduce(
        name + "_cores", g, (nl, N_CHIPS, nbd), (None, None, tr_d2d, c),
        lambda l, k, i: (l, k, _core() * nbd + i, 0), [lambda l, k, i: (l, k, (1 - _core()) * nbd + i, 0)], SIBLING,
        (nl, N_CHIPS, r2, c), (None, None, tr_d2d, c), lambda l, k, i: (l, k, i, 0), wire_dtype=GRAD_WIRE_DTYPE)
    tr = _rows_tile(r2, c, budget=ICI_CHUNK_BYTES, step=16)
    nb = r2 // tr
    blk4 = (None, None, tr, c)
    masks = (2, 1, 3)
    mine = stream_reduce(
        name + "_chips", chip_sum, (nl, nb), blk4,
        lambda l, i: (l, _chip(), i, 0),
        [functools.partial(lambda l, i, m: (l, jnp.bitwise_xor(_chip(), m), i, 0), m=m) for m in masks], CHIPS,
        (nl, r2, c), (None, tr, c), lambda l, i: (l, i, 0), wire_dtype=GRAD_WIRE_DTYPE)
    both = stream_gather(
        name + "_join", mine, (nl, nbd, 2), (None, tr_d2d, c), lambda l, i, q: (l, i, 0), SIBLING,
        (nl, 2, r2, c), (None, None, tr_d2d, c), lambda l, i, q: (l, _core() + q - 2 * _core() * q, i, 0))
    return both.reshape(nl, r, c)


def add2(name, a, b):
    shape = a.shape
    a2, b2 = a.reshape(-1, shape[-1]), b.reshape(-1, shape[-1])
    rows, w = a2.shape
    tm = _rows_tile(rows, w)
    return _rowwise(name, lambda u, v: u + v, rows, tm, [(a2, w, 0), (b2, w, 0)], [], [(w, F32)])[0].reshape(shape)


def _rows_tile(rows, width, budget=2 * 1024 * 1024, step=SUBLANE):
    best = step
    t = step
    while t <= rows:
        if rows % t == 0 and t * width * 4 <= budget:
            best = t
        t += step
    return best


def all_reduce_small(flat):
    other = sibling_swap("ar_swap", [flat])[0]
    chip = add2("ar_add_cores", flat, other)
    slots = chip_gather("ar_chips", chip.reshape((1,) + chip.shape))
    rows = flat.shape[0]
    tm = _rows_tile(rows, LANE)
    nb = rows // tm
    s2 = slots.reshape(N_CHIPS * rows, LANE)
    return _rowwise("ar_sum_chips", lambda a, b, c, d: ((a + b) + c) + d, rows, tm,
                    [(s2, LANE, 0, k * nb) for k in range(N_CHIPS)], [], [(LANE, F32)])[0]


def _adamw_math(w, g, m, v):
    m = ADAM_B1 * m + (1.0 - ADAM_B1) * g
    v = ADAM_B2 * v + (1.0 - ADAM_B2) * (g * g)
    m_hat = m / (1.0 - ADAM_B1 ** ADAM_STEP)
    v_hat = v / (1.0 - ADAM_B2 ** ADAM_STEP)
    delta = -ADAM_LR * (m_hat / (jnp.sqrt(v_hat) + ADAM_EPS) + ADAM_WD * w)
    return delta, m, v


def adamw(name, w, g, m, v):
    shape = w.shape
    width = shape[-1]
    flat = [t.reshape(-1, width) for t in (w, g, m, v)]
    rows = flat[0].shape[0]
    tm = _rows_tile(rows, width, budget=1024 * 1024)
    res = _rowwise(name, _adamw_math, rows, tm, [(t, width, 0) for t in flat], [], [(width, F32)] * 3)
    return [r.reshape(shape) for r in res]


def ffn_in(name, xn, wi, riders=(), tm=512):
    n, d = xn.shape
    c = wi.shape[2]
    tm = _tile(n, tm)
    grid = (n // tm, 2)
    n_r = len(riders)

    def body(*refs):
        x_ref, wg_ref, wu_ref = refs[:3]
        g_ref, u_ref, a_ref = refs[3 + n_r:6 + n_r]
        if n_r:
            _ride(grid, [l for _, l in riders], refs[3:3 + n_r], refs[6 + n_r:6 + 2 * n_r], refs[6 + 2 * n_r:])
        x = x_ref[...].astype(MXU_DTYPE)
        g = jnp.dot(x, wg_ref[...].astype(MXU_DTYPE), preferred_element_type=F32)
        u = jnp.dot(x, wu_ref[...].astype(MXU_DTYPE), preferred_element_type=F32)
        g_ref[...] = g.astype(g_ref.dtype)
        u_ref[...] = u.astype(u_ref.dtype)
        a_ref[...] = (g * jax.nn.sigmoid(g) * u).astype(a_ref.dtype)

    any_spec = pl.BlockSpec(memory_space=pl.ANY)
    out_blk = pl.BlockSpec((tm, c), lambda i, j: (i, j))
    res = pl.pallas_call(
        body, name=name, grid=grid,
        in_specs=[pl.BlockSpec((tm, d), lambda i, j: (i, 0)), pl.BlockSpec((None, d, c), lambda i, j: (j, 0, 0)),
                  pl.BlockSpec((None, d, c), lambda i, j: (j + 2, 0, 0))] + [any_spec] * n_r,
        out_specs=[out_blk] * 3 + [any_spec] * n_r,
        out_shape=[jax.ShapeDtypeStruct((n, 2 * c), BF16)] * 3 + _ride_out_shapes(riders),
        scratch_shapes=_ride_scratch(n_r) if n_r else [],
        compiler_params=_params(("arbitrary", "arbitrary") if n_r else ("parallel", "parallel")),
    )(xn, wi, wi, *[r for r, _ in riders])
    return res[:3], list(res[3:])


def _rows(a):
    return a.reshape(N_CHIPS * a.shape[1], a.shape[2])


def _ffn_fwd(tag, h, g_norm, wi, wo, ride_in=(), ride_out=()):
    xn = norm_fwd(tag + "_norm", h, g_norm)
    (g, u, act), got_in = ffn_in(tag + "_wi", xn, wi, riders=ride_in)
    if wo is None:
        wo = _rows(got_in[0])
    out = mm_nn(tag + "_wo", act, wo, F32, alpha=0.5, res=h, tm=512, tk=2816, riders=ride_out)
    got_out = []
    if ride_out:
        out, got_out = out[0], list(out[1:])
    return out, (xn, g, u, act), got_in, got_out


def ffn_dact(name, d, wo_t, gate, up, tm=512):
    n, dm = d.shape
    ff = wo_t.shape[1]
    c = ff // 2
    tm = _tile(n, tm)

    def body(d_ref, w_ref, g_ref, u_ref, o_ref):
        da = 0.5 * jnp.dot(d_ref[...].astype(MXU_DTYPE), w_ref[...].astype(MXU_DTYPE), preferred_element_type=F32)
        g, u = g_ref[...].astype(F32), u_ref[...].astype(F32)
        sg = jax.nn.sigmoid(g)
        dg = da * u * (sg * (1.0 + g * (1.0 - sg)))
        du = da * (g * sg)
        o_ref[...] = jnp.concatenate([dg, du], axis=1).astype(o_ref.dtype)

    half = pl.BlockSpec((tm, c), lambda i, j: (i, j))
    return pl.pallas_call(
        body, name=name, grid=(n // tm, 2),
        in_specs=[pl.BlockSpec((tm, dm), lambda i, j: (i, 0)), pl.BlockSpec((dm, c), lambda i, j: (0, j)), half, half],
        out_specs=pl.BlockSpec((tm, 2 * c), lambda i, j: (i, j)),
        out_shape=jax.ShapeDtypeStruct((n, 2 * ff), BF16),
        compiler_params=_params(("parallel", "parallel")),
    )(d, wo_t, gate, up)


def _ffn_bwd(tag, d, h, g_norm, wi_t, wo_t, saved, into_wi, into_wo):
    xn, gate, up, act = saved
    n, dm = h.shape
    dwo = mm_tn(tag + "_dwo", act, d, F32, alpha=0.5, tm=1408, into=into_wo)
    dgu = ffn_dact(tag + "_dact", d, wo_t, gate, up)
    t, k = xn.shape
    c = dgu.shape[1] // N_CHIPS
    tm_, tk_ = _tile(k, 1024), _tile(t, 1024)
    dwi = _mm(tag + "_dwi", xn, dgu, (N_CHIPS, k, c), F32, (k // tm_, N_CHIPS, t // tk_),
              pl.BlockSpec((tk_, tm_), lambda i, j, kk: (kk, i)),
              pl.BlockSpec((tk_, c), lambda i, j, kk: (kk, j)),
              pl.BlockSpec((None, tm_, c), lambda i, j, kk: ((j % 2) * 2 + j // 2, i, 0)), TN, into=into_wi)
    tmx, tnx = _tile(n, 1024), _tile(dm, 1024)
    dxn = _mm(tag + "_dxn", dgu, wi_t, (n, dm), F32, (n // tmx, dm // tnx, N_CHIPS),
              pl.BlockSpec((tmx, c), lambda i, j, kk: (i, kk)),
              pl.BlockSpec((None, c, tnx), lambda i, j, kk: ((kk % 2) * 2 + kk // 2, 0, j)),
              pl.BlockSpec((tmx, tnx), lambda i, j, kk: (i, j)), NN)
    d_in, dg_norm = norm_bwd(tag + "_dnorm", dxn, h, g_norm, d)
    return d_in, dg_norm.reshape(dm), dwi, dwo


def _col(v):
    return v.reshape(SSM_COLS, 1)


def _layer_fwd(h, lw, ffn1_wi, ride, next_wi, p_l, seq):
    n, d = h.shape
    got = {"ffn1_wi": ffn1_wi}
    h1, ffn1_saved, got_in, got_out = _ffn_fwd("ffn1", h, lw["ffn1_norm"], ffn1_wi, None, (ride("ffn1_wo"),),
                                               (ride("w_in"), ride("ssm_w_glu"), ride("w_out")))
    got["ffn1_wo"] = got_in[0]
    got["w_in"], got["ssm_w_glu"], got["w_out"] = got_out

    xn2 = norm_fwd("mix_norm", h1, lw["mix_norm"])
    z = mm_nn("mix_in", xn2, _rows(got["w_in"]), F32)
    log_dt = jnp.repeat(lw["ssm_log_dt"], SSM_STATE)
    b_re, b_im = lw["ssm_b_re"].reshape(SSM_COLS, SSM_CH), lw["ssm_b_im"].reshape(SSM_COLS, SSM_CH)
    pw_re, pw_im, bb_re, bb_im = ssm_prep("ssm_prep", _col(lw["ssm_lambda_re"]), _col(lw["ssm_lambda_im"]), _col(log_dt), b_re, b_im)
    bbd = expand_b(bb_re, bb_im)
    cbd = expand_c(lw["ssm_c_re"], lw["ssm_c_im"])
    bu, got["ple_w_gate"], got["ple_w_proj"] = ssm_in("ssm_in", z, bbd, riders=(ride("ple_w_gate"), ride("ple_w_proj")))
    s, got["ffn2_wi"] = ssm_scan("ssm_scan", scan_coefficients(pw_re, pw_im, False), bu, seq, False, riders=(ride("ffn2_wi"),))
    y0c, got["ffn2_wo"] = ssm_out("ssm_out", s, cbd, riders=(ride("ffn2_wo"),))

    def skip_gelu(yc, zs, dvec):
        y0 = yc + dvec * zs
        return y0, _gelu(y0)

    y0, y1 = _rowwise("ssm_gelu", skip_gelu, n, 512, [(y0c, SSM_WIDTH, 0), (z, SSM_WIDTH, 0)],
                      [lw["ssm_d"].reshape(1, SSM_WIDTH)], [(SSM_WIDTH, F32), (SSM_WIDTH, F32)])
    t = mm_nn("ssm_glu_mm", y1, _rows(got["ssm_w_glu"]), F32)
    y2 = _rowwise("ssm_glu", lambda a, b: a * jax.nn.sigmoid(b), n, 512, [(y1, SSM_WIDTH, 0), (t, SSM_WIDTH, 0)], [],
                  [(SSM_WIDTH, BF16)])[0]

    q = pool_window("pool_window", z, SSM_WIDTH // POOL_CH, seq, BF16, False)
    wp_eff = lw["pool_w"] * lw["pool_scale"].reshape(POOL_GROUPS, 1, POOL_CH)
    yp = pool_mm("pool_mm", q, wp_eff, BF16)
    m = jnp.concatenate([y2, yp], axis=1)
    h2 = mm_nn("mix_out", m, _rows(got["w_out"]), F32, res=h1)

    h3, ffn2_saved, got_in, _ = _ffn_fwd("ffn2", h2, lw["ffn2_norm"], got["ffn2_wi"], _rows(got["ffn2_wo"]),
                                         () if next_wi is None else (next_wi,))
    next_ffn1_wi = got_in[0] if got_in else None

    xn4 = norm_fwd("ple_norm", h3, lw["ple_norm"])
    tg = mm_nn("ple_gate", xn4, _rows(got["ple_w_gate"]), F32)
    e = mm_nn_colsharded("ple_proj", p_l, got["ple_w_proj"], F32)
    h4 = _rowwise("ple_add", lambda a, b, c: a + jax.nn.sigmoid(b) * c, n, 512, [(h3, d, 0), (tg, d, 0), (e, d, 0)], [], [(d, F32)])[0]
    saved = dict(h=h, h1=h1, h2=h2, h3=h3, ffn1=ffn1_saved, ffn2=ffn2_saved, xn2=xn2, z=z, s=s, y0=y0, y1=y1, t=t, m=m, q=q,
                 xn4=xn4, tg=tg, e=e, pw_re=pw_re, pw_im=pw_im, bbd=bbd, cbd=cbd)
    return h4, saved, got, next_ffn1_wi


def _layer_bwd(d, lw, p_l, sv, seq, bufs, layer):
    n, dm = d.shape
    g = {}
    def ple_bwd(dd, tg, e):
        gate = jax.nn.sigmoid(tg)
        return dd * e * gate * (1.0 - gate), dd * gate

    dtg, de = _rowwise("ple_dadd", ple_bwd, n, 512, [(d, dm, 0), (sv["tg"], dm, 0), (sv["e"], dm, 0)], [], [(dm, BF16), (dm, BF16)])
    g["ple_w_proj"] = mm_tn_colsharded("ple_dproj", p_l, de, F32, into=(bufs["ple_w_proj"], layer))
    g["ple_w_gate"] = mm_tn("ple_dgate_w", sv["xn4"], dtg, F32, into=(bufs["ple_w_gate"], layer))
    dxn4 = mm_nn("ple_dgate_x", dtg, lw["ple_w_gate_t"], F32)
    d, dg = norm_bwd("ple_dnorm", dxn4, sv["h3"], lw["ple_norm"], d)
    g["ple_norm"] = dg.reshape(dm)

    d, g["ffn2_norm"], g["ffn2_wi"], g["ffn2_wo"] = _ffn_bwd("ffn2b", d, sv["h2"], lw["ffn2_norm"], lw["ffn2_wi_t"], lw["ffn2_wo_t"], sv["ffn2"],
                                                            (bufs["ffn2_wi"], layer), (bufs["ffn2_wo"], layer))

    dmix = mm_nn("mix_dout_x", d, lw["w_out_t"], F32)
    g["w_out"] = mm_tn("mix_dout_w", sv["m"], d, F32, into=(bufs["w_out"], layer))
    pool_cb = SSM_WIDTH // POOL_CH
    wp_eff = lw["pool_w"] * lw["pool_scale"].reshape(POOL_GROUPS, 1, POOL_CH)
    dq = pool_mm_t("pool_dmm_x", dmix, pool_cb, wp_eff, F32)
    dwp_eff = pool_grad_w("pool_dmm_w", sv["q"], dmix, pool_cb)
    g["pool_w"] = dwp_eff * lw["pool_scale"].reshape(POOL_GROUPS, 1, POOL_CH)
    g["pool_scale"] = jnp.sum(dwp_eff * lw["pool_w"], axis=1).reshape(POOL_WIDTH)
    dzp = pool_window("pool_dwindow", dq, 0, seq, BF16, True)

    def glu_bwd(dy2, y1, t):
        sg = jax.nn.sigmoid(t)
        return dy2 * y1 * sg * (1.0 - sg), dy2 * sg

    dt_, dy1a = _rowwise("ssm_dglu", glu_bwd, n, 512, [(dmix, SSM_WIDTH, 0), (sv["y1"], SSM_WIDTH, 0), (sv["t"], SSM_WIDTH, 0)], [],
                         [(SSM_WIDTH, BF16), (SSM_WIDTH, F32)])
    g["ssm_w_glu"] = mm_tn("ssm_dglu_w", sv["y1"], dt_, F32, into=(bufs["ssm_w_glu"], layer))
    dy1b = mm_nn("ssm_dglu_x", dt_, lw["ssm_w_glu_t"], F32)

    def gelu_bwd(da, db, y0, zs, dvec):
        dy0 = (da + db) * _gelu_grad(y0)
        return dy0, dy0 * dvec, jnp.sum(dy0 * zs, axis=0, keepdims=True)

    dy0, dzs_a, dd = _rowwise("ssm_dgelu", gelu_bwd, n, 512,
                              [(dy1a, SSM_WIDTH, 0), (dy1b, SSM_WIDTH, 0), (sv["y0"], SSM_WIDTH, 0), (sv["z"], SSM_WIDTH, 0)],
                              [lw["ssm_d"].reshape(1, SSM_WIDTH)], [(SSM_WIDTH, F32), (SSM_WIDTH, F32)], [(1, SSM_WIDTH)])
    g["ssm_d"] = dd.reshape(SSM_WIDTH)
    g["ssm_c_re"], g["ssm_c_im"] = compact_c(ssm_grad_c("ssm_dc", sv["s"], dy0))
    v = ssm_out_t("ssm_dout", dy0, sv["cbd"])
    a, dlam = ssm_scan("ssm_scan_adj", scan_coefficients(sv["pw_re"], sv["pw_im"], True), v, seq, True, states=sv["s"])
    dbb_re, dbb_im = compact_b(ssm_grad_b("ssm_db", sv["z"], a))
    dzs = ssm_in_t("ssm_din", a, sv["bbd"], dzs_a)
    dlam = jnp.sum(dlam, axis=1)
    log_dt = jnp.repeat(lw["ssm_log_dt"], SSM_STATE)
    b_re, b_im = lw["ssm_b_re"].reshape(SSM_COLS, SSM_CH), lw["ssm_b_im"].reshape(SSM_COLS, SSM_CH)
    glr, gli, gld, gbr, gbi = ssm_prep_bwd("ssm_prep_bwd", _col(lw["ssm_lambda_re"]), _col(lw["ssm_lambda_im"]), _col(log_dt), b_re, b_im,
                                           _col(dlam[0]), _col(dlam[1]), dbb_re, dbb_im)
    g["ssm_lambda_re"] = glr.reshape(SSM_GROUPS, SSM_STATE)
    g["ssm_lambda_im"] = gli.reshape(SSM_GROUPS, SSM_STATE)
    g["ssm_log_dt"] = jnp.sum(gld.reshape(SSM_GROUPS, SSM_STATE), axis=1)
    g["ssm_b_re"] = gbr.reshape(SSM_GROUPS, SSM_STATE, SSM_CH)
    g["ssm_b_im"] = gbi.reshape(SSM_GROUPS, SSM_STATE, SSM_CH)

    dz = jnp.concatenate([dzs.astype(BF16), dzp], axis=1)
    g["w_in"] = mm_tn("mix_din_w", sv["xn2"], dz, F32, into=(bufs["w_in"], layer))
    dxn2 = mm_nn("mix_din_x", dz, lw["w_in_t"], F32)
    d, dg = norm_bwd("mix_dnorm", dxn2, sv["h1"], lw["mix_norm"], d)
    g["mix_norm"] = dg.reshape(dm)

    d, g["ffn1_norm"], g["ffn1_wi"], g["ffn1_wo"] = _ffn_bwd("ffn1b", d, sv["h"], lw["ffn1_norm"], lw["ffn1_wi_t"], lw["ffn1_wo_t"], sv["ffn1"],
                                                            (bufs["ffn1_wi"], layer), (bufs["ffn1_wo"], layer))
    return d, g


def _flatten_small(tensors):
    flat = jnp.concatenate([t.reshape(-1) for t in tensors])
    pad = (-flat.shape[0]) % (SUBLANE * LANE)
    return jnp.pad(flat, (0, pad)).reshape(-1, LANE)


def _unflatten_small(flat, like):
    flat = flat.reshape(-1)
    out, off = [], 0
    for t in like:
        out.append(flat[off:off + t.size].reshape(t.shape))
        off += t.size
    return out


def kernel(x, p, ffn1_norm, ffn1_wi, ffn1_wo, mix_norm, w_in, ssm_lambda_re, ssm_lambda_im, ssm_log_dt, ssm_b_re, ssm_b_im, ssm_c_re, ssm_c_im, ssm_d, ssm_w_glu, pool_w, pool_scale, w_out, ffn2_norm, ffn2_wi, ffn2_wo, ple_norm, ple_w_gate, ple_w_proj, final_norm, loss_target, m_ffn1_norm, m_ffn1_wi, m_ffn1_wo, m_mix_norm, m_w_in, m_ssm_lambda_re, m_ssm_lambda_im, m_ssm_log_dt, m_ssm_b_re, m_ssm_b_im, m_ssm_c_re, m_ssm_c_im, m_ssm_d, m_ssm_w_glu, m_pool_w, m_pool_scale, m_w_out, m_ffn2_norm, m_ffn2_wi, m_ffn2_wo, m_ple_norm, m_ple_w_gate, m_ple_w_proj, m_final_norm, v_ffn1_norm, v_ffn1_wi, v_ffn1_wo, v_mix_norm, v_w_in, v_ssm_lambda_re, v_ssm_lambda_im, v_ssm_log_dt, v_ssm_b_re, v_ssm_b_im, v_ssm_c_re, v_ssm_c_im, v_ssm_d, v_ssm_w_glu, v_pool_w, v_pool_scale, v_w_out, v_ffn2_norm, v_ffn2_wi, v_ffn2_wo, v_ple_norm, v_ple_w_gate, v_ple_w_proj, v_final_norm):
    given = dict(locals())
    w = {k: given[k] for k in WEIGHTS}
    mom = {k: given["m_" + k] for k in WEIGHTS}
    var = {k: given["v_" + k] for k in WEIGHTS}
    bsz, seq, dm = x.shape
    n = bsz * seq
    depth = ffn1_wi.shape[0]

    shards = {k: w[k].astype(MXU_DTYPE) for k in BIG}

    def layer_weights(l, gathered):
        lw = {k: w[k][l] for k in SMALL if k != "final_norm"}
        for k, a in gathered.items():
            if k in COL_SHARDED:
                lw[k] = a
                if k != "ple_w_proj":
                    lw[k + "_t"] = jnp.swapaxes(a, 1, 2)
            else:
                lw[k] = a.reshape(N_CHIPS * a.shape[1], a.shape[2])
                lw[k + "_t"] = lw[k].T
        return lw

    ffn1_wi_l = gather_weight("gather_ffn1_wi", shards["ffn1_wi"][:1])[0]

    p2 = p.reshape(depth, n, p.shape[-1])
    h_last = x.reshape(n, dm)
    saved, layers = [], []
    for l in range(depth):
        small = {k: w[k][l] for k in SMALL if k != "final_norm"}
        next_wi = (shards["ffn1_wi"], l + 1) if l + 1 < depth else None
        h_last, sv, gathered, ffn1_wi_l = _layer_fwd(h_last, small, ffn1_wi_l, functools.partial(lambda k, l: (shards[k], l), l=l),
                                                      next_wi, p2[l], seq)
        layers.append(layer_weights(l, gathered))
        saved.append(sv)

    def head(hh, tgt, gf):
        r = lax.rsqrt(jnp.mean(hh * hh, axis=-1, keepdims=True) + NORM_EPS)
        xh = hh * r
        diff = xh * gf - tgt
        dy = diff * (1.0 / dm)
        dxh = dy * gf
        dx = r * (dxh - xh * jnp.mean(dxh * xh, axis=-1, keepdims=True))
        return dx, jnp.sum(diff * diff, axis=0, keepdims=True) * (0.5 / dm), jnp.sum(dy * xh, axis=0, keepdims=True)

    d_last, loss_cols, g_final = _rowwise("loss_head", head, n, 512, [(h_last, dm, 0), (loss_target.reshape(n, dm), dm, 0)],
                                          [final_norm.reshape(1, dm)], [(dm, F32)], [(1, dm), (1, dm)])
    loss = lax.psum(jnp.sum(loss_cols), ("x", "y", "c"))

    d_x = d_last
    layer_grads = [None] * depth
    bufs = {k: lax.empty((depth, N_CHIPS) + w[k].shape[1:], F32) for k in BIG}
    for l in reversed(range(depth)):
        d_x, layer_grads[l] = _layer_bwd(d_x, layers[l], p2[l], saved[l], seq, bufs, l)
        bufs = {k: layer_grads[l][k] for k in BIG}
    grads = {k: jnp.stack([g[k] for g in layer_grads]) for k in layer_grads[0] if k not in BIG}
    grads.update(bufs)
    grad_x = d_x.reshape(bsz, seq, dm)

    big_sum = [reduce_scatter_streamed("rs_" + k, grads[k]) for k in BIG]
    small_keys = [k for k in SMALL]
    small_parts = [grads[k] if k != "final_norm" else g_final.reshape(dm) for k in small_keys]
    small_sum = _unflatten_small(all_reduce_small(_flatten_small(small_parts)), small_parts)
    g_out = dict(zip(BIG, big_sum))
    g_out.update(dict(zip(small_keys, small_sum)))
    for k in BIG:
        g_out[k] = g_out[k].reshape(w[k].shape)

    delta, new_m, new_v = {}, {}, {}
    for k in BIG:
        delta[k], new_m[k], new_v[k] = adamw("adamw_" + k, w[k], g_out[k], mom[k], var[k])
    sw = adamw("adamw_small", *[_flatten_small([t[k] for k in small_keys]) for t in (w, g_out, mom, var)])
    for name, flat in zip((delta, new_m, new_v), sw):
        for k, t in zip(small_keys, _unflatten_small(flat, [w[k] for k in small_keys])):
            name[k] = t

    return (loss, grad_x, *[g_out[k] for k in WEIGHTS], *[delta[k] for k in WEIGHTS],
            *[new_m[k] for k in WEIGHTS], *[new_v[k] for k in WEIGHTS])
```

```python
import functools
import math

import jax
import jax.numpy as jnp
from jax import lax
from jax.experimental import pallas as pl
from jax.experimental.pallas import tpu as pltpu

F32 = jnp.float32
BF16 = jnp.bfloat16
MXU_DTYPE = jnp.bfloat16
GRAD_WIRE_DTYPE = jnp.bfloat16
STATE_DTYPE = jnp.bfloat16
VMEM_LIMIT = 56 * 1024 * 1024
LANE = 128
SUBLANE = 8

N_CHIPS = 4
SSM_GROUPS = 32
SSM_STATE = 64
SSM_CH = 16
SSM_WIDTH = SSM_GROUPS * SSM_CH
SSM_COLS = SSM_GROUPS * SSM_STATE
POOL_GROUPS = 4
POOL_CH = 128
POOL_WIDTH = POOL_GROUPS * POOL_CH
SSM_TILE = 256
NORM_EPS = 1e-6
ADAM_LR = 0.001
ADAM_B1 = 0.9
ADAM_B2 = 0.999
ADAM_EPS = 1e-08
ADAM_WD = 0.01
ADAM_STEP = 10
MESH_ID = pl.DeviceIdType.MESH

BIG = ("ffn1_wi", "ffn1_wo", "w_in", "ssm_w_glu", "w_out", "ffn2_wi", "ffn2_wo", "ple_w_gate", "ple_w_proj")
COL_SHARDED = ("ffn1_wi", "ffn2_wi", "ple_w_proj")
SMALL = ("ffn1_norm", "mix_norm", "ssm_lambda_re", "ssm_lambda_im", "ssm_log_dt", "ssm_b_re", "ssm_b_im",
         "ssm_c_re", "ssm_c_im", "ssm_d", "pool_w", "pool_scale", "ffn2_norm", "ple_norm", "final_norm")
WEIGHTS = ("ffn1_norm", "ffn1_wi", "ffn1_wo", "mix_norm", "w_in", "ssm_lambda_re", "ssm_lambda_im", "ssm_log_dt",
           "ssm_b_re", "ssm_b_im", "ssm_c_re", "ssm_c_im", "ssm_d", "ssm_w_glu", "pool_w", "pool_scale", "w_out",
           "ffn2_norm", "ffn2_wi", "ffn2_wo", "ple_norm", "ple_w_gate", "ple_w_proj", "final_norm")


def _tile(dim, target):
    best = None
    t = LANE
    while t <= min(dim, target):
        if dim % t == 0:
            best = t
        t += LANE
    return best if best is not None else dim


def _params(sem):
    return pltpu.CompilerParams(dimension_semantics=sem, vmem_limit_bytes=VMEM_LIMIT)


CHIPS = ((1, 0, 0), (0, 1, 0), (1, 1, 0))


def _ride_scratch(n):
    return [pltpu.SemaphoreType.DMA((3 * n,)), pltpu.SemaphoreType.DMA((3 * n,)), pltpu.SemaphoreType.DMA((n,))]


def _ride_copies(layers, ins, outs, send_sems, recv_sems, local_sems):
    x, y, c = lax.axis_index("x"), lax.axis_index("y"), lax.axis_index("c")
    me = 2 * x + y
    copies = []
    for i in range(len(ins)):
        src = ins[i].at[pl.ds(layers[i], 1)]
        dst = outs[i].at[pl.ds(me, 1)]
        copies.append(pltpu.make_async_copy(src, dst, local_sems.at[i]))
        for f, (fx, fy, _) in enumerate(CHIPS):
            peer = ((1 - x) if fx else x, (1 - y) if fy else y, c)
            copies.append(pltpu.make_async_remote_copy(
                src_ref=src, dst_ref=dst, send_sem=send_sems.at[3 * i + f], recv_sem=recv_sems.at[3 * i + f],
                device_id=peer, device_id_type=pl.DeviceIdType.MESH))
    return copies


def _ride(grid, layers, ins, outs, sems):
    if not ins:
        return
    ids = [pl.program_id(a) for a in range(len(grid))]
    first = functools.reduce(jnp.logical_and, [i == 0 for i in ids])
    last = functools.reduce(jnp.logical_and, [i == g - 1 for i, g in zip(ids, grid)])

    @pl.when(first)
    def _():
        for cp in _ride_copies(layers, ins, outs, *sems):
            cp.start()

    @pl.when(last)
    def _():
        for cp in _ride_copies(layers, ins, outs, *sems):
            cp.wait()


def _ride_out_shapes(riders):
    return [jax.ShapeDtypeStruct((N_CHIPS,) + r.shape[1:], r.dtype) for r, _ in riders]


def _mm(name, a, b, out_shape, out_dtype, grid, a_spec, b_spec, o_spec, contract, alpha=1.0, res=None, riders=(), into=None):
    n_k = grid[2]
    acc_shape = tuple(d for d in o_spec.block_shape if d is not None)
    n_in = 2 + (res is not None) + (into is not None)
    n_r = len(riders)
    ride_layers = [l for _, l in riders]

    def body(*refs):
        a_ref, b_ref = refs[0], refs[1]
        r_ref = refs[2] if res is not None else None
        o_ref = refs[n_in + n_r]
        if n_r:
            _ride(grid, ride_layers, refs[n_in:n_in + n_r], refs[n_in + n_r + 1:n_in + 2 * n_r + 1], refs[len(refs) - 3:])

        def product():
            return lax.dot_general(a_ref[...].astype(MXU_DTYPE), b_ref[...].astype(MXU_DTYPE),
                                   (contract, ((), ())), preferred_element_type=F32)

        def finish(v):
            if alpha != 1.0:
                v = v * alpha
            if r_ref is not None:
                v = v + r_ref[...].astype(F32)
            o_ref[...] = v.astype(out_dtype)

        if n_k == 1:
            finish(product())
            return
        acc = refs[n_in + 2 * n_r + 1]
        k = pl.program_id(2)

        @pl.when(k == 0)
        def _():
            acc[...] = product()

        @pl.when(k > 0)
        def _():
            acc[...] += product()

        @pl.when(k == n_k - 1)
        def _():
            finish(acc[...])

    in_specs = [a_spec, b_spec]
    operands = [a, b]
    if res is not None:
        in_specs.append(o_spec)
        operands.append(res)
    scratch = [pltpu.VMEM(acc_shape, F32)] if n_k > 1 else []
    any_spec = pl.BlockSpec(memory_space=pl.ANY)
    if into is not None:
        buf, layer = into
        assert buf.shape[1:] == tuple(out_shape) and buf.dtype == out_dtype and not n_r
        placed = pl.BlockSpec((None,) + tuple(o_spec.block_shape),
                              functools.partial(lambda i, j, kk, m, l: (l,) + tuple(m(i, j, kk)), m=o_spec.index_map, l=layer))
        return pl.pallas_call(
            body, name=name, grid=grid, in_specs=in_specs + [any_spec], out_specs=placed,
            out_shape=jax.ShapeDtypeStruct(buf.shape, out_dtype), scratch_shapes=scratch,
            input_output_aliases={len(operands): 0},
            compiler_params=_params(("parallel", "parallel", "arbitrary")),
        )(*operands, buf)
    if not n_r:
        return pl.pallas_call(
            body, name=name, grid=grid, in_specs=in_specs, out_specs=o_spec,
            out_shape=jax.ShapeDtypeStruct(out_shape, out_dtype), scratch_shapes=scratch,
            compiler_params=_params(("parallel", "parallel", "arbitrary")),
        )(*operands)
    return pl.pallas_call(
        body, name=name, grid=grid, in_specs=in_specs + [any_spec] * n_r, out_specs=[o_spec] + [any_spec] * n_r,
        out_shape=[jax.ShapeDtypeStruct(out_shape, out_dtype)] + _ride_out_shapes(riders),
        scratch_shapes=scratch + _ride_scratch(n_r),
        compiler_params=_params(("arbitrary", "arbitrary", "arbitrary")),
    )(*operands, *[r for r, _ in riders])


NN = ((1,), (0,))
NT = ((1,), (1,))
TN = ((0,), (0,))


def mm_nn(name, a, b, out_dtype, alpha=1.0, res=None, tm=1024, tn=1024, tk=1024, riders=()):
    m, k = a.shape
    n = b.shape[1]
    tm, tn, tk = _tile(m, tm), _tile(n, tn), _tile(k, tk)
    return _mm(name, a, b, (m, n), out_dtype, (m // tm, n // tn, k // tk),
               pl.BlockSpec((tm, tk), lambda i, j, kk: (i, kk)),
               pl.BlockSpec((tk, tn), lambda i, j, kk: (kk, j)),
               pl.BlockSpec((tm, tn), lambda i, j, kk: (i, j)), NN, alpha, res, riders)


def mm_tn(name, a, b, out_dtype, alpha=1.0, tm=1024, tn=1024, tk=1024, into=None):
    k, m = a.shape
    n = b.shape[1]
    tm, tn, tk = _tile(m, tm), _tile(n, tn), _tile(k, tk)
    if into is not None:
        shape = into[0].shape
        into = (into[0].reshape(shape[0], m, n), into[1])
    out = _mm(name, a, b, (m, n), out_dtype, (m // tm, n // tn, k // tk),
              pl.BlockSpec((tk, tm), lambda i, j, kk: (kk, i)),
              pl.BlockSpec((tk, tn), lambda i, j, kk: (kk, j)),
              pl.BlockSpec((tm, tn), lambda i, j, kk: (i, j)), TN, alpha, into=into)
    return out if into is None else out.reshape(shape)


def mm_nn_colsharded(name, a, w, out_dtype, tm=1024, tk=1024, riders=()):
    m, k = a.shape
    c = w.shape[2]
    tm, tk = _tile(m, tm), _tile(k, tk)
    return _mm(name, a, w, (m, N_CHIPS * c), out_dtype, (m // tm, N_CHIPS, k // tk),
               pl.BlockSpec((tm, tk), lambda i, j, kk: (i, kk)),
               pl.BlockSpec((None, tk, c), lambda i, j, kk: (j, kk, 0)),
               pl.BlockSpec((tm, c), lambda i, j, kk: (i, j)), NN, riders=riders)


def mm_tn_colsharded(name, a, b, out_dtype, tm=1024, tk=1024, into=None):
    t, k = a.shape
    c = b.shape[1] // N_CHIPS
    tm, tk = _tile(k, tm), _tile(t, tk)
    return _mm(name, a, b, (N_CHIPS, k, c), out_dtype, (k // tm, N_CHIPS, t // tk),
               pl.BlockSpec((tk, tm), lambda i, j, kk: (kk, i)),
               pl.BlockSpec((tk, c), lambda i, j, kk: (kk, j)),
               pl.BlockSpec((None, tm, c), lambda i, j, kk: (j, i, 0)), TN, into=into)


def _rowwise(name, fn, n_rows, tm, row_ins, bcast_ins, outs, accs=()):
    tm = min(tm, n_rows)
    grid = (n_rows // tm,)
    n_row, n_b, n_out = len(row_ins), len(bcast_ins), len(outs)

    def body(*refs):
        ins = [r[...] for r in refs[:n_row + n_b]]
        out_refs = refs[n_row + n_b:n_row + n_b + n_out]
        acc_refs = refs[n_row + n_b + n_out:]
        res = fn(*ins)
        if not isinstance(res, (tuple, list)):
            res = (res,)
        for o_ref, v in zip(out_refs, res[:n_out]):
            o_ref[...] = v.astype(o_ref.dtype)
        if acc_refs:
            @pl.when(pl.program_id(0) == 0)
            def _():
                for a_ref in acc_refs:
                    a_ref[...] = jnp.zeros_like(a_ref)
            for a_ref, v in zip(acc_refs, res[n_out:]):
                a_ref[...] += v

    in_specs, operands = [], []
    for spec in row_ins:
        arr, width, cb = spec[0], spec[1], spec[2]
        rb = spec[3] if len(spec) > 3 else 0
        in_specs.append(pl.BlockSpec((tm, width), functools.partial(lambda i, cb, rb: (i + rb, cb), cb=cb, rb=rb)))
        operands.append(arr)
    for arr in bcast_ins:
        in_specs.append(pl.BlockSpec(arr.shape, functools.partial(lambda i, nd: (0,) * nd, nd=arr.ndim)))
        operands.append(arr)
    out_specs = [pl.BlockSpec((tm, w), lambda i: (i, 0)) for w, _ in outs]
    out_specs += [pl.BlockSpec((r, w), lambda i: (0, 0)) for r, w in accs]
    out_shape = [jax.ShapeDtypeStruct((n_rows, w), dt) for w, dt in outs]
    out_shape += [jax.ShapeDtypeStruct((r, w), F32) for r, w in accs]
    res = pl.pallas_call(
        body, name=name, grid=grid, in_specs=in_specs, out_specs=out_specs, out_shape=out_shape,
        compiler_params=_params(("arbitrary",) if accs else ("parallel",)),
    )(*operands)
    return res


def _rms(x, g):
    r = lax.rsqrt(jnp.mean(x * x, axis=-1, keepdims=True) + NORM_EPS)
    return x * r * g


def _rms_bwd(dy, x, g):
    r = lax.rsqrt(jnp.mean(x * x, axis=-1, keepdims=True) + NORM_EPS)
    xh = x * r
    dxh = dy * g
    dx = r * (dxh - xh * jnp.mean(dxh * xh, axis=-1, keepdims=True))
    return dx, jnp.sum(dy * xh, axis=0, keepdims=True)


def norm_fwd(name, h, g):
    n, d = h.shape
    return _rowwise(name, lambda x, gg: _rms(x, gg), n, 512, [(h, d, 0)], [g.reshape(1, d)], [(d, BF16)])[0]


def norm_bwd(name, dxn, h, g, d_res):
    n, d = h.shape

    def fn(dy, x, dr, gg):
        dx, dg = _rms_bwd(dy, x, gg)
        return dr + dx, dg

    return _rowwise(name, fn, n, 512, [(dxn, d, 0), (h, d, 0), (d_res, d, 0)], [g.reshape(1, d)], [(d, F32)], [(1, d)])


_GELU_C = math.sqrt(2.0 / math.pi)


def _gelu(x):
    return 0.5 * x * (1.0 + jnp.tanh(_GELU_C * (x + 0.044715 * (x * x * x))))


def _gelu_grad(x):
    th = jnp.tanh(_GELU_C * (x + 0.044715 * (x * x * x)))
    return 0.5 * (1.0 + th) + 0.5 * x * (1.0 - th * th) * (_GELU_C * (1.0 + 3.0 * 0.044715 * (x * x)))


def _ssm_discretize(lam_re, lam_im, log_dt, b_re, b_im):
    dt = jnp.exp(log_dt)
    e = jnp.exp(lam_re * dt)
    lb_re = e * jnp.cos(lam_im * dt)
    lb_im = e * jnp.sin(lam_im * dt)
    nr, ni = lb_re - 1.0, lb_im
    den = lam_re * lam_re + lam_im * lam_im
    cr = (nr * lam_re + ni * lam_im) / den
    ci = (ni * lam_re - nr * lam_im) / den
    return lb_re, lb_im, cr * b_re - ci * b_im, cr * b_im + ci * b_re


def ssm_prep(name, lam_re, lam_im, log_dt, b_re, b_im):
    def body(lr, li, ld, br, bi, pr_ref, pi_ref, bbr_ref, bbi_ref):
        lb_re, lb_im, bb_re, bb_im = _ssm_discretize(lr[...], li[...], ld[...], br[...], bi[...])
        bbr_ref[...] = bb_re
        bbi_ref[...] = bb_im
        pr, pi = lb_re, lb_im
        cols_r, cols_i = [pr], [pi]
        for _ in range(SUBLANE - 1):
            pr, pi = pr * lb_re - pi * lb_im, pr * lb_im + pi * lb_re
            cols_r.append(pr)
            cols_i.append(pi)
        lane = lax.broadcasted_iota(jnp.int32, (SSM_COLS, SUBLANE), 1)
        out_r = jnp.zeros((SSM_COLS, SUBLANE), F32)
        out_i = jnp.zeros((SSM_COLS, SUBLANE), F32)
        for r in range(SUBLANE):
            out_r = jnp.where(lane == r, cols_r[r], out_r)
            out_i = jnp.where(lane == r, cols_i[r], out_i)
        pr_ref[...] = out_r
        pi_ref[...] = out_i

    shapes = [jax.ShapeDtypeStruct((SSM_COLS, SUBLANE), F32)] * 2 + [jax.ShapeDtypeStruct((SSM_COLS, SSM_CH), F32)] * 2
    return pl.pallas_call(body, name=name, out_shape=shapes,
                          compiler_params=pltpu.CompilerParams(vmem_limit_bytes=VMEM_LIMIT))(lam_re, lam_im, log_dt, b_re, b_im)


def ssm_prep_bwd(name, lam_re, lam_im, log_dt, b_re, b_im, d_lb_re, d_lb_im, d_bb_re, d_bb_im):
    def body(lr, li, ld, br, bi, g0, g1, g2, g3, o0, o1, o2, o3, o4):
        _, vjp = jax.vjp(_ssm_discretize, lr[...], li[...], ld[...], br[...], bi[...])
        res = vjp((g0[...], g1[...], g2[...], g3[...]))
        for o, v in zip((o0, o1, o2, o3, o4), res):
            o[...] = v

    col = jax.ShapeDtypeStruct((SSM_COLS, 1), F32)
    mat = jax.ShapeDtypeStruct((SSM_COLS, SSM_CH), F32)
    return pl.pallas_call(body, name=name, out_shape=[col, col, col, mat, mat],
                          compiler_params=pltpu.CompilerParams(vmem_limit_bytes=VMEM_LIMIT))(
        lam_re, lam_im, log_dt, b_re, b_im, d_lb_re, d_lb_im, d_bb_re, d_bb_im)


def scan_coefficients(pw_re, pw_im, reverse):
    pr, pi = pw_re.T, pw_im.T
    if reverse:
        pi = -pi
    row = jnp.arange(SUBLANE)[:, None]
    out = []
    for d in (1, 2, 4):
        valid = (row < SUBLANE - d) if reverse else (row >= d)
        out.append(jnp.where(valid, pr[d - 1][None, :], 0.0))
        out.append(jnp.where(valid, pi[d - 1][None, :], 0.0))
    out.append(pr[::-1] if reverse else pr)
    out.append(pi[::-1] if reverse else pi)
    return jnp.stack(out)


def ssm_scan(name, coef, x, seq, reverse, states=None, riders=()):
    n = x.shape[1]
    n_seq = n // seq
    cw = LANE
    n_cb = SSM_COLS // cw
    pair = 2 * SUBLANE
    n_pairs = seq // pair
    pairs_per_step = 2 if n_pairs % 2 == 0 else 1
    with_dlam = states is not None
    n_r = len(riders)
    assert not (n_r and with_dlam)

    def body(*refs):
        if with_dlam:
            coef_ref, x_ref, s_ref, o_ref, dl_ref = refs
        else:
            coef_ref, x_ref, o_ref = refs[0], refs[1], refs[2 + n_r]
            if n_r:
                _ride((n_cb, n_seq), [l for _, l in riders], refs[2:2 + n_r], refs[3 + n_r:3 + 2 * n_r], refs[3 + 2 * n_r:])
        c = [coef_ref[i] for i in range(8)]
        row16 = lax.broadcasted_iota(jnp.int32, (pair, cw), 0)
        zero = jnp.zeros((SUBLANE, cw), F32)

        edge = 0 if reverse else SUBLANE - 1

        def bcast_row(v, r, rows=SUBLANE):
            return jnp.broadcast_to(v[r:r + 1, :], (rows, cw))

        p8r, p8i = bcast_row(c[6], edge), bcast_row(c[7], edge)

        def local_scan(xr, xi):
            for si, d in enumerate((1, 2, 4)):
                sh = (SUBLANE - d) if reverse else d
                sr, sm = pltpu.roll(xr, sh, 0), pltpu.roll(xi, sh, 0)
                lre, lim = c[2 * si], c[2 * si + 1]
                xr, xi = xr + lre * sr - lim * sm, xi + lre * sm + lim * sr
            return xr, xi

        def step(it, carry):
            work = []
            for u in range(pairs_per_step):
                k = it * pairs_per_step + u
                pidx = (n_pairs - 1 - k) if reverse else k
                off = pl.multiple_of(pidx * pair, pair)
                xr16 = x_ref[0, pl.ds(off, pair), :].astype(F32)
                xi16 = x_ref[1, pl.ds(off, pair), :].astype(F32)
                halves = (1, 0) if reverse else (0, 1)
                tiles = {h: local_scan(xr16[h * SUBLANE:(h + 1) * SUBLANE], xi16[h * SUBLANE:(h + 1) * SUBLANE]) for h in halves}
                work.append((pidx, off, halves, tiles))
            cre, cim = carry[0], carry[1]
            acc = carry[2:]
            for pidx, off, halves, tiles in work:
                done = {}
                for h in halves:
                    lr, li = tiles[h]
                    done[h] = (lr + c[6] * cre - c[7] * cim, li + c[6] * cim + c[7] * cre)
                    cre, cim = (bcast_row(lr, edge) + p8r * cre - p8i * cim, bcast_row(li, edge) + p8r * cim + p8i * cre)
                or16 = jnp.concatenate([done[0][0], done[1][0]], axis=0)
                oi16 = jnp.concatenate([done[0][1], done[1][1]], axis=0)
                o_ref[0, pl.ds(off, pair), :] = or16.astype(o_ref.dtype)
                o_ref[1, pl.ds(off, pair), :] = oi16.astype(o_ref.dtype)
                if with_dlam:
                    poff = pl.multiple_of(jnp.maximum(pidx - 1, 0) * pair, pair)
                    first = pidx > 0
                    sr16 = s_ref[0, pl.ds(off, pair), :].astype(F32)
                    si16 = s_ref[1, pl.ds(off, pair), :].astype(F32)
                    pr_last = jnp.where(first, bcast_row(s_ref[0, pl.ds(poff, pair), :].astype(F32), pair - 1, pair), 0.0)
                    pi_last = jnp.where(first, bcast_row(s_ref[1, pl.ds(poff, pair), :].astype(F32), pair - 1, pair), 0.0)
                    spr = jnp.where(row16 == 0, pr_last, pltpu.roll(sr16, 1, 0))
                    spi = jnp.where(row16 == 0, pi_last, pltpu.roll(si16, 1, 0))
                    dre = or16 * spr + oi16 * spi
                    dim = oi16 * spr - or16 * spi
                    acc = (acc[0] + dre[:SUBLANE] + dre[SUBLANE:], acc[1] + dim[:SUBLANE] + dim[SUBLANE:])
            return (cre, cim) + tuple(acc)

        init = (zero, zero, zero, zero) if with_dlam else (zero, zero)
        fin = lax.fori_loop(0, n_pairs // pairs_per_step, step, init)
        if with_dlam:
            @pl.when(pl.program_id(1) == 0)
            def _():
                dl_ref[...] = jnp.zeros_like(dl_ref)
            dl_ref[0] += fin[2]
            dl_ref[1] += fin[3]

    blk = pl.BlockSpec((2, seq, cw), lambda j, b: (0, b, j))
    in_specs = [pl.BlockSpec((8, SUBLANE, cw), lambda j, b: (0, 0, j)), blk]
    operands = [coef, x]
    out_specs = [blk]
    out_shape = [jax.ShapeDtypeStruct(x.shape, STATE_DTYPE)]
    if with_dlam:
        in_specs.append(blk)
        operands.append(states)
        out_specs.append(pl.BlockSpec((2, SUBLANE, cw), lambda j, b: (0, 0, j)))
        out_shape.append(jax.ShapeDtypeStruct((2, SUBLANE, SSM_COLS), F32))
    scratch = []
    if n_r:
        any_spec = pl.BlockSpec(memory_space=pl.ANY)
        in_specs += [any_spec] * n_r
        operands += [r for r, _ in riders]
        out_specs += [any_spec] * n_r
        out_shape += _ride_out_shapes(riders)
        scratch = _ride_scratch(n_r)
    res = pl.pallas_call(
        body, name=name, grid=(n_cb, n_seq), in_specs=in_specs, out_specs=out_specs, out_shape=out_shape,
        scratch_shapes=scratch, compiler_params=_params(("arbitrary" if n_r else "parallel", "arbitrary")),
    )(*operands)
    return res if (with_dlam or n_r) else res[0]


SSM_WIDE = 4 * SSM_TILE


def ssm_in(name, z, bbd, tm=1024, riders=()):
    n = z.shape[0]
    tm = _tile(n, tm)
    t, w = SSM_TILE, SSM_WIDE
    return _mm(name, z, bbd, (2, n, SSM_COLS), STATE_DTYPE, (n // tm, 2 * SSM_COLS // w, 1),
               pl.BlockSpec((tm, t), lambda i, j, kk: (i, j % 2)),
               pl.BlockSpec((t, w), lambda i, j, kk: (j % 2, j)),
               pl.BlockSpec((None, tm, w), lambda i, j, kk: (j // 2, i, j % 2)), NN, riders=riders)


def ssm_out(name, s, cbd, tm=1024, riders=()):
    n = s.shape[1]
    tm = _tile(n, tm)
    t = SSM_TILE
    w = SSM_WIDE
    return _mm(name, s, cbd, (n, SSM_WIDTH), F32, (n // tm, SSM_WIDTH // t, 2),
               pl.BlockSpec((None, tm, w), lambda i, j, kk: (kk, i, j)),
               pl.BlockSpec((w, t), lambda i, j, kk: (2 * kk + j, j)),
               pl.BlockSpec((tm, t), lambda i, j, kk: (i, j)), NN, riders=riders)


def ssm_out_t(name, dy, cbd, tm=1024):
    n = dy.shape[0]
    tm = _tile(n, tm)
    t, w = SSM_TILE, SSM_WIDE
    return _mm(name, dy, cbd, (2, n, SSM_COLS), STATE_DTYPE, (n // tm, 2 * SSM_COLS // w, 1),
               pl.BlockSpec((tm, t), lambda i, j, kk: (i, j % 2)),
               pl.BlockSpec((w, t), lambda i, j, kk: (j, j % 2)),
               pl.BlockSpec((None, tm, w), lambda i, j, kk: (j // 2, i, j % 2)), NT)


def ssm_in_t(name, a, bbd, res, tm=1024):
    n = a.shape[1]
    tm = _tile(n, tm)
    t, w = SSM_TILE, SSM_WIDE
    return _mm(name, a, bbd, (n, SSM_WIDTH), F32, (n // tm, SSM_WIDTH // t, 2),
               pl.BlockSpec((None, tm, w), lambda i, j, kk: (kk, i, j)),
               pl.BlockSpec((t, w), lambda i, j, kk: (j, 2 * kk + j)),
               pl.BlockSpec((tm, t), lambda i, j, kk: (i, j)), NT, 1.0, res)


def ssm_grad_c(name, s, dy, tk=1024):
    n = s.shape[1]
    tk = _tile(n, tk)
    t, w = SSM_TILE, SSM_WIDE
    return _mm(name, s, dy, (2 * SSM_COLS, t), F32, (2 * SSM_COLS // w, 1, n // tk),
               pl.BlockSpec((None, tk, w), lambda i, j, kk: (i // 2, kk, i % 2)),
               pl.BlockSpec((tk, t), lambda i, j, kk: (kk, i % 2)),
               pl.BlockSpec((w, t), lambda i, j, kk: (i, 0)), TN)


def ssm_grad_b(name, z, a, tk=1024):
    n = z.shape[0]
    tk = _tile(n, tk)
    t, w = SSM_TILE, SSM_WIDE
    return _mm(name, z, a, (t, 2 * SSM_COLS), F32, (1, 2 * SSM_COLS // w, n // tk),
               pl.BlockSpec((tk, t), lambda i, j, kk: (kk, j % 2)),
               pl.BlockSpec((None, tk, w), lambda i, j, kk: (j // 2, kk, j % 2)),
               pl.BlockSpec((t, w), lambda i, j, kk: (0, j)), TN)


_GROUP_TILE = SSM_TILE // SSM_CH


def expand_b(bb_re, bb_im):
    b = jnp.stack([bb_re, bb_im]).reshape(2, SSM_GROUPS, SSM_STATE, SSM_CH)
    eye = jnp.eye(SSM_GROUPS, dtype=F32)
    return jnp.einsum("rgph,gk->ghrkp", b, eye).reshape(SSM_WIDTH, 2 * SSM_COLS).astype(MXU_DTYPE)


def expand_c(c_re, c_im):
    c = jnp.stack([c_re, -c_im])
    eye = jnp.eye(SSM_GROUPS, dtype=F32)
    return jnp.einsum("rghp,gk->rgpkh", c, eye).reshape(2 * SSM_COLS, SSM_WIDTH).astype(MXU_DTYPE)


def _group_pick():
    return (jnp.arange(SSM_GROUPS)[:, None] % _GROUP_TILE == jnp.arange(_GROUP_TILE)[None, :]).astype(F32)


def compact_c(dc):
    x = dc.reshape(2, SSM_GROUPS, SSM_STATE, _GROUP_TILE, SSM_CH)
    g = jnp.einsum("rgpch,gc->rghp", x, _group_pick())
    return g[0], -g[1]


def compact_b(db):
    x = db.reshape(_GROUP_TILE, SSM_CH, 2, SSM_GROUPS, SSM_STATE)
    g = jnp.einsum("chrgp,gc->rgph", x, _group_pick()).reshape(2, SSM_COLS, SSM_CH)
    return g[0], g[1]


def pool_window(name, x, col_block0, seq, out_dtype, adjoint):
    n = x.shape[0]

    def body(x_ref, o_ref):
        win = 2 << pl.program_id(1)
        row = lax.broadcasted_iota(jnp.int32, (seq, POOL_CH), 0)
        v = x_ref[...].astype(F32)
        cnt = jnp.minimum(row + 1, win).astype(F32)
        s = v / cnt if adjoint else v
        for d in (1, 2, 4, 8):
            if adjoint:
                sh = jnp.where((row < seq - d) & (d < win), pltpu.roll(s, seq - d, 0), 0.0)
            else:
                sh = jnp.where((row >= d) & (d < win), pltpu.roll(s, d, 0), 0.0)
            s = s + sh
        o_ref[...] = ((s - v) if adjoint else (s / cnt - v)).astype(out_dtype)

    return pl.pallas_call(
        body, name=name, grid=(n // seq, POOL_GROUPS),
        in_specs=[pl.BlockSpec((seq, POOL_CH), lambda b, g: (b, col_block0 + g))],
        out_specs=pl.BlockSpec((seq, POOL_CH), lambda b, g: (b, g)),
        out_shape=jax.ShapeDtypeStruct((n, POOL_WIDTH), out_dtype),
        compiler_params=_params(("parallel", "parallel")),
    )(x)


def _pool_groups(name, x, col_block, w, out_dtype, contract, tm=1024):
    n = x.shape[0]
    tm = _tile(n, tm)

    def body(x_ref, w_ref, o_ref):
        for g in range(POOL_GROUPS):
            cols = slice(g * POOL_CH, (g + 1) * POOL_CH)
            o_ref[:, cols] = lax.dot_general(x_ref[:, cols].astype(MXU_DTYPE), w_ref[g].astype(MXU_DTYPE),
                                             (contract, ((), ())), preferred_element_type=F32).astype(out_dtype)

    return pl.pallas_call(
        body, name=name, grid=(n // tm,),
        in_specs=[pl.BlockSpec((tm, POOL_WIDTH), lambda i: (i, col_block)),
                  pl.BlockSpec((POOL_GROUPS, POOL_CH, POOL_CH), lambda i: (0, 0, 0))],
        out_specs=pl.BlockSpec((tm, POOL_WIDTH), lambda i: (i, 0)),
        out_shape=jax.ShapeDtypeStruct((n, POOL_WIDTH), out_dtype),
        compiler_params=_params(("parallel",)),
    )(x, w)


def pool_mm(name, q, w, out_dtype):
    return _pool_groups(name, q, 0, w, out_dtype, NN)


def pool_mm_t(name, dy, col_block, w, out_dtype):
    return _pool_groups(name, dy, col_block, w, out_dtype, NT)


def pool_grad_w(name, q, dy, col_block0, tk=1024):
    n = q.shape[0]
    tk = _tile(n, tk)
    return _mm(name, q, dy, (POOL_GROUPS, POOL_CH, POOL_CH), F32, (POOL_GROUPS, 1, n // tk),
               pl.BlockSpec((tk, POOL_CH), lambda i, j, kk: (kk, i)),
               pl.BlockSpec((tk, POOL_CH), lambda i, j, kk: (kk, col_block0 + i)),
               pl.BlockSpec((None, POOL_CH, POOL_CH), lambda i, j, kk: (i, 0, 0)), TN)


def _any_specs(n):
    return [pl.BlockSpec(memory_space=pl.ANY)] * n


def _place():
    x, y, c = lax.axis_index("x"), lax.axis_index("y"), lax.axis_index("c")
    return x, y, c


def sibling_swap(name, arrays):
    n = len(arrays)

    def body(*refs):
        ins, outs = refs[:n], refs[n:2 * n]
        send_sems, recv_sems = refs[2 * n:]
        x, y, c = _place()
        copies = []
        for i in range(n):
            away = pltpu.make_async_remote_copy(
                src_ref=ins[i], dst_ref=outs[i], send_sem=send_sems.at[i], recv_sem=recv_sems.at[i],
                device_id=(x, y, 1 - c), device_id_type=MESH_ID)
            away.start()
            copies.append(away)
        for cp in copies:
            cp.wait()

    return pl.pallas_call(
        body, name=name, in_specs=_any_specs(n), out_specs=_any_specs(n),
        out_shape=[jax.ShapeDtypeStruct(a.shape, a.dtype) for a in arrays],
        scratch_shapes=[pltpu.SemaphoreType.DMA((n,)), pltpu.SemaphoreType.DMA((n,))],
    )(*arrays)


def chip_gather(name, a):
    def body(in_ref, out_ref, send_sems, recv_sems, local_sems):
        copies = _ride_copies([0], [in_ref], [out_ref], send_sems, recv_sems, local_sems)
        for cp in copies:
            cp.start()
        for cp in copies:
            cp.wait()

    return pl.pallas_call(
        body, name=name, in_specs=_any_specs(1), out_specs=_any_specs(1)[0],
        out_shape=jax.ShapeDtypeStruct((N_CHIPS,) + a.shape[1:], a.dtype), scratch_shapes=_ride_scratch(1),
    )(a)


SIBLING = ((0, 0, 1),)
ICI_CHUNK_BYTES = 2 * 1024 * 1024
D2D_CHUNK_BYTES = 4 * 1024 * 1024


def _peer(flip):
    x, y, c = _place()
    return tuple((1 - v) if f else v for v, f in zip((x, y, c), flip))


def _core():
    return lax.axis_index("c")


def _chip():
    return 2 * lax.axis_index("x") + lax.axis_index("y")


def _linear_step(grid):
    i = pl.program_id(0)
    for a in range(1, len(grid)):
        i = i * grid[a] + pl.program_id(a)
    return i


def stream_reduce(name, x, grid, block, own_map, send_maps, flips, out_shape, out_block, out_map, wire_dtype=None):
    n_steps = math.prod(grid)
    n_p = len(flips)
    vm_block = block = tuple(1 if d is None else d for d in block)
    out_block = tuple(1 if d is None else d for d in out_block)
    staged = wire_dtype is not None and wire_dtype != x.dtype
    slot_dtype = wire_dtype if staged else x.dtype

    def body(own_ref, *rest):
        send_refs = rest[:n_p]
        o_ref, recv, send_sems, recv_sems, credits = rest[n_p:n_p + 5]
        stage = rest[n_p + 5] if staged else None
        i = _linear_step(grid)
        s = i % 2
        copies = []
        for j, flip in enumerate(flips):
            src = send_refs[j]
            if staged:
                stage[j, s] = send_refs[j][...].astype(wire_dtype)
                src = stage.at[j, s]

            @pl.when(i >= 2)
            def _():
                pl.semaphore_wait(credits.at[j, s], 1)
            cp = pltpu.make_async_remote_copy(
                src_ref=src, dst_ref=recv.at[j, s], send_sem=send_sems.at[j, s], recv_sem=recv_sems.at[j, s],
                device_id=_peer(flip), device_id_type=MESH_ID)
            cp.start()
            copies.append(cp)
        acc = own_ref[...]
        for j, cp in enumerate(copies):
            cp.wait_recv()
            acc = acc + recv[j, s].astype(acc.dtype)
        o_ref[...] = acc.reshape(o_ref.shape)
        for cp in copies:
            cp.wait_send()
        for j, flip in enumerate(flips):
            @pl.when(i < n_steps - 2)
            def _():
                pl.semaphore_signal(credits.at[j, s], inc=1, device_id=_peer(flip), device_id_type=MESH_ID)

    in_specs = [pl.BlockSpec(block, own_map)] + [pl.BlockSpec(block, m) for m in send_maps]
    scratch = [pltpu.VMEM((n_p, 2) + vm_block, slot_dtype), pltpu.SemaphoreType.DMA((n_p, 2)),
               pltpu.SemaphoreType.DMA((n_p, 2)), pltpu.SemaphoreType.REGULAR((n_p, 2))]
    if staged:
        scratch.append(pltpu.VMEM((n_p, 2) + vm_block, slot_dtype))
    return pl.pallas_call(
        body, name=name, grid=grid, in_specs=in_specs, out_specs=pl.BlockSpec(out_block, out_map),
        out_shape=jax.ShapeDtypeStruct(out_shape, x.dtype), scratch_shapes=scratch,
        compiler_params=_params(("arbitrary",) * len(grid)),
    )(*([x] * (1 + n_p)))


def stream_gather(name, x, grid, block, in_map, flips, out_shape, out_block, out_map):
    n_p = len(flips)
    assert grid[-1] == n_p + 1
    n_steps = math.prod(grid[:-1])
    vm_block = block = tuple(1 if d is None else d for d in block)
    out_block = tuple(1 if d is None else d for d in out_block)

    def body(x_ref, o_ref, recv, send_sems, recv_sems, credits):
        i = _linear_step(grid[:-1])
        q = pl.program_id(len(grid) - 1)
        s = i % 2

        def copy(j):
            return pltpu.make_async_remote_copy(
                src_ref=x_ref, dst_ref=recv.at[j, s], send_sem=send_sems.at[j, s], recv_sem=recv_sems.at[j, s],
                device_id=_peer(flips[j]), device_id_type=MESH_ID)

        @pl.when(q == 0)
        def _():
            for j in range(n_p):
                @pl.when(i >= 2)
                def _():
                    pl.semaphore_wait(credits.at[j, s], 1)
                copy(j).start()
            o_ref[...] = x_ref[...].reshape(o_ref.shape)
            for j in range(n_p):
                copy(j).wait_send()

        for j in range(n_p):
            @pl.when(q == j + 1)
            def _():
                copy(j).wait_recv()
                o_ref[...] = recv[j, s].reshape(o_ref.shape)

                @pl.when(i < n_steps - 2)
                def _():
                    pl.semaphore_signal(credits.at[j, s], inc=1, device_id=_peer(flips[j]), device_id_type=MESH_ID)

    return pl.pallas_call(
        body, name=name, grid=grid, in_specs=[pl.BlockSpec(block, in_map)], out_specs=pl.BlockSpec(out_block, out_map),
        out_shape=jax.ShapeDtypeStruct(out_shape, x.dtype),
        scratch_shapes=[pltpu.VMEM((n_p, 2) + vm_block, x.dtype), pltpu.SemaphoreType.DMA((n_p, 2)),
                        pltpu.SemaphoreType.DMA((n_p, 2)), pltpu.SemaphoreType.REGULAR((n_p, 2))],
        compiler_params=_params(("arbitrary",) * len(grid)),
    )(x)


def _chip_of_substep(q):
    mask = jnp.where(q == 1, 2, jnp.where(q == 2, 1, jnp.where(q == 3, 3, 0)))
    return jnp.bitwise_xor(_chip(), mask)


def gather_weight(name, shard):
    nl, r, c = shard.shape
    r2 = r // 2
    f32_per_elem = 4 // shard.dtype.itemsize
    tr = _rows_tile(r2, c, budget=ICI_CHUNK_BYTES * f32_per_elem, step=16)
    nb = r2 // tr
    half = stream_gather(
        name + "_chips", shard, (nl, nb, N_CHIPS), (None, tr, c), lambda l, i, q: (l, _core() * nb + i, 0), CHIPS,
        (nl, N_CHIPS, r2, c), (None, None, tr, c), lambda l, i, q: (l, _chip_of_substep(q), i, 0))
    trd = _rows_tile(r2, c, budget=D2D_CHUNK_BYTES * f32_per_elem, step=16)
    nbd = r2 // trd
    both = stream_gather(
        name + "_cores", half, (nl, N_CHIPS, nbd, 2), (None, None, trd, c), lambda l, k, i, q: (l, k, i, 0), SIBLING,
        (nl, N_CHIPS, 2, r2, c), (None, None, None, trd, c), lambda l, k, i, q: (l, k, _core() + q - 2 * _core() * q, i, 0))
    return both.reshape(nl, N_CHIPS, r, c)


def reduce_scatter_streamed(name, g):
    nl, _, r, c = g.shape
    r2 = r // 2
    tr_d2d = _rows_tile(r2, c, budget=D2D_CHUNK_BYTES, step=16)
    nbd = r2 // tr_d2d
    chip_sum = stream_reduce(
        name + "_cores", g, (nl, N_CHIPS, nbd), (None, None, tr_d2d, c),
        lambda l, k, i: (l, k, _core() * nbd + i, 0), [lambda l, k, i: (l, k, (1 - _core()) * nbd + i, 0)], SIBLING,
        (nl, N_CHIPS, r2, c), (None, None, tr_d2d, c), lambda l, k, i: (l, k, i, 0), wire_dtype=GRAD_WIRE_DTYPE)
    tr = _rows_tile(r2, c, budget=ICI_CHUNK_BYTES, step=16)
    nb = r2 // tr
    blk4 = (None, None, tr, c)
    masks = (2, 1, 3)
    mine = stream_reduce(
        name + "_chips", chip_sum, (nl, nb), blk4,
        lambda l, i: (l, _chip(), i, 0),
        [functools.partial(lambda l, i, m: (l, jnp.bitwise_xor(_chip(), m), i, 0), m=m) for m in masks], CHIPS,
        (nl, r2, c), (None, tr, c), lambda l, i: (l, i, 0), wire_dtype=GRAD_WIRE_DTYPE)
    both = stream_gather(
        name + "_join", mine, (nl, nbd, 2), (None, tr_d2d, c), lambda l, i, q: (l, i, 0), SIBLING,
        (nl, 2, r2, c), (None, None, tr_d2d, c), lambda l, i, q: (l, _core() + q - 2 * _core() * q, i, 0))
    return both.reshape(nl, r, c)


def add2(name, a, b):
    shape = a.shape
    a2, b2 = a.reshape(-1, shape[-1]), b.reshape(-1, shape[-1])
    rows, w = a2.shape
    tm = _rows_tile(rows, w)
    return _rowwise(name, lambda u, v: u + v, rows, tm, [(a2, w, 0), (b2, w, 0)], [], [(w, F32)])[0].reshape(shape)


def _rows_tile(rows, width, budget=2 * 1024 * 1024, step=SUBLANE):
    best = step
    t = step
    while t <= rows:
        if rows % t == 0 and t * width * 4 <= budget:
            best = t
        t += step
    return best


def all_reduce_small(flat):
    other = sibling_swap("ar_swap", [flat])[0]
    chip = add2("ar_add_cores", flat, other)
    slots = chip_gather("ar_chips", chip.reshape((1,) + chip.shape))
    rows = flat.shape[0]
    tm = _rows_tile(rows, LANE)
    nb = rows // tm
    s2 = slots.reshape(N_CHIPS * rows, LANE)
    return _rowwise("ar_sum_chips", lambda a, b, c, d: ((a + b) + c) + d, rows, tm,
                    [(s2, LANE, 0, k * nb) for k in range(N_CHIPS)], [], [(LANE, F32)])[0]


def _adamw_math(w, g, m, v):
    m = ADAM_B1 * m + (1.0 - ADAM_B1) * g
    v = ADAM_B2 * v + (1.0 - ADAM_B2) * (g * g)
    m_hat = m / (1.0 - ADAM_B1 ** ADAM_STEP)
    v_hat = v / (1.0 - ADAM_B2 ** ADAM_STEP)
    delta = -ADAM_LR * (m_hat / (jnp.sqrt(v_hat) + ADAM_EPS) + ADAM_WD * w)
    return delta, m, v


def adamw(name, w, g, m, v):
    shape = w.shape
    width = shape[-1]
    flat = [t.reshape(-1, width) for t in (w, g, m, v)]
    rows = flat[0].shape[0]
    tm = _rows_tile(rows, width, budget=1024 * 1024)
    res = _rowwise(name, _adamw_math, rows, tm, [(t, width, 0) for t in flat], [], [(width, F32)] * 3)
    return [r.reshape(shape) for r in res]


def ffn_in(name, xn, wi, riders=(), tm=512):
    n, d = xn.shape
    c = wi.shape[2]
    tm = _tile(n, tm)
    grid = (n // tm, 2)
    n_r = len(riders)

    def body(*refs):
        x_ref, wg_ref, wu_ref = refs[:3]
        g_ref, u_ref, a_ref = refs[3 + n_r:6 + n_r]
        if n_r:
            _ride(grid, [l for _, l in riders], refs[3:3 + n_r], refs[6 + n_r:6 + 2 * n_r], refs[6 + 2 * n_r:])
        x = x_ref[...].astype(MXU_DTYPE)
        g = jnp.dot(x, wg_ref[...].astype(MXU_DTYPE), preferred_element_type=F32)
        u = jnp.dot(x, wu_ref[...].astype(MXU_DTYPE), preferred_element_type=F32)
        g_ref[...] = g.astype(g_ref.dtype)
        u_ref[...] = u.astype(u_ref.dtype)
        a_ref[...] = (g * jax.nn.sigmoid(g) * u).astype(a_ref.dtype)

    any_spec = pl.BlockSpec(memory_space=pl.ANY)
    out_blk = pl.BlockSpec((tm, c), lambda i, j: (i, j))
    res = pl.pallas_call(
        body, name=name, grid=grid,
        in_specs=[pl.BlockSpec((tm, d), lambda i, j: (i, 0)), pl.BlockSpec((None, d, c), lambda i, j: (j, 0, 0)),
                  pl.BlockSpec((None, d, c), lambda i, j: (j + 2, 0, 0))] + [any_spec] * n_r,
        out_specs=[out_blk] * 3 + [any_spec] * n_r,
        out_shape=[jax.ShapeDtypeStruct((n, 2 * c), BF16)] * 3 + _ride_out_shapes(riders),
        scratch_shapes=_ride_scratch(n_r) if n_r else [],
        compiler_params=_params(("arbitrary", "arbitrary") if n_r else ("parallel", "parallel")),
    )(xn, wi, wi, *[r for r, _ in riders])
    return res[:3], list(res[3:])


def _rows(a):
    return a.reshape(N_CHIPS * a.shape[1], a.shape[2])


def _ffn_fwd(tag, h, g_norm, wi, wo, ride_in=(), ride_out=()):
    xn = norm_fwd(tag + "_norm", h, g_norm)
    (g, u, act), got_in = ffn_in(tag + "_wi", xn, wi, riders=ride_in)
    if wo is None:
        wo = _rows(got_in[0])
    out = mm_nn(tag + "_wo", act, wo, F32, alpha=0.5, res=h, tm=512, tk=2816, riders=ride_out)
    got_out = []
    if ride_out:
        out, got_out = out[0], list(out[1:])
    return out, (xn, g, u, act), got_in, got_out


def ffn_dact(name, d, wo_t, gate, up, tm=512):
    n, dm = d.shape
    ff = wo_t.shape[1]
    c = ff // 2
    tm = _tile(n, tm)

    def body(d_ref, w_ref, g_ref, u_ref, o_ref):
        da = 0.5 * jnp.dot(d_ref[...].astype(MXU_DTYPE), w_ref[...].astype(MXU_DTYPE), preferred_element_type=F32)
        g, u = g_ref[...].astype(F32), u_ref[...].astype(F32)
        sg = jax.nn.sigmoid(g)
        dg = da * u * (sg * (1.0 + g * (1.0 - sg)))
        du = da * (g * sg)
        o_ref[...] = jnp.concatenate([dg, du], axis=1).astype(o_ref.dtype)

    half = pl.BlockSpec((tm, c), lambda i, j: (i, j))
    return pl.pallas_call(
        body, name=name, grid=(n // tm, 2),
        in_specs=[pl.BlockSpec((tm, dm), lambda i, j: (i, 0)), pl.BlockSpec((dm, c), lambda i, j: (0, j)), half, half],
        out_specs=pl.BlockSpec((tm, 2 * c), lambda i, j: (i, j)),
        out_shape=jax.ShapeDtypeStruct((n, 2 * ff), BF16),
        compiler_params=_params(("parallel", "parallel")),
    )(d, wo_t, gate, up)


def _ffn_bwd(tag, d, h, g_norm, wi_t, wo_t, saved, into_wi, into_wo):
    xn, gate, up, act = saved
    n, dm = h.shape
    dwo = mm_tn(tag + "_dwo", act, d, F32, alpha=0.5, tm=1408, into=into_wo)
    dgu = ffn_dact(tag + "_dact", d, wo_t, gate, up)
    t, k = xn.shape
    c = dgu.shape[1] // N_CHIPS
    tm_, tk_ = _tile(k, 1024), _tile(t, 1024)
    dwi = _mm(tag + "_dwi", xn, dgu, (N_CHIPS, k, c), F32, (k // tm_, N_CHIPS, t // tk_),
              pl.BlockSpec((tk_, tm_), lambda i, j, kk: (kk, i)),
              pl.BlockSpec((tk_, c), lambda i, j, kk: (kk, j)),
              pl.BlockSpec((None, tm_, c), lambda i, j, kk: ((j % 2) * 2 + j // 2, i, 0)), TN, into=into_wi)
    tmx, tnx = _tile(n, 1024), _tile(dm, 1024)
    dxn = _mm(tag + "_dxn", dgu, wi_t, (n, dm), F32, (n // tmx, dm // tnx, N_CHIPS),
              pl.BlockSpec((tmx, c), lambda i, j, kk: (i, kk)),
              pl.BlockSpec((None, c, tnx), lambda i, j, kk: ((kk % 2) * 2 + kk // 2, 0, j)),
              pl.BlockSpec((tmx, tnx), lambda i, j, kk: (i, j)), NN)
    d_in, dg_norm = norm_bwd(tag + "_dnorm", dxn, h, g_norm, d)
    return d_in, dg_norm.reshape(dm), dwi, dwo


def _col(v):
    return v.reshape(SSM_COLS, 1)


def _layer_fwd(h, lw, ffn1_wi, ride, next_wi, p_l, seq):
    n, d = h.shape
    got = {"ffn1_wi": ffn1_wi}
    h1, ffn1_saved, got_in, got_out = _ffn_fwd("ffn1", h, lw["ffn1_norm"], ffn1_wi, None, (ride("ffn1_wo"),),
                                               (ride("w_in"), ride("ssm_w_glu"), ride("w_out")))
    got["ffn1_wo"] = got_in[0]
    got["w_in"], got["ssm_w_glu"], got["w_out"] = got_out

    xn2 = norm_fwd("mix_norm", h1, lw["mix_norm"])
    z = mm_nn("mix_in", xn2, _rows(got["w_in"]), F32)
    log_dt = jnp.repeat(lw["ssm_log_dt"], SSM_STATE)
    b_re, b_im = lw["ssm_b_re"].reshape(SSM_COLS, SSM_CH), lw["ssm_b_im"].reshape(SSM_COLS, SSM_CH)
    pw_re, pw_im, bb_re, bb_im = ssm_prep("ssm_prep", _col(lw["ssm_lambda_re"]), _col(lw["ssm_lambda_im"]), _col(log_dt), b_re, b_im)
    bbd = expand_b(bb_re, bb_im)
    cbd = expand_c(lw["ssm_c_re"], lw["ssm_c_im"])
    bu, got["ple_w_gate"], got["ple_w_proj"] = ssm_in("ssm_in", z, bbd, riders=(ride("ple_w_gate"), ride("ple_w_proj")))
    s, got["ffn2_wi"] = ssm_scan("ssm_scan", scan_coefficients(pw_re, pw_im, False), bu, seq, False, riders=(ride("ffn2_wi"),))
    y0c, got["ffn2_wo"] = ssm_out("ssm_out", s, cbd, riders=(ride("ffn2_wo"),))

    def skip_gelu(yc, zs, dvec):
        y0 = yc + dvec * zs
        return y0, _gelu(y0)

    y0, y1 = _rowwise("ssm_gelu", skip_gelu, n, 512, [(y0c, SSM_WIDTH, 0), (z, SSM_WIDTH, 0)],
                      [lw["ssm_d"].reshape(1, SSM_WIDTH)], [(SSM_WIDTH, F32), (SSM_WIDTH, F32)])
    t = mm_nn("ssm_glu_mm", y1, _rows(got["ssm_w_glu"]), F32)
    y2 = _rowwise("ssm_glu", lambda a, b: a * jax.nn.sigmoid(b), n, 512, [(y1, SSM_WIDTH, 0), (t, SSM_WIDTH, 0)], [],
                  [(SSM_WIDTH, BF16)])[0]

    q = pool_window("pool_window", z, SSM_WIDTH // POOL_CH, seq, BF16, False)
    wp_eff = lw["pool_w"] * lw["pool_scale"].reshape(POOL_GROUPS, 1, POOL_CH)
    yp = pool_mm("pool_mm", q, wp_eff, BF16)
    m = jnp.concatenate([y2, yp], axis=1)
    h2 = mm_nn("mix_out", m, _rows(got["w_out"]), F32, res=h1)

    h3, ffn2_saved, got_in, _ = _ffn_fwd("ffn2", h2, lw["ffn2_norm"], got["ffn2_wi"], _rows(got["ffn2_wo"]),
                                         () if next_wi is None else (next_wi,))
    next_ffn1_wi = got_in[0] if got_in else None

    xn4 = norm_fwd("ple_norm", h3, lw["ple_norm"])
    tg = mm_nn("ple_gate", xn4, _rows(got["ple_w_gate"]), F32)
    e = mm_nn_colsharded("ple_proj", p_l, got["ple_w_proj"], F32)
    h4 = _rowwise("ple_add", lambda a, b, c: a + jax.nn.sigmoid(b) * c, n, 512, [(h3, d, 0), (tg, d, 0), (e, d, 0)], [], [(d, F32)])[0]
    saved = dict(h=h, h1=h1, h2=h2, h3=h3, ffn1=ffn1_saved, ffn2=ffn2_saved, xn2=xn2, z=z, s=s, y0=y0, y1=y1, t=t, m=m, q=q,
                 xn4=xn4, tg=tg, e=e, pw_re=pw_re, pw_im=pw_im, bbd=bbd, cbd=cbd)
    return h4, saved, got, next_ffn1_wi


def _layer_bwd(d, lw, p_l, sv, seq, bufs, layer):
    n, dm = d.shape
    g = {}
    def ple_bwd(dd, tg, e):
        gate = jax.nn.sigmoid(tg)
        return dd * e * gate * (1.0 - gate), dd * gate

    dtg, de = _rowwise("ple_dadd", ple_bwd, n, 512, [(d, dm, 0), (sv["tg"], dm, 0), (sv["e"], dm, 0)], [], [(dm, BF16), (dm, BF16)])
    g["ple_w_proj"] = mm_tn_colsharded("ple_dproj", p_l, de, F32, into=(bufs["ple_w_proj"], layer))
    g["ple_w_gate"] = mm_tn("ple_dgate_w", sv["xn4"], dtg, F32, into=(bufs["ple_w_gate"], layer))
    dxn4 = mm_nn("ple_dgate_x", dtg, lw["ple_w_gate_t"], F32)
    d, dg = norm_bwd("ple_dnorm", dxn4, sv["h3"], lw["ple_norm"], d)
    g["ple_norm"] = dg.reshape(dm)

    d, g["ffn2_norm"], g["ffn2_wi"], g["ffn2_wo"] = _ffn_bwd("ffn2b", d, sv["h2"], lw["ffn2_norm"], lw["ffn2_wi_t"], lw["ffn2_wo_t"], sv["ffn2"],
                                                            (bufs["ffn2_wi"], layer), (bufs["ffn2_wo"], layer))

    dmix = mm_nn("mix_dout_x", d, lw["w_out_t"], F32)
    g["w_out"] = mm_tn("mix_dout_w", sv["m"], d, F32, into=(bufs["w_out"], layer))
    pool_cb = SSM_WIDTH // POOL_CH
    wp_eff = lw["pool_w"] * lw["pool_scale"].reshape(POOL_GROUPS, 1, POOL_CH)
    dq = pool_mm_t("pool_dmm_x", dmix, SSM_WIDTH // POOL_WIDTH, wp_eff, F32)
    dwp_eff = pool_grad_w("pool_dmm_w", sv["q"], dmix, pool_cb)
    g["pool_w"] = dwp_eff * lw["pool_scale"].reshape(POOL_GROUPS, 1, POOL_CH)
    g["pool_scale"] = jnp.sum(dwp_eff * lw["pool_w"], axis=1).reshape(POOL_WIDTH)
    dzp = pool_window("pool_dwindow", dq, 0, seq, BF16, True)

    def glu_bwd(dy2, y1, t):
        sg = jax.nn.sigmoid(t)
        return dy2 * y1 * sg * (1.0 - sg), dy2 * sg

    dt_, dy1a = _rowwise("ssm_dglu", glu_bwd, n, 512, [(dmix, SSM_WIDTH, 0), (sv["y1"], SSM_WIDTH, 0), (sv["t"], SSM_WIDTH, 0)], [],
                         [(SSM_WIDTH, BF16), (SSM_WIDTH, F32)])
    g["ssm_w_glu"] = mm_tn("ssm_dglu_w", sv["y1"], dt_, F32, into=(bufs["ssm_w_glu"], layer))
    dy1b = mm_nn("ssm_dglu_x", dt_, lw["ssm_w_glu_t"], F32)

    def gelu_bwd(da, db, y0, zs, dvec):
        dy0 = (da + db) * _gelu_grad(y0)
        return dy0, dy0 * dvec, jnp.sum(dy0 * zs, axis=0, keepdims=True)

    dy0, dzs_a, dd = _rowwise("ssm_dgelu", gelu_bwd, n, 512,
                              [(dy1a, SSM_WIDTH, 0), (dy1b, SSM_WIDTH, 0), (sv["y0"], SSM_WIDTH, 0), (sv["z"], SSM_WIDTH, 0)],
                              [lw["ssm_d"].reshape(1, SSM_WIDTH)], [(SSM_WIDTH, F32), (SSM_WIDTH, F32)], [(1, SSM_WIDTH)])
    g["ssm_d"] = dd.reshape(SSM_WIDTH)
    g["ssm_c_re"], g["ssm_c_im"] = compact_c(ssm_grad_c("ssm_dc", sv["s"], dy0))
    v = ssm_out_t("ssm_dout", dy0, sv["cbd"])
    a, dlam = ssm_scan("ssm_scan_adj", scan_coefficients(sv["pw_re"], sv["pw_im"], True), v, seq, True, states=sv["s"])
    dbb_re, dbb_im = compact_b(ssm_grad_b("ssm_db", sv["z"], a))
    dzs = ssm_in_t("ssm_din", a, sv["bbd"], dzs_a)
    dlam = jnp.sum(dlam, axis=1)
    log_dt = jnp.repeat(lw["ssm_log_dt"], SSM_STATE)
    b_re, b_im = lw["ssm_b_re"].reshape(SSM_COLS, SSM_CH), lw["ssm_b_im"].reshape(SSM_COLS, SSM_CH)
    glr, gli, gld, gbr, gbi = ssm_prep_bwd("ssm_prep_bwd", _col(lw["ssm_lambda_re"]), _col(lw["ssm_lambda_im"]), _col(log_dt), b_re, b_im,
                                           _col(dlam[0]), _col(dlam[1]), dbb_re, dbb_im)
    g["ssm_lambda_re"] = glr.reshape(SSM_GROUPS, SSM_STATE)
    g["ssm_lambda_im"] = gli.reshape(SSM_GROUPS, SSM_STATE)
    g["ssm_log_dt"] = jnp.sum(gld.reshape(SSM_GROUPS, SSM_STATE), axis=1)
    g["ssm_b_re"] = gbr.reshape(SSM_GROUPS, SSM_STATE, SSM_CH)
    g["ssm_b_im"] = gbi.reshape(SSM_GROUPS, SSM_STATE, SSM_CH)

    dz = jnp.concatenate([dzs.astype(BF16), dzp], axis=1)
    g["w_in"] = mm_tn("mix_din_w", sv["xn2"], dz, F32, into=(bufs["w_in"], layer))
    dxn2 = mm_nn("mix_din_x", dz, lw["w_in_t"], F32)
    d, dg = norm_bwd("mix_dnorm", dxn2, sv["h1"], lw["mix_norm"], d)
    g["mix_norm"] = dg.reshape(dm)

    d, g["ffn1_norm"], g["ffn1_wi"], g["ffn1_wo"] = _ffn_bwd("ffn1b", d, sv["h"], lw["ffn1_norm"], lw["ffn1_wi_t"], lw["ffn1_wo_t"], sv["ffn1"],
                                                            (bufs["ffn1_wi"], layer), (bufs["ffn1_wo"], layer))
    return d, g


def _flatten_small(tensors):
    flat = jnp.concatenate([t.reshape(-1) for t in tensors])
    pad = (-flat.shape[0]) % (SUBLANE * LANE)
    return jnp.pad(flat, (0, pad)).reshape(-1, LANE)


def _unflatten_small(flat, like):
    flat = flat.reshape(-1)
    out, off = [], 0
    for t in like:
        out.append(flat[off:off + t.size].reshape(t.shape))
        off += t.size
    return out


def kernel(x, p, ffn1_norm, ffn1_wi, ffn1_wo, mix_norm, w_in, ssm_lambda_re, ssm_lambda_im, ssm_log_dt, ssm_b_re, ssm_b_im, ssm_c_re, ssm_c_im, ssm_d, ssm_w_glu, pool_w, pool_scale, w_out, ffn2_norm, ffn2_wi, ffn2_wo, ple_norm, ple_w_gate, ple_w_proj, final_norm, loss_target, m_ffn1_norm, m_ffn1_wi, m_ffn1_wo, m_mix_norm, m_w_in, m_ssm_lambda_re, m_ssm_lambda_im, m_ssm_log_dt, m_ssm_b_re, m_ssm_b_im, m_ssm_c_re, m_ssm_c_im, m_ssm_d, m_ssm_w_glu, m_pool_w, m_pool_scale, m_w_out, m_ffn2_norm, m_ffn2_wi, m_ffn2_wo, m_ple_norm, m_ple_w_gate, m_ple_w_proj, m_final_norm, v_ffn1_norm, v_ffn1_wi, v_ffn1_wo, v_mix_norm, v_w_in, v_ssm_lambda_re, v_ssm_lambda_im, v_ssm_log_dt, v_ssm_b_re, v_ssm_b_im, v_ssm_c_re, v_ssm_c_im, v_ssm_d, v_ssm_w_glu, v_pool_w, v_pool_scale, v_w_out, v_ffn2_norm, v_ffn2_wi, v_ffn2_wo, v_ple_norm, v_ple_w_gate, v_ple_w_proj, v_final_norm):
    given = dict(locals())
    w = {k: given[k] for k in WEIGHTS}
    mom = {k: given["m_" + k] for k in WEIGHTS}
    var = {k: given["v_" + k] for k in WEIGHTS}
    bsz, seq, dm = x.shape
    n = bsz * seq
    depth = ffn1_wi.shape[0]

    shards = {k: w[k].astype(MXU_DTYPE) for k in BIG}

    def layer_weights(l, gathered):
        lw = {k: w[k][l] for k in SMALL if k != "final_norm"}
        for k, a in gathered.items():
            if k in COL_SHARDED:
                lw[k] = a
                if k != "ple_w_proj":
                    lw[k + "_t"] = jnp.swapaxes(a, 1, 2)
            else:
                lw[k] = a.reshape(N_CHIPS * a.shape[1], a.shape[2])
                lw[k + "_t"] = lw[k].T
        return lw

    ffn1_wi_l = gather_weight("gather_ffn1_wi", shards["ffn1_wi"][:1])[0]

    p2 = p.reshape(depth, n, p.shape[-1])
    h_last = x.reshape(n, dm)
    saved, layers = [], []
    for l in range(depth):
        small = {k: w[k][l] for k in SMALL if k != "final_norm"}
        next_wi = (shards["ffn1_wi"], l + 1) if l + 1 < depth else None
        h_last, sv, gathered, ffn1_wi_l = _layer_fwd(h_last, small, ffn1_wi_l, functools.partial(lambda k, l: (shards[k], l), l=l),
                                                      next_wi, p2[l], seq)
        layers.append(layer_weights(l, gathered))
        saved.append(sv)

    def head(hh, tgt, gf):
        r = lax.rsqrt(jnp.mean(hh * hh, axis=-1, keepdims=True) + NORM_EPS)
        xh = hh * r
        diff = xh * gf - tgt
        dy = diff * (1.0 / dm)
        dxh = dy * gf
        dx = r * (dxh - xh * jnp.mean(dxh * xh, axis=-1, keepdims=True))
        return dx, jnp.sum(diff * diff, axis=0, keepdims=True) * (0.5 / dm), jnp.sum(dy * xh, axis=0, keepdims=True)

    d_last, loss_cols, g_final = _rowwise("loss_head", head, n, 512, [(h_last, dm, 0), (loss_target.reshape(n, dm), dm, 0)],
                                          [final_norm.reshape(1, dm)], [(dm, F32)], [(1, dm), (1, dm)])
    loss = lax.psum(jnp.sum(loss_cols), ("x", "y", "c"))

    d_x = d_last
    layer_grads = [None] * depth
    bufs = {k: lax.empty((depth, N_CHIPS) + w[k].shape[1:], F32) for k in BIG}
    for l in reversed(range(depth)):
        d_x, layer_grads[l] = _layer_bwd(d_x, layers[l], p2[l], saved[l], seq, bufs, l)
        bufs = {k: layer_grads[l][k] for k in BIG}
    grads = {k: jnp.stack([g[k] for g in layer_grads]) for k in layer_grads[0] if k not in BIG}
    grads.update(bufs)
    grad_x = d_x.reshape(bsz, seq, dm)

    big_sum = [reduce_scatter_streamed("rs_" + k, grads[k]) for k in BIG]
    small_keys = [k for k in SMALL]
    small_parts = [grads[k] if k != "final_norm" else g_final.reshape(dm) for k in small_keys]
    small_sum = _unflatten_small(all_reduce_small(_flatten_small(small_parts)), small_parts)
    g_out = dict(zip(BIG, big_sum))
    g_out.update(dict(zip(small_keys, small_sum)))
    for k in BIG:
        g_out[k] = g_out[k].reshape(w[k].shape)

    delta, new_m, new_v = {}, {}, {}
    for k in BIG:
        delta[k], new_m[k], new_v[k] = adamw("adamw_" + k, w[k], g_out[k], mom[k], var[k])
    sw = adamw("adamw_small", *[_flatten_small([t[k] for k in small_keys]) for t in (w, g_out, mom, var)])
    for name, flat in zip((delta, new_m, new_v), sw):
        for k, t in zip(small_keys, _unflatten_small(flat, [w[k] for k in small_keys])):
            name[k] = t

    return (loss, grad_x, *[g_out[k] for k in WEIGHTS], *[delta[k] for k in WEIGHTS],
            *[new_m[k] for k in WEIGHTS], *[new_v[k] for k in WEIGHTS])
```

```python
import functools
import math

import jax
import jax.numpy as jnp
from jax import lax
from jax.experimental import pallas as pl
from jax.experimental.pallas import tpu as pltpu

F32 = jnp.float32
BF16 = jnp.bfloat16
MXU_DTYPE = jnp.bfloat16
GRAD_WIRE_DTYPE = jnp.bfloat16
STATE_DTYPE = jnp.bfloat16
VMEM_LIMIT = 56 * 1024 * 1024
LANE = 128
SUBLANE = 8

N_CHIPS = 4
SSM_GROUPS = 32
SSM_STATE = 64
SSM_CH = 16
SSM_WIDTH = SSM_GROUPS * SSM_CH
SSM_COLS = SSM_GROUPS * SSM_STATE
POOL_GROUPS = 4
POOL_CH = 128
POOL_WIDTH = POOL_GROUPS * POOL_CH
SSM_TILE = 256
NORM_EPS = 1e-6
ADAM_LR = 0.001
ADAM_B1 = 0.9
ADAM_B2 = 0.999
ADAM_EPS = 1e-08
ADAM_WD = 0.01
ADAM_STEP = 10
MESH_ID = pl.DeviceIdType.MESH

BIG = ("ffn1_wi", "ffn1_wo", "w_in", "ssm_w_glu", "w_out", "ffn2_wi", "ffn2_wo", "ple_w_gate", "ple_w_proj")
COL_SHARDED = ("ffn1_wi", "ffn2_wi", "ple_w_proj")
SMALL = ("ffn1_norm", "mix_norm", "ssm_lambda_re", "ssm_lambda_im", "ssm_log_dt", "ssm_b_re", "ssm_b_im",
         "ssm_c_re", "ssm_c_im", "ssm_d", "pool_w", "pool_scale", "ffn2_norm", "ple_norm", "final_norm")
WEIGHTS = ("ffn1_norm", "ffn1_wi", "ffn1_wo", "mix_norm", "w_in", "ssm_lambda_re", "ssm_lambda_im", "ssm_log_dt",
           "ssm_b_re", "ssm_b_im", "ssm_c_re", "ssm_c_im", "ssm_d", "ssm_w_glu", "pool_w", "pool_scale", "w_out",
           "ffn2_norm", "ffn2_wi", "ffn2_wo", "ple_norm", "ple_w_gate", "ple_w_proj", "final_norm")


def _tile(dim, target):
    best = None
    t = LANE
    while t <= min(dim, target):
        if dim % t == 0:
            best = t
        t += LANE
    return best if best is not None else dim


def _params(sem):
    return pltpu.CompilerParams(dimension_semantics=sem, vmem_limit_bytes=VMEM_LIMIT)


CHIPS = ((1, 0, 0), (0, 1, 0), (1, 1, 0))


def _ride_scratch(n):
    return [pltpu.SemaphoreType.DMA((3 * n,)), pltpu.SemaphoreType.DMA((3 * n,)), pltpu.SemaphoreType.DMA((n,))]


def _ride_copies(layers, ins, outs, send_sems, recv_sems, local_sems):
    x, y, c = lax.axis_index("x"), lax.axis_index("y"), lax.axis_index("c")
    me = 2 * x + y
    copies = []
    for i in range(len(ins)):
        src = ins[i].at[pl.ds(layers[i], 1)]
        dst = outs[i].at[pl.ds(me, 1)]
        copies.append(pltpu.make_async_copy(src, dst, local_sems.at[i]))
        for f, (fx, fy, _) in enumerate(CHIPS):
            peer = ((1 - x) if fx else x, (1 - y) if fy else y, c)
            copies.append(pltpu.make_async_remote_copy(
                src_ref=src, dst_ref=dst, send_sem=send_sems.at[3 * i + f], recv_sem=recv_sems.at[3 * i + f],
                device_id=peer, device_id_type=pl.DeviceIdType.MESH))
    return copies


def _ride(grid, layers, ins, outs, sems):
    if not ins:
        return
    ids = [pl.program_id(a) for a in range(len(grid))]
    first = functools.reduce(jnp.logical_and, [i == 0 for i in ids])
    last = functools.reduce(jnp.logical_and, [i == g - 1 for i, g in zip(ids, grid)])

    @pl.when(first)
    def _():
        for cp in _ride_copies(layers, ins, outs, *sems):
            cp.start()

    @pl.when(last)
    def _():
        for cp in _ride_copies(layers, ins, outs, *sems):
            cp.wait()


def _ride_out_shapes(riders):
    return [jax.ShapeDtypeStruct((N_CHIPS,) + r.shape[1:], r.dtype) for r, _ in riders]


def _mm(name, a, b, out_shape, out_dtype, grid, a_spec, b_spec, o_spec, contract, alpha=1.0, res=None, riders=(), into=None):
    n_k = grid[2]
    acc_shape = tuple(d for d in o_spec.block_shape if d is not None)
    n_in = 2 + (res is not None) + (into is not None)
    n_r = len(riders)
    ride_layers = [l for _, l in riders]

    def body(*refs):
        a_ref, b_ref = refs[0], refs[1]
        r_ref = refs[2] if res is not None else None
        o_ref = refs[n_in + n_r]
        if n_r:
            _ride(grid, ride_layers, refs[n_in:n_in + n_r], refs[n_in + n_r + 1:n_in + 2 * n_r + 1], refs[len(refs) - 3:])

        def product():
            return lax.dot_general(a_ref[...].astype(MXU_DTYPE), b_ref[...].astype(MXU_DTYPE),
                                   (contract, ((), ())), preferred_element_type=F32)

        def finish(v):
            if alpha != 1.0:
                v = v * alpha
            if r_ref is not None:
                v = v + r_ref[...].astype(F32)
            o_ref[...] = v.astype(out_dtype)

        if n_k == 1:
            finish(product())
            return
        acc = refs[n_in + 2 * n_r + 1]
        k = pl.program_id(2)

        @pl.when(k == 0)
        def _():
            acc[...] = product()

        @pl.when(k > 0)
        def _():
            acc[...] += product()

        @pl.when(k == n_k - 1)
        def _():
            finish(acc[...])

    in_specs = [a_spec, b_spec]
    operands = [a, b]
    if res is not None:
        in_specs.append(o_spec)
        operands.append(res)
    scratch = [pltpu.VMEM(acc_shape, F32)] if n_k > 1 else []
    any_spec = pl.BlockSpec(memory_space=pl.ANY)
    if into is not None:
        buf, layer = into
        assert buf.shape[1:] == tuple(out_shape) and buf.dtype == out_dtype and not n_r
        placed = pl.BlockSpec((None,) + tuple(o_spec.block_shape),
                              functools.partial(lambda i, j, kk, m, l: (l,) + tuple(m(i, j, kk)), m=o_spec.index_map, l=layer))
        return pl.pallas_call(
            body, name=name, grid=grid, in_specs=in_specs + [any_spec], out_specs=placed,
            out_shape=jax.ShapeDtypeStruct(buf.shape, out_dtype), scratch_shapes=scratch,
            input_output_aliases={len(operands): 0},
            compiler_params=_params(("parallel", "parallel", "arbitrary")),
        )(*operands, buf)
    if not n_r:
        return pl.pallas_call(
            body, name=name, grid=grid, in_specs=in_specs, out_specs=o_spec,
            out_shape=jax.ShapeDtypeStruct(out_shape, out_dtype), scratch_shapes=scratch,
            compiler_params=_params(("parallel", "parallel", "arbitrary")),
        )(*operands)
    return pl.pallas_call(
        body, name=name, grid=grid, in_specs=in_specs + [any_spec] * n_r, out_specs=[o_spec] + [any_spec] * n_r,
        out_shape=[jax.ShapeDtypeStruct(out_shape, out_dtype)] + _ride_out_shapes(riders),
        scratch_shapes=scratch + _ride_scratch(n_r),
        compiler_params=_params(("arbitrary", "arbitrary", "arbitrary")),
    )(*operands, *[r for r, _ in riders])


NN = ((1,), (0,))
NT = ((1,), (1,))
TN = ((0,), (0,))


def mm_nn(name, a, b, out_dtype, alpha=1.0, res=None, tm=1024, tn=1024, tk=1024, riders=()):
    m, k = a.shape
    n = b.shape[1]
    tm, tn, tk = _tile(m, tm), _tile(n, tn), _tile(k, tk)
    return _mm(name, a, b, (m, n), out_dtype, (m // tm, n // tn, k // tk),
               pl.BlockSpec((tm, tk), lambda i, j, kk: (i, kk)),
               pl.BlockSpec((tk, tn), lambda i, j, kk: (kk, j)),
               pl.BlockSpec((tm, tn), lambda i, j, kk: (i, j)), NN, alpha, res, riders)


def mm_tn(name, a, b, out_dtype, alpha=1.0, tm=1024, tn=1024, tk=1024, into=None):
    k, m = a.shape
    n = b.shape[1]
    tm, tn, tk = _tile(m, tm), _tile(n, tn), _tile(k, tk)
    if into is not None:
        shape = into[0].shape
        into = (into[0].reshape(shape[0], m, n), into[1])
    out = _mm(name, a, b, (m, n), out_dtype, (m // tm, n // tn, k // tk),
              pl.BlockSpec((tk, tm), lambda i, j, kk: (kk, i)),
              pl.BlockSpec((tk, tn), lambda i, j, kk: (kk, j)),
              pl.BlockSpec((tm, tn), lambda i, j, kk: (i, j)), TN, alpha, into=into)
    return out if into is None else out.reshape(shape)


def mm_nn_colsharded(name, a, w, out_dtype, tm=1024, tk=1024, riders=()):
    m, k = a.shape
    c = w.shape[2]
    tm, tk = _tile(m, tm), _tile(k, tk)
    return _mm(name, a, w, (m, N_CHIPS * c), out_dtype, (m // tm, N_CHIPS, k // tk),
               pl.BlockSpec((tm, tk), lambda i, j, kk: (i, kk)),
               pl.BlockSpec((None, tk, c), lambda i, j, kk: (j, kk, 0)),
               pl.BlockSpec((tm, c), lambda i, j, kk: (i, j)), NN, riders=riders)


def mm_tn_colsharded(name, a, b, out_dtype, tm=1024, tk=1024, into=None):
    t, k = a.shape
    c = b.shape[1] // N_CHIPS
    tm, tk = _tile(k, tm), _tile(t, tk)
    return _mm(name, a, b, (N_CHIPS, k, c), out_dtype, (k // tm, N_CHIPS, t // tk),
               pl.BlockSpec((tk, tm), lambda i, j, kk: (kk, i)),
               pl.BlockSpec((tk, c), lambda i, j, kk: (kk, j)),
               pl.BlockSpec((None, tm, c), lambda i, j, kk: (j, i, 0)), TN, into=into)


def _rowwise(name, fn, n_rows, tm, row_ins, bcast_ins, outs, accs=()):
    tm = min(tm, n_rows)
    grid = (n_rows // tm,)
    n_row, n_b, n_out = len(row_ins), len(bcast_ins), len(outs)

    def body(*refs):
        ins = [r[...] for r in refs[:n_row + n_b]]
        out_refs = refs[n_row + n_b:n_row + n_b + n_out]
        acc_refs = refs[n_row + n_b + n_out:]
        res = fn(*ins)
        if not isinstance(res, (tuple, list)):
            res = (res,)
        for o_ref, v in zip(out_refs, res[:n_out]):
            o_ref[...] = v.astype(o_ref.dtype)
        if acc_refs:
            @pl.when(pl.program_id(0) == 0)
            def _():
                for a_ref in acc_refs:
                    a_ref[...] = jnp.zeros_like(a_ref)
            for a_ref, v in zip(acc_refs, res[n_out:]):
                a_ref[...] += v

    in_specs, operands = [], []
    for spec in row_ins:
        arr, width, cb = spec[0], spec[1], spec[2]
        rb = spec[3] if len(spec) > 3 else 0
        in_specs.append(pl.BlockSpec((tm, width), functools.partial(lambda i, cb, rb: (i + rb, cb), cb=cb, rb=rb)))
        operands.append(arr)
    for arr in bcast_ins:
        in_specs.append(pl.BlockSpec(arr.shape, functools.partial(lambda i, nd: (0,) * nd, nd=arr.ndim)))
        operands.append(arr)
    out_specs = [pl.BlockSpec((tm, w), lambda i: (i, 0)) for w, _ in outs]
    out_specs += [pl.BlockSpec((r, w), lambda i: (0, 0)) for r, w in accs]
    out_shape = [jax.ShapeDtypeStruct((n_rows, w), dt) for w, dt in outs]
    out_shape += [jax.ShapeDtypeStruct((r, w), F32) for r, w in accs]
    res = pl.pallas_call(
        body, name=name, grid=grid, in_specs=in_specs, out_specs=out_specs, out_shape=out_shape,
        compiler_params=_params(("arbitrary",) if accs else ("parallel",)),
    )(*operands)
    return res


def _rms(x, g):
    r = lax.rsqrt(jnp.mean(x * x, axis=-1, keepdims=True) + NORM_EPS)
    return x * r * g


def _rms_bwd(dy, x, g):
    r = lax.rsqrt(jnp.mean(x * x, axis=-1, keepdims=True) + NORM_EPS)
    xh = x * r
    dxh = dy * g
    dx = r * (dxh - xh * jnp.mean(dxh * xh, axis=-1, keepdims=True))
    return dx, jnp.sum(dy * xh, axis=0, keepdims=True)


def norm_fwd(name, h, g):
    n, d = h.shape
    return _rowwise(name, lambda x, gg: _rms(x, gg), n, 512, [(h, d, 0)], [g.reshape(1, d)], [(d, BF16)])[0]


def norm_bwd(name, dxn, h, g, d_res):
    n, d = h.shape

    def fn(dy, x, dr, gg):
        dx, dg = _rms_bwd(dy, x, gg)
        out = dr + dx
        return out, out, dg

    return _rowwise(name, fn, n, 512, [(dxn, d, 0), (h, d, 0), (d_res, d, 0)], [g.reshape(1, d)], [(d, F32), (d, MXU_DTYPE)], [(1, d)])


_GELU_C = math.sqrt(2.0 / math.pi)


def _gelu(x):
    return 0.5 * x * (1.0 + jnp.tanh(_GELU_C * (x + 0.044715 * (x * x * x))))


def _gelu_grad(x):
    th = jnp.tanh(_GELU_C * (x + 0.044715 * (x * x * x)))
    return 0.5 * (1.0 + th) + 0.5 * x * (1.0 - th * th) * (_GELU_C * (1.0 + 3.0 * 0.044715 * (x * x)))


def _ssm_discretize(lam_re, lam_im, log_dt, b_re, b_im):
    dt = jnp.exp(log_dt)
    e = jnp.exp(lam_re * dt)
    lb_re = e * jnp.cos(lam_im * dt)
    lb_im = e * jnp.sin(lam_im * dt)
    nr, ni = lb_re - 1.0, lb_im
    den = lam_re * lam_re + lam_im * lam_im
    cr = (nr * lam_re + ni * lam_im) / den
    ci = (ni * lam_re - nr * lam_im) / den
    return lb_re, lb_im, cr * b_re - ci * b_im, cr * b_im + ci * b_re


def ssm_prep(name, lam_re, lam_im, log_dt, b_re, b_im):
    def body(lr, li, ld, br, bi, pr_ref, pi_ref, bbr_ref, bbi_ref):
        lb_re, lb_im, bb_re, bb_im = _ssm_discretize(lr[...], li[...], ld[...], br[...], bi[...])
        bbr_ref[...] = bb_re
        bbi_ref[...] = bb_im
        pr, pi = lb_re, lb_im
        cols_r, cols_i = [pr], [pi]
        for _ in range(SUBLANE - 1):
            pr, pi = pr * lb_re - pi * lb_im, pr * lb_im + pi * lb_re
            cols_r.append(pr)
            cols_i.append(pi)
        lane = lax.broadcasted_iota(jnp.int32, (SSM_COLS, SUBLANE), 1)
        out_r = jnp.zeros((SSM_COLS, SUBLANE), F32)
        out_i = jnp.zeros((SSM_COLS, SUBLANE), F32)
        for r in range(SUBLANE):
            out_r = jnp.where(lane == r, cols_r[r], out_r)
            out_i = jnp.where(lane == r, cols_i[r], out_i)
        pr_ref[...] = out_r
        pi_ref[...] = out_i

    shapes = [jax.ShapeDtypeStruct((SSM_COLS, SUBLANE), F32)] * 2 + [jax.ShapeDtypeStruct((SSM_COLS, SSM_CH), F32)] * 2
    return pl.pallas_call(body, name=name, out_shape=shapes,
                          compiler_params=pltpu.CompilerParams(vmem_limit_bytes=VMEM_LIMIT))(lam_re, lam_im, log_dt, b_re, b_im)


def ssm_prep_bwd(name, lam_re, lam_im, log_dt, b_re, b_im, d_lb_re, d_lb_im, d_bb_re, d_bb_im):
    def body(lr, li, ld, br, bi, g0, g1, g2, g3, o0, o1, o2, o3, o4):
        _, vjp = jax.vjp(_ssm_discretize, lr[...], li[...], ld[...], br[...], bi[...])
        res = vjp((g0[...], g1[...], g2[...], g3[...]))
        for o, v in zip((o0, o1, o2, o3, o4), res):
            o[...] = v

    col = jax.ShapeDtypeStruct((SSM_COLS, 1), F32)
    mat = jax.ShapeDtypeStruct((SSM_COLS, SSM_CH), F32)
    return pl.pallas_call(body, name=name, out_shape=[col, col, col, mat, mat],
                          compiler_params=pltpu.CompilerParams(vmem_limit_bytes=VMEM_LIMIT))(
        lam_re, lam_im, log_dt, b_re, b_im, d_lb_re, d_lb_im, d_bb_re, d_bb_im)


def scan_coefficients(pw_re, pw_im, reverse):
    pr, pi = pw_re.T, pw_im.T
    if reverse:
        pi = -pi
    row = jnp.arange(SUBLANE)[:, None]
    out = []
    for d in (1, 2, 4):
        valid = (row < SUBLANE - d) if reverse else (row >= d)
        out.append(jnp.where(valid, pr[d - 1][None, :], 0.0))
        out.append(jnp.where(valid, pi[d - 1][None, :], 0.0))
    out.append(pr[::-1] if reverse else pr)
    out.append(pi[::-1] if reverse else pi)
    return jnp.stack(out)


def ssm_scan(name, coef, x, seq, reverse, states=None, riders=()):
    n = x.shape[1]
    n_seq = n // seq
    cw = LANE
    n_cb = SSM_COLS // cw
    pair = 2 * SUBLANE
    n_pairs = seq // pair
    pairs_per_step = 2 if n_pairs % 2 == 0 else 1
    with_dlam = states is not None
    n_r = len(riders)
    assert not (n_r and with_dlam)

    def body(*refs):
        if with_dlam:
            coef_ref, x_ref, s_ref, o_ref, dl_ref = refs
        else:
            coef_ref, x_ref, o_ref = refs[0], refs[1], refs[2 + n_r]
            if n_r:
                _ride((n_cb, n_seq), [l for _, l in riders], refs[2:2 + n_r], refs[3 + n_r:3 + 2 * n_r], refs[3 + 2 * n_r:])
        c = [coef_ref[i] for i in range(8)]
        row16 = lax.broadcasted_iota(jnp.int32, (pair, cw), 0)
        zero = jnp.zeros((SUBLANE, cw), F32)

        edge = 0 if reverse else SUBLANE - 1

        def bcast_row(v, r, rows=SUBLANE):
            return jnp.broadcast_to(v[r:r + 1, :], (rows, cw))

        p8r, p8i = bcast_row(c[6], edge), bcast_row(c[7], edge)

        def local_scan(xr, xi):
            for si, d in enumerate((1, 2, 4)):
                sh = (SUBLANE - d) if reverse else d
                sr, sm = pltpu.roll(xr, sh, 0), pltpu.roll(xi, sh, 0)
                lre, lim = c[2 * si], c[2 * si + 1]
                xr, xi = xr + lre * sr - lim * sm, xi + lre * sm + lim * sr
            return xr, xi

        def step(it, carry):
            work = []
            for u in range(pairs_per_step):
                k = it * pairs_per_step + u
                pidx = (n_pairs - 1 - k) if reverse else k
                off = pl.multiple_of(pidx * pair, pair)
                xr16 = x_ref[0, pl.ds(off, pair), :].astype(F32)
                xi16 = x_ref[1, pl.ds(off, pair), :].astype(F32)
                halves = (1, 0) if reverse else (0, 1)
                tiles = {h: local_scan(xr16[h * SUBLANE:(h + 1) * SUBLANE], xi16[h * SUBLANE:(h + 1) * SUBLANE]) for h in halves}
                work.append((pidx, off, halves, tiles))
            cre, cim = carry[0], carry[1]
            acc = carry[2:]
            for pidx, off, halves, tiles in work:
                done = {}
                for h in halves:
                    lr, li = tiles[h]
                    done[h] = (lr + c[6] * cre - c[7] * cim, li + c[6] * cim + c[7] * cre)
                    cre, cim = (bcast_row(lr, edge) + p8r * cre - p8i * cim, bcast_row(li, edge) + p8r * cim + p8i * cre)
                or16 = jnp.concatenate([done[0][0], done[1][0]], axis=0)
                oi16 = jnp.concatenate([done[0][1], done[1][1]], axis=0)
                o_ref[0, pl.ds(off, pair), :] = or16.astype(o_ref.dtype)
                o_ref[1, pl.ds(off, pair), :] = oi16.astype(o_ref.dtype)
                if with_dlam:
                    poff = pl.multiple_of(jnp.maximum(pidx - 1, 0) * pair, pair)
                    first = pidx > 0
                    sr16 = s_ref[0, pl.ds(off, pair), :].astype(F32)
                    si16 = s_ref[1, pl.ds(off, pair), :].astype(F32)
                    pr_last = jnp.where(first, bcast_row(s_ref[0, pl.ds(poff, pair), :].astype(F32), pair - 1, pair), 0.0)
                    pi_last = jnp.where(first, bcast_row(s_ref[1, pl.ds(poff, pair), :].astype(F32), pair - 1, pair), 0.0)
                    spr = jnp.where(row16 == 0, pr_last, pltpu.roll(sr16, 1, 0))
                    spi = jnp.where(row16 == 0, pi_last, pltpu.roll(si16, 1, 0))
                    dre = or16 * spr + oi16 * spi
                    dim = oi16 * spr - or16 * spi
                    acc = (acc[0] + dre[:SUBLANE] + dre[SUBLANE:], acc[1] + dim[:SUBLANE] + dim[SUBLANE:])
            return (cre, cim) + tuple(acc)

        init = (zero, zero, zero, zero) if with_dlam else (zero, zero)
        fin = lax.fori_loop(0, n_pairs // pairs_per_step, step, init)
        if with_dlam:
            @pl.when(pl.program_id(1) == 0)
            def _():
                dl_ref[...] = jnp.zeros_like(dl_ref)
            dl_ref[0] += fin[2]
            dl_ref[1] += fin[3]

    blk = pl.BlockSpec((2, seq, cw), lambda j, b: (0, b, j))
    in_specs = [pl.BlockSpec((8, SUBLANE, cw), lambda j, b: (0, 0, j)), blk]
    operands = [coef, x]
    out_specs = [blk]
    out_shape = [jax.ShapeDtypeStruct(x.shape, STATE_DTYPE)]
    if with_dlam:
        in_specs.append(blk)
        operands.append(states)
        out_specs.append(pl.BlockSpec((2, SUBLANE, cw), lambda j, b: (0, 0, j)))
        out_shape.append(jax.ShapeDtypeStruct((2, SUBLANE, SSM_COLS), F32))
    scratch = []
    if n_r:
        any_spec = pl.BlockSpec(memory_space=pl.ANY)
        in_specs += [any_spec] * n_r
        operands += [r for r, _ in riders]
        out_specs += [any_spec] * n_r
        out_shape += _ride_out_shapes(riders)
        scratch = _ride_scratch(n_r)
    res = pl.pallas_call(
        body, name=name, grid=(n_cb, n_seq), in_specs=in_specs, out_specs=out_specs, out_shape=out_shape,
        scratch_shapes=scratch, compiler_params=_params(("arbitrary" if n_r else "parallel", "arbitrary")),
    )(*operands)
    return res if (with_dlam or n_r) else res[0]


SSM_WIDE = 4 * SSM_TILE


def ssm_in(name, z, bbd, tm=1024, riders=()):
    n = z.shape[0]
    tm = _tile(n, tm)
    t, w = SSM_TILE, SSM_WIDE
    return _mm(name, z, bbd, (2, n, SSM_COLS), STATE_DTYPE, (n // tm, 2 * SSM_COLS // w, 1),
               pl.BlockSpec((tm, t), lambda i, j, kk: (i, j % 2)),
               pl.BlockSpec((t, w), lambda i, j, kk: (j % 2, j)),
               pl.BlockSpec((None, tm, w), lambda i, j, kk: (j // 2, i, j % 2)), NN, riders=riders)


def ssm_out(name, s, cbd, tm=1024, riders=()):
    n = s.shape[1]
    tm = _tile(n, tm)
    t = SSM_TILE
    w = SSM_WIDE
    return _mm(name, s, cbd, (n, SSM_WIDTH), F32, (n // tm, SSM_WIDTH // t, 2),
               pl.BlockSpec((None, tm, w), lambda i, j, kk: (kk, i, j)),
               pl.BlockSpec((w, t), lambda i, j, kk: (2 * kk + j, j)),
               pl.BlockSpec((tm, t), lambda i, j, kk: (i, j)), NN, riders=riders)


def ssm_out_t(name, dy, cbd, tm=1024):
    n = dy.shape[0]
    tm = _tile(n, tm)
    t, w = SSM_TILE, SSM_WIDE
    return _mm(name, dy, cbd, (2, n, SSM_COLS), STATE_DTYPE, (n // tm, 2 * SSM_COLS // w, 1),
               pl.BlockSpec((tm, t), lambda i, j, kk: (i, j % 2)),
               pl.BlockSpec((w, t), lambda i, j, kk: (j, j % 2)),
               pl.BlockSpec((None, tm, w), lambda i, j, kk: (j // 2, i, j % 2)), NT)


def ssm_in_t(name, a, bbd, res, tm=1024):
    n = a.shape[1]
    tm = _tile(n, tm)
    t, w = SSM_TILE, SSM_WIDE
    return _mm(name, a, bbd, (n, SSM_WIDTH), F32, (n // tm, SSM_WIDTH // t, 2),
               pl.BlockSpec((None, tm, w), lambda i, j, kk: (kk, i, j)),
               pl.BlockSpec((t, w), lambda i, j, kk: (j, 2 * kk + j)),
               pl.BlockSpec((tm, t), lambda i, j, kk: (i, j)), NT, 1.0, res)


def ssm_grad_c(name, s, dy, tk=1024):
    n = s.shape[1]
    tk = _tile(n, tk)
    t, w = SSM_TILE, SSM_WIDE
    return _mm(name, s, dy, (2 * SSM_COLS, t), F32, (2 * SSM_COLS // w, 1, n // tk),
               pl.BlockSpec((None, tk, w), lambda i, j, kk: (i // 2, kk, i % 2)),
               pl.BlockSpec((tk, t), lambda i, j, kk: (kk, i % 2)),
               pl.BlockSpec((w, t), lambda i, j, kk: (i, 0)), TN)


def ssm_grad_b(name, z, a, tk=1024):
    n = z.shape[0]
    tk = _tile(n, tk)
    t, w = SSM_TILE, SSM_WIDE
    return _mm(name, z, a, (t, 2 * SSM_COLS), F32, (1, 2 * SSM_COLS // w, n // tk),
               pl.BlockSpec((tk, t), lambda i, j, kk: (kk, j % 2)),
               pl.BlockSpec((None, tk, w), lambda i, j, kk: (j // 2, kk, j % 2)),
               pl.BlockSpec((t, w), lambda i, j, kk: (0, j)), TN)


_GROUP_TILE = SSM_TILE // SSM_CH


def expand_b(bb_re, bb_im):
    b = jnp.stack([bb_re, bb_im]).reshape(2, SSM_GROUPS, SSM_STATE, SSM_CH)
    eye = jnp.eye(SSM_GROUPS, dtype=F32)
    return jnp.einsum("rgph,gk->ghrkp", b, eye).reshape(SSM_WIDTH, 2 * SSM_COLS).astype(MXU_DTYPE)


def expand_c(c_re, c_im):
    c = jnp.stack([c_re, -c_im])
    eye = jnp.eye(SSM_GROUPS, dtype=F32)
    return jnp.einsum("rghp,gk->rgpkh", c, eye).reshape(2 * SSM_COLS, SSM_WIDTH).astype(MXU_DTYPE)


def _group_pick():
    return (jnp.arange(SSM_GROUPS)[:, None] % _GROUP_TILE == jnp.arange(_GROUP_TILE)[None, :]).astype(F32)


def compact_c(dc):
    x = dc.reshape(2, SSM_GROUPS, SSM_STATE, _GROUP_TILE, SSM_CH)
    g = jnp.einsum("rgpch,gc->rghp", x, _group_pick())
    return g[0], -g[1]


def compact_b(db):
    x = db.reshape(_GROUP_TILE, SSM_CH, 2, SSM_GROUPS, SSM_STATE)
    g = jnp.einsum("chrgp,gc->rgph", x, _group_pick()).reshape(2, SSM_COLS, SSM_CH)
    return g[0], g[1]


def pool_window(name, x, col_block0, seq, out_dtype, adjoint):
    n = x.shape[0]

    def body(x_ref, o_ref):
        win = 2 << pl.program_id(1)
        row = lax.broadcasted_iota(jnp.int32, (seq, POOL_CH), 0)
        v = x_ref[...].astype(F32)
        cnt = jnp.minimum(row + 1, win).astype(F32)
        s = v / cnt if adjoint else v
        for d in (1, 2, 4, 8):
            if adjoint:
                sh = jnp.where((row < seq - d) & (d < win), pltpu.roll(s, seq - d, 0), 0.0)
            else:
                sh = jnp.where((row >= d) & (d < win), pltpu.roll(s, d, 0), 0.0)
            s = s + sh
        o_ref[...] = ((s - v) if adjoint else (s / cnt - v)).astype(out_dtype)

    return pl.pallas_call(
        body, name=name, grid=(n // seq, POOL_GROUPS),
        in_specs=[pl.BlockSpec((seq, POOL_CH), lambda b, g: (b, col_block0 + g))],
        out_specs=pl.BlockSpec((seq, POOL_CH), lambda b, g: (b, g)),
        out_shape=jax.ShapeDtypeStruct((n, POOL_WIDTH), out_dtype),
        compiler_params=_params(("parallel", "parallel")),
    )(x)


def _pool_groups(name, x, col_block, w, out_dtype, contract, tm=1024):
    n = x.shape[0]
    tm = _tile(n, tm)

    def body(x_ref, w_ref, o_ref):
        for g in range(POOL_GROUPS):
            cols = slice(g * POOL_CH, (g + 1) * POOL_CH)
            o_ref[:, cols] = lax.dot_general(x_ref[:, cols].astype(MXU_DTYPE), w_ref[g].astype(MXU_DTYPE),
                                             (contract, ((), ())), preferred_element_type=F32).astype(out_dtype)

    return pl.pallas_call(
        body, name=name, grid=(n // tm,),
        in_specs=[pl.BlockSpec((tm, POOL_WIDTH), lambda i: (i, col_block)),
                  pl.BlockSpec((POOL_GROUPS, POOL_CH, POOL_CH), lambda i: (0, 0, 0))],
        out_specs=pl.BlockSpec((tm, POOL_WIDTH), lambda i: (i, 0)),
        out_shape=jax.ShapeDtypeStruct((n, POOL_WIDTH), out_dtype),
        compiler_params=_params(("parallel",)),
    )(x, w)


def pool_mm(name, q, w, out_dtype):
    return _pool_groups(name, q, 0, w, out_dtype, NN)


def pool_mm_t(name, dy, col_block, w, out_dtype):
    return _pool_groups(name, dy, col_block, w, out_dtype, NT)


def pool_grad_w(name, q, dy, col_block0, tk=1024):
    n = q.shape[0]
    tk = _tile(n, tk)
    return _mm(name, q, dy, (POOL_GROUPS, POOL_CH, POOL_CH), F32, (POOL_GROUPS, 1, n // tk),
               pl.BlockSpec((tk, POOL_CH), lambda i, j, kk: (kk, i)),
               pl.BlockSpec((tk, POOL_CH), lambda i, j, kk: (kk, col_block0 + i)),
               pl.BlockSpec((None, POOL_CH, POOL_CH), lambda i, j, kk: (i, 0, 0)), TN)


def _any_specs(n):
    return [pl.BlockSpec(memory_space=pl.ANY)] * n


def _place():
    x, y, c = lax.axis_index("x"), lax.axis_index("y"), lax.axis_index("c")
    return x, y, c


def sibling_swap(name, arrays):
    n = len(arrays)

    def body(*refs):
        ins, outs = refs[:n], refs[n:2 * n]
        send_sems, recv_sems = refs[2 * n:]
        x, y, c = _place()
        copies = []
        for i in range(n):
            away = pltpu.make_async_remote_copy(
                src_ref=ins[i], dst_ref=outs[i], send_sem=send_sems.at[i], recv_sem=recv_sems.at[i],
                device_id=(x, y, 1 - c), device_id_type=MESH_ID)
            away.start()
            copies.append(away)
        for cp in copies:
            cp.wait()

    return pl.pallas_call(
        body, name=name, in_specs=_any_specs(n), out_specs=_any_specs(n),
        out_shape=[jax.ShapeDtypeStruct(a.shape, a.dtype) for a in arrays],
        scratch_shapes=[pltpu.SemaphoreType.DMA((n,)), pltpu.SemaphoreType.DMA((n,))],
    )(*arrays)


def chip_gather(name, a):
    def body(in_ref, out_ref, send_sems, recv_sems, local_sems):
        copies = _ride_copies([0], [in_ref], [out_ref], send_sems, recv_sems, local_sems)
        for cp in copies:
            cp.start()
        for cp in copies:
            cp.wait()

    return pl.pallas_call(
        body, name=name, in_specs=_any_specs(1), out_specs=_any_specs(1)[0],
        out_shape=jax.ShapeDtypeStruct((N_CHIPS,) + a.shape[1:], a.dtype), scratch_shapes=_ride_scratch(1),
    )(a)


SIBLING = ((0, 0, 1),)
ICI_CHUNK_BYTES = 2 * 1024 * 1024
D2D_CHUNK_BYTES = 4 * 1024 * 1024


def _peer(flip):
    x, y, c = _place()
    return tuple((1 - v) if f else v for v, f in zip((x, y, c), flip))


def _core():
    return lax.axis_index("c")


def _chip():
    return 2 * lax.axis_index("x") + lax.axis_index("y")


def _linear_step(grid):
    i = pl.program_id(0)
    for a in range(1, len(grid)):
        i = i * grid[a] + pl.program_id(a)
    return i


def stream_reduce(name, x, grid, block, own_map, send_maps, flips, out_shape, out_block, out_map, wire_dtype=None):
    n_steps = math.prod(grid)
    n_p = len(flips)
    vm_block = block = tuple(1 if d is None else d for d in block)
    out_block = tuple(1 if d is None else d for d in out_block)
    staged = wire_dtype is not None and wire_dtype != x.dtype
    slot_dtype = wire_dtype if staged else x.dtype

    def body(own_ref, *rest):
        send_refs = rest[:n_p]
        o_ref, recv, send_sems, recv_sems, credits = rest[n_p:n_p + 5]
        stage = rest[n_p + 5] if staged else None
        i = _linear_step(grid)
        s = i % 2
        copies = []
        for j, flip in enumerate(flips):
            src = send_refs[j]
            if staged:
                stage[j, s] = send_refs[j][...].astype(wire_dtype)
                src = stage.at[j, s]

            @pl.when(i >= 2)
            def _():
                pl.semaphore_wait(credits.at[j, s], 1)
            cp = pltpu.make_async_remote_copy(
                src_ref=src, dst_ref=recv.at[j, s], send_sem=send_sems.at[j, s], recv_sem=recv_sems.at[j, s],
                device_id=_peer(flip), device_id_type=MESH_ID)
            cp.start()
            copies.append(cp)
        acc = own_ref[...]
        for j, cp in enumerate(copies):
            cp.wait_recv()
            acc = acc + recv[j, s].astype(acc.dtype)
        o_ref[...] = acc.reshape(o_ref.shape)
        for cp in copies:
            cp.wait_send()
        for j, flip in enumerate(flips):
            @pl.when(i < n_steps - 2)
            def _():
                pl.semaphore_signal(credits.at[j, s], inc=1, device_id=_peer(flip), device_id_type=MESH_ID)

    in_specs = [pl.BlockSpec(block, own_map)] + [pl.BlockSpec(block, m) for m in send_maps]
    scratch = [pltpu.VMEM((n_p, 2) + vm_block, slot_dtype), pltpu.SemaphoreType.DMA((n_p, 2)),
               pltpu.SemaphoreType.DMA((n_p, 2)), pltpu.SemaphoreType.REGULAR((n_p, 2))]
    if staged:
        scratch.append(pltpu.VMEM((n_p, 2) + vm_block, slot_dtype))
    return pl.pallas_call(
        body, name=name, grid=grid, in_specs=in_specs, out_specs=pl.BlockSpec(out_block, out_map),
        out_shape=jax.ShapeDtypeStruct(out_shape, x.dtype), scratch_shapes=scratch,
        compiler_params=_params(("arbitrary",) * len(grid)),
    )(*([x] * (1 + n_p)))


def stream_gather(name, x, grid, block, in_map, flips, out_shape, out_block, out_map):
    n_p = len(flips)
    assert grid[-1] == n_p + 1
    n_steps = math.prod(grid[:-1])
    vm_block = block = tuple(1 if d is None else d for d in block)
    out_block = tuple(1 if d is None else d for d in out_block)

    def body(x_ref, o_ref, recv, send_sems, recv_sems, credits):
        i = _linear_step(grid[:-1])
        q = pl.program_id(len(grid) - 1)
        s = i % 2

        def copy(j):
            return pltpu.make_async_remote_copy(
                src_ref=x_ref, dst_ref=recv.at[j, s], send_sem=send_sems.at[j, s], recv_sem=recv_sems.at[j, s],
                device_id=_peer(flips[j]), device_id_type=MESH_ID)

        @pl.when(q == 0)
        def _():
            for j in range(n_p):
                @pl.when(i >= 2)
                def _():
                    pl.semaphore_wait(credits.at[j, s], 1)
                copy(j).start()
            o_ref[...] = x_ref[...].reshape(o_ref.shape)
            for j in range(n_p):
                copy(j).wait_send()

        for j in range(n_p):
            @pl.when(q == j + 1)
            def _():
                copy(j).wait_recv()
                o_ref[...] = recv[j, s].reshape(o_ref.shape)

                @pl.when(i < n_steps - 2)
                def _():
                    pl.semaphore_signal(credits.at[j, s], inc=1, device_id=_peer(flips[j]), device_id_type=MESH_ID)

    return pl.pallas_call(
        body, name=name, grid=grid, in_specs=[pl.BlockSpec(block, in_map)], out_specs=pl.BlockSpec(out_block, out_map),
        out_shape=jax.ShapeDtypeStruct(out_shape, x.dtype),
        scratch_shapes=[pltpu.VMEM((n_p, 2) + vm_block, x.dtype), pltpu.SemaphoreType.DMA((n_p, 2)),
                        pltpu.SemaphoreType.DMA((n_p, 2)), pltpu.SemaphoreType.REGULAR((n_p, 2))],
        compiler_params=_params(("arbitrary",) * len(grid)),
    )(x)


def _chip_of_substep(q):
    mask = jnp.where(q == 1, 2, jnp.where(q == 2, 1, jnp.where(q == 3, 3, 0)))
    return jnp.bitwise_xor(_chip(), mask)


def gather_weight(name, shard):
    nl, r, c = shard.shape
    r2 = r // 2
    f32_per_elem = 4 // shard.dtype.itemsize
    tr = _rows_tile(r2, c, budget=ICI_CHUNK_BYTES * f32_per_elem, step=16)
    nb = r2 // tr
    half = stream_gather(
        name + "_chips", shard, (nl, nb, N_CHIPS), (None, tr, c), lambda l, i, q: (l, _core() * nb + i, 0), CHIPS,
        (nl, N_CHIPS, r2, c), (None, None, tr, c), lambda l, i, q: (l, _chip_of_substep(q), i, 0))
    trd = _rows_tile(r2, c, budget=D2D_CHUNK_BYTES * f32_per_elem, step=16)
    nbd = r2 // trd
    both = stream_gather(
        name + "_cores", half, (nl, N_CHIPS, nbd, 2), (None, None, trd, c), lambda l, k, i, q: (l, k, i, 0), SIBLING,
        (nl, N_CHIPS, 2, r2, c), (None, None, None, trd, c), lambda l, k, i, q: (l, k, _core() + q - 2 * _core() * q, i, 0))
    return both.reshape(nl, N_CHIPS, r, c)


def reduce_scatter_streamed(name, g):
    nl, _, r, c = g.shape
    r2 = r // 2
    tr_d2d = _rows_tile(r2, c, budget=D2D_CHUNK_BYTES, step=16)
    nbd = r2 // tr_d2d
    chip_sum = stream_reduce(
        name + "_cores", g, (nl, N_CHIPS, nbd), (None, None, tr_d2d, c),
        lambda l, k, i: (l, k, _core() * nbd + i, 0), [lambda l, k, i: (l, k, (1 - _core()) * nbd + i, 0)], SIBLING,
        (nl, N_CHIPS, r2, c), (None, None, tr_d2d, c), lambda l, k, i: (l, k, i, 0), wire_dtype=GRAD_WIRE_DTYPE)
    tr = _rows_tile(r2, c, budget=ICI_CHUNK_BYTES, step=16)
    nb = r2 // tr
    blk4 = (None, None, tr, c)
    masks = (2, 1, 3)
    mine = stream_reduce(
        name + "_chips", chip_sum, (nl, nb), blk4,
        lambda l, i: (l, _chip(), i, 0),
        [functools.partial(lambda l, i, m: (l, jnp.bitwise_xor(_chip(), m), i, 0), m=m) for m in masks], CHIPS,
        (nl, r2, c), (None, tr, c), lambda l, i: (l, i, 0), wire_dtype=GRAD_WIRE_DTYPE)
    both = stream_gather(
        name + "_join", mine, (nl, nbd, 2), (None, tr_d2d, c), lambda l, i, q: (l, i, 0), SIBLING,
        (nl, 2, r2, c), (None, None, tr_d2d, c), lambda l, i, q: (l, _core() + q - 2 * _core() * q, i, 0))
    return both.reshape(nl, r, c)


def add2(name, a, b):
    shape = a.shape
    a2, b2 = a.reshape(-1, shape[-1]), b.reshape(-1, shape[-1])
    rows, w = a2.shape
    tm = _rows_tile(rows, w)
    return _rowwise(name, lambda u, v: u + v, rows, tm, [(a2, w, 0), (b2, w, 0)], [], [(w, F32)])[0].reshape(shape)


def _rows_tile(rows, width, budget=2 * 1024 * 1024, step=SUBLANE):
    best = step
    t = step
    while t <= rows:
        if rows % t == 0 and t * width * 4 <= budget:
            best = t
        t += step
    return best


def all_reduce_small(flat):
    other = sibling_swap("ar_swap", [flat])[0]
    chip = add2("ar_add_cores", flat, other)
    slots = chip_gather("ar_chips", chip.reshape((1,) + chip.shape))
    rows = flat.shape[0]
    tm = _rows_tile(rows, LANE)
    nb = rows // tm
    s2 = slots.reshape(N_CHIPS * rows, LANE)
    return _rowwise("ar_sum_chips", lambda a, b, c, d: ((a + b) + c) + d, rows, tm,
                    [(s2, LANE, 0, k * nb) for k in range(N_CHIPS)], [], [(LANE, F32)])[0]


def _adamw_math(w, g, m, v):
    m = ADAM_B1 * m + (1.0 - ADAM_B1) * g
    v = ADAM_B2 * v + (1.0 - ADAM_B2) * (g * g)
    m_hat = m / (1.0 - ADAM_B1 ** ADAM_STEP)
    v_hat = v / (1.0 - ADAM_B2 ** ADAM_STEP)
    delta = -ADAM_LR * (m_hat / (jnp.sqrt(v_hat) + ADAM_EPS) + ADAM_WD * w)
    return delta, m, v


def adamw(name, w, g, m, v):
    shape = w.shape
    width = shape[-1]
    flat = [t.reshape(-1, width) for t in (w, g, m, v)]
    rows = flat[0].shape[0]
    tm = _rows_tile(rows, width, budget=1024 * 1024)
    res = _rowwise(name, _adamw_math, rows, tm, [(t, width, 0) for t in flat], [], [(width, F32)] * 3)
    return [r.reshape(shape) for r in res]


def ffn_in(name, xn, wi, riders=(), tm=512):
    n, d = xn.shape
    c = wi.shape[2]
    tm = _tile(n, tm)
    grid = (n // tm, 2)
    n_r = len(riders)

    def body(*refs):
        x_ref, wg_ref, wu_ref = refs[:3]
        g_ref, u_ref, a_ref = refs[3 + n_r:6 + n_r]
        if n_r:
            _ride(grid, [l for _, l in riders], refs[3:3 + n_r], refs[6 + n_r:6 + 2 * n_r], refs[6 + 2 * n_r:])
        x = x_ref[...].astype(MXU_DTYPE)
        g = jnp.dot(x, wg_ref[...].astype(MXU_DTYPE), preferred_element_type=F32)
        u = jnp.dot(x, wu_ref[...].astype(MXU_DTYPE), preferred_element_type=F32)
        g_ref[...] = g.astype(g_ref.dtype)
        u_ref[...] = u.astype(u_ref.dtype)
        a_ref[...] = (g * jax.nn.sigmoid(g) * u).astype(a_ref.dtype)

    any_spec = pl.BlockSpec(memory_space=pl.ANY)
    out_blk = pl.BlockSpec((tm, c), lambda i, j: (i, j))
    res = pl.pallas_call(
        body, name=name, grid=grid,
        in_specs=[pl.BlockSpec((tm, d), lambda i, j: (i, 0)), pl.BlockSpec((None, d, c), lambda i, j: (j, 0, 0)),
                  pl.BlockSpec((None, d, c), lambda i, j: (j + 2, 0, 0))] + [any_spec] * n_r,
        out_specs=[out_blk] * 3 + [any_spec] * n_r,
        out_shape=[jax.ShapeDtypeStruct((n, 2 * c), BF16)] * 3 + _ride_out_shapes(riders),
        scratch_shapes=_ride_scratch(n_r) if n_r else [],
        compiler_params=_params(("arbitrary", "arbitrary") if n_r else ("parallel", "parallel")),
    )(xn, wi, wi, *[r for r, _ in riders])
    return res[:3], list(res[3:])


def _rows(a):
    return a.reshape(N_CHIPS * a.shape[1], a.shape[2])


def _ffn_fwd(tag, h, g_norm, wi, wo, ride_in=(), ride_out=()):
    xn = norm_fwd(tag + "_norm", h, g_norm)
    (g, u, act), got_in = ffn_in(tag + "_wi", xn, wi, riders=ride_in)
    if wo is None:
        wo = _rows(got_in[0])
    out = mm_nn(tag + "_wo", act, wo, F32, alpha=0.5, res=h, tm=512, tk=2816, riders=ride_out)
    got_out = []
    if ride_out:
        out, got_out = out[0], list(out[1:])
    return out, (xn, g, u, act), got_in, got_out


def ffn_dact(name, d, wo_t, gate, up, tm=512):
    n, dm = d.shape
    ff = wo_t.shape[1]
    c = ff // 2
    tm = _tile(n, tm)

    def body(d_ref, w_ref, g_ref, u_ref, o_ref):
        da = 0.5 * jnp.dot(d_ref[...].astype(MXU_DTYPE), w_ref[...].astype(MXU_DTYPE), preferred_element_type=F32)
        g, u = g_ref[...].astype(F32), u_ref[...].astype(F32)
        sg = jax.nn.sigmoid(g)
        dg = da * u * (sg * (1.0 + g * (1.0 - sg)))
        du = da * (g * sg)
        o_ref[...] = jnp.concatenate([dg, du], axis=1).astype(o_ref.dtype)

    half = pl.BlockSpec((tm, c), lambda i, j: (i, j))
    return pl.pallas_call(
        body, name=name, grid=(n // tm, 2),
        in_specs=[pl.BlockSpec((tm, dm), lambda i, j: (i, 0)), pl.BlockSpec((dm, c), lambda i, j: (0, j)), half, half],
        out_specs=pl.BlockSpec((tm, 2 * c), lambda i, j: (i, j)),
        out_shape=jax.ShapeDtypeStruct((n, 2 * ff), BF16),
        compiler_params=_params(("parallel", "parallel")),
    )(d, wo_t, gate, up)


def _ffn_bwd(tag, d, d_mxu, h, g_norm, wi_t, wo_t, saved, into_wi, into_wo):
    xn, gate, up, act = saved
    n, dm = h.shape
    dwo = mm_tn(tag + "_dwo", act, d_mxu, F32, alpha=0.5, tm=1408, into=into_wo)
    dgu = ffn_dact(tag + "_dact", d_mxu, wo_t, gate, up)
    t, k = xn.shape
    c = dgu.shape[1] // N_CHIPS
    tm_, tk_ = _tile(k, 1024), _tile(t, 1024)
    dwi = _mm(tag + "_dwi", xn, dgu, (N_CHIPS, k, c), F32, (k // tm_, N_CHIPS, t // tk_),
              pl.BlockSpec((tk_, tm_), lambda i, j, kk: (kk, i)),
              pl.BlockSpec((tk_, c), lambda i, j, kk: (kk, j)),
              pl.BlockSpec((None, tm_, c), lambda i, j, kk: ((j % 2) * 2 + j // 2, i, 0)), TN, into=into_wi)
    tmx, tnx = _tile(n, 1024), _tile(dm, 1024)
    dxn = _mm(tag + "_dxn", dgu, wi_t, (n, dm), F32, (n // tmx, dm // tnx, N_CHIPS),
              pl.BlockSpec((tmx, c), lambda i, j, kk: (i, kk)),
              pl.BlockSpec((None, c, tnx), lambda i, j, kk: ((kk % 2) * 2 + kk // 2, 0, j)),
              pl.BlockSpec((tmx, tnx), lambda i, j, kk: (i, j)), NN)
    d_in, d_in_mxu, dg_norm = norm_bwd(tag + "_dnorm", dxn, h, g_norm, d)
    return d_in, d_in_mxu, dg_norm.reshape(dm), dwi, dwo


def _col(v):
    return v.reshape(SSM_COLS, 1)


def _layer_fwd(h, lw, ffn1_wi, ride, next_wi, p_l, seq):
    n, d = h.shape
    got = {"ffn1_wi": ffn1_wi}
    h1, ffn1_saved, got_in, got_out = _ffn_fwd("ffn1", h, lw["ffn1_norm"], ffn1_wi, None, (ride("ffn1_wo"),),
                                               (ride("w_in"), ride("ssm_w_glu"), ride("w_out")))
    got["ffn1_wo"] = got_in[0]
    got["w_in"], got["ssm_w_glu"], got["w_out"] = got_out

    xn2 = norm_fwd("mix_norm", h1, lw["mix_norm"])
    z = mm_nn("mix_in", xn2, _rows(got["w_in"]), F32)
    log_dt = jnp.repeat(lw["ssm_log_dt"], SSM_STATE)
    b_re, b_im = lw["ssm_b_re"].reshape(SSM_COLS, SSM_CH), lw["ssm_b_im"].reshape(SSM_COLS, SSM_CH)
    pw_re, pw_im, bb_re, bb_im = ssm_prep("ssm_prep", _col(lw["ssm_lambda_re"]), _col(lw["ssm_lambda_im"]), _col(log_dt), b_re, b_im)
    bbd = expand_b(bb_re, bb_im)
    cbd = expand_c(lw["ssm_c_re"], lw["ssm_c_im"])
    bu, got["ple_w_gate"], got["ple_w_proj"] = ssm_in("ssm_in", z, bbd, riders=(ride("ple_w_gate"), ride("ple_w_proj")))
    s, got["ffn2_wi"] = ssm_scan("ssm_scan", scan_coefficients(pw_re, pw_im, False), bu, seq, False, riders=(ride("ffn2_wi"),))
    y0c, got["ffn2_wo"] = ssm_out("ssm_out", s, cbd, riders=(ride("ffn2_wo"),))

    def skip_gelu(yc, zs, dvec):
        y0 = yc + dvec * zs
        return y0, _gelu(y0)

    y0, y1 = _rowwise("ssm_gelu", skip_gelu, n, 512, [(y0c, SSM_WIDTH, 0), (z, SSM_WIDTH, 0)],
                      [lw["ssm_d"].reshape(1, SSM_WIDTH)], [(SSM_WIDTH, F32), (SSM_WIDTH, F32)])
    t = mm_nn("ssm_glu_mm", y1, _rows(got["ssm_w_glu"]), F32)
    y2 = _rowwise("ssm_glu", lambda a, b: a * jax.nn.sigmoid(b), n, 512, [(y1, SSM_WIDTH, 0), (t, SSM_WIDTH, 0)], [],
                  [(SSM_WIDTH, BF16)])[0]

    q = pool_window("pool_window", z, SSM_WIDTH // POOL_CH, seq, BF16, False)
    wp_eff = lw["pool_w"] * lw["pool_scale"].reshape(POOL_GROUPS, 1, POOL_CH)
    yp = pool_mm("pool_mm", q, wp_eff, BF16)
    m = jnp.concatenate([y2, yp], axis=1)
    h2 = mm_nn("mix_out", m, _rows(got["w_out"]), F32, res=h1)

    h3, ffn2_saved, got_in, _ = _ffn_fwd("ffn2", h2, lw["ffn2_norm"], got["ffn2_wi"], _rows(got["ffn2_wo"]),
                                         () if next_wi is None else (next_wi,))
    next_ffn1_wi = got_in[0] if got_in else None

    xn4 = norm_fwd("ple_norm", h3, lw["ple_norm"])
    tg = mm_nn("ple_gate", xn4, _rows(got["ple_w_gate"]), F32)
    e = mm_nn_colsharded("ple_proj", p_l, got["ple_w_proj"], F32)
    h4 = _rowwise("ple_add", lambda a, b, c: a + jax.nn.sigmoid(b) * c, n, 512, [(h3, d, 0), (tg, d, 0), (e, d, 0)], [], [(d, F32)])[0]
    saved = dict(h=h, h1=h1, h2=h2, h3=h3, ffn1=ffn1_saved, ffn2=ffn2_saved, xn2=xn2, z=z, s=s, y0=y0, y1=y1, t=t, m=m, q=q,
                 xn4=xn4, tg=tg, e=e, pw_re=pw_re, pw_im=pw_im, bbd=bbd, cbd=cbd)
    return h4, saved, got, next_ffn1_wi


def _layer_bwd(d, lw, p_l, sv, seq, bufs, layer):
    n, dm = d.shape
    g = {}
    def ple_bwd(dd, tg, e):
        gate = jax.nn.sigmoid(tg)
        return dd * e * gate * (1.0 - gate), dd * gate

    dtg, de = _rowwise("ple_dadd", ple_bwd, n, 512, [(d, dm, 0), (sv["tg"], dm, 0), (sv["e"], dm, 0)], [], [(dm, BF16), (dm, BF16)])
    g["ple_w_proj"] = mm_tn_colsharded("ple_dproj", p_l, de, F32, into=(bufs["ple_w_proj"], layer))
    g["ple_w_gate"] = mm_tn("ple_dgate_w", sv["xn4"], dtg, F32, into=(bufs["ple_w_gate"], layer))
    dxn4 = mm_nn("ple_dgate_x", dtg, lw["ple_w_gate_t"], F32)
    d, d_mxu, dg = norm_bwd("ple_dnorm", dxn4, sv["h3"], lw["ple_norm"], d)
    g["ple_norm"] = dg.reshape(dm)

    d, d_mxu, g["ffn2_norm"], g["ffn2_wi"], g["ffn2_wo"] = _ffn_bwd(
        "ffn2b", d, d_mxu, sv["h2"], lw["ffn2_norm"], lw["ffn2_wi_t"], lw["ffn2_wo_t"], sv["ffn2"],
        (bufs["ffn2_wi"], layer), (bufs["ffn2_wo"], layer))

    dmix = mm_nn("mix_dout_x", d_mxu, lw["w_out_t"], F32)
    g["w_out"] = mm_tn("mix_dout_w", sv["m"], d_mxu, F32, into=(bufs["w_out"], layer))
    pool_cb = SSM_WIDTH // POOL_CH
    wp_eff = lw["pool_w"] * lw["pool_scale"].reshape(POOL_GROUPS, 1, POOL_CH)
    dq = pool_mm_t("pool_dmm_x", dmix, SSM_WIDTH // POOL_WIDTH, wp_eff, F32)
    dwp_eff = pool_grad_w("pool_dmm_w", sv["q"], dmix, pool_cb)
    g["pool_w"] = dwp_eff * lw["pool_scale"].reshape(POOL_GROUPS, 1, POOL_CH)
    g["pool_scale"] = jnp.sum(dwp_eff * lw["pool_w"], axis=1).reshape(POOL_WIDTH)
    dzp = pool_window("pool_dwindow", dq, 0, seq, BF16, True)

    def glu_bwd(dy2, y1, t):
        sg = jax.nn.sigmoid(t)
        return dy2 * y1 * sg * (1.0 - sg), dy2 * sg

    dt_, dy1a = _rowwise("ssm_dglu", glu_bwd, n, 512, [(dmix, SSM_WIDTH, 0), (sv["y1"], SSM_WIDTH, 0), (sv["t"], SSM_WIDTH, 0)], [],
                         [(SSM_WIDTH, BF16), (SSM_WIDTH, F32)])
    g["ssm_w_glu"] = mm_tn("ssm_dglu_w", sv["y1"], dt_, F32, into=(bufs["ssm_w_glu"], layer))
    dy1b = mm_nn("ssm_dglu_x", dt_, lw["ssm_w_glu_t"], F32)

    def gelu_bwd(da, db, y0, zs, dvec):
        dy0 = (da + db) * _gelu_grad(y0)
        return dy0, dy0 * dvec, jnp.sum(dy0 * zs, axis=0, keepdims=True)

    dy0, dzs_a, dd = _rowwise("ssm_dgelu", gelu_bwd, n, 512,
                              [(dy1a, SSM_WIDTH, 0), (dy1b, SSM_WIDTH, 0), (sv["y0"], SSM_WIDTH, 0), (sv["z"], SSM_WIDTH, 0)],
                              [lw["ssm_d"].reshape(1, SSM_WIDTH)], [(SSM_WIDTH, F32), (SSM_WIDTH, F32)], [(1, SSM_WIDTH)])
    g["ssm_d"] = dd.reshape(SSM_WIDTH)
    g["ssm_c_re"], g["ssm_c_im"] = compact_c(ssm_grad_c("ssm_dc", sv["s"], dy0))
    v = ssm_out_t("ssm_dout", dy0, sv["cbd"])
    a, dlam = ssm_scan("ssm_scan_adj", scan_coefficients(sv["pw_re"], sv["pw_im"], True), v, seq, True, states=sv["s"])
    dbb_re, dbb_im = compact_b(ssm_grad_b("ssm_db", sv["z"], a))
    dzs = ssm_in_t("ssm_din", a, sv["bbd"], dzs_a)
    dlam = jnp.sum(dlam, axis=1)
    log_dt = jnp.repeat(lw["ssm_log_dt"], SSM_STATE)
    b_re, b_im = lw["ssm_b_re"].reshape(SSM_COLS, SSM_CH), lw["ssm_b_im"].reshape(SSM_COLS, SSM_CH)
    glr, gli, gld, gbr, gbi = ssm_prep_bwd("ssm_prep_bwd", _col(lw["ssm_lambda_re"]), _col(lw["ssm_lambda_im"]), _col(log_dt), b_re, b_im,
                                           _col(dlam[0]), _col(dlam[1]), dbb_re, dbb_im)
    g["ssm_lambda_re"] = glr.reshape(SSM_GROUPS, SSM_STATE)
    g["ssm_lambda_im"] = gli.reshape(SSM_GROUPS, SSM_STATE)
    g["ssm_log_dt"] = jnp.sum(gld.reshape(SSM_GROUPS, SSM_STATE), axis=1)
    g["ssm_b_re"] = gbr.reshape(SSM_GROUPS, SSM_STATE, SSM_CH)
    g["ssm_b_im"] = gbi.reshape(SSM_GROUPS, SSM_STATE, SSM_CH)

    dz = jnp.concatenate([dzs.astype(BF16), dzp], axis=1)
    g["w_in"] = mm_tn("mix_din_w", sv["xn2"], dz, F32, into=(bufs["w_in"], layer))
    dxn2 = mm_nn("mix_din_x", dz, lw["w_in_t"], F32)
    d, d_mxu, dg = norm_bwd("mix_dnorm", dxn2, sv["h1"], lw["mix_norm"], d)
    g["mix_norm"] = dg.reshape(dm)

    d, _, g["ffn1_norm"], g["ffn1_wi"], g["ffn1_wo"] = _ffn_bwd(
        "ffn1b", d, d_mxu, sv["h"], lw["ffn1_norm"], lw["ffn1_wi_t"], lw["ffn1_wo_t"], sv["ffn1"],
        (bufs["ffn1_wi"], layer), (bufs["ffn1_wo"], layer))
    return d, g


def _flatten_small(tensors):
    flat = jnp.concatenate([t.reshape(-1) for t in tensors])
    pad = (-flat.shape[0]) % (SUBLANE * LANE)
    return jnp.pad(flat, (0, pad)).reshape(-1, LANE)


def _unflatten_small(flat, like):
    flat = flat.reshape(-1)
    out, off = [], 0
    for t in like:
        out.append(flat[off:off + t.size].reshape(t.shape))
        off += t.size
    return out


def kernel(x, p, ffn1_norm, ffn1_wi, ffn1_wo, mix_norm, w_in, ssm_lambda_re, ssm_lambda_im, ssm_log_dt, ssm_b_re, ssm_b_im, ssm_c_re, ssm_c_im, ssm_d, ssm_w_glu, pool_w, pool_scale, w_out, ffn2_norm, ffn2_wi, ffn2_wo, ple_norm, ple_w_gate, ple_w_proj, final_norm, loss_target, m_ffn1_norm, m_ffn1_wi, m_ffn1_wo, m_mix_norm, m_w_in, m_ssm_lambda_re, m_ssm_lambda_im, m_ssm_log_dt, m_ssm_b_re, m_ssm_b_im, m_ssm_c_re, m_ssm_c_im, m_ssm_d, m_ssm_w_glu, m_pool_w, m_pool_scale, m_w_out, m_ffn2_norm, m_ffn2_wi, m_ffn2_wo, m_ple_norm, m_ple_w_gate, m_ple_w_proj, m_final_norm, v_ffn1_norm, v_ffn1_wi, v_ffn1_wo, v_mix_norm, v_w_in, v_ssm_lambda_re, v_ssm_lambda_im, v_ssm_log_dt, v_ssm_b_re, v_ssm_b_im, v_ssm_c_re, v_ssm_c_im, v_ssm_d, v_ssm_w_glu, v_pool_w, v_pool_scale, v_w_out, v_ffn2_norm, v_ffn2_wi, v_ffn2_wo, v_ple_norm, v_ple_w_gate, v_ple_w_proj, v_final_norm):
    given = dict(locals())
    w = {k: given[k] for k in WEIGHTS}
    mom = {k: given["m_" + k] for k in WEIGHTS}
    var = {k: given["v_" + k] for k in WEIGHTS}
    bsz, seq, dm = x.shape
    n = bsz * seq
    depth = ffn1_wi.shape[0]

    shards = {k: w[k].astype(MXU_DTYPE) for k in BIG}

    def layer_weights(l, gathered):
        lw = {k: w[k][l] for k in SMALL if k != "final_norm"}
        for k, a in gathered.items():
            if k in COL_SHARDED:
                lw[k] = a
                if k != "ple_w_proj":
                    lw[k + "_t"] = jnp.swapaxes(a, 1, 2)
            else:
                lw[k] = a.reshape(N_CHIPS * a.shape[1], a.shape[2])
                lw[k + "_t"] = lw[k].T
        return lw

    ffn1_wi_l = gather_weight("gather_ffn1_wi", shards["ffn1_wi"][:1])[0]

    p2 = p.reshape(depth, n, p.shape[-1])
    h_last = x.reshape(n, dm)
    saved, layers = [], []
    for l in range(depth):
        small = {k: w[k][l] for k in SMALL if k != "final_norm"}
        next_wi = (shards["ffn1_wi"], l + 1) if l + 1 < depth else None
        h_last, sv, gathered, ffn1_wi_l = _layer_fwd(h_last, small, ffn1_wi_l, functools.partial(lambda k, l: (shards[k], l), l=l),
                                                      next_wi, p2[l], seq)
        layers.append(layer_weights(l, gathered))
        saved.append(sv)

    def head(hh, tgt, gf):
        r = lax.rsqrt(jnp.mean(hh * hh, axis=-1, keepdims=True) + NORM_EPS)
        xh = hh * r
        diff = xh * gf - tgt
        dy = diff * (1.0 / dm)
        dxh = dy * gf
        dx = r * (dxh - xh * jnp.mean(dxh * xh, axis=-1, keepdims=True))
        return dx, jnp.sum(diff * diff, axis=0, keepdims=True) * (0.5 / dm), jnp.sum(dy * xh, axis=0, keepdims=True)

    d_last, loss_cols, g_final = _rowwise("loss_head", head, n, 512, [(h_last, dm, 0), (loss_target.reshape(n, dm), dm, 0)],
                                          [final_norm.reshape(1, dm)], [(dm, F32)], [(1, dm), (1, dm)])
    loss = lax.psum(jnp.sum(loss_cols), ("x", "y", "c"))

    d_x = d_last
    layer_grads = [None] * depth
    bufs = {k: lax.empty((depth, N_CHIPS) + w[k].shape[1:], F32) for k in BIG}
    for l in reversed(range(depth)):
        d_x, layer_grads[l] = _layer_bwd(d_x, layers[l], p2[l], saved[l], seq, bufs, l)
        bufs = {k: layer_grads[l][k] for k in BIG}
    grads = {k: jnp.stack([g[k] for g in layer_grads]) for k in layer_grads[0] if k not in BIG}
    grads.update(bufs)
    grad_x = d_x.reshape(bsz, seq, dm)

    big_sum = [reduce_scatter_streamed("rs_" + k, grads[k]) for k in BIG]
    small_keys = [k for k in SMALL]
    small_parts = [grads[k] if k != "final_norm" else g_final.reshape(dm) for k in small_keys]
    small_sum = _unflatten_small(all_reduce_small(_flatten_small(small_parts)), small_parts)
    g_out = dict(zip(BIG, big_sum))
    g_out.update(dict(zip(small_keys, small_sum)))
    for k in BIG:
        g_out[k] = g_out[k].reshape(w[k].shape)

    delta, new_m, new_v = {}, {}, {}
    for k in BIG:
        delta[k], new_m[k], new_v[k] = adamw("adamw_" + k, w[k], g_out[k], mom[k], var[k])
    sw = adamw("adamw_small", *[_flatten_small([t[k] for k in small_keys]) for t in (w, g_out, mom, var)])
    for name, flat in zip((delta, new_m, new_v), sw):
        for k, t in zip(small_keys, _unflatten_small(flat, [w[k] for k in small_keys])):
            name[k] = t

    return (loss, grad_x, *[g_out[k] for k in WEIGHTS], *[delta[k] for k in WEIGHTS],
            *[new_m[k] for k in WEIGHTS], *[new_v[k] for k in WEIGHTS])
```
